```python
import jax, jax.numpy as jnp
from jax import lax
import numpy as np

D_MODEL = 1024
BATCH = 8
SEQ = 16384
DEPTH = 2

MIX_WIDTH = D_MODEL
CONV_WIDTH = MIX_WIDTH // 2
ATT_WIDTH = MIX_WIDTH - CONV_WIDTH
HEAD_DIM = 64
N_ATT_HEADS = ATT_WIDTH // HEAD_DIM
N_CONV_GROUPS = CONV_WIDTH // HEAD_DIM
CONV_K = 3
N_MEM = 256
N_XHEADS = 4
XHEAD_DIM = D_MODEL // N_XHEADS
D_FF = 2816
Q_BLOCK = 128
EPS = 1e-6
IN_COLS = 3 * CONV_WIDTH + 3 * ATT_WIDTH + N_ATT_HEADS
SPLITS = [CONV_WIDTH, 2 * CONV_WIDTH, 3 * CONV_WIDTH,
          3 * CONV_WIDTH + ATT_WIDTH, 3 * CONV_WIDTH + 2 * ATT_WIDTH,
          3 * CONV_WIDTH + 3 * ATT_WIDTH]

kernel_name = "hybrid_conv_fox_macaron_memxattn"


def rmsnorm(x, g):
    xf = x.astype(jnp.float32)
    y = xf * lax.rsqrt(jnp.mean(xf * xf, axis=-1, keepdims=True) + EPS)
    return (y * g.astype(jnp.float32)).astype(x.dtype)


def swiglu(h, w_gu, w_down):
    gate, up = jnp.split(h @ w_gu, 2, axis=-1)
    return (jax.nn.silu(gate) * up) @ w_down


def causal_depthwise_conv(u, w):
    kern = w[:, None, :].astype(u.dtype)
    return lax.conv_general_dilated(
        u, kern, window_strides=(1,), padding=[(CONV_K - 1, 0)],
        dimension_numbers=('NWC', 'WIO', 'NWC'), feature_group_count=u.shape[-1])


def forgetting_attention(q, k, v, log_f):
    B, S, H, Dh = q.shape
    nb = S // Q_BLOCK
    scale = Dh ** -0.5
    c = jnp.cumsum(log_f, axis=1).transpose(0, 2, 1)
    qb = q.reshape(B, nb, Q_BLOCK, H, Dh).transpose(1, 0, 3, 2, 4)
    cqb = c.reshape(B, H, nb, Q_BLOCK).transpose(2, 0, 1, 3)
    kpos = jnp.arange(S)

    def block(args):
        i, q_i, cq_i = args
        s = jnp.einsum('bhqd,bkhd->bhqk', q_i, k,
                       preferred_element_type=jnp.float32) * scale
        s = s + cq_i[..., None] - c[:, :, None, :]
        qpos = i * Q_BLOCK + jnp.arange(Q_BLOCK)
        mask = kpos[None, :] <= qpos[:, None]
        s = jnp.where(mask, s, -jnp.inf)
        p = jax.nn.softmax(s, axis=-1)
        return jnp.einsum('bhqk,bkhd->bqhd', p.astype(v.dtype), v)

    out = lax.map(block, (jnp.arange(nb), qb, cqb))
    return out.transpose(1, 0, 2, 3, 4).reshape(B, S, H * Dh)


def memory_cross_attention(h, m, w_q, w_kv, w_o):
    B, S, _ = h.shape
    M = m.shape[1]
    q = (h @ w_q).reshape(B, S, N_XHEADS, XHEAD_DIM)
    k, v = jnp.split(m @ w_kv, 2, axis=-1)
    k = k.reshape(B, M, N_XHEADS, XHEAD_DIM)
    v = v.reshape(B, M, N_XHEADS, XHEAD_DIM)
    s = jnp.einsum('bshd,bmhd->bhsm', q, k,
                   preferred_element_type=jnp.float32) * (XHEAD_DIM ** -0.5)
    p = jax.nn.softmax(s, axis=-1)
    o = jnp.einsum('bhsm,bmhd->bshd', p.astype(v.dtype), v).reshape(B, S, D_MODEL)
    return o @ w_o


def _fwd_setup_inputs(seed: int = 0) -> dict:
    key = jax.random.key(seed)
    ks = jax.random.split(key, 24)
    f32 = jnp.float32

    def w(k, shape, fan_in):
        return jax.random.normal(k, shape, f32) * (fan_in ** -0.5)

    def gain(k, shape):
        return 1.0 + 0.1 * jax.random.normal(k, shape, f32)

    return {
        "x": jax.random.normal(ks[0], (BATCH, SEQ, D_MODEL), f32),
        "mem": jax.random.normal(ks[1], (BATCH, N_MEM, D_MODEL), f32),
        "g_ffn1": gain(ks[2], (DEPTH, D_MODEL)),
        "w_ffn1_gu": w(ks[3], (DEPTH, D_MODEL, 2 * D_FF), D_MODEL),
        "w_ffn1_down": w(ks[4], (DEPTH, D_FF, D_MODEL), D_FF),
        "g_mix": gain(ks[5], (DEPTH, D_MODEL)),
        "w_mix_in": w(ks[6], (DEPTH, D_MODEL, IN_COLS), D_MODEL),
        "w_conv": w(ks[7], (DEPTH, CONV_K, CONV_WIDTH), CONV_K),
        "b_f": 2.0 + 0.5 * jax.random.normal(ks[8], (DEPTH, N_ATT_HEADS), f32),
        "g_conv_out": gain(ks[9], (DEPTH, CONV_WIDTH)),
        "g_att_out": gain(ks[10], (DEPTH, ATT_WIDTH)),
        "w_mix_out": w(ks[11], (DEPTH, MIX_WIDTH, D_MODEL), MIX_WIDTH),
        "g_xattn": gain(ks[12], (DEPTH, D_MODEL)),
        "g_mem": gain(ks[13], (DEPTH, D_MODEL)),
        "w_xq": w(ks[14], (DEPTH, D_MODEL, D_MODEL), D_MODEL),
        "w_xkv": w(ks[15], (DEPTH, D_MODEL, 2 * D_MODEL), D_MODEL),
        "w_xo": w(ks[16], (DEPTH, D_MODEL, D_MODEL), D_MODEL),
        "g_ffn2": gain(ks[17], (DEPTH, D_MODEL)),
        "w_ffn2_gu": w(ks[18], (DEPTH, D_MODEL, 2 * D_FF), D_MODEL),
        "w_ffn2_down": w(ks[19], (DEPTH, D_FF, D_MODEL), D_FF),
        "g_final": gain(ks[20], (D_MODEL,)),
    }


def _fwd_reference(x, mem, g_ffn1, w_ffn1_gu, w_ffn1_down, g_mix, w_mix_in, w_conv, b_f,
              g_conv_out, g_att_out, w_mix_out, g_xattn, g_mem, w_xq, w_xkv, w_xo,
              g_ffn2, w_ffn2_gu, w_ffn2_down, g_final):
    B, S, _ = x.shape
    for l in range(DEPTH):
        x = x + 0.5 * swiglu(rmsnorm(x, g_ffn1[l]), w_ffn1_gu[l], w_ffn1_down[l])

        h = rmsnorm(x, g_mix[l])
        z = h @ w_mix_in[l]
        zb, zc, zv, zq, zk, zval, zf = jnp.split(z, SPLITS, axis=-1)

        y_conv = zb * causal_depthwise_conv(zc * zv, w_conv[l])

        log_f = jax.nn.log_sigmoid((zf + b_f[l]).astype(jnp.float32))
        q = zq.reshape(B, S, N_ATT_HEADS, HEAD_DIM)
        k = zk.reshape(B, S, N_ATT_HEADS, HEAD_DIM)
        v = zval.reshape(B, S, N_ATT_HEADS, HEAD_DIM)
        y_att = forgetting_attention(q, k, v, log_f)

        y = jnp.concatenate([rmsnorm(y_conv, g_conv_out[l]),
                             rmsnorm(y_att, g_att_out[l])], axis=-1)
        x = x + y @ w_mix_out[l]

        x = x + memory_cross_attention(rmsnorm(x, g_xattn[l]), rmsnorm(mem, g_mem[l]),
                                       w_xq[l], w_xkv[l], w_xo[l])

        x = x + 0.5 * swiglu(rmsnorm(x, g_ffn2[l]), w_ffn2_gu[l], w_ffn2_down[l])
    return rmsnorm(x, g_final)


import jax as _jax
import jax.numpy as _jnp

TWIN_FORMAT = 'train_step'
FWD_PARAMS = ['x', 'mem', 'g_ffn1', 'w_ffn1_gu', 'w_ffn1_down', 'g_mix', 'w_mix_in', 'w_conv', 'b_f', 'g_conv_out', 'g_att_out', 'w_mix_out', 'g_xattn', 'g_mem', 'w_xq', 'w_xkv', 'w_xo', 'g_ffn2', 'w_ffn2_gu', 'w_ffn2_down', 'g_final']
TWIN_WEIGHTS = ['g_ffn1', 'w_ffn1_gu', 'w_ffn1_down', 'g_mix', 'w_mix_in', 'w_conv', 'b_f', 'g_conv_out', 'g_att_out', 'w_mix_out', 'g_xattn', 'g_mem', 'w_xq', 'w_xkv', 'w_xo', 'g_ffn2', 'w_ffn2_gu', 'w_ffn2_down', 'g_final']
TWIN_DIFF_INPUT = 'x'
TWIN_INPUTS = ['x', 'mem', 'g_ffn1', 'w_ffn1_gu', 'w_ffn1_down', 'g_mix', 'w_mix_in', 'w_conv', 'b_f', 'g_conv_out', 'g_att_out', 'w_mix_out', 'g_xattn', 'g_mem', 'w_xq', 'w_xkv', 'w_xo', 'g_ffn2', 'w_ffn2_gu', 'w_ffn2_down', 'g_final', 'loss_target', 'm_g_ffn1', 'm_w_ffn1_gu', 'm_w_ffn1_down', 'm_g_mix', 'm_w_mix_in', 'm_w_conv', 'm_b_f', 'm_g_conv_out', 'm_g_att_out', 'm_w_mix_out', 'm_g_xattn', 'm_g_mem', 'm_w_xq', 'm_w_xkv', 'm_w_xo', 'm_g_ffn2', 'm_w_ffn2_gu', 'm_w_ffn2_down', 'm_g_final', 'v_g_ffn1', 'v_w_ffn1_gu', 'v_w_ffn1_down', 'v_g_mix', 'v_w_mix_in', 'v_w_conv', 'v_b_f', 'v_g_conv_out', 'v_g_att_out', 'v_w_mix_out', 'v_g_xattn', 'v_g_mem', 'v_w_xq', 'v_w_xkv', 'v_w_xo', 'v_g_ffn2', 'v_w_ffn2_gu', 'v_w_ffn2_down', 'v_g_final']
TWIN_OUTPUTS = ['loss', 'grad_x', 'grad_g_ffn1', 'grad_w_ffn1_gu', 'grad_w_ffn1_down', 'grad_g_mix', 'grad_w_mix_in', 'grad_w_conv', 'grad_b_f', 'grad_g_conv_out', 'grad_g_att_out', 'grad_w_mix_out', 'grad_g_xattn', 'grad_g_mem', 'grad_w_xq', 'grad_w_xkv', 'grad_w_xo', 'grad_g_ffn2', 'grad_w_ffn2_gu', 'grad_w_ffn2_down', 'grad_g_final', 'delta_g_ffn1', 'delta_w_ffn1_gu', 'delta_w_ffn1_down', 'delta_g_mix', 'delta_w_mix_in', 'delta_w_conv', 'delta_b_f', 'delta_g_conv_out', 'delta_g_att_out', 'delta_w_mix_out', 'delta_g_xattn', 'delta_g_mem', 'delta_w_xq', 'delta_w_xkv', 'delta_w_xo', 'delta_g_ffn2', 'delta_w_ffn2_gu', 'delta_w_ffn2_down', 'delta_g_final', 'new_m_g_ffn1', 'new_m_w_ffn1_gu', 'new_m_w_ffn1_down', 'new_m_g_mix', 'new_m_w_mix_in', 'new_m_w_conv', 'new_m_b_f', 'new_m_g_conv_out', 'new_m_g_att_out', 'new_m_w_mix_out', 'new_m_g_xattn', 'new_m_g_mem', 'new_m_w_xq', 'new_m_w_xkv', 'new_m_w_xo', 'new_m_g_ffn2', 'new_m_w_ffn2_gu', 'new_m_w_ffn2_down', 'new_m_g_final', 'new_v_g_ffn1', 'new_v_w_ffn1_gu', 'new_v_w_ffn1_down', 'new_v_g_mix', 'new_v_w_mix_in', 'new_v_w_conv', 'new_v_b_f', 'new_v_g_conv_out', 'new_v_g_att_out', 'new_v_w_mix_out', 'new_v_g_xattn', 'new_v_g_mem', 'new_v_w_xq', 'new_v_w_xkv', 'new_v_w_xo', 'new_v_g_ffn2', 'new_v_w_ffn2_gu', 'new_v_w_ffn2_down', 'new_v_g_final']
TWIN_LEAF_KINDS = {'loss': 'loss', 'grad_x': 'grad_x', 'grad_g_ffn1': 'grad_w', 'grad_w_ffn1_gu': 'grad_w', 'grad_w_ffn1_down': 'grad_w', 'grad_g_mix': 'grad_w', 'grad_w_mix_in': 'grad_w', 'grad_w_conv': 'grad_w', 'grad_b_f': 'grad_w', 'grad_g_conv_out': 'grad_w', 'grad_g_att_out': 'grad_w', 'grad_w_mix_out': 'grad_w', 'grad_g_xattn': 'grad_w', 'grad_g_mem': 'grad_w', 'grad_w_xq': 'grad_w', 'grad_w_xkv': 'grad_w', 'grad_w_xo': 'grad_w', 'grad_g_ffn2': 'grad_w', 'grad_w_ffn2_gu': 'grad_w', 'grad_w_ffn2_down': 'grad_w', 'grad_g_final': 'grad_w', 'delta_g_ffn1': 'delta_w', 'delta_w_ffn1_gu': 'delta_w', 'delta_w_ffn1_down': 'delta_w', 'delta_g_mix': 'delta_w', 'delta_w_mix_in': 'delta_w', 'delta_w_conv': 'delta_w', 'delta_b_f': 'delta_w', 'delta_g_conv_out': 'delta_w', 'delta_g_att_out': 'delta_w', 'delta_w_mix_out': 'delta_w', 'delta_g_xattn': 'delta_w', 'delta_g_mem': 'delta_w', 'delta_w_xq': 'delta_w', 'delta_w_xkv': 'delta_w', 'delta_w_xo': 'delta_w', 'delta_g_ffn2': 'delta_w', 'delta_w_ffn2_gu': 'delta_w', 'delta_w_ffn2_down': 'delta_w', 'delta_g_final': 'delta_w', 'new_m_g_ffn1': 'new_m', 'new_m_w_ffn1_gu': 'new_m', 'new_m_w_ffn1_down': 'new_m', 'new_m_g_mix': 'new_m', 'new_m_w_mix_in': 'new_m', 'new_m_w_conv': 'new_m', 'new_m_b_f': 'new_m', 'new_m_g_conv_out': 'new_m', 'new_m_g_att_out': 'new_m', 'new_m_w_mix_out': 'new_m', 'new_m_g_xattn': 'new_m', 'new_m_g_mem': 'new_m', 'new_m_w_xq': 'new_m', 'new_m_w_xkv': 'new_m', 'new_m_w_xo': 'new_m', 'new_m_g_ffn2': 'new_m', 'new_m_w_ffn2_gu': 'new_m', 'new_m_w_ffn2_down': 'new_m', 'new_m_g_final': 'new_m', 'new_v_g_ffn1': 'new_v', 'new_v_w_ffn1_gu': 'new_v', 'new_v_w_ffn1_down': 'new_v', 'new_v_g_mix': 'new_v', 'new_v_w_mix_in': 'new_v', 'new_v_w_conv': 'new_v', 'new_v_b_f': 'new_v', 'new_v_g_conv_out': 'new_v', 'new_v_g_att_out': 'new_v', 'new_v_w_mix_out': 'new_v', 'new_v_g_xattn': 'new_v', 'new_v_g_mem': 'new_v', 'new_v_w_xq': 'new_v', 'new_v_w_xkv': 'new_v', 'new_v_w_xo': 'new_v', 'new_v_g_ffn2': 'new_v', 'new_v_w_ffn2_gu': 'new_v', 'new_v_w_ffn2_down': 'new_v', 'new_v_g_final': 'new_v'}


def _forward(args):
    return _fwd_reference(*[args[k] for k in FWD_PARAMS])


def _output_shape():
    def fwd():
        inp = _fwd_setup_inputs(0)
        return _fwd_reference(*[inp[k] for k in FWD_PARAMS])
    out = _jax.eval_shape(fwd)
    return out.shape, out.dtype

N_MICROBATCH = 1
ADAM_LR = 0.001
ADAM_B1 = 0.9
ADAM_B2 = 0.999
ADAM_EPS = 1e-08
ADAM_WD = 0.01
ADAM_STEP = 10
PER_EXAMPLE_BATCH_AXIS = {'x': 0, 'mem': 0, 'loss_target': 0}
SHARED_INPUTS = []
_WEIGHT_DTYPES = {'g_ffn1': _jnp.float32, 'w_ffn1_gu': _jnp.float32, 'w_ffn1_down': _jnp.float32, 'g_mix': _jnp.float32, 'w_mix_in': _jnp.float32, 'w_conv': _jnp.float32, 'b_f': _jnp.float32, 'g_conv_out': _jnp.float32, 'g_att_out': _jnp.float32, 'w_mix_out': _jnp.float32, 'g_xattn': _jnp.float32, 'g_mem': _jnp.float32, 'w_xq': _jnp.float32, 'w_xkv': _jnp.float32, 'w_xo': _jnp.float32, 'g_ffn2': _jnp.float32, 'w_ffn2_gu': _jnp.float32, 'w_ffn2_down': _jnp.float32, 'g_final': _jnp.float32}
MOMENT_SCALE = {'g_ffn1': 2.262552e-01, 'w_ffn1_gu': 8.634550e-02, 'w_ffn1_down': 1.417929e-01, 'g_mix': 4.656714e-01, 'w_mix_in': 2.524597e-01, 'w_conv': 2.937410e-01, 'b_f': 1.478760e+00, 'g_conv_out': 3.637534e-01, 'g_att_out': 3.189662e-01, 'w_mix_out': 3.014639e-01, 'g_xattn': 3.161424e-02, 'g_mem': 5.111144e-02, 'w_xq': 2.982127e-02, 'w_xkv': 3.166015e-02, 'w_xo': 3.336110e-02, 'g_ffn2': 1.214013e-01, 'w_ffn2_gu': 4.841731e-02, 'w_ffn2_down': 8.025680e-02, 'g_final': 1.292488e+02}


def _to_microbatches(a, axis):
    t = _jnp.moveaxis(a, axis, 0)
    t = t.reshape((N_MICROBATCH, t.shape[0] // N_MICROBATCH) + t.shape[1:])
    return _jnp.moveaxis(t, 1, axis + 1)


def setup_inputs(seed: int = 0) -> dict:
    inp = _fwd_setup_inputs(seed)
    key = _jax.random.fold_in(_jax.random.key(seed), 7919)
    shape, _ = _output_shape()
    out = dict(inp)
    out["loss_target"] = _jax.random.normal(_jax.random.fold_in(key, 0), shape, _jnp.float32)
    for i, name in enumerate(TWIN_WEIGHTS):
        w = inp[name].astype(_jnp.float32)
        if MOMENT_SCALE is None:
            s = _jnp.sqrt(_jnp.mean(_jnp.square(w)) + 1e-30)
        else:
            s = MOMENT_SCALE[name]
        km, kv = _jax.random.split(_jax.random.fold_in(key, i + 1))
        out[name] = w
        out["m_" + name] = s * _jax.random.normal(km, w.shape, _jnp.float32)
        out["v_" + name] = (s * s) * _jax.random.uniform(kv, w.shape, _jnp.float32, 0.5, 1.5)
    if N_MICROBATCH > 1:
        for name, axis in PER_EXAMPLE_BATCH_AXIS.items():
            out[name] = _to_microbatches(out[name], axis)
    return {'x': out['x'], 'mem': out['mem'], 'g_ffn1': out['g_ffn1'], 'w_ffn1_gu': out['w_ffn1_gu'], 'w_ffn1_down': out['w_ffn1_down'], 'g_mix': out['g_mix'], 'w_mix_in': out['w_mix_in'], 'w_conv': out['w_conv'], 'b_f': out['b_f'], 'g_conv_out': out['g_conv_out'], 'g_att_out': out['g_att_out'], 'w_mix_out': out['w_mix_out'], 'g_xattn': out['g_xattn'], 'g_mem': out['g_mem'], 'w_xq': out['w_xq'], 'w_xkv': out['w_xkv'], 'w_xo': out['w_xo'], 'g_ffn2': out['g_ffn2'], 'w_ffn2_gu': out['w_ffn2_gu'], 'w_ffn2_down': out['w_ffn2_down'], 'g_final': out['g_final'], 'loss_target': out['loss_target'], 'm_g_ffn1': out['m_g_ffn1'], 'm_w_ffn1_gu': out['m_w_ffn1_gu'], 'm_w_ffn1_down': out['m_w_ffn1_down'], 'm_g_mix': out['m_g_mix'], 'm_w_mix_in': out['m_w_mix_in'], 'm_w_conv': out['m_w_conv'], 'm_b_f': out['m_b_f'], 'm_g_conv_out': out['m_g_conv_out'], 'm_g_att_out': out['m_g_att_out'], 'm_w_mix_out': out['m_w_mix_out'], 'm_g_xattn': out['m_g_xattn'], 'm_g_mem': out['m_g_mem'], 'm_w_xq': out['m_w_xq'], 'm_w_xkv': out['m_w_xkv'], 'm_w_xo': out['m_w_xo'], 'm_g_ffn2': out['m_g_ffn2'], 'm_w_ffn2_gu': out['m_w_ffn2_gu'], 'm_w_ffn2_down': out['m_w_ffn2_down'], 'm_g_final': out['m_g_final'], 'v_g_ffn1': out['v_g_ffn1'], 'v_w_ffn1_gu': out['v_w_ffn1_gu'], 'v_w_ffn1_down': out['v_w_ffn1_down'], 'v_g_mix': out['v_g_mix'], 'v_w_mix_in': out['v_w_mix_in'], 'v_w_conv': out['v_w_conv'], 'v_b_f': out['v_b_f'], 'v_g_conv_out': out['v_g_conv_out'], 'v_g_att_out': out['v_g_att_out'], 'v_w_mix_out': out['v_w_mix_out'], 'v_g_xattn': out['v_g_xattn'], 'v_g_mem': out['v_g_mem'], 'v_w_xq': out['v_w_xq'], 'v_w_xkv': out['v_w_xkv'], 'v_w_xo': out['v_w_xo'], 'v_g_ffn2': out['v_g_ffn2'], 'v_w_ffn2_gu': out['v_w_ffn2_gu'], 'v_w_ffn2_down': out['v_w_ffn2_down'], 'v_g_final': out['v_g_final']}


def _loss(weights, diff, rest, loss_target):
    with _jax.named_scope("forward"):
        args = {**rest, TWIN_DIFF_INPUT: diff, **{k: w.astype(_WEIGHT_DTYPES[k]) for k, w in weights.items()}}
        y = _forward(args)
    with _jax.named_scope("loss_head"):
        err = _jnp.square(y.astype(_jnp.float32) - loss_target)
        return 0.5 * _jnp.sum(_jnp.mean(err, axis=-1)) if err.ndim else 0.5 * err


def _adamw(w, g, m, v):
    m = ADAM_B1 * m + (1.0 - ADAM_B1) * g
    v = ADAM_B2 * v + (1.0 - ADAM_B2) * _jnp.square(g)
    m_hat = m / (1.0 - ADAM_B1 ** ADAM_STEP)
    v_hat = v / (1.0 - ADAM_B2 ** ADAM_STEP)
    delta = -ADAM_LR * (m_hat / (_jnp.sqrt(v_hat) + ADAM_EPS) + ADAM_WD * w)
    return delta, m, v


def reference(x, mem, g_ffn1, w_ffn1_gu, w_ffn1_down, g_mix, w_mix_in, w_conv, b_f, g_conv_out, g_att_out, w_mix_out, g_xattn, g_mem, w_xq, w_xkv, w_xo, g_ffn2, w_ffn2_gu, w_ffn2_down, g_final, loss_target, m_g_ffn1, m_w_ffn1_gu, m_w_ffn1_down, m_g_mix, m_w_mix_in, m_w_conv, m_b_f, m_g_conv_out, m_g_att_out, m_w_mix_out, m_g_xattn, m_g_mem, m_w_xq, m_w_xkv, m_w_xo, m_g_ffn2, m_w_ffn2_gu, m_w_ffn2_down, m_g_final, v_g_ffn1, v_w_ffn1_gu, v_w_ffn1_down, v_g_mix, v_w_mix_in, v_w_conv, v_b_f, v_g_conv_out, v_g_att_out, v_w_mix_out, v_g_xattn, v_g_mem, v_w_xq, v_w_xkv, v_w_xo, v_g_ffn2, v_w_ffn2_gu, v_w_ffn2_down, v_g_final):
    given = dict(x=x, mem=mem, g_ffn1=g_ffn1, w_ffn1_gu=w_ffn1_gu, w_ffn1_down=w_ffn1_down, g_mix=g_mix, w_mix_in=w_mix_in, w_conv=w_conv, b_f=b_f, g_conv_out=g_conv_out, g_att_out=g_att_out, w_mix_out=w_mix_out, g_xattn=g_xattn, g_mem=g_mem, w_xq=w_xq, w_xkv=w_xkv, w_xo=w_xo, g_ffn2=g_ffn2, w_ffn2_gu=w_ffn2_gu, w_ffn2_down=w_ffn2_down, g_final=g_final, loss_target=loss_target, m_g_ffn1=m_g_ffn1, m_w_ffn1_gu=m_w_ffn1_gu, m_w_ffn1_down=m_w_ffn1_down, m_g_mix=m_g_mix, m_w_mix_in=m_w_mix_in, m_w_conv=m_w_conv, m_b_f=m_b_f, m_g_conv_out=m_g_conv_out, m_g_att_out=m_g_att_out, m_w_mix_out=m_w_mix_out, m_g_xattn=m_g_xattn, m_g_mem=m_g_mem, m_w_xq=m_w_xq, m_w_xkv=m_w_xkv, m_w_xo=m_w_xo, m_g_ffn2=m_g_ffn2, m_w_ffn2_gu=m_w_ffn2_gu, m_w_ffn2_down=m_w_ffn2_down, m_g_final=m_g_final, v_g_ffn1=v_g_ffn1, v_w_ffn1_gu=v_w_ffn1_gu, v_w_ffn1_down=v_w_ffn1_down, v_g_mix=v_g_mix, v_w_mix_in=v_w_mix_in, v_w_conv=v_w_conv, v_b_f=v_b_f, v_g_conv_out=v_g_conv_out, v_g_att_out=v_g_att_out, v_w_mix_out=v_w_mix_out, v_g_xattn=v_g_xattn, v_g_mem=v_g_mem, v_w_xq=v_w_xq, v_w_xkv=v_w_xkv, v_w_xo=v_w_xo, v_g_ffn2=v_g_ffn2, v_w_ffn2_gu=v_w_ffn2_gu, v_w_ffn2_down=v_w_ffn2_down, v_g_final=v_g_final)
    weights = {n: given[n] for n in TWIN_WEIGHTS}
    shared = {n: given[n] for n in SHARED_INPUTS}
    per_example = {n: given[n] for n in ['x', 'mem']}
    grad_fn = _jax.value_and_grad(_loss, argnums=(0, 1))

    def one_microbatch(ex, loss_target):
        ex = dict(ex)
        diff = ex.pop(TWIN_DIFF_INPUT)
        return grad_fn(weights, diff, {**shared, **ex}, loss_target)

    if N_MICROBATCH == 1:
        loss, (grad_w, grad_x) = one_microbatch(per_example, given["loss_target"])
    else:
        def body(carry, xs):
            loss_sum, grad_sum = carry
            l_k, (gw_k, gx_k) = one_microbatch(xs[0], xs[1])
            with _jax.named_scope("update"):
                return (loss_sum + l_k, _jax.tree.map(_jnp.add, grad_sum, gw_k)), gx_k

        init = (_jnp.zeros((), _jnp.float32), _jax.tree.map(_jnp.zeros_like, weights))
        (loss, grad_w), grad_x = _jax.lax.scan(body, init, (per_example, given["loss_target"]))
    with _jax.named_scope("update"):
        delta_w, new_m, new_v = {}, {}, {}
        for n in TWIN_WEIGHTS:
            delta_w[n], new_m[n], new_v[n] = _adamw(weights[n], grad_w[n], given["m_" + n], given["v_" + n])
    return (loss, grad_x, *[grad_w[n] for n in TWIN_WEIGHTS], *[delta_w[n] for n in TWIN_WEIGHTS],
            *[new_m[n] for n in TWIN_WEIGHTS], *[new_v[n] for n in TWIN_WEIGHTS])
```

```python
import functools
import math

import numpy as np
import jax
import jax.numpy as jnp
from jax import lax
from jax.experimental import pallas as pl
from jax.experimental.pallas import tpu as pltpu

F32 = jnp.float32
BF16 = jnp.bfloat16

EPS = 1e-6
HEAD_DIM = 64
LANES = 128
N_XHEADS = 4
CONV_K = 3
ADAM_LR, ADAM_B1, ADAM_B2, ADAM_EPS, ADAM_WD, ADAM_STEP = 0.001, 0.9, 0.999, 1e-08, 0.01, 10
VMEM_LIMIT_BYTES = 56 * 1024 * 1024
NEG_BIG = -1e30
MESH_AXES = ("x", "y", "c")
N_CHIPS = 4


def _params(*sem):
    return pltpu.CompilerParams(dimension_semantics=sem, vmem_limit_bytes=VMEM_LIMIT_BYTES)


_DIMS = {"nn": (((1,), (0,)), ((), ())), "nt": (((1,), (1,)), ((), ())), "tn": (((0,), (0,)), ((), ()))}


def _matmul(a, b, mode, out_dtype, *, tm, tn, tk, name, alpha=1.0, residual=None):
    if mode == "nn":
        (m, k), (k2, n) = a.shape, b.shape
    elif mode == "nt":
        (m, k), (n, k2) = a.shape, b.shape
    else:
        (k, m), (k2, n) = a.shape, b.shape
    assert k == k2, (a.shape, b.shape, mode)
    tm, tn, tk = min(tm, m), min(tn, n), min(tk, k)
    assert m % tm == 0 and n % tn == 0 and k % tk == 0, (m, n, k, tm, tn, tk)
    nk = k // tk
    dims = _DIMS[mode]

    def body(*refs):
        if residual is None:
            a_ref, b_ref, o_ref, *scratch = refs
            r_ref = None
        else:
            a_ref, b_ref, r_ref, o_ref, *scratch = refs
        prod = lax.dot_general(a_ref[...].astype(BF16), b_ref[...].astype(BF16), dims, preferred_element_type=F32)

        def finish(acc):
            if alpha != 1.0:
                acc = acc * alpha
            if r_ref is not None:
                acc = acc + r_ref[...].astype(F32)
            o_ref[...] = acc.astype(o_ref.dtype)

        if nk == 1:
            finish(prod)
        else:
            acc_ref = scratch[0]
            kk = pl.program_id(2)

            @pl.when(kk == 0)
            def _():
                acc_ref[...] = prod

            @pl.when(kk > 0)
            def _():
                acc_ref[...] += prod

            @pl.when(kk == nk - 1)
            def _():
                finish(acc_ref[...])

    if mode == "nn":
        a_spec = pl.BlockSpec((tm, tk), lambda i, j, kk: (i, kk))
        b_spec = pl.BlockSpec((tk, tn), lambda i, j, kk: (kk, j))
    elif mode == "nt":
        a_spec = pl.BlockSpec((tm, tk), lambda i, j, kk: (i, kk))
        b_spec = pl.BlockSpec((tn, tk), lambda i, j, kk: (j, kk))
    else:
        a_spec = pl.BlockSpec((tk, tm), lambda i, j, kk: (kk, i))
        b_spec = pl.BlockSpec((tk, tn), lambda i, j, kk: (kk, j))
    o_spec = pl.BlockSpec((tm, tn), lambda i, j, kk: (i, j))
    in_specs, args = [a_spec, b_spec], [a, b]
    if residual is not None:
        in_specs.append(o_spec)
        args.append(residual)
    return pl.pallas_call(
        body, name=name, grid=(m // tm, n // tn, nk), in_specs=in_specs, out_specs=o_spec,
        out_shape=jax.ShapeDtypeStruct((m, n), out_dtype),
        scratch_shapes=[pltpu.VMEM((tm, tn), F32)] if nk > 1 else [],
        compiler_params=_params("parallel", "parallel", "arbitrary"),
    )(*args)


def _row_tile(rows, want):
    t = min(rows, want)
    assert rows % t == 0, (rows, t)
    return t


def _rmsnorm_fwd(x, g, *, name, tr=1024):
    s, d = x.shape
    tr = _row_tile(s, tr)

    def body(x_ref, g_ref, o_ref):
        xf = x_ref[...]
        r = lax.rsqrt(jnp.mean(xf * xf, axis=-1, keepdims=True) + EPS)
        o_ref[...] = (xf * r * g_ref[...]).astype(o_ref.dtype)

    return pl.pallas_call(
        body, name=name, grid=(s // tr,),
        in_specs=[pl.BlockSpec((tr, d), lambda i: (i, 0)), pl.BlockSpec((1, d), lambda i: (0, 0))],
        out_specs=pl.BlockSpec((tr, d), lambda i: (i, 0)),
        out_shape=jax.ShapeDtypeStruct((s, d), BF16),
        compiler_params=_params("parallel"),
    )(x, g.reshape(1, d))


def _rmsnorm_bwd(x, g, dh, dres, *, name, tr=512):
    s, d = x.shape
    tr = _row_tile(s, tr)

    def body(x_ref, g_ref, dh_ref, *rest):
        if dres is None:
            dx_ref, dg_ref = rest
        else:
            dres_ref, dx_ref, dg_ref = rest
        xf = x_ref[...]
        r = lax.rsqrt(jnp.mean(xf * xf, axis=-1, keepdims=True) + EPS)
        xhat = xf * r
        dhf = dh_ref[...].astype(F32)
        dxhat = dhf * g_ref[...]
        dx = r * (dxhat - xhat * jnp.mean(dxhat * xhat, axis=-1, keepdims=True))
        if dres is not None:
            dx = dx + dres_ref[...]
        dx_ref[...] = dx

        @pl.when(pl.program_id(0) == 0)
        def _():
            dg_ref[...] = jnp.zeros_like(dg_ref)

        dg_ref[...] += jnp.sum(dhf * xhat, axis=0, keepdims=True)

    row = pl.BlockSpec((tr, d), lambda i: (i, 0))
    vec = pl.BlockSpec((1, d), lambda i: (0, 0))
    in_specs, args = [row, vec, row], [x, g.reshape(1, d), dh]
    if dres is not None:
        in_specs.append(row)
        args.append(dres)
    dx, dg = pl.pallas_call(
        body, name=name, grid=(s // tr,), in_specs=in_specs, out_specs=[row, vec],
        out_shape=[jax.ShapeDtypeStruct((s, d), F32), jax.ShapeDtypeStruct((1, d), F32)],
        compiler_params=_params("arbitrary"),
    )(*args)
    return dx, dg.reshape(d)


def _swiglu_fwd(gu, *, name, tr=512):
    s, f2 = gu.shape
    f = f2 // 2
    tr = _row_tile(s, tr)

    def body(gu_ref, a_ref):
        gate = gu_ref[:, :f].astype(F32)
        up = gu_ref[:, f:].astype(F32)
        a_ref[...] = (gate * jax.nn.sigmoid(gate) * up).astype(a_ref.dtype)

    return pl.pallas_call(
        body, name=name, grid=(s // tr,),
        in_specs=[pl.BlockSpec((tr, f2), lambda i: (i, 0))],
        out_specs=pl.BlockSpec((tr, f), lambda i: (i, 0)),
        out_shape=jax.ShapeDtypeStruct((s, f), BF16),
        compiler_params=_params("parallel"),
    )(gu)


def _swiglu_bwd(da, gu, *, name, tr=512):
    s, f2 = gu.shape
    f = f2 // 2
    tr = _row_tile(s, tr)

    def body(da_ref, gu_ref, dgu_ref):
        gate = gu_ref[:, :f].astype(F32)
        up = gu_ref[:, f:].astype(F32)
        daf = da_ref[...].astype(F32)
        sig = jax.nn.sigmoid(gate)
        silu = gate * sig
        dgu_ref[:, :f] = (daf * up * (sig + silu * (1.0 - sig))).astype(dgu_ref.dtype)
        dgu_ref[:, f:] = (daf * silu).astype(dgu_ref.dtype)

    return pl.pallas_call(
        body, name=name, grid=(s // tr,),
        in_specs=[pl.BlockSpec((tr, f), lambda i: (i, 0)), pl.BlockSpec((tr, f2), lambda i: (i, 0))],
        out_specs=pl.BlockSpec((tr, f2), lambda i: (i, 0)),
        out_shape=jax.ShapeDtypeStruct((s, f2), BF16),
        compiler_params=_params("parallel"),
    )(da, gu)


_NT = (((1,), (1,)), ((), ()))
_NN = (((1,), (0,)), ((), ()))
_TN = (((0,), (0,)), ((), ()))
_QK_SCALE = HEAD_DIM ** -0.5


def _dot(a, b, dims):
    return lax.dot_general(a, b, dims, preferred_element_type=F32)


def _split_heads(v2, is_a):
    zero = jnp.zeros_like(v2)
    return jnp.where(is_a, v2, zero), jnp.where(is_a, zero, v2)


def _fattn_fwd(z, ccol, crow, *, qcol, kcol, vcol, t, name):
    s = z.shape[0]
    npairs, nq = crow.shape[0], crow.shape[1]
    assert nq * t == s

    def body(q_ref, k_ref, v_ref, cc_ref, cr_ref, o_ref, lse_ref, m_scr, l_scr, acc_scr):
        i = pl.program_id(1)
        is_a = lax.broadcasted_iota(jnp.int32, (t, LANES), 1) < HEAD_DIM
        qh = _split_heads(q_ref[...] * jnp.asarray(_QK_SCALE, BF16), is_a)
        cq = (cc_ref[:, 0:1], cc_ref[:, 1:2])
        m_scr[...] = jnp.full(m_scr.shape, NEG_BIG, F32)
        l_scr[...] = jnp.zeros(l_scr.shape, F32)
        acc_scr[...] = jnp.zeros(acc_scr.shape, F32)

        def step(j, diagonal):
            off = pl.multiple_of(j * t, t)
            k2 = k_ref[pl.ds(off, t), :]
            vh = _split_heads(v_ref[pl.ds(off, t), :], is_a)
            crj = cr_ref[j]
            pv, alphas = None, []
            for h in range(2):
                sc = _dot(qh[h], k2, _NT) + (cq[h] - crj[h:h + 1, :])
                if diagonal:
                    row = lax.broadcasted_iota(jnp.int32, (t, t), 0)
                    col = lax.broadcasted_iota(jnp.int32, (t, t), 1)
                    sc = jnp.where(row >= col, sc, NEG_BIG)
                m_old = m_scr[h]
                m_new = jnp.maximum(m_old, jnp.max(sc, axis=1, keepdims=True))
                alpha = jnp.exp(m_old - m_new)
                p = jnp.exp(sc - m_new)
                l_scr[h] = alpha * l_scr[h] + jnp.sum(p, axis=1, keepdims=True)
                m_scr[h] = m_new
                alphas.append(alpha)
                d = _dot(p.astype(BF16), vh[h], _NN)
                pv = d if pv is None else pv + d
            acc_scr[...] = acc_scr[...] * jnp.where(is_a, alphas[0], alphas[1]) + pv

        def loop_body(j, carry):
            step(j, False)
            return carry

        lax.fori_loop(0, i, loop_body, 0)
        step(i, True)
        o_ref[...] = acc_scr[...] * jnp.where(is_a, 1.0 / l_scr[0], 1.0 / l_scr[1])
        lse_ref[:, 0:1] = m_scr[0] + jnp.log(l_scr[0])
        lse_ref[:, 1:2] = m_scr[1] + jnp.log(l_scr[1])

    col_vec = pl.BlockSpec((None, t, 2), lambda p, i: (p, i, 0))
    return pl.pallas_call(
        body, name=name, grid=(npairs, nq),
        in_specs=[
            pl.BlockSpec((t, LANES), lambda p, i: (i, qcol + p)),
            pl.BlockSpec((s, LANES), lambda p, i: (0, kcol + p)),
            pl.BlockSpec((s, LANES), lambda p, i: (0, vcol + p)),
            col_vec,
            pl.BlockSpec((None, nq, 2, t), lambda p, i: (p, 0, 0, 0)),
        ],
        out_specs=[pl.BlockSpec((t, LANES), lambda p, i: (i, p)), col_vec],
        out_shape=[jax.ShapeDtypeStruct((s, npairs * LANES), F32), jax.ShapeDtypeStruct((npairs, s, 2), F32)],
        scratch_shapes=[pltpu.VMEM((2, t, 1), F32), pltpu.VMEM((2, t, 1), F32), pltpu.VMEM((t, LANES), F32)],
        compiler_params=_params("parallel", "arbitrary"),
    )(z, z, z, ccol, crow)


def _fattn_dq(z, dy, y, lse, ccol, crow, *, qcol, kcol, vcol, t, name):
    s = z.shape[0]
    npairs, nq = crow.shape[0], crow.shape[1]

    def body(q_ref, k_ref, v_ref, dy_ref, y_ref, lse_ref, cc_ref, cr_ref, dq_ref, delta_ref, dcq_ref, acc_scr, rs_scr):
        i = pl.program_id(1)
        is_a = lax.broadcasted_iota(jnp.int32, (t, LANES), 1) < HEAD_DIM
        qh = _split_heads(q_ref[...] * jnp.asarray(_QK_SCALE, BF16), is_a)
        cq = (cc_ref[:, 0:1], cc_ref[:, 1:2])
        lse = (lse_ref[:, 0:1], lse_ref[:, 1:2])
        dyf = dy_ref[...]
        prod = _split_heads(dyf * y_ref[...], is_a)
        delta = (jnp.sum(prod[0], axis=1, keepdims=True), jnp.sum(prod[1], axis=1, keepdims=True))
        delta_ref[:, 0:1] = delta[0]
        delta_ref[:, 1:2] = delta[1]
        dyh = _split_heads(dyf.astype(BF16), is_a)
        acc_scr[...] = jnp.zeros(acc_scr.shape, F32)
        rs_scr[...] = jnp.zeros(rs_scr.shape, F32)

        def step(j, diagonal):
            off = pl.multiple_of(j * t, t)
            k2 = k_ref[pl.ds(off, t), :]
            v2 = v_ref[pl.ds(off, t), :]
            kh = _split_heads(k2, is_a)
            crj = cr_ref[j]
            total = None
            for h in range(2):
                sc = _dot(qh[h], k2, _NT) + (cq[h] - crj[h:h + 1, :])
                if diagonal:
                    row = lax.broadcasted_iota(jnp.int32, (t, t), 0)
                    col = lax.broadcasted_iota(jnp.int32, (t, t), 1)
                    sc = jnp.where(row >= col, sc, NEG_BIG)
                p = jnp.exp(sc - lse[h])
                dp = _dot(dyh[h], v2, _NT)
                ds = p * (dp - delta[h])
                rs_scr[h] += jnp.sum(ds, axis=1, keepdims=True)
                d = _dot(ds.astype(BF16), kh[h], _NN)
                total = d if total is None else total + d
            acc_scr[...] += total

        def loop_body(j, carry):
            step(j, False)
            return carry

        lax.fori_loop(0, i, loop_body, 0)
        step(i, True)
        dq_ref[...] = (acc_scr[...] * _QK_SCALE).astype(dq_ref.dtype)
        dcq_ref[:, 0:1] = rs_scr[0]
        dcq_ref[:, 1:2] = rs_scr[1]

    col_vec = pl.BlockSpec((None, t, 2), lambda p, i: (p, i, 0))
    tile = pl.BlockSpec((t, LANES), lambda p, i: (i, p))
    vec_shape = jax.ShapeDtypeStruct((npairs, s, 2), F32)
    return pl.pallas_call(
        body, name=name, grid=(npairs, nq),
        in_specs=[
            pl.BlockSpec((t, LANES), lambda p, i: (i, qcol + p)),
            pl.BlockSpec((s, LANES), lambda p, i: (0, kcol + p)),
            pl.BlockSpec((s, LANES), lambda p, i: (0, vcol + p)),
            tile, tile, col_vec, col_vec,
            pl.BlockSpec((None, nq, 2, t), lambda p, i: (p, 0, 0, 0)),
        ],
        out_specs=[tile, col_vec, col_vec],
        out_shape=[jax.ShapeDtypeStruct((s, npairs * LANES), BF16), vec_shape, vec_shape],
        scratch_shapes=[pltpu.VMEM((t, LANES), F32), pltpu.VMEM((2, t, 1), F32)],
        compiler_params=_params("parallel", "arbitrary"),
    )(z, z, z, dy, y, lse, ccol, crow)


def _fattn_dkv(z, dyb, lse_row, delta_row, ccol, crow, *, qcol, kcol, vcol, t, name):
    s = z.shape[0]
    npairs, nq = crow.shape[0], crow.shape[1]

    def body(k_ref, v_ref, q_ref, dy_ref, lse_ref, dl_ref, cc_ref, cr_ref, dk_ref, dv_ref, dc_ref,
             dk_scr, dv_scr, dc_scr):
        j = pl.program_id(1)
        is_a = lax.broadcasted_iota(jnp.int32, (t, LANES), 1) < HEAD_DIM
        kh = _split_heads(k_ref[...], is_a)
        vh = _split_heads(v_ref[...], is_a)
        ck = (cc_ref[:, 0:1], cc_ref[:, 1:2])
        dk_scr[...] = jnp.zeros(dk_scr.shape, F32)
        dv_scr[...] = jnp.zeros(dv_scr.shape, F32)
        dc_scr[...] = jnp.zeros(dc_scr.shape, F32)

        def step(i, diagonal):
            off = pl.multiple_of(i * t, t)
            q2 = q_ref[pl.ds(off, t), :] * jnp.asarray(_QK_SCALE, BF16)
            dy2 = dy_ref[pl.ds(off, t), :]
            qh = _split_heads(q2, is_a)
            dyh = _split_heads(dy2, is_a)
            cri, lsei, dli = cr_ref[i], lse_ref[i], dl_ref[i]
            dk_tot, dv_tot = None, None
            for h in range(2):
                st = _dot(kh[h], q2, _NT) + (cri[h:h + 1, :] - ck[h])
                if diagonal:
                    row = lax.broadcasted_iota(jnp.int32, (t, t), 0)
                    col = lax.broadcasted_iota(jnp.int32, (t, t), 1)
                    st = jnp.where(col >= row, st, NEG_BIG)
                pt = jnp.exp(st - lsei[h:h + 1, :])
                dv_h = _dot(pt.astype(BF16), dyh[h], _NN)
                dpt = _dot(vh[h], dy2, _NT)
                dst = pt * (dpt - dli[h:h + 1, :])
                dk_h = _dot(dst.astype(BF16), qh[h], _NN)
                dc_scr[h] += jnp.sum(dst, axis=1, keepdims=True)
                dk_tot = dk_h if dk_tot is None else dk_tot + dk_h
                dv_tot = dv_h if dv_tot is None else dv_tot + dv_h
            dk_scr[...] += dk_tot
            dv_scr[...] += dv_tot

        def loop_body(i, carry):
            step(i, False)
            return carry

        step(j, True)
        lax.fori_loop(j + 1, nq, loop_body, 0)
        dk_ref[...] = dk_scr[...].astype(dk_ref.dtype)
        dv_ref[...] = dv_scr[...].astype(dv_ref.dtype)
        dc_ref[:, 0:1] = -dc_scr[0]
        dc_ref[:, 1:2] = -dc_scr[1]

    col_vec = pl.BlockSpec((None, t, 2), lambda p, j: (p, j, 0))
    rows = pl.BlockSpec((None, nq, 2, t), lambda p, j: (p, 0, 0, 0))
    tile = pl.BlockSpec((t, LANES), lambda p, j: (j, p))
    return pl.pallas_call(
        body, name=name, grid=(npairs, nq),
        in_specs=[
            pl.BlockSpec((t, LANES), lambda p, j: (j, kcol + p)),
            pl.BlockSpec((t, LANES), lambda p, j: (j, vcol + p)),
            pl.BlockSpec((s, LANES), lambda p, j: (0, qcol + p)),
            pl.BlockSpec((s, LANES), lambda p, j: (0, p)),
            rows, rows, col_vec, rows,
        ],
        out_specs=[tile, tile, col_vec],
        out_shape=[jax.ShapeDtypeStruct((s, npairs * LANES), BF16), jax.ShapeDtypeStruct((s, npairs * LANES), BF16),
                   jax.ShapeDtypeStruct((npairs, s, 2), F32)],
        scratch_shapes=[pltpu.VMEM((t, LANES), F32), pltpu.VMEM((t, LANES), F32), pltpu.VMEM((2, t, 1), F32)],
        compiler_params=_params("parallel", "arbitrary"),
    )(z, z, z, dyb, lse_row, delta_row, ccol, crow)


def _log_sigmoid(x):
    return jnp.minimum(x, 0.0) - jnp.log(1.0 + jnp.exp(-jnp.abs(x)))


def _cumsum_fwd(zf, b, *, name, t=512):
    s, w = zf.shape
    t = _row_tile(s, t)

    def body(zf_ref, b_ref, c_ref, carry):
        @pl.when(pl.program_id(0) == 0)
        def _():
            carry[...] = jnp.zeros(carry.shape, F32)

        lf = _log_sigmoid(zf_ref[...] + b_ref[...])
        row = lax.broadcasted_iota(jnp.int32, (t, t), 0)
        col = lax.broadcasted_iota(jnp.int32, (t, t), 1)
        tri = (row >= col).astype(F32)
        c = lax.dot_general(tri, lf, _NN, precision=lax.Precision.HIGHEST, preferred_element_type=F32) + carry[...]
        c_ref[...] = c
        carry[...] = c[t - 1:t, :]

    return pl.pallas_call(
        body, name=name, grid=(s // t,),
        in_specs=[pl.BlockSpec((t, w), lambda i: (i, 0)), pl.BlockSpec((1, w), lambda i: (0, 0))],
        out_specs=pl.BlockSpec((t, w), lambda i: (i, 0)),
        out_shape=jax.ShapeDtypeStruct((s, w), F32),
        scratch_shapes=[pltpu.VMEM((1, w), F32)],
        compiler_params=_params("arbitrary"),
    )(zf, b)


def _cumsum_bwd(dcq, dck, zf, b, *, name, t=512):
    s, w = zf.shape
    t = _row_tile(s, t)
    nb = s // t

    def body(dcq_ref, dck_ref, zf_ref, b_ref, dzf_ref, db_ref, carry):
        @pl.when(pl.program_id(0) == 0)
        def _():
            carry[...] = jnp.zeros(carry.shape, F32)
            db_ref[...] = jnp.zeros(db_ref.shape, F32)

        row = lax.broadcasted_iota(jnp.int32, (t, t), 0)
        col = lax.broadcasted_iota(jnp.int32, (t, t), 1)
        tri = (row <= col).astype(F32)
        dc = dcq_ref[...] + dck_ref[...]
        dlf = lax.dot_general(tri, dc, _NN, precision=lax.Precision.HIGHEST, preferred_element_type=F32) + carry[...]
        carry[...] = dlf[0:1, :]
        dzf = dlf * jax.nn.sigmoid(-(zf_ref[...] + b_ref[...]))
        dzf_ref[...] = dzf
        db_ref[...] += jnp.sum(dzf, axis=0, keepdims=True)

    blk = pl.BlockSpec((t, w), lambda i: (nb - 1 - i, 0))
    vec = pl.BlockSpec((1, w), lambda i: (0, 0))
    return pl.pallas_call(
        body, name=name, grid=(nb,), in_specs=[blk, blk, blk, vec], out_specs=[blk, vec],
        out_shape=[jax.ShapeDtypeStruct((s, w), F32), jax.ShapeDtypeStruct((1, w), F32)],
        scratch_shapes=[pltpu.VMEM((1, w), F32)],
        compiler_params=_params("arbitrary"),
    )(dcq, dck, zf, b)


HALO = 8


def _rms_rows(v):
    return lax.rsqrt(jnp.mean(v * v, axis=-1, keepdims=True) + EPS)


def _mixpost_fwd(z, yatt, wconv, gc, ga, *, name, tr=512):
    s = z.shape[0]
    cw, aw = gc.shape[-1], ga.shape[-1]
    tr = _row_tile(s, tr)

    def body(zb_ref, zc_ref, zv_ref, ya_ref, w_ref, gc_ref, ga_ref, ycat_ref, cv_ref, u_scr):
        @pl.when(pl.program_id(0) == 0)
        def _():
            u_scr[0:HALO, :] = jnp.zeros((HALO, cw), F32)

        u = zc_ref[...].astype(F32) * zv_ref[...].astype(F32)
        u_scr[HALO:HALO + tr, :] = u
        cv = w_ref[0:1, :] * u_scr[HALO - 2:HALO - 2 + tr, :] + w_ref[1:2, :] * u_scr[HALO - 1:HALO - 1 + tr, :] + w_ref[2:3, :] * u
        u_scr[0:HALO, :] = u_scr[tr:tr + HALO, :]
        cv_ref[...] = cv
        yc = zb_ref[...].astype(F32) * cv
        ya = ya_ref[...]
        ycat_ref[:, :cw] = (yc * _rms_rows(yc) * gc_ref[...]).astype(ycat_ref.dtype)
        ycat_ref[:, cw:] = (ya * _rms_rows(ya) * ga_ref[...]).astype(ycat_ref.dtype)

    return pl.pallas_call(
        body, name=name, grid=(s // tr,),
        in_specs=[
            pl.BlockSpec((tr, cw), lambda i: (i, 0)), pl.BlockSpec((tr, cw), lambda i: (i, 1)),
            pl.BlockSpec((tr, cw), lambda i: (i, 2)), pl.BlockSpec((tr, aw), lambda i: (i, 0)),
            pl.BlockSpec((HALO, cw), lambda i: (0, 0)), pl.BlockSpec((1, cw), lambda i: (0, 0)),
            pl.BlockSpec((1, aw), lambda i: (0, 0)),
        ],
        out_specs=[pl.BlockSpec((tr, cw + aw), lambda i: (i, 0)), pl.BlockSpec((tr, cw), lambda i: (i, 0))],
        out_shape=[jax.ShapeDtypeStruct((s, cw + aw), BF16), jax.ShapeDtypeStruct((s, cw), F32)],
        scratch_shapes=[pltpu.VMEM((tr + HALO, cw), F32)],
        compiler_params=_params("arbitrary"),
    )(z, z, z, yatt, wconv, gc.reshape(1, cw), ga.reshape(1, aw))


def _mixpost_bwd(dycat, z, yatt, cv, wconv, gc, ga, *, name, tr=512):
    s = z.shape[0]
    cw, aw = gc.shape[-1], ga.shape[-1]
    tr = _row_tile(s, tr)
    nb = s // tr

    def body(dy_ref, zb_ref, zc_ref, zv_ref, ya_ref, cv_ref, w_ref, gc_ref, ga_ref,
             dz_ref, dya_ref, dw_ref, dgc_ref, dga_ref, d_scr):
        @pl.when(pl.program_id(0) == 0)
        def _():
            d_scr[tr:tr + HALO, :] = jnp.zeros((HALO, cw), F32)
            dw_ref[...] = jnp.zeros(dw_ref.shape, F32)
            dgc_ref[...] = jnp.zeros(dgc_ref.shape, F32)
            dga_ref[...] = jnp.zeros(dga_ref.shape, F32)

        zb, zc, zv = zb_ref[...].astype(F32), zc_ref[...].astype(F32), zv_ref[...].astype(F32)
        cvv = cv_ref[...]

        def norm_bwd(v, dn, g):
            r = _rms_rows(v)
            vh = v * r
            dvh = dn * g
            return r * (dvh - vh * jnp.mean(dvh * vh, axis=-1, keepdims=True)), jnp.sum(dn * vh, axis=0, keepdims=True)

        dyc, dgc = norm_bwd(zb * cvv, dy_ref[:, :cw], gc_ref[...])
        dya, dga = norm_bwd(ya_ref[...], dy_ref[:, cw:], ga_ref[...])
        dgc_ref[...] += dgc
        dga_ref[...] += dga
        dya_ref[...] = dya
        dcv = dyc * zb
        d_scr[0:tr, :] = dcv
        d1 = d_scr[1:tr + 1, :]
        d2 = d_scr[2:tr + 2, :]
        du = w_ref[2:3, :] * dcv + w_ref[1:2, :] * d1 + w_ref[0:1, :] * d2
        u = zc * zv
        dw_ref[0:1, :] += jnp.sum(u * d2, axis=0, keepdims=True)
        dw_ref[1:2, :] += jnp.sum(u * d1, axis=0, keepdims=True)
        dw_ref[2:3, :] += jnp.sum(u * dcv, axis=0, keepdims=True)
        d_scr[tr:tr + HALO, :] = d_scr[0:HALO, :]
        dz_ref[:, :cw] = (dyc * cvv).astype(dz_ref.dtype)
        dz_ref[:, cw:2 * cw] = (du * zv).astype(dz_ref.dtype)
        dz_ref[:, 2 * cw:] = (du * zc).astype(dz_ref.dtype)

    def rows(width, colblk=0):
        return pl.BlockSpec((tr, width), lambda i: (nb - 1 - i, colblk))

    def fixed(r, width):
        return pl.BlockSpec((r, width), lambda i: (0, 0))

    return pl.pallas_call(
        body, name=name, grid=(nb,),
        in_specs=[rows(cw + aw), rows(cw, 0), rows(cw, 1), rows(cw, 2), rows(aw), rows(cw),
                  fixed(HALO, cw), fixed(1, cw), fixed(1, aw)],
        out_specs=[rows(3 * cw), rows(aw), fixed(HALO, cw), fixed(1, cw), fixed(1, aw)],
        out_shape=[jax.ShapeDtypeStruct((s, 3 * cw), BF16), jax.ShapeDtypeStruct((s, aw), F32),
                   jax.ShapeDtypeStruct((HALO, cw), F32), jax.ShapeDtypeStruct((1, cw), F32),
                   jax.ShapeDtypeStruct((1, aw), F32)],
        scratch_shapes=[pltpu.VMEM((tr + HALO, cw), F32)],
        compiler_params=_params("arbitrary"),
    )(dycat, z, z, z, yatt, cv, wconv, gc.reshape(1, cw), ga.reshape(1, aw))


def _xattn_fwd(q, kv, *, name, tq=1024):
    s, d = q.shape
    m = kv.shape[0]
    dh = d // N_XHEADS
    scale = dh ** -0.5
    tq = _row_tile(s, tq)

    def body(q_ref, kv_ref, o_ref):
        for h in range(N_XHEADS):
            lo, hi = h * dh, (h + 1) * dh
            sc = _dot(q_ref[:, lo:hi], kv_ref[:, lo:hi], _NT) * scale
            p = jnp.exp(sc - jnp.max(sc, axis=1, keepdims=True))
            o = _dot(p.astype(BF16), kv_ref[:, d + lo:d + hi], _NN) / jnp.sum(p, axis=1, keepdims=True)
            o_ref[:, lo:hi] = o.astype(o_ref.dtype)

    return pl.pallas_call(
        body, name=name, grid=(s // tq,),
        in_specs=[pl.BlockSpec((tq, d), lambda i: (i, 0)), pl.BlockSpec((m, 2 * d), lambda i: (0, 0))],
        out_specs=pl.BlockSpec((tq, d), lambda i: (i, 0)),
        out_shape=jax.ShapeDtypeStruct((s, d), BF16),
        compiler_params=_params("parallel"),
    )(q, kv)


def _xattn_bwd(q, kv, do, *, name, tq=1024):
    s, d = q.shape
    m = kv.shape[0]
    dh = d // N_XHEADS
    scale = dh ** -0.5
    tq = _row_tile(s, tq)

    def body(q_ref, kv_ref, do_ref, dq_ref, dkv_ref):
        @pl.when(pl.program_id(0) == 0)
        def _():
            dkv_ref[...] = jnp.zeros(dkv_ref.shape, F32)

        for h in range(N_XHEADS):
            lo, hi = h * dh, (h + 1) * dh
            qh, kh, vh, doh = q_ref[:, lo:hi], kv_ref[:, lo:hi], kv_ref[:, d + lo:d + hi], do_ref[:, lo:hi]
            sc = _dot(qh, kh, _NT) * scale
            e = jnp.exp(sc - jnp.max(sc, axis=1, keepdims=True))
            p = e / jnp.sum(e, axis=1, keepdims=True)
            dp = _dot(doh, vh, _NT)
            ds = p * (dp - jnp.sum(dp * p, axis=1, keepdims=True))
            dsb = ds.astype(BF16)
            dq_ref[:, lo:hi] = (_dot(dsb, kh, _NN) * scale).astype(dq_ref.dtype)
            dkv_ref[:, lo:hi] += _dot(dsb, qh, _TN) * scale
            dkv_ref[:, d + lo:d + hi] += _dot(p.astype(BF16), doh, _TN)

    return pl.pallas_call(
        body, name=name, grid=(s // tq,),
        in_specs=[pl.BlockSpec((tq, d), lambda i: (i, 0)), pl.BlockSpec((m, 2 * d), lambda i: (0, 0)),
                  pl.BlockSpec((tq, d), lambda i: (i, 0))],
        out_specs=[pl.BlockSpec((tq, d), lambda i: (i, 0)), pl.BlockSpec((m, 2 * d), lambda i: (0, 0))],
        out_shape=[jax.ShapeDtypeStruct((s, d), BF16), jax.ShapeDtypeStruct((m, 2 * d), F32)],
        compiler_params=_params("arbitrary"),
    )(q, kv, do)


def _final_loss(x, g, target, *, name, tr=512):
    s, d = x.shape
    tr = _row_tile(s, tr)

    def body(x_ref, g_ref, t_ref, dx_ref, dg_ref, sq_ref):
        @pl.when(pl.program_id(0) == 0)
        def _():
            dg_ref[...] = jnp.zeros(dg_ref.shape, F32)
            sq_ref[...] = jnp.zeros(sq_ref.shape, F32)

        xf = x_ref[...]
        r = _rms_rows(xf)
        xhat = xf * r
        err = xhat * g_ref[...] - t_ref[...]
        sq_ref[...] += jnp.sum(err * err, axis=0, keepdims=True)
        dy = err * (1.0 / d)
        dg_ref[...] += jnp.sum(dy * xhat, axis=0, keepdims=True)
        dxhat = dy * g_ref[...]
        dx_ref[...] = r * (dxhat - xhat * jnp.mean(dxhat * xhat, axis=-1, keepdims=True))

    row = pl.BlockSpec((tr, d), lambda i: (i, 0))
    vec = pl.BlockSpec((1, d), lambda i: (0, 0))
    return pl.pallas_call(
        body, name=name, grid=(s // tr,), in_specs=[row, vec, row], out_specs=[row, vec, vec],
        out_shape=[jax.ShapeDtypeStruct((s, d), F32), jax.ShapeDtypeStruct((1, d), F32), jax.ShapeDtypeStruct((1, d), F32)],
        compiler_params=_params("arbitrary"),
    )(x, g.reshape(1, d), target)


def _adamw(w, g, m, v, *, name, tr=512):
    shape = w.shape
    cols = shape[-1]
    rows = w.size // cols
    tr = tr if rows % tr == 0 else rows

    def body(w_ref, g_ref, m_ref, v_ref, d_ref, nm_ref, nv_ref):
        gf = g_ref[...]
        nm = ADAM_B1 * m_ref[...] + (1.0 - ADAM_B1) * gf
        nv = ADAM_B2 * v_ref[...] + (1.0 - ADAM_B2) * (gf * gf)
        m_hat = nm / (1.0 - ADAM_B1 ** ADAM_STEP)
        v_hat = nv / (1.0 - ADAM_B2 ** ADAM_STEP)
        d_ref[...] = -ADAM_LR * (m_hat / (jnp.sqrt(v_hat) + ADAM_EPS) + ADAM_WD * w_ref[...])
        nm_ref[...] = nm
        nv_ref[...] = nv

    blk = pl.BlockSpec((tr, cols), lambda i: (i, 0))
    out = jax.ShapeDtypeStruct((rows, cols), F32)
    outs = pl.pallas_call(
        body, name=name, grid=(rows // tr,), in_specs=[blk] * 4, out_specs=[blk] * 3, out_shape=[out] * 3,
        compiler_params=_params("parallel"),
    )(*[t.reshape(rows, cols) for t in (w, g, m, v)])
    return tuple(o.reshape(shape) for o in outs)


_HBM = pl.BlockSpec(memory_space=pl.ANY)
_MESH_ID = pl.DeviceIdType.MESH


def _place():
    x, y, c = (lax.axis_index(a) for a in MESH_AXES)
    return x, y, c, [(1 - x, y), (x, 1 - y), (1 - x, 1 - y)]


def _remote(src, dst, send_sems, recv_sems, k, to):
    return pltpu.make_async_remote_copy(src_ref=src, dst_ref=dst, send_sem=send_sems.at[k], recv_sem=recv_sems.at[k],
                                        device_id=to, device_id_type=_MESH_ID)


def _comm_call(body, out_shape, n_remote, name, *args):
    return pl.pallas_call(
        body, name=name, in_specs=[_HBM] * len(args), out_specs=_HBM, out_shape=out_shape,
        scratch_shapes=[pltpu.SemaphoreType.DMA((n_remote,)), pltpu.SemaphoreType.DMA((n_remote,)), pltpu.SemaphoreType.DMA],
    )(*args)


def _allgather_chips(w, *, name):
    def body(w_ref, out_ref, send_sems, recv_sems, local_sem):
        x, y, c, chips = _place()
        me = 2 * x + y
        mine = pltpu.make_async_copy(w_ref, out_ref.at[me], local_sem)
        mine.start()
        sends = [_remote(w_ref, out_ref.at[me], send_sems, recv_sems, k, (px, py, c)) for k, (px, py) in enumerate(chips)]
        for cp in sends:
            cp.start()
        for k, (px, py) in enumerate(chips):
            _remote(w_ref, out_ref.at[2 * px + py], send_sems, recv_sems, k, (px, py, c)).wait_recv()
        for cp in sends:
            cp.wait_send()
        mine.wait()

    return _comm_call(body, jax.ShapeDtypeStruct((N_CHIPS,) + w.shape, w.dtype), 3, name, w)


def _swap_halves(g, *, name):
    def body(g_ref, out_ref, send_sems, recv_sems, local_sem):
        x, y, c, _ = _place()
        cp = _remote(g_ref.at[1 - c], out_ref, send_sems, recv_sems, 0, (x, y, 1 - c))
        cp.start()
        cp.wait()

    return _comm_call(body, jax.ShapeDtypeStruct(g.shape[1:], g.dtype), 1, name, g)


def _scatter_chips(p, *, name):
    def body(p_ref, out_ref, send_sems, recv_sems, local_sem):
        x, y, c, chips = _place()
        me = 2 * x + y
        mine = pltpu.make_async_copy(p_ref.at[me], out_ref.at[me], local_sem)
        mine.start()
        sends = [_remote(p_ref.at[2 * px + py], out_ref.at[me], send_sems, recv_sems, k, (px, py, c))
                 for k, (px, py) in enumerate(chips)]
        for cp in sends:
            cp.start()
        for k, (px, py) in enumerate(chips):
            _remote(p_ref.at[me], out_ref.at[2 * px + py], send_sems, recv_sems, k, (px, py, c)).wait_recv()
        for cp in sends:
            cp.wait_send()
        mine.wait()

    return _comm_call(body, jax.ShapeDtypeStruct(p.shape, p.dtype), 3, name, p)


def _join_halves(r, *, name):
    def body(r_ref, out_ref, send_sems, recv_sems, local_sem):
        x, y, c, _ = _place()
        mine = pltpu.make_async_copy(r_ref, out_ref.at[c], local_sem)
        mine.start()
        cp = _remote(r_ref, out_ref.at[c], send_sems, recv_sems, 0, (x, y, 1 - c))
        cp.start()
        _remote(r_ref, out_ref.at[1 - c], send_sems, recv_sems, 0, (x, y, 1 - c)).wait_recv()
        cp.wait_send()
        mine.wait()

    return _comm_call(body, jax.ShapeDtypeStruct((2,) + r.shape, r.dtype), 1, name, r)


def _add_halves(g, recv, c, *, name, tr=512):
    _, nblk, rows, cols = g.shape
    tr = _row_tile(rows, tr)

    def body(c_ref, g_ref, r_ref, o_ref):
        o_ref[...] = g_ref[...] + r_ref[...]

    blk = pl.BlockSpec((None, tr, cols), lambda b, i, c_ref: (b, i, 0))
    return pl.pallas_call(
        body, name=name,
        grid_spec=pltpu.PrefetchScalarGridSpec(
            num_scalar_prefetch=1, grid=(nblk, rows // tr),
            in_specs=[pl.BlockSpec((None, None, tr, cols), lambda b, i, c_ref: (c_ref[0], b, i, 0)), blk],
            out_specs=blk),
        out_shape=jax.ShapeDtypeStruct(recv.shape, F32),
        compiler_params=_params("parallel", "parallel"),
    )(c.reshape(1).astype(jnp.int32), g, recv)


def _sum_slots(p, *, name, tr=512):
    nblk, rows, cols = p.shape
    tr = _row_tile(rows, tr)

    def body(p_ref, o_ref):
        acc = p_ref[0]
        for q in range(1, nblk):
            acc = acc + p_ref[q]
        o_ref[...] = acc

    return pl.pallas_call(
        body, name=name, grid=(rows // tr,),
        in_specs=[pl.BlockSpec((nblk, tr, cols), lambda i: (0, i, 0))],
        out_specs=pl.BlockSpec((tr, cols), lambda i: (i, 0)),
        out_shape=jax.ShapeDtypeStruct((rows, cols), F32),
        compiler_params=_params("parallel"),
    )(p)


SHARDED = (("w_ffn1_gu", "col"), ("w_ffn1_down", "row"), ("w_mix_in", "col"), ("w_conv", "col"), ("w_mix_out", "row"),
           ("w_xq", "row"), ("w_xkv", "col"), ("w_xo", "row"), ("w_ffn2_gu", "col"), ("w_ffn2_down", "row"))
REPLICATED = ("g_ffn1", "g_mix", "b_f", "g_conv_out", "g_att_out", "g_xattn", "g_mem", "g_ffn2", "g_final")
WEIGHTS = ("g_ffn1", "w_ffn1_gu", "w_ffn1_down", "g_mix", "w_mix_in", "w_conv", "b_f", "g_conv_out", "g_att_out",
           "w_mix_out", "g_xattn", "g_mem", "w_xq", "w_xkv", "w_xo", "g_ffn2", "w_ffn2_gu", "w_ffn2_down", "g_final")
FLAT_COLS = 512
FLAT_ROW_TILE = 512


def _full_from_blocks(blocks, kind):
    n, l, a, b = blocks.shape
    if kind == "row":
        return blocks.transpose(1, 0, 2, 3).reshape(l, n * a, b)
    return blocks.transpose(1, 2, 0, 3).reshape(l, a, n * b)


def _blocks_from_full(full, kind):
    l, a, b = full.shape
    if kind == "row":
        return full.reshape(l, N_CHIPS, a // N_CHIPS, b).transpose(1, 0, 2, 3)
    return full.reshape(l, a, N_CHIPS, b // N_CHIPS).transpose(2, 0, 1, 3)


def _gather_weights(shards):
    pieces, sizes = [], []
    for name, _ in SHARDED:
        w = shards[name]
        flat = lax.bitcast_convert_type(w, BF16).reshape(-1) if name == "w_conv" else w.astype(BF16).reshape(-1)
        pieces.append(flat)
        sizes.append(flat.shape[0])
    total = sum(sizes)
    rows = -(-total // FLAT_COLS)
    flat = jnp.concatenate(pieces + [jnp.zeros((rows * FLAT_COLS - total,), BF16)])
    gathered = _allgather_chips(flat.reshape(rows, FLAT_COLS), name="allgather_weights").reshape(N_CHIPS, rows * FLAT_COLS)
    full, off = {}, 0
    for (name, kind), n in zip(SHARDED, sizes):
        blk = gathered[:, off:off + n]
        off += n
        shape = shards[name].shape
        if name == "w_conv":
            blk = lax.bitcast_convert_type(blk.reshape((N_CHIPS,) + shape + (2,)), F32)
        else:
            blk = blk.reshape((N_CHIPS,) + shape)
        full[name] = _full_from_blocks(blk, kind)
    return full


def _reduce_gradients(grads, c):
    rep = jnp.concatenate([grads[n].reshape(-1) for n in REPLICATED])
    blocks = [_blocks_from_full(grads[name], kind).reshape(N_CHIPS, -1) for name, kind in SHARDED]
    sizes = [b.shape[1] for b in blocks]
    total = sum(sizes) + rep.shape[0]
    unit = 2 * FLAT_COLS * FLAT_ROW_TILE
    rows = -(-total // unit) * FLAT_ROW_TILE
    pad = 2 * rows * FLAT_COLS - total
    flat = jnp.concatenate(blocks + [jnp.broadcast_to(rep, (N_CHIPS, rep.shape[0])), jnp.zeros((N_CHIPS, pad), F32)], axis=1)
    g = flat.reshape(N_CHIPS, 2, rows, FLAT_COLS).transpose(1, 0, 2, 3)
    from_sibling = _swap_halves(g, name="grad_swap_halves")
    chip_sum = _add_halves(g, from_sibling, c, name="grad_add_halves")
    from_chips = _scatter_chips(chip_sum, name="grad_scatter_chips")
    half = _sum_slots(from_chips, name="grad_sum_chips")
    reduced = _join_halves(half, name="grad_join_halves").reshape(-1)
    out, off = {}, 0
    for (name, kind), n in zip(SHARDED, sizes):
        full_shape = grads[name].shape
        shard_shape = (full_shape[0], full_shape[1] // N_CHIPS, full_shape[2]) if kind == "row" else \
            (full_shape[0], full_shape[1], full_shape[2] // N_CHIPS)
        out[name] = reduced[off:off + n].reshape(shard_shape)
        off += n
    for name in REPLICATED:
        n = grads[name].size
        out[name] = reduced[off:off + n].reshape(grads[name].shape)
        off += n
    return out


def _ffn_fwd(x, g, w_gu, w_down, tag):
    h = _rmsnorm_fwd(x, g, name=f"{tag}_norm")
    gu = _matmul(h, w_gu, "nn", BF16, tm=2048, tn=512, tk=w_gu.shape[0], name=f"{tag}_gu")
    a = _swiglu_fwd(gu, name=f"{tag}_act")
    y = _matmul(a, w_down, "nn", F32, tm=1024, tn=1024, tk=w_down.shape[0], alpha=0.5, residual=x, name=f"{tag}_down")
    return y, (x, h, gu, a)


def _ffn_bwd(dy, saved, g, w_gu, w_down, tag):
    x, h, gu, a = saved
    f = w_down.shape[0]
    da = _matmul(dy, w_down, "nt", BF16, tm=1024, tn=f // 2, tk=w_down.shape[1], alpha=0.5, name=f"{tag}_da")
    dw_down = _matmul(a, dy, "tn", F32, tm=f // 2, tn=1024, tk=1024, alpha=0.5, name=f"{tag}_dwdown")
    dgu = _swiglu_bwd(da, gu, name=f"{tag}_dact")
    dw_gu = _matmul(h, dgu, "tn", F32, tm=1024, tn=512, tk=2048, name=f"{tag}_dwgu")
    dh = _matmul(dgu, w_gu, "nt", F32, tm=1024, tn=1024, tk=f, name=f"{tag}_dh")
    dx, dg = _rmsnorm_bwd(x, g, dh, dy, name=f"{tag}_dnorm")
    return dx, dg, dw_gu, dw_down


def _tokens_on_lanes(v, t):
    p, s, _ = v.shape
    return v.transpose(0, 2, 1).reshape(p, 2, s // t, t).transpose(0, 2, 1, 3)


def _mix_fwd(x, w, l, tag):
    s, d = x.shape
    gc, ga = w["g_conv_out"][l], w["g_att_out"][l]
    cw, aw = gc.shape[0], ga.shape[0]
    nh = aw // HEAD_DIM
    zw = 3 * cw + 3 * aw
    t = 512 if s >= 2048 else s // 4
    cols = dict(qcol=3 * cw // LANES, kcol=(3 * cw + aw) // LANES, vcol=(3 * cw + 2 * aw) // LANES, t=t)
    h = _rmsnorm_fwd(x, w["g_mix"][l], name=f"{tag}_norm")
    w_in = w["w_mix_in"][l]
    w_main = w_in[:, :zw]
    w_f = jnp.pad(w_in[:, zw:], ((0, 0), (0, LANES - nh)))
    z = _matmul(h, w_main, "nn", BF16, tm=2048, tn=512, tk=d, name=f"{tag}_in")
    zf = _matmul(h, w_f, "nn", F32, tm=2048, tn=LANES, tk=d, name=f"{tag}_in_f")
    b = jnp.pad(w["b_f"][l], (0, LANES - nh)).reshape(1, LANES)
    c = _cumsum_fwd(zf, b, name=f"{tag}_cumsum")
    ccol = c[:, :nh].reshape(s, nh // 2, 2).transpose(1, 0, 2)
    crow = _tokens_on_lanes(ccol, t)
    yatt, lse = _fattn_fwd(z, ccol, crow, name=f"{tag}_attn", **cols)
    wc = jnp.pad(w["w_conv"][l], ((0, HALO - CONV_K), (0, 0)))
    ycat, cv = _mixpost_fwd(z, yatt, wc, gc, ga, name=f"{tag}_post")
    y = _matmul(ycat, w["w_mix_out"][l], "nn", F32, tm=1024, tn=1024, tk=d, residual=x, name=f"{tag}_out")
    return y, (x, h, w_main, w_f, z, zf, b, ccol, crow, yatt, lse, wc, ycat, cv, cols)


def _mix_bwd(dy, saved, w, l, tag):
    x, h, w_main, w_f, z, zf, b, ccol, crow, yatt, lse, wc, ycat, cv, cols = saved
    s, d = x.shape
    gc, ga = w["g_conv_out"][l], w["g_att_out"][l]
    nh = ga.shape[0] // HEAD_DIM
    zw = w_main.shape[1]
    t = cols["t"]
    dycat = _matmul(dy, w["w_mix_out"][l], "nt", F32, tm=1024, tn=1024, tk=d, name=f"{tag}_dycat")
    dw_out = _matmul(ycat, dy, "tn", F32, tm=1024, tn=1024, tk=1024, name=f"{tag}_dwout")
    dz_conv, dyatt, dwc, dgc, dga = _mixpost_bwd(dycat, z, yatt, cv, wc, gc, ga, name=f"{tag}_dpost")
    dq, delta, dcq = _fattn_dq(z, dyatt, yatt, lse, ccol, crow, name=f"{tag}_attn_dq", **cols)
    dk, dv, dck = _fattn_dkv(z, dyatt.astype(BF16), _tokens_on_lanes(lse, t), _tokens_on_lanes(delta, t), ccol, crow,
                             name=f"{tag}_attn_dkv", **cols)
    def heads_on_lanes(v):
        return jnp.pad(v.transpose(1, 0, 2).reshape(s, nh), ((0, 0), (0, LANES - nh)))

    dzf, db = _cumsum_bwd(heads_on_lanes(dcq), heads_on_lanes(dck), zf, b, name=f"{tag}_dcumsum")
    dz = jnp.concatenate([dz_conv, dq, dk, dv], axis=1)
    dw_main = _matmul(h, dz, "tn", F32, tm=1024, tn=512, tk=2048, name=f"{tag}_dwin")
    dw_f = _matmul(h, dzf, "tn", F32, tm=1024, tn=LANES, tk=2048, name=f"{tag}_dwin_f")
    dh = _matmul(dz, w_main, "nt", F32, tm=1024, tn=1024, tk=zw // 2, name=f"{tag}_dh")
    dh = _matmul(dzf, w_f, "nt", F32, tm=1024, tn=1024, tk=LANES, residual=dh, name=f"{tag}_dh_f")
    dx, dg = _rmsnorm_bwd(x, w["g_mix"][l], dh, dy, name=f"{tag}_dnorm")
    grads = dict(g_mix=dg, w_mix_in=jnp.concatenate([dw_main, dw_f[:, :nh]], axis=1), w_conv=dwc[:CONV_K], b_f=db[0, :nh],
                 g_conv_out=dgc[0], g_att_out=dga[0], w_mix_out=dw_out)
    return dx, grads


def _xattn_block_fwd(x, mem, w, l, tag):
    d = x.shape[1]
    h = _rmsnorm_fwd(x, w["g_xattn"][l], name=f"{tag}_norm")
    mn = _rmsnorm_fwd(mem, w["g_mem"][l], name=f"{tag}_mem_norm")
    q = _matmul(h, w["w_xq"][l], "nn", BF16, tm=1024, tn=1024, tk=d, name=f"{tag}_q")
    kv = _matmul(mn, w["w_xkv"][l], "nn", BF16, tm=1024, tn=1024, tk=d, name=f"{tag}_kv")
    o = _xattn_fwd(q, kv, name=f"{tag}_core")
    y = _matmul(o, w["w_xo"][l], "nn", F32, tm=1024, tn=1024, tk=d, residual=x, name=f"{tag}_o")
    return y, (x, h, mn, q, kv, o)


def _xattn_block_bwd(dy, saved, mem, w, l, tag):
    x, h, mn, q, kv, o = saved
    d = x.shape[1]
    do = _matmul(dy, w["w_xo"][l], "nt", BF16, tm=1024, tn=1024, tk=d, name=f"{tag}_do")
    dw_xo = _matmul(o, dy, "tn", F32, tm=1024, tn=1024, tk=1024, name=f"{tag}_dwo")
    dq, dkv = _xattn_bwd(q, kv, do, name=f"{tag}_dcore")
    dw_xq = _matmul(h, dq, "tn", F32, tm=1024, tn=1024, tk=2048, name=f"{tag}_dwq")
    dh = _matmul(dq, w["w_xq"][l], "nt", F32, tm=1024, tn=1024, tk=d, name=f"{tag}_dh")
    dx, dg = _rmsnorm_bwd(x, w["g_xattn"][l], dh, dy, name=f"{tag}_dnorm")
    dw_xkv = _matmul(mn, dkv, "tn", F32, tm=1024, tn=1024, tk=1024, name=f"{tag}_dwkv")
    dmn = _matmul(dkv, w["w_xkv"][l], "nt", F32, tm=1024, tn=1024, tk=1024, name=f"{tag}_dmem")
    _, dg_mem = _rmsnorm_bwd(mem, w["g_mem"][l], dmn, None, name=f"{tag}_dmem_norm")
    return dx, dict(g_xattn=dg, g_mem=dg_mem, w_xq=dw_xq, w_xkv=dw_xkv, w_xo=dw_xo)


def kernel(x, mem, g_ffn1, w_ffn1_gu, w_ffn1_down, g_mix, w_mix_in, w_conv, b_f, g_conv_out, g_att_out, w_mix_out, g_xattn, g_mem, w_xq, w_xkv, w_xo, g_ffn2, w_ffn2_gu, w_ffn2_down, g_final, loss_target, m_g_ffn1, m_w_ffn1_gu, m_w_ffn1_down, m_g_mix, m_w_mix_in, m_w_conv, m_b_f, m_g_conv_out, m_g_att_out, m_w_mix_out, m_g_xattn, m_g_mem, m_w_xq, m_w_xkv, m_w_xo, m_g_ffn2, m_w_ffn2_gu, m_w_ffn2_down, m_g_final, v_g_ffn1, v_w_ffn1_gu, v_w_ffn1_down, v_g_mix, v_w_mix_in, v_w_conv, v_b_f, v_g_conv_out, v_g_att_out, v_w_mix_out, v_g_xattn, v_g_mem, v_w_xq, v_w_xkv, v_w_xo, v_g_ffn2, v_w_ffn2_gu, v_w_ffn2_down, v_g_final):
    local = dict(zip(WEIGHTS, (g_ffn1, w_ffn1_gu, w_ffn1_down, g_mix, w_mix_in, w_conv, b_f, g_conv_out, g_att_out, w_mix_out,
                               g_xattn, g_mem, w_xq, w_xkv, w_xo, g_ffn2, w_ffn2_gu, w_ffn2_down, g_final)))
    mom1 = dict(zip(WEIGHTS, (m_g_ffn1, m_w_ffn1_gu, m_w_ffn1_down, m_g_mix, m_w_mix_in, m_w_conv, m_b_f, m_g_conv_out,
                              m_g_att_out, m_w_mix_out, m_g_xattn, m_g_mem, m_w_xq, m_w_xkv, m_w_xo, m_g_ffn2, m_w_ffn2_gu,
                              m_w_ffn2_down, m_g_final)))
    mom2 = dict(zip(WEIGHTS, (v_g_ffn1, v_w_ffn1_gu, v_w_ffn1_down, v_g_mix, v_w_mix_in, v_w_conv, v_b_f, v_g_conv_out,
                              v_g_att_out, v_w_mix_out, v_g_xattn, v_g_mem, v_w_xq, v_w_xkv, v_w_xo, v_g_ffn2, v_w_ffn2_gu,
                              v_w_ffn2_down, v_g_final)))
    depth = g_ffn1.shape[0]
    s, d = x.shape[1], x.shape[2]
    w = dict(local)
    w.update(_gather_weights({name: local[name] for name, _ in SHARDED}))

    xs = x.reshape(s, d)
    mems = mem.reshape(mem.shape[1], d)
    saved = []
    for l in range(depth):
        xs, s1 = _ffn_fwd(xs, w["g_ffn1"][l], w["w_ffn1_gu"][l], w["w_ffn1_down"][l], f"l{l}_ffn1")
        xs, s2 = _mix_fwd(xs, w, l, f"l{l}_mix")
        xs, s3 = _xattn_block_fwd(xs, mems, w, l, f"l{l}_xattn")
        xs, s4 = _ffn_fwd(xs, w["g_ffn2"][l], w["w_ffn2_gu"][l], w["w_ffn2_down"][l], f"l{l}_ffn2")
        saved.append((s1, s2, s3, s4))

    dx, dg_final, sq = _final_loss(xs, g_final, loss_target.reshape(s, d), name="loss_head")
    loss = lax.psum(jnp.sum(sq) * (0.5 / d), MESH_AXES)

    per_layer = []
    for l in reversed(range(depth)):
        s1, s2, s3, s4 = saved[l]
        grads = {}
        dx, grads["g_ffn2"], grads["w_ffn2_gu"], grads["w_ffn2_down"] = _ffn_bwd(
            dx, s4, w["g_ffn2"][l], w["w_ffn2_gu"][l], w["w_ffn2_down"][l], f"l{l}_ffn2")
        dx, g3 = _xattn_block_bwd(dx, s3, mems, w, l, f"l{l}_xattn")
        dx, g2 = _mix_bwd(dx, s2, w, l, f"l{l}_mix")
        dx, grads["g_ffn1"], grads["w_ffn1_gu"], grads["w_ffn1_down"] = _ffn_bwd(
            dx, s1, w["g_ffn1"][l], w["w_ffn1_gu"][l], w["w_ffn1_down"][l], f"l{l}_ffn1")
        grads.update(g2)
        grads.update(g3)
        per_layer.append(grads)
    per_layer.reverse()
    grads = {name: jnp.stack([per_layer[l][name] for l in range(depth)]) for name in WEIGHTS if name != "g_final"}
    grads["g_final"] = dg_final.reshape(d)

    reduced = _reduce_gradients(grads, lax.axis_index("c"))
    deltas, new_m, new_v = {}, {}, {}
    for name in WEIGHTS:
        deltas[name], new_m[name], new_v[name] = _adamw(local[name], reduced[name], mom1[name], mom2[name], name=f"adamw_{name}")
    return (loss, dx.reshape(x.shape), *[reduced[n] for n in WEIGHTS], *[deltas[n] for n in WEIGHTS],
            *[new_m[n] for n in WEIGHTS], *[new_v[n] for n in WEIGHTS])
```

```python
import functools
import math

import numpy as np
import jax
import jax.numpy as jnp
from jax import lax
from jax.experimental import pallas as pl
from jax.experimental.pallas import tpu as pltpu

F32 = jnp.float32
BF16 = jnp.bfloat16

EPS = 1e-6
HEAD_DIM = 64
LANES = 128
N_XHEADS = 4
CONV_K = 3
ADAM_LR, ADAM_B1, ADAM_B2, ADAM_EPS, ADAM_WD, ADAM_STEP = 0.001, 0.9, 0.999, 1e-08, 0.01, 10
VMEM_LIMIT_BYTES = 56 * 1024 * 1024
NEG_BIG = -1e30
MESH_AXES = ("x", "y", "c")
N_CHIPS = 4


def _params(*sem):
    return pltpu.CompilerParams(dimension_semantics=sem, vmem_limit_bytes=VMEM_LIMIT_BYTES)


_DIMS = {"nn": (((1,), (0,)), ((), ())), "nt": (((1,), (1,)), ((), ())), "tn": (((0,), (0,)), ((), ()))}


def _matmul(a, b, mode, out_dtype, *, tm, tn, tk, name, alpha=1.0, residual=None):
    if mode == "nn":
        (m, k), (k2, n) = a.shape, b.shape
    elif mode == "nt":
        (m, k), (n, k2) = a.shape, b.shape
    else:
        (k, m), (k2, n) = a.shape, b.shape
    assert k == k2, (a.shape, b.shape, mode)
    tm, tn, tk = min(tm, m), min(tn, n), min(tk, k)
    assert m % tm == 0 and n % tn == 0 and k % tk == 0, (m, n, k, tm, tn, tk)
    nk = k // tk
    dims = _DIMS[mode]

    def body(*refs):
        if residual is None:
            a_ref, b_ref, o_ref, *scratch = refs
            r_ref = None
        else:
            a_ref, b_ref, r_ref, o_ref, *scratch = refs
        prod = lax.dot_general(a_ref[...].astype(BF16), b_ref[...].astype(BF16), dims, preferred_element_type=F32)

        def finish(acc):
            if alpha != 1.0:
                acc = acc * alpha
            if r_ref is not None:
                acc = acc + r_ref[...].astype(F32)
            o_ref[...] = acc.astype(o_ref.dtype)

        if nk == 1:
            finish(prod)
        else:
            acc_ref = scratch[0]
            kk = pl.program_id(2)

            @pl.when(kk == 0)
            def _():
                acc_ref[...] = prod

            @pl.when(kk > 0)
            def _():
                acc_ref[...] += prod

            @pl.when(kk == nk - 1)
            def _():
                finish(acc_ref[...])

    if mode == "nn":
        a_spec = pl.BlockSpec((tm, tk), lambda i, j, kk: (i, kk))
        b_spec = pl.BlockSpec((tk, tn), lambda i, j, kk: (kk, j))
    elif mode == "nt":
        a_spec = pl.BlockSpec((tm, tk), lambda i, j, kk: (i, kk))
        b_spec = pl.BlockSpec((tn, tk), lambda i, j, kk: (j, kk))
    else:
        a_spec = pl.BlockSpec((tk, tm), lambda i, j, kk: (kk, i))
        b_spec = pl.BlockSpec((tk, tn), lambda i, j, kk: (kk, j))
    o_spec = pl.BlockSpec((tm, tn), lambda i, j, kk: (i, j))
    in_specs, args = [a_spec, b_spec], [a, b]
    if residual is not None:
        in_specs.append(o_spec)
        args.append(residual)
    return pl.pallas_call(
        body, name=name, grid=(m // tm, n // tn, nk), in_specs=in_specs, out_specs=o_spec,
        out_shape=jax.ShapeDtypeStruct((m, n), out_dtype),
        scratch_shapes=[pltpu.VMEM((tm, tn), F32)] if nk > 1 else [],
        compiler_params=_params("parallel", "parallel", "arbitrary"),
    )(*args)


def _row_tile(rows, want):
    t = min(rows, want)
    assert rows % t == 0, (rows, t)
    return t


def _rmsnorm_fwd(x, g, *, name, tr=1024):
    s, d = x.shape
    tr = _row_tile(s, tr)

    def body(x_ref, g_ref, o_ref):
        xf = x_ref[...]
        r = lax.rsqrt(jnp.mean(xf * xf, axis=-1, keepdims=True) + EPS)
        o_ref[...] = (xf * r * g_ref[...]).astype(o_ref.dtype)

    return pl.pallas_call(
        body, name=name, grid=(s // tr,),
        in_specs=[pl.BlockSpec((tr, d), lambda i: (i, 0)), pl.BlockSpec((1, d), lambda i: (0, 0))],
        out_specs=pl.BlockSpec((tr, d), lambda i: (i, 0)),
        out_shape=jax.ShapeDtypeStruct((s, d), BF16),
        compiler_params=_params("parallel"),
    )(x, g.reshape(1, d))


def _rmsnorm_bwd(x, g, dh, dres, *, name, tr=512):
    s, d = x.shape
    tr = _row_tile(s, tr)

    def body(x_ref, g_ref, dh_ref, *rest):
        if dres is None:
            dx_ref, dg_ref = rest
        else:
            dres_ref, dx_ref, dg_ref = rest
        xf = x_ref[...]
        r = lax.rsqrt(jnp.mean(xf * xf, axis=-1, keepdims=True) + EPS)
        xhat = xf * r
        dhf = dh_ref[...].astype(F32)
        dxhat = dhf * g_ref[...]
        dx = r * (dxhat - xhat * jnp.mean(dxhat * xhat, axis=-1, keepdims=True))
        if dres is not None:
            dx = dx + dres_ref[...]
        dx_ref[...] = dx

        @pl.when(pl.program_id(0) == 0)
        def _():
            dg_ref[...] = jnp.zeros_like(dg_ref)

        dg_ref[...] += jnp.sum(dhf * xhat, axis=0, keepdims=True)

    row = pl.BlockSpec((tr, d), lambda i: (i, 0))
    vec = pl.BlockSpec((1, d), lambda i: (0, 0))
    in_specs, args = [row, vec, row], [x, g.reshape(1, d), dh]
    if dres is not None:
        in_specs.append(row)
        args.append(dres)
    dx, dg = pl.pallas_call(
        body, name=name, grid=(s // tr,), in_specs=in_specs, out_specs=[row, vec],
        out_shape=[jax.ShapeDtypeStruct((s, d), F32), jax.ShapeDtypeStruct((1, d), F32)],
        compiler_params=_params("arbitrary"),
    )(*args)
    return dx, dg.reshape(d)


def _swiglu_fwd(gu, *, name, tr=512):
    s, f2 = gu.shape
    f = f2 // 2
    tr = _row_tile(s, tr)

    def body(gu_ref, a_ref):
        gate = gu_ref[:, :f].astype(F32)
        up = gu_ref[:, f:].astype(F32)
        a_ref[...] = (gate * jax.nn.sigmoid(gate) * up).astype(a_ref.dtype)

    return pl.pallas_call(
        body, name=name, grid=(s // tr,),
        in_specs=[pl.BlockSpec((tr, f2), lambda i: (i, 0))],
        out_specs=pl.BlockSpec((tr, f), lambda i: (i, 0)),
        out_shape=jax.ShapeDtypeStruct((s, f), BF16),
        compiler_params=_params("parallel"),
    )(gu)


def _swiglu_bwd(da, gu, *, name, tr=512):
    s, f2 = gu.shape
    f = f2 // 2
    tr = _row_tile(s, tr)

    def body(da_ref, gu_ref, dgu_ref):
        gate = gu_ref[:, :f].astype(F32)
        up = gu_ref[:, f:].astype(F32)
        daf = da_ref[...].astype(F32)
        sig = jax.nn.sigmoid(gate)
        silu = gate * sig
        dgu_ref[:, :f] = (daf * up * (sig + silu * (1.0 - sig))).astype(dgu_ref.dtype)
        dgu_ref[:, f:] = (daf * silu).astype(dgu_ref.dtype)

    return pl.pallas_call(
        body, name=name, grid=(s // tr,),
        in_specs=[pl.BlockSpec((tr, f), lambda i: (i, 0)), pl.BlockSpec((tr, f2), lambda i: (i, 0))],
        out_specs=pl.BlockSpec((tr, f2), lambda i: (i, 0)),
        out_shape=jax.ShapeDtypeStruct((s, f2), BF16),
        compiler_params=_params("parallel"),
    )(da, gu)


_NT = (((1,), (1,)), ((), ()))
_NN = (((1,), (0,)), ((), ()))
_TN = (((0,), (0,)), ((), ()))
_QK_SCALE = HEAD_DIM ** -0.5


def _dot(a, b, dims):
    return lax.dot_general(a, b, dims, preferred_element_type=F32)


def _split_heads(v2, is_a):
    zero = jnp.zeros_like(v2)
    return jnp.where(is_a, v2, zero), jnp.where(is_a, zero, v2)


SKIP_BELOW = 107.0
_SMEM = pl.BlockSpec(memory_space=pltpu.SMEM)


def _attn_bounds(z, c, *, qcol_units, kcol_units, aw, t, name):
    s = z.shape[0]
    nq = s // t
    nh = aw // HEAD_DIM

    def body(q_ref, k_ref, o_ref):
        d = lax.broadcasted_iota(jnp.int32, (aw, LANES), 0)
        hh = lax.broadcasted_iota(jnp.int32, (aw, LANES), 1)
        onehot = ((d >= hh * HEAD_DIM) & (d < (hh + 1) * HEAD_DIM)).astype(F32)
        for r, ref in enumerate((q_ref, k_ref)):
            v = ref[...].astype(F32)
            sq = lax.dot_general(v * v, onehot, _NN, precision=lax.Precision.HIGHEST, preferred_element_type=F32)
            o_ref[r:r + 1, :] = jnp.max(sq, axis=0, keepdims=True)
        o_ref[2:, :] = jnp.zeros((HALO - 2, LANES), F32)

    sq = pl.pallas_call(
        body, name=name, grid=(nq,),
        in_specs=[pl.BlockSpec((t, aw), lambda i: (i, qcol_units)), pl.BlockSpec((t, aw), lambda i: (i, kcol_units))],
        out_specs=pl.BlockSpec((None, HALO, LANES), lambda i: (i, 0, 0)),
        out_shape=jax.ShapeDtypeStruct((nq, HALO, LANES), F32),
        compiler_params=_params("parallel"),
    )(z, z)
    norms = jnp.sqrt(sq[:, :2, :nh]) * 1.01
    qn = (norms[:, 0, :] * _QK_SCALE).T.reshape(-1)
    kn = norms[:, 1, :].T.reshape(-1)
    cs = c[0::t, :nh].T.reshape(-1)
    ce = c[t - 1::t, :nh].T.reshape(-1)
    return qn, kn, cs, ce


def _block_active(bounds, head, i, j, nq):
    qn_ref, kn_ref, cs_ref, ce_ref = bounds
    qi = qn_ref[head * nq + i]
    upper = qi * kn_ref[head * nq + j] + (cs_ref[head * nq + i] - ce_ref[head * nq + j])
    lower = -(qi * kn_ref[head * nq + i])
    return upper - lower > -SKIP_BELOW


def _fattn_fwd(z, ccol, crow, bounds, *, qcol, kcol, vcol, t, name):
    s = z.shape[0]
    npairs, nq = crow.shape[0], crow.shape[1]
    assert nq * t == s

    def body(qn_ref, kn_ref, cs_ref, ce_ref, q_ref, k_ref, v_ref, cc_ref, cr_ref, o_ref, lse_ref, m_scr, l_scr, acc_scr):
        pair, i = pl.program_id(0), pl.program_id(1)
        is_a = lax.broadcasted_iota(jnp.int32, (t, LANES), 1) < HEAD_DIM
        qh = _split_heads(q_ref[...] * jnp.asarray(_QK_SCALE, BF16), is_a)
        cq = (cc_ref[:, 0:1], cc_ref[:, 1:2])
        m_scr[...] = jnp.full(m_scr.shape, NEG_BIG, F32)
        l_scr[...] = jnp.zeros(l_scr.shape, F32)
        acc_scr[...] = jnp.zeros(acc_scr.shape, F32)

        def head_step(h, j, diagonal):
            off = pl.multiple_of(j * t, t)
            k2 = k_ref[pl.ds(off, t), :]
            vh = _split_heads(v_ref[pl.ds(off, t), :], is_a)[h]
            sc = _dot(qh[h], k2, _NT) + (cq[h] - cr_ref[j][h:h + 1, :])
            if diagonal:
                row = lax.broadcasted_iota(jnp.int32, (t, t), 0)
                col = lax.broadcasted_iota(jnp.int32, (t, t), 1)
                sc = jnp.where(row >= col, sc, NEG_BIG)
            m_old = m_scr[h]
            m_new = jnp.maximum(m_old, jnp.max(sc, axis=1, keepdims=True))
            alpha = jnp.exp(m_old - m_new)
            p = jnp.exp(sc - m_new)
            l_scr[h] = alpha * l_scr[h] + jnp.sum(p, axis=1, keepdims=True)
            m_scr[h] = m_new
            acc_scr[h] = acc_scr[h] * alpha + _dot(p.astype(BF16), vh, _NN)

        def loop_body(j, carry):
            for h in range(2):
                pl.when(_block_active((qn_ref, kn_ref, cs_ref, ce_ref), 2 * pair + h, i, j, nq))(
                    functools.partial(head_step, h, j, False))
            return carry

        lax.fori_loop(0, i, loop_body, 0)
        for h in range(2):
            head_step(h, i, True)
        o_ref[...] = jnp.where(is_a, acc_scr[0] / l_scr[0], acc_scr[1] / l_scr[1])
        lse_ref[:, 0:1] = m_scr[0] + jnp.log(l_scr[0])
        lse_ref[:, 1:2] = m_scr[1] + jnp.log(l_scr[1])

    col_vec = pl.BlockSpec((None, t, 2), lambda p, i: (p, i, 0))
    return pl.pallas_call(
        body, name=name, grid=(npairs, nq),
        in_specs=[
            _SMEM, _SMEM, _SMEM, _SMEM,
            pl.BlockSpec((t, LANES), lambda p, i: (i, qcol + p)),
            pl.BlockSpec((s, LANES), lambda p, i: (0, kcol + p)),
            pl.BlockSpec((s, LANES), lambda p, i: (0, vcol + p)),
            col_vec,
            pl.BlockSpec((None, nq, 2, t), lambda p, i: (p, 0, 0, 0)),
        ],
        out_specs=[pl.BlockSpec((t, LANES), lambda p, i: (i, p)), col_vec],
        out_shape=[jax.ShapeDtypeStruct((s, npairs * LANES), F32), jax.ShapeDtypeStruct((npairs, s, 2), F32)],
        scratch_shapes=[pltpu.VMEM((2, t, 1), F32), pltpu.VMEM((2, t, 1), F32), pltpu.VMEM((2, t, LANES), F32)],
        compiler_params=_params("parallel", "arbitrary"),
    )(*bounds, z, z, z, ccol, crow)


def _fattn_dq(z, dy, y, lse, ccol, crow, bounds, *, qcol, kcol, vcol, t, name):
    s = z.shape[0]
    npairs, nq = crow.shape[0], crow.shape[1]

    def body(qn_ref, kn_ref, cs_ref, ce_ref, q_ref, k_ref, v_ref, dy_ref, y_ref, lse_ref, cc_ref, cr_ref,
             dq_ref, delta_ref, dcq_ref, acc_scr, rs_scr):
        pair, i = pl.program_id(0), pl.program_id(1)
        is_a = lax.broadcasted_iota(jnp.int32, (t, LANES), 1) < HEAD_DIM
        qh = _split_heads(q_ref[...] * jnp.asarray(_QK_SCALE, BF16), is_a)
        cq = (cc_ref[:, 0:1], cc_ref[:, 1:2])
        lse = (lse_ref[:, 0:1], lse_ref[:, 1:2])
        dyf = dy_ref[...]
        prod = _split_heads(dyf * y_ref[...], is_a)
        delta = (jnp.sum(prod[0], axis=1, keepdims=True), jnp.sum(prod[1], axis=1, keepdims=True))
        delta_ref[:, 0:1] = delta[0]
        delta_ref[:, 1:2] = delta[1]
        dyh = _split_heads(dyf.astype(BF16), is_a)
        acc_scr[...] = jnp.zeros(acc_scr.shape, F32)
        rs_scr[...] = jnp.zeros(rs_scr.shape, F32)

        def head_step(h, j, diagonal):
            off = pl.multiple_of(j * t, t)
            k2 = k_ref[pl.ds(off, t), :]
            v2 = v_ref[pl.ds(off, t), :]
            sc = _dot(qh[h], k2, _NT) + (cq[h] - cr_ref[j][h:h + 1, :])
            if diagonal:
                row = lax.broadcasted_iota(jnp.int32, (t, t), 0)
                col = lax.broadcasted_iota(jnp.int32, (t, t), 1)
                sc = jnp.where(row >= col, sc, NEG_BIG)
            p = jnp.exp(sc - lse[h])
            dp = _dot(dyh[h], v2, _NT)
            ds = p * (dp - delta[h])
            rs_scr[h] += jnp.sum(ds, axis=1, keepdims=True)
            acc_scr[...] += _dot(ds.astype(BF16), _split_heads(k2, is_a)[h], _NN)

        def loop_body(j, carry):
            for h in range(2):
                pl.when(_block_active((qn_ref, kn_ref, cs_ref, ce_ref), 2 * pair + h, i, j, nq))(
                    functools.partial(head_step, h, j, False))
            return carry

        lax.fori_loop(0, i, loop_body, 0)
        for h in range(2):
            head_step(h, i, True)
        dq_ref[...] = (acc_scr[...] * _QK_SCALE).astype(dq_ref.dtype)
        dcq_ref[:, 0:1] = rs_scr[0]
        dcq_ref[:, 1:2] = rs_scr[1]

    col_vec = pl.BlockSpec((None, t, 2), lambda p, i: (p, i, 0))
    tile = pl.BlockSpec((t, LANES), lambda p, i: (i, p))
    vec_shape = jax.ShapeDtypeStruct((npairs, s, 2), F32)
    return pl.pallas_call(
        body, name=name, grid=(npairs, nq),
        in_specs=[
            _SMEM, _SMEM, _SMEM, _SMEM,
            pl.BlockSpec((t, LANES), lambda p, i: (i, qcol + p)),
            pl.BlockSpec((s, LANES), lambda p, i: (0, kcol + p)),
            pl.BlockSpec((s, LANES), lambda p, i: (0, vcol + p)),
            tile, tile, col_vec, col_vec,
            pl.BlockSpec((None, nq, 2, t), lambda p, i: (p, 0, 0, 0)),
        ],
        out_specs=[tile, col_vec, col_vec],
        out_shape=[jax.ShapeDtypeStruct((s, npairs * LANES), BF16), vec_shape, vec_shape],
        scratch_shapes=[pltpu.VMEM((t, LANES), F32), pltpu.VMEM((2, t, 1), F32)],
        compiler_params=_params("parallel", "arbitrary"),
    )(*bounds, z, z, z, dy, y, lse, ccol, crow)


def _fattn_dkv(z, dyb, lse_row, delta_row, ccol, crow, bounds, *, qcol, kcol, vcol, t, name):
    s = z.shape[0]
    npairs, nq = crow.shape[0], crow.shape[1]

    def body(qn_ref, kn_ref, cs_ref, ce_ref, k_ref, v_ref, q_ref, dy_ref, lse_ref, dl_ref, cc_ref, cr_ref,
             dk_ref, dv_ref, dc_ref, dk_scr, dv_scr, dc_scr):
        pair, j = pl.program_id(0), pl.program_id(1)
        is_a = lax.broadcasted_iota(jnp.int32, (t, LANES), 1) < HEAD_DIM
        kh = _split_heads(k_ref[...], is_a)
        vh = _split_heads(v_ref[...], is_a)
        ck = (cc_ref[:, 0:1], cc_ref[:, 1:2])
        dk_scr[...] = jnp.zeros(dk_scr.shape, F32)
        dv_scr[...] = jnp.zeros(dv_scr.shape, F32)
        dc_scr[...] = jnp.zeros(dc_scr.shape, F32)

        def head_step(h, i, diagonal):
            off = pl.multiple_of(i * t, t)
            q2 = q_ref[pl.ds(off, t), :] * jnp.asarray(_QK_SCALE, BF16)
            dy2 = dy_ref[pl.ds(off, t), :]
            st = _dot(kh[h], q2, _NT) + (cr_ref[i][h:h + 1, :] - ck[h])
            if diagonal:
                row = lax.broadcasted_iota(jnp.int32, (t, t), 0)
                col = lax.broadcasted_iota(jnp.int32, (t, t), 1)
                st = jnp.where(col >= row, st, NEG_BIG)
            pt = jnp.exp(st - lse_ref[i][h:h + 1, :])
            dv_scr[...] += _dot(pt.astype(BF16), _split_heads(dy2, is_a)[h], _NN)
            dpt = _dot(vh[h], dy2, _NT)
            dst = pt * (dpt - dl_ref[i][h:h + 1, :])
            dk_scr[...] += _dot(dst.astype(BF16), _split_heads(q2, is_a)[h], _NN)
            dc_scr[h] += jnp.sum(dst, axis=1, keepdims=True)

        def loop_body(i, carry):
            for h in range(2):
                pl.when(_block_active((qn_ref, kn_ref, cs_ref, ce_ref), 2 * pair + h, i, j, nq))(
                    functools.partial(head_step, h, i, False))
            return carry

        for h in range(2):
            head_step(h, j, True)
        lax.fori_loop(j + 1, nq, loop_body, 0)
        dk_ref[...] = dk_scr[...].astype(dk_ref.dtype)
        dv_ref[...] = dv_scr[...].astype(dv_ref.dtype)
        dc_ref[:, 0:1] = -dc_scr[0]
        dc_ref[:, 1:2] = -dc_scr[1]

    col_vec = pl.BlockSpec((None, t, 2), lambda p, j: (p, j, 0))
    rows = pl.BlockSpec((None, nq, 2, t), lambda p, j: (p, 0, 0, 0))
    tile = pl.BlockSpec((t, LANES), lambda p, j: (j, p))
    return pl.pallas_call(
        body, name=name, grid=(npairs, nq),
        in_specs=[
            _SMEM, _SMEM, _SMEM, _SMEM,
            pl.BlockSpec((t, LANES), lambda p, j: (j, kcol + p)),
            pl.BlockSpec((t, LANES), lambda p, j: (j, vcol + p)),
            pl.BlockSpec((s, LANES), lambda p, j: (0, qcol + p)),
            pl.BlockSpec((s, LANES), lambda p, j: (0, p)),
            rows, rows, col_vec, rows,
        ],
        out_specs=[tile, tile, col_vec],
        out_shape=[jax.ShapeDtypeStruct((s, npairs * LANES), BF16), jax.ShapeDtypeStruct((s, npairs * LANES), BF16),
                   jax.ShapeDtypeStruct((npairs, s, 2), F32)],
        scratch_shapes=[pltpu.VMEM((t, LANES), F32), pltpu.VMEM((t, LANES), F32), pltpu.VMEM((2, t, 1), F32)],
        compiler_params=_params("parallel", "arbitrary"),
    )(*bounds, z, z, z, dyb, lse_row, delta_row, ccol, crow)


def _log_sigmoid(x):
    return jnp.minimum(x, 0.0) - jnp.log(1.0 + jnp.exp(-jnp.abs(x)))


def _cumsum_fwd(zf, b, *, name, t=512):
    s, w = zf.shape
    t = _row_tile(s, t)

    def body(zf_ref, b_ref, c_ref, carry):
        @pl.when(pl.program_id(0) == 0)
        def _():
            carry[...] = jnp.zeros(carry.shape, F32)

        lf = _log_sigmoid(zf_ref[...] + b_ref[...])
        row = lax.broadcasted_iota(jnp.int32, (t, t), 0)
        col = lax.broadcasted_iota(jnp.int32, (t, t), 1)
        tri = (row >= col).astype(F32)
        c = lax.dot_general(tri, lf, _NN, precision=lax.Precision.HIGHEST, preferred_element_type=F32) + carry[...]
        c_ref[...] = c
        carry[...] = c[t - 1:t, :]

    return pl.pallas_call(
        body, name=name, grid=(s // t,),
        in_specs=[pl.BlockSpec((t, w), lambda i: (i, 0)), pl.BlockSpec((1, w), lambda i: (0, 0))],
        out_specs=pl.BlockSpec((t, w), lambda i: (i, 0)),
        out_shape=jax.ShapeDtypeStruct((s, w), F32),
        scratch_shapes=[pltpu.VMEM((1, w), F32)],
        compiler_params=_params("arbitrary"),
    )(zf, b)


def _cumsum_bwd(dcq, dck, zf, b, *, name, t=512):
    s, w = zf.shape
    t = _row_tile(s, t)
    nb = s // t

    def body(dcq_ref, dck_ref, zf_ref, b_ref, dzf_ref, db_ref, carry):
        @pl.when(pl.program_id(0) == 0)
        def _():
            carry[...] = jnp.zeros(carry.shape, F32)
            db_ref[...] = jnp.zeros(db_ref.shape, F32)

        row = lax.broadcasted_iota(jnp.int32, (t, t), 0)
        col = lax.broadcasted_iota(jnp.int32, (t, t), 1)
        tri = (row <= col).astype(F32)
        dc = dcq_ref[...] + dck_ref[...]
        dlf = lax.dot_general(tri, dc, _NN, precision=lax.Precision.HIGHEST, preferred_element_type=F32) + carry[...]
        carry[...] = dlf[0:1, :]
        dzf = dlf * jax.nn.sigmoid(-(zf_ref[...] + b_ref[...]))
        dzf_ref[...] = dzf
        db_ref[...] += jnp.sum(dzf, axis=0, keepdims=True)

    blk = pl.BlockSpec((t, w), lambda i: (nb - 1 - i, 0))
    vec = pl.BlockSpec((1, w), lambda i: (0, 0))
    return pl.pallas_call(
        body, name=name, grid=(nb,), in_specs=[blk, blk, blk, vec], out_specs=[blk, vec],
        out_shape=[jax.ShapeDtypeStruct((s, w), F32), jax.ShapeDtypeStruct((1, w), F32)],
        scratch_shapes=[pltpu.VMEM((1, w), F32)],
        compiler_params=_params("arbitrary"),
    )(dcq, dck, zf, b)


HALO = 8


def _rms_rows(v):
    return lax.rsqrt(jnp.mean(v * v, axis=-1, keepdims=True) + EPS)


def _mixpost_fwd(z, yatt, wconv, gc, ga, *, name, tr=512):
    s = z.shape[0]
    cw, aw = gc.shape[-1], ga.shape[-1]
    tr = _row_tile(s, tr)

    def body(zb_ref, zc_ref, zv_ref, ya_ref, w_ref, gc_ref, ga_ref, ycat_ref, cv_ref, u_scr):
        @pl.when(pl.program_id(0) == 0)
        def _():
            u_scr[0:HALO, :] = jnp.zeros((HALO, cw), F32)

        u = zc_ref[...].astype(F32) * zv_ref[...].astype(F32)
        u_scr[HALO:HALO + tr, :] = u
        cv = w_ref[0:1, :] * u_scr[HALO - 2:HALO - 2 + tr, :] + w_ref[1:2, :] * u_scr[HALO - 1:HALO - 1 + tr, :] + w_ref[2:3, :] * u
        u_scr[0:HALO, :] = u_scr[tr:tr + HALO, :]
        cv_ref[...] = cv
        yc = zb_ref[...].astype(F32) * cv
        ya = ya_ref[...]
        ycat_ref[:, :cw] = (yc * _rms_rows(yc) * gc_ref[...]).astype(ycat_ref.dtype)
        ycat_ref[:, cw:] = (ya * _rms_rows(ya) * ga_ref[...]).astype(ycat_ref.dtype)

    return pl.pallas_call(
        body, name=name, grid=(s // tr,),
        in_specs=[
            pl.BlockSpec((tr, cw), lambda i: (i, 0)), pl.BlockSpec((tr, cw), lambda i: (i, 1)),
            pl.BlockSpec((tr, cw), lambda i: (i, 2)), pl.BlockSpec((tr, aw), lambda i: (i, 0)),
            pl.BlockSpec((HALO, cw), lambda i: (0, 0)), pl.BlockSpec((1, cw), lambda i: (0, 0)),
            pl.BlockSpec((1, aw), lambda i: (0, 0)),
        ],
        out_specs=[pl.BlockSpec((tr, cw + aw), lambda i: (i, 0)), pl.BlockSpec((tr, cw), lambda i: (i, 0))],
        out_shape=[jax.ShapeDtypeStruct((s, cw + aw), BF16), jax.ShapeDtypeStruct((s, cw), F32)],
        scratch_shapes=[pltpu.VMEM((tr + HALO, cw), F32)],
        compiler_params=_params("arbitrary"),
    )(z, z, z, yatt, wconv, gc.reshape(1, cw), ga.reshape(1, aw))


def _mixpost_bwd(dycat, z, yatt, cv, wconv, gc, ga, *, name, tr=512):
    s = z.shape[0]
    cw, aw = gc.shape[-1], ga.shape[-1]
    tr = _row_tile(s, tr)
    nb = s // tr

    def body(dy_ref, zb_ref, zc_ref, zv_ref, ya_ref, cv_ref, w_ref, gc_ref, ga_ref,
             dz_ref, dya_ref, dw_ref, dgc_ref, dga_ref, d_scr):
        @pl.when(pl.program_id(0) == 0)
        def _():
            d_scr[tr:tr + HALO, :] = jnp.zeros((HALO, cw), F32)
            dw_ref[...] = jnp.zeros(dw_ref.shape, F32)
            dgc_ref[...] = jnp.zeros(dgc_ref.shape, F32)
            dga_ref[...] = jnp.zeros(dga_ref.shape, F32)

        zb, zc, zv = zb_ref[...].astype(F32), zc_ref[...].astype(F32), zv_ref[...].astype(F32)
        cvv = cv_ref[...]

        def norm_bwd(v, dn, g):
            r = _rms_rows(v)
            vh = v * r
            dvh = dn * g
            return r * (dvh - vh * jnp.mean(dvh * vh, axis=-1, keepdims=True)), jnp.sum(dn * vh, axis=0, keepdims=True)

        dyc, dgc = norm_bwd(zb * cvv, dy_ref[:, :cw], gc_ref[...])
        dya, dga = norm_bwd(ya_ref[...], dy_ref[:, cw:], ga_ref[...])
        dgc_ref[...] += dgc
        dga_ref[...] += dga
        dya_ref[...] = dya
        dcv = dyc * zb
        d_scr[0:tr, :] = dcv
        d1 = d_scr[1:tr + 1, :]
        d2 = d_scr[2:tr + 2, :]
        du = w_ref[2:3, :] * dcv + w_ref[1:2, :] * d1 + w_ref[0:1, :] * d2
        u = zc * zv
        dw_ref[0:1, :] += jnp.sum(u * d2, axis=0, keepdims=True)
        dw_ref[1:2, :] += jnp.sum(u * d1, axis=0, keepdims=True)
        dw_ref[2:3, :] += jnp.sum(u * dcv, axis=0, keepdims=True)
        d_scr[tr:tr + HALO, :] = d_scr[0:HALO, :]
        dz_ref[:, :cw] = (dyc * cvv).astype(dz_ref.dtype)
        dz_ref[:, cw:2 * cw] = (du * zv).astype(dz_ref.dtype)
        dz_ref[:, 2 * cw:] = (du * zc).astype(dz_ref.dtype)

    def rows(width, colblk=0):
        return pl.BlockSpec((tr, width), lambda i: (nb - 1 - i, colblk))

    def fixed(r, width):
        return pl.BlockSpec((r, width), lambda i: (0, 0))

    return pl.pallas_call(
        body, name=name, grid=(nb,),
        in_specs=[rows(cw + aw), rows(cw, 0), rows(cw, 1), rows(cw, 2), rows(aw), rows(cw),
                  fixed(HALO, cw), fixed(1, cw), fixed(1, aw)],
        out_specs=[rows(3 * cw), rows(aw), fixed(HALO, cw), fixed(1, cw), fixed(1, aw)],
        out_shape=[jax.ShapeDtypeStruct((s, 3 * cw), BF16), jax.ShapeDtypeStruct((s, aw), F32),
                   jax.ShapeDtypeStruct((HALO, cw), F32), jax.ShapeDtypeStruct((1, cw), F32),
                   jax.ShapeDtypeStruct((1, aw), F32)],
        scratch_shapes=[pltpu.VMEM((tr + HALO, cw), F32)],
        compiler_params=_params("arbitrary"),
    )(dycat, z, z, z, yatt, cv, wconv, gc.reshape(1, cw), ga.reshape(1, aw))


def _xattn_fwd(q, kv, *, name, tq=1024):
    s, d = q.shape
    m = kv.shape[0]
    dh = d // N_XHEADS
    scale = dh ** -0.5
    tq = _row_tile(s, tq)

    def body(q_ref, kv_ref, o_ref):
        for h in range(N_XHEADS):
            lo, hi = h * dh, (h + 1) * dh
            sc = _dot(q_ref[:, lo:hi], kv_ref[:, lo:hi], _NT) * scale
            p = jnp.exp(sc - jnp.max(sc, axis=1, keepdims=True))
            o = _dot(p.astype(BF16), kv_ref[:, d + lo:d + hi], _NN) / jnp.sum(p, axis=1, keepdims=True)
            o_ref[:, lo:hi] = o.astype(o_ref.dtype)

    return pl.pallas_call(
        body, name=name, grid=(s // tq,),
        in_specs=[pl.BlockSpec((tq, d), lambda i: (i, 0)), pl.BlockSpec((m, 2 * d), lambda i: (0, 0))],
        out_specs=pl.BlockSpec((tq, d), lambda i: (i, 0)),
        out_shape=jax.ShapeDtypeStruct((s, d), BF16),
        compiler_params=_params("parallel"),
    )(q, kv)


def _xattn_bwd(q, kv, do, *, name, tq=1024):
    s, d = q.shape
    m = kv.shape[0]
    dh = d // N_XHEADS
    scale = dh ** -0.5
    tq = _row_tile(s, tq)

    def body(q_ref, kv_ref, do_ref, dq_ref, dkv_ref):
        @pl.when(pl.program_id(0) == 0)
        def _():
            dkv_ref[...] = jnp.zeros(dkv_ref.shape, F32)

        for h in range(N_XHEADS):
            lo, hi = h * dh, (h + 1) * dh
            qh, kh, vh, doh = q_ref[:, lo:hi], kv_ref[:, lo:hi], kv_ref[:, d + lo:d + hi], do_ref[:, lo:hi]
            sc = _dot(qh, kh, _NT) * scale
            e = jnp.exp(sc - jnp.max(sc, axis=1, keepdims=True))
            p = e / jnp.sum(e, axis=1, keepdims=True)
            dp = _dot(doh, vh, _NT)
            ds = p * (dp - jnp.sum(dp * p, axis=1, keepdims=True))
            dsb = ds.astype(BF16)
            dq_ref[:, lo:hi] = (_dot(dsb, kh, _NN) * scale).astype(dq_ref.dtype)
            dkv_ref[:, lo:hi] += _dot(dsb, qh, _TN) * scale
            dkv_ref[:, d + lo:d + hi] += _dot(p.astype(BF16), doh, _TN)

    return pl.pallas_call(
        body, name=name, grid=(s // tq,),
        in_specs=[pl.BlockSpec((tq, d), lambda i: (i, 0)), pl.BlockSpec((m, 2 * d), lambda i: (0, 0)),
                  pl.BlockSpec((tq, d), lambda i: (i, 0))],
        out_specs=[pl.BlockSpec((tq, d), lambda i: (i, 0)), pl.BlockSpec((m, 2 * d), lambda i: (0, 0))],
        out_shape=[jax.ShapeDtypeStruct((s, d), BF16), jax.ShapeDtypeStruct((m, 2 * d), F32)],
        compiler_params=_params("arbitrary"),
    )(q, kv, do)


def _final_loss(x, g, target, *, name, tr=512):
    s, d = x.shape
    tr = _row_tile(s, tr)

    def body(x_ref, g_ref, t_ref, dx_ref, dg_ref, sq_ref):
        @pl.when(pl.program_id(0) == 0)
        def _():
            dg_ref[...] = jnp.zeros(dg_ref.shape, F32)
            sq_ref[...] = jnp.zeros(sq_ref.shape, F32)

        xf = x_ref[...]
        r = _rms_rows(xf)
        xhat = xf * r
        err = xhat * g_ref[...] - t_ref[...]
        sq_ref[...] += jnp.sum(err * err, axis=0, keepdims=True)
        dy = err * (1.0 / d)
        dg_ref[...] += jnp.sum(dy * xhat, axis=0, keepdims=True)
        dxhat = dy * g_ref[...]
        dx_ref[...] = r * (dxhat - xhat * jnp.mean(dxhat * xhat, axis=-1, keepdims=True))

    row = pl.BlockSpec((tr, d), lambda i: (i, 0))
    vec = pl.BlockSpec((1, d), lambda i: (0, 0))
    return pl.pallas_call(
        body, name=name, grid=(s // tr,), in_specs=[row, vec, row], out_specs=[row, vec, vec],
        out_shape=[jax.ShapeDtypeStruct((s, d), F32), jax.ShapeDtypeStruct((1, d), F32), jax.ShapeDtypeStruct((1, d), F32)],
        compiler_params=_params("arbitrary"),
    )(x, g.reshape(1, d), target)


def _adamw(w, g, m, v, *, name, tr=512):
    shape = w.shape
    cols = shape[-1]
    rows = w.size // cols
    tr = tr if rows % tr == 0 else rows

    def body(w_ref, g_ref, m_ref, v_ref, d_ref, nm_ref, nv_ref):
        gf = g_ref[...]
        nm = ADAM_B1 * m_ref[...] + (1.0 - ADAM_B1) * gf
        nv = ADAM_B2 * v_ref[...] + (1.0 - ADAM_B2) * (gf * gf)
        m_hat = nm / (1.0 - ADAM_B1 ** ADAM_STEP)
        v_hat = nv / (1.0 - ADAM_B2 ** ADAM_STEP)
        d_ref[...] = -ADAM_LR * (m_hat / (jnp.sqrt(v_hat) + ADAM_EPS) + ADAM_WD * w_ref[...])
        nm_ref[...] = nm
        nv_ref[...] = nv

    blk = pl.BlockSpec((tr, cols), lambda i: (i, 0))
    out = jax.ShapeDtypeStruct((rows, cols), F32)
    outs = pl.pallas_call(
        body, name=name, grid=(rows // tr,), in_specs=[blk] * 4, out_specs=[blk] * 3, out_shape=[out] * 3,
        compiler_params=_params("parallel"),
    )(*[t.reshape(rows, cols) for t in (w, g, m, v)])
    return tuple(o.reshape(shape) for o in outs)


_HBM = pl.BlockSpec(memory_space=pl.ANY)
_MESH_ID = pl.DeviceIdType.MESH


def _place():
    x, y, c = (lax.axis_index(a) for a in MESH_AXES)
    return x, y, c, [(1 - x, y), (x, 1 - y), (1 - x, 1 - y)]


def _remote(src, dst, send_sems, recv_sems, k, to):
    return pltpu.make_async_remote_copy(src_ref=src, dst_ref=dst, send_sem=send_sems.at[k], recv_sem=recv_sems.at[k],
                                        device_id=to, device_id_type=_MESH_ID)


def _comm_call(body, out_shape, n_remote, name, *args):
    return pl.pallas_call(
        body, name=name, in_specs=[_HBM] * len(args), out_specs=_HBM, out_shape=out_shape,
        scratch_shapes=[pltpu.SemaphoreType.DMA((n_remote,)), pltpu.SemaphoreType.DMA((n_remote,)), pltpu.SemaphoreType.DMA],
    )(*args)


def _allgather_chips(w, *, name):
    def body(w_ref, out_ref, send_sems, recv_sems, local_sem):
        x, y, c, chips = _place()
        me = 2 * x + y
        mine = pltpu.make_async_copy(w_ref, out_ref.at[me], local_sem)
        mine.start()
        sends = [_remote(w_ref, out_ref.at[me], send_sems, recv_sems, k, (px, py, c)) for k, (px, py) in enumerate(chips)]
        for cp in sends:
            cp.start()
        for k, (px, py) in enumerate(chips):
            _remote(w_ref, out_ref.at[2 * px + py], send_sems, recv_sems, k, (px, py, c)).wait_recv()
        for cp in sends:
            cp.wait_send()
        mine.wait()

    return _comm_call(body, jax.ShapeDtypeStruct((N_CHIPS,) + w.shape, w.dtype), 3, name, w)


def _swap_halves(g, *, name):
    def body(g_ref, out_ref, send_sems, recv_sems, local_sem):
        x, y, c, _ = _place()
        cp = _remote(g_ref.at[1 - c], out_ref, send_sems, recv_sems, 0, (x, y, 1 - c))
        cp.start()
        cp.wait()

    return _comm_call(body, jax.ShapeDtypeStruct(g.shape[1:], g.dtype), 1, name, g)


def _scatter_chips(p, *, name):
    def body(p_ref, out_ref, send_sems, recv_sems, local_sem):
        x, y, c, chips = _place()
        me = 2 * x + y
        mine = pltpu.make_async_copy(p_ref.at[me], out_ref.at[me], local_sem)
        mine.start()
        sends = [_remote(p_ref.at[2 * px + py], out_ref.at[me], send_sems, recv_sems, k, (px, py, c))
                 for k, (px, py) in enumerate(chips)]
        for cp in sends:
            cp.start()
        for k, (px, py) in enumerate(chips):
            _remote(p_ref.at[me], out_ref.at[2 * px + py], send_sems, recv_sems, k, (px, py, c)).wait_recv()
        for cp in sends:
            cp.wait_send()
        mine.wait()

    return _comm_call(body, jax.ShapeDtypeStruct(p.shape, p.dtype), 3, name, p)


def _join_halves(r, *, name):
    def body(r_ref, out_ref, send_sems, recv_sems, local_sem):
        x, y, c, _ = _place()
        mine = pltpu.make_async_copy(r_ref, out_ref.at[c], local_sem)
        mine.start()
        cp = _remote(r_ref, out_ref.at[c], send_sems, recv_sems, 0, (x, y, 1 - c))
        cp.start()
        _remote(r_ref, out_ref.at[1 - c], send_sems, recv_sems, 0, (x, y, 1 - c)).wait_recv()
        cp.wait_send()
        mine.wait()

    return _comm_call(body, jax.ShapeDtypeStruct((2,) + r.shape, r.dtype), 1, name, r)


def _add_halves(g, recv, c, *, name, tr=512):
    _, nblk, rows, cols = g.shape
    tr = _row_tile(rows, tr)

    def body(c_ref, g_ref, r_ref, o_ref):
        o_ref[...] = g_ref[...] + r_ref[...]

    blk = pl.BlockSpec((None, tr, cols), lambda b, i, c_ref: (b, i, 0))
    return pl.pallas_call(
        body, name=name,
        grid_spec=pltpu.PrefetchScalarGridSpec(
            num_scalar_prefetch=1, grid=(nblk, rows // tr),
            in_specs=[pl.BlockSpec((None, None, tr, cols), lambda b, i, c_ref: (c_ref[0], b, i, 0)), blk],
            out_specs=blk),
        out_shape=jax.ShapeDtypeStruct(recv.shape, F32),
        compiler_params=_params("parallel", "parallel"),
    )(c.reshape(1).astype(jnp.int32), g, recv)


def _sum_slots(p, *, name, tr=512):
    nblk, rows, cols = p.shape
    tr = _row_tile(rows, tr)

    def body(p_ref, o_ref):
        acc = p_ref[0]
        for q in range(1, nblk):
            acc = acc + p_ref[q]
        o_ref[...] = acc

    return pl.pallas_call(
        body, name=name, grid=(rows // tr,),
        in_specs=[pl.BlockSpec((nblk, tr, cols), lambda i: (0, i, 0))],
        out_specs=pl.BlockSpec((tr, cols), lambda i: (i, 0)),
        out_shape=jax.ShapeDtypeStruct((rows, cols), F32),
        compiler_params=_params("parallel"),
    )(p)


SHARDED = (("w_ffn1_gu", "col"), ("w_ffn1_down", "row"), ("w_mix_in", "col"), ("w_conv", "col"), ("w_mix_out", "row"),
           ("w_xq", "row"), ("w_xkv", "col"), ("w_xo", "row"), ("w_ffn2_gu", "col"), ("w_ffn2_down", "row"))
REPLICATED = ("g_ffn1", "g_mix", "b_f", "g_conv_out", "g_att_out", "g_xattn", "g_mem", "g_ffn2", "g_final")
WEIGHTS = ("g_ffn1", "w_ffn1_gu", "w_ffn1_down", "g_mix", "w_mix_in", "w_conv", "b_f", "g_conv_out", "g_att_out",
           "w_mix_out", "g_xattn", "g_mem", "w_xq", "w_xkv", "w_xo", "g_ffn2", "w_ffn2_gu", "w_ffn2_down", "g_final")
FLAT_COLS = 512
FLAT_ROW_TILE = 512


def _full_from_blocks(blocks, kind):
    n, l, a, b = blocks.shape
    if kind == "row":
        return blocks.transpose(1, 0, 2, 3).reshape(l, n * a, b)
    return blocks.transpose(1, 2, 0, 3).reshape(l, a, n * b)


def _blocks_from_full(full, kind):
    l, a, b = full.shape
    if kind == "row":
        return full.reshape(l, N_CHIPS, a // N_CHIPS, b).transpose(1, 0, 2, 3)
    return full.reshape(l, a, N_CHIPS, b // N_CHIPS).transpose(2, 0, 1, 3)


def _gather_weights(shards):
    pieces, sizes = [], []
    for name, _ in SHARDED:
        w = shards[name]
        flat = lax.bitcast_convert_type(w, BF16).reshape(-1) if name == "w_conv" else w.astype(BF16).reshape(-1)
        pieces.append(flat)
        sizes.append(flat.shape[0])
    total = sum(sizes)
    rows = -(-total // FLAT_COLS)
    flat = jnp.concatenate(pieces + [jnp.zeros((rows * FLAT_COLS - total,), BF16)])
    gathered = _allgather_chips(flat.reshape(rows, FLAT_COLS), name="allgather_weights").reshape(N_CHIPS, rows * FLAT_COLS)
    full, off = {}, 0
    for (name, kind), n in zip(SHARDED, sizes):
        blk = gathered[:, off:off + n]
        off += n
        shape = shards[name].shape
        if name == "w_conv":
            blk = lax.bitcast_convert_type(blk.reshape((N_CHIPS,) + shape + (2,)), F32)
        else:
            blk = blk.reshape((N_CHIPS,) + shape)
        full[name] = _full_from_blocks(blk, kind)
    return full


def _reduce_gradients(grads, c):
    rep = jnp.concatenate([grads[n].reshape(-1) for n in REPLICATED])
    blocks = [_blocks_from_full(grads[name], kind).reshape(N_CHIPS, -1) for name, kind in SHARDED]
    sizes = [b.shape[1] for b in blocks]
    total = sum(sizes) + rep.shape[0]
    unit = 2 * FLAT_COLS * FLAT_ROW_TILE
    rows = -(-total // unit) * FLAT_ROW_TILE
    pad = 2 * rows * FLAT_COLS - total
    flat = jnp.concatenate(blocks + [jnp.broadcast_to(rep, (N_CHIPS, rep.shape[0])), jnp.zeros((N_CHIPS, pad), F32)], axis=1)
    g = flat.reshape(N_CHIPS, 2, rows, FLAT_COLS).transpose(1, 0, 2, 3)
    from_sibling = _swap_halves(g, name="grad_swap_halves")
    chip_sum = _add_halves(g, from_sibling, c, name="grad_add_halves")
    from_chips = _scatter_chips(chip_sum, name="grad_scatter_chips")
    half = _sum_slots(from_chips, name="grad_sum_chips")
    reduced = _join_halves(half, name="grad_join_halves").reshape(-1)
    out, off = {}, 0
    for (name, kind), n in zip(SHARDED, sizes):
        full_shape = grads[name].shape
        shard_shape = (full_shape[0], full_shape[1] // N_CHIPS, full_shape[2]) if kind == "row" else \
            (full_shape[0], full_shape[1], full_shape[2] // N_CHIPS)
        out[name] = reduced[off:off + n].reshape(shard_shape)
        off += n
    for name in REPLICATED:
        n = grads[name].size
        out[name] = reduced[off:off + n].reshape(grads[name].shape)
        off += n
    return out


def _ffn_fwd(x, g, w_gu, w_down, tag):
    h = _rmsnorm_fwd(x, g, name=f"{tag}_norm")
    gu = _matmul(h, w_gu, "nn", BF16, tm=2048, tn=512, tk=w_gu.shape[0], name=f"{tag}_gu")
    a = _swiglu_fwd(gu, name=f"{tag}_act")
    y = _matmul(a, w_down, "nn", F32, tm=1024, tn=1024, tk=w_down.shape[0], alpha=0.5, residual=x, name=f"{tag}_down")
    return y, (x, h, gu, a)


def _ffn_bwd(dy, saved, g, w_gu, w_down, tag):
    x, h, gu, a = saved
    f = w_down.shape[0]
    da = _matmul(dy, w_down, "nt", BF16, tm=1024, tn=f // 2, tk=w_down.shape[1], alpha=0.5, name=f"{tag}_da")
    dw_down = _matmul(a, dy, "tn", F32, tm=f // 2, tn=1024, tk=1024, alpha=0.5, name=f"{tag}_dwdown")
    dgu = _swiglu_bwd(da, gu, name=f"{tag}_dact")
    dw_gu = _matmul(h, dgu, "tn", F32, tm=1024, tn=512, tk=2048, name=f"{tag}_dwgu")
    dh = _matmul(dgu, w_gu, "nt", F32, tm=1024, tn=1024, tk=f, name=f"{tag}_dh")
    dx, dg = _rmsnorm_bwd(x, g, dh, dy, name=f"{tag}_dnorm")
    return dx, dg, dw_gu, dw_down


def _tokens_on_lanes(v, t):
    p, s, _ = v.shape
    return v.transpose(0, 2, 1).reshape(p, 2, s // t, t).transpose(0, 2, 1, 3)


def _mix_fwd(x, w, l, tag):
    s, d = x.shape
    gc, ga = w["g_conv_out"][l], w["g_att_out"][l]
    cw, aw = gc.shape[0], ga.shape[0]
    nh = aw // HEAD_DIM
    zw = 3 * cw + 3 * aw
    t = 512 if s >= 2048 else s // 4
    cols = dict(qcol=3 * cw // LANES, kcol=(3 * cw + aw) // LANES, vcol=(3 * cw + 2 * aw) // LANES, t=t)
    h = _rmsnorm_fwd(x, w["g_mix"][l], name=f"{tag}_norm")
    w_in = w["w_mix_in"][l]
    w_main = w_in[:, :zw]
    w_f = jnp.pad(w_in[:, zw:], ((0, 0), (0, LANES - nh)))
    z = _matmul(h, w_main, "nn", BF16, tm=2048, tn=512, tk=d, name=f"{tag}_in")
    zf = _matmul(h, w_f, "nn", F32, tm=2048, tn=LANES, tk=d, name=f"{tag}_in_f")
    b = jnp.pad(w["b_f"][l], (0, LANES - nh)).reshape(1, LANES)
    c = _cumsum_fwd(zf, b, name=f"{tag}_cumsum")
    ccol = c[:, :nh].reshape(s, nh // 2, 2).transpose(1, 0, 2)
    crow = _tokens_on_lanes(ccol, t)
    assert (3 * cw) % aw == 0
    bounds = _attn_bounds(z, c, qcol_units=3 * cw // aw, kcol_units=3 * cw // aw + 1, aw=aw, t=t, name=f"{tag}_bounds")
    yatt, lse = _fattn_fwd(z, ccol, crow, bounds, name=f"{tag}_attn", **cols)
    wc = jnp.pad(w["w_conv"][l], ((0, HALO - CONV_K), (0, 0)))
    ycat, cv = _mixpost_fwd(z, yatt, wc, gc, ga, name=f"{tag}_post")
    y = _matmul(ycat, w["w_mix_out"][l], "nn", F32, tm=1024, tn=1024, tk=d, residual=x, name=f"{tag}_out")
    return y, (x, h, w_main, w_f, z, zf, b, ccol, crow, bounds, yatt, lse, wc, ycat, cv, cols)


def _mix_bwd(dy, saved, w, l, tag):
    x, h, w_main, w_f, z, zf, b, ccol, crow, bounds, yatt, lse, wc, ycat, cv, cols = saved
    s, d = x.shape
    gc, ga = w["g_conv_out"][l], w["g_att_out"][l]
    nh = ga.shape[0] // HEAD_DIM
    zw = w_main.shape[1]
    t = cols["t"]
    dycat = _matmul(dy, w["w_mix_out"][l], "nt", F32, tm=1024, tn=1024, tk=d, name=f"{tag}_dycat")
    dw_out = _matmul(ycat, dy, "tn", F32, tm=1024, tn=1024, tk=1024, name=f"{tag}_dwout")
    dz_conv, dyatt, dwc, dgc, dga = _mixpost_bwd(dycat, z, yatt, cv, wc, gc, ga, name=f"{tag}_dpost")
    dq, delta, dcq = _fattn_dq(z, dyatt, yatt, lse, ccol, crow, bounds, name=f"{tag}_attn_dq", **cols)
    dk, dv, dck = _fattn_dkv(z, dyatt.astype(BF16), _tokens_on_lanes(lse, t), _tokens_on_lanes(delta, t), ccol, crow,
                             bounds, name=f"{tag}_attn_dkv", **cols)
    def heads_on_lanes(v):
        return jnp.pad(v.transpose(1, 0, 2).reshape(s, nh), ((0, 0), (0, LANES - nh)))

    dzf, db = _cumsum_bwd(heads_on_lanes(dcq), heads_on_lanes(dck), zf, b, name=f"{tag}_dcumsum")
    dz = jnp.concatenate([dz_conv, dq, dk, dv], axis=1)
    dw_main = _matmul(h, dz, "tn", F32, tm=1024, tn=512, tk=2048, name=f"{tag}_dwin")
    dw_f = _matmul(h, dzf, "tn", F32, tm=1024, tn=LANES, tk=2048, name=f"{tag}_dwin_f")
    dh = _matmul(dz, w_main, "nt", F32, tm=1024, tn=1024, tk=zw // 2, name=f"{tag}_dh")
    dh = _matmul(dzf, w_f, "nt", F32, tm=1024, tn=1024, tk=LANES, residual=dh, name=f"{tag}_dh_f")
    dx, dg = _rmsnorm_bwd(x, w["g_mix"][l], dh, dy, name=f"{tag}_dnorm")
    grads = dict(g_mix=dg, w_mix_in=jnp.concatenate([dw_main, dw_f[:, :nh]], axis=1), w_conv=dwc[:CONV_K], b_f=db[0, :nh],
                 g_conv_out=dgc[0], g_att_out=dga[0], w_mix_out=dw_out)
    return dx, grads


def _xattn_block_fwd(x, mem, w, l, tag):
    d = x.shape[1]
    h = _rmsnorm_fwd(x, w["g_xattn"][l], name=f"{tag}_norm")
    mn = _rmsnorm_fwd(mem, w["g_mem"][l], name=f"{tag}_mem_norm")
    q = _matmul(h, w["w_xq"][l], "nn", BF16, tm=1024, tn=1024, tk=d, name=f"{tag}_q")
    kv = _matmul(mn, w["w_xkv"][l], "nn", BF16, tm=1024, tn=1024, tk=d, name=f"{tag}_kv")
    o = _xattn_fwd(q, kv, name=f"{tag}_core")
    y = _matmul(o, w["w_xo"][l], "nn", F32, tm=1024, tn=1024, tk=d, residual=x, name=f"{tag}_o")
    return y, (x, h, mn, q, kv, o)


def _xattn_block_bwd(dy, saved, mem, w, l, tag):
    x, h, mn, q, kv, o = saved
    d = x.shape[1]
    do = _matmul(dy, w["w_xo"][l], "nt", BF16, tm=1024, tn=1024, tk=d, name=f"{tag}_do")
    dw_xo = _matmul(o, dy, "tn", F32, tm=1024, tn=1024, tk=1024, name=f"{tag}_dwo")
    dq, dkv = _xattn_bwd(q, kv, do, name=f"{tag}_dcore")
    dw_xq = _matmul(h, dq, "tn", F32, tm=1024, tn=1024, tk=2048, name=f"{tag}_dwq")
    dh = _matmul(dq, w["w_xq"][l], "nt", F32, tm=1024, tn=1024, tk=d, name=f"{tag}_dh")
    dx, dg = _rmsnorm_bwd(x, w["g_xattn"][l], dh, dy, name=f"{tag}_dnorm")
    dw_xkv = _matmul(mn, dkv, "tn", F32, tm=1024, tn=1024, tk=1024, name=f"{tag}_dwkv")
    dmn = _matmul(dkv, w["w_xkv"][l], "nt", F32, tm=1024, tn=1024, tk=1024, name=f"{tag}_dmem")
    _, dg_mem = _rmsnorm_bwd(mem, w["g_mem"][l], dmn, None, name=f"{tag}_dmem_norm")
    return dx, dict(g_xattn=dg, g_mem=dg_mem, w_xq=dw_xq, w_xkv=dw_xkv, w_xo=dw_xo)


def kernel(x, mem, g_ffn1, w_ffn1_gu, w_ffn1_down, g_mix, w_mix_in, w_conv, b_f, g_conv_out, g_att_out, w_mix_out, g_xattn, g_mem, w_xq, w_xkv, w_xo, g_ffn2, w_ffn2_gu, w_ffn2_down, g_final, loss_target, m_g_ffn1, m_w_ffn1_gu, m_w_ffn1_down, m_g_mix, m_w_mix_in, m_w_conv, m_b_f, m_g_conv_out, m_g_att_out, m_w_mix_out, m_g_xattn, m_g_mem, m_w_xq, m_w_xkv, m_w_xo, m_g_ffn2, m_w_ffn2_gu, m_w_ffn2_down, m_g_final, v_g_ffn1, v_w_ffn1_gu, v_w_ffn1_down, v_g_mix, v_w_mix_in, v_w_conv, v_b_f, v_g_conv_out, v_g_att_out, v_w_mix_out, v_g_xattn, v_g_mem, v_w_xq, v_w_xkv, v_w_xo, v_g_ffn2, v_w_ffn2_gu, v_w_ffn2_down, v_g_final):
    local = dict(zip(WEIGHTS, (g_ffn1, w_ffn1_gu, w_ffn1_down, g_mix, w_mix_in, w_conv, b_f, g_conv_out, g_att_out, w_mix_out,
                               g_xattn, g_mem, w_xq, w_xkv, w_xo, g_ffn2, w_ffn2_gu, w_ffn2_down, g_final)))
    mom1 = dict(zip(WEIGHTS, (m_g_ffn1, m_w_ffn1_gu, m_w_ffn1_down, m_g_mix, m_w_mix_in, m_w_conv, m_b_f, m_g_conv_out,
                              m_g_att_out, m_w_mix_out, m_g_xattn, m_g_mem, m_w_xq, m_w_xkv, m_w_xo, m_g_ffn2, m_w_ffn2_gu,
                              m_w_ffn2_down, m_g_final)))
    mom2 = dict(zip(WEIGHTS, (v_g_ffn1, v_w_ffn1_gu, v_w_ffn1_down, v_g_mix, v_w_mix_in, v_w_conv, v_b_f, v_g_conv_out,
                              v_g_att_out, v_w_mix_out, v_g_xattn, v_g_mem, v_w_xq, v_w_xkv, v_w_xo, v_g_ffn2, v_w_ffn2_gu,
                              v_w_ffn2_down, v_g_final)))
    depth = g_ffn1.shape[0]
    s, d = x.shape[1], x.shape[2]
    w = dict(local)
    w.update(_gather_weights({name: local[name] for name, _ in SHARDED}))

    xs = x.reshape(s, d)
    mems = mem.reshape(mem.shape[1], d)
    saved = []
    for l in range(depth):
        xs, s1 = _ffn_fwd(xs, w["g_ffn1"][l], w["w_ffn1_gu"][l], w["w_ffn1_down"][l], f"l{l}_ffn1")
        xs, s2 = _mix_fwd(xs, w, l, f"l{l}_mix")
        xs, s3 = _xattn_block_fwd(xs, mems, w, l, f"l{l}_xattn")
        xs, s4 = _ffn_fwd(xs, w["g_ffn2"][l], w["w_ffn2_gu"][l], w["w_ffn2_down"][l], f"l{l}_ffn2")
        saved.append((s1, s2, s3, s4))

    dx, dg_final, sq = _final_loss(xs, g_final, loss_target.reshape(s, d), name="loss_head")
    loss = lax.psum(jnp.sum(sq) * (0.5 / d), MESH_AXES)

    per_layer = []
    for l in reversed(range(depth)):
        s1, s2, s3, s4 = saved[l]
        grads = {}
        dx, grads["g_ffn2"], grads["w_ffn2_gu"], grads["w_ffn2_down"] = _ffn_bwd(
            dx, s4, w["g_ffn2"][l], w["w_ffn2_gu"][l], w["w_ffn2_down"][l], f"l{l}_ffn2")
        dx, g3 = _xattn_block_bwd(dx, s3, mems, w, l, f"l{l}_xattn")
        dx, g2 = _mix_bwd(dx, s2, w, l, f"l{l}_mix")
        dx, grads["g_ffn1"], grads["w_ffn1_gu"], grads["w_ffn1_down"] = _ffn_bwd(
            dx, s1, w["g_ffn1"][l], w["w_ffn1_gu"][l], w["w_ffn1_down"][l], f"l{l}_ffn1")
        grads.update(g2)
        grads.update(g3)
        per_layer.append(grads)
    per_layer.reverse()
    grads = {name: jnp.stack([per_layer[l][name] for l in range(depth)]) for name in WEIGHTS if name != "g_final"}
    grads["g_final"] = dg_final.reshape(d)

    reduced = _reduce_gradients(grads, lax.axis_index("c"))
    deltas, new_m, new_v = {}, {}, {}
    for name in WEIGHTS:
        deltas[name], new_m[name], new_v[name] = _adamw(local[name], reduced[name], mom1[name], mom2[name], name=f"adamw_{name}")
    return (loss, dx.reshape(x.shape), *[reduced[n] for n in WEIGHTS], *[deltas[n] for n in WEIGHTS],
            *[new_m[n] for n in WEIGHTS], *[new_v[n] for n in WEIGHTS])
```

```python
import functools

import jax
import jax.numpy as jnp
from jax import lax
from jax.experimental import pallas as pl
from jax.experimental.pallas import tpu as pltpu

F32 = jnp.float32
BF16 = jnp.bfloat16

EPS = 1e-6
HEAD_DIM = 64
LANES = 128
N_XHEADS = 4
CONV_K = 3
ADAM_LR, ADAM_B1, ADAM_B2, ADAM_EPS, ADAM_WD, ADAM_STEP = 0.001, 0.9, 0.999, 1e-08, 0.01, 10
VMEM_LIMIT_BYTES = 56 * 1024 * 1024
NEG_BIG = -1e30
MESH_AXES = ("x", "y", "c")
N_CHIPS = 4


def _params(*sem):
    return pltpu.CompilerParams(dimension_semantics=sem, vmem_limit_bytes=VMEM_LIMIT_BYTES)


_DIMS = {"nn": (((1,), (0,)), ((), ())), "nt": (((1,), (1,)), ((), ())), "tn": (((0,), (0,)), ((), ()))}


def _matmul(a, b, mode, out_dtype, *, tm, tn, tk, name, alpha=1.0, residual=None):
    if mode == "nn":
        (m, k), (k2, n) = a.shape, b.shape
    elif mode == "nt":
        (m, k), (n, k2) = a.shape, b.shape
    else:
        (k, m), (k2, n) = a.shape, b.shape
    assert k == k2, (a.shape, b.shape, mode)
    tm, tn, tk = min(tm, m), min(tn, n), min(tk, k)
    assert m % tm == 0 and n % tn == 0 and k % tk == 0, (m, n, k, tm, tn, tk)
    nk = k // tk
    dims = _DIMS[mode]

    def body(*refs):
        if residual is None:
            a_ref, b_ref, o_ref, *scratch = refs
            r_ref = None
        else:
            a_ref, b_ref, r_ref, o_ref, *scratch = refs
        prod = lax.dot_general(a_ref[...].astype(BF16), b_ref[...].astype(BF16), dims, preferred_element_type=F32)

        def finish(acc):
            if alpha != 1.0:
                acc = acc * alpha
            if r_ref is not None:
                acc = acc + r_ref[...].astype(F32)
            o_ref[...] = acc.astype(o_ref.dtype)

        if nk == 1:
            finish(prod)
        else:
            acc_ref = scratch[0]
            kk = pl.program_id(2)

            @pl.when(kk == 0)
            def _():
                acc_ref[...] = prod

            @pl.when(kk > 0)
            def _():
                acc_ref[...] += prod

            @pl.when(kk == nk - 1)
            def _():
                finish(acc_ref[...])

    if mode == "nn":
        a_spec = pl.BlockSpec((tm, tk), lambda i, j, kk: (i, kk))
        b_spec = pl.BlockSpec((tk, tn), lambda i, j, kk: (kk, j))
    elif mode == "nt":
        a_spec = pl.BlockSpec((tm, tk), lambda i, j, kk: (i, kk))
        b_spec = pl.BlockSpec((tn, tk), lambda i, j, kk: (j, kk))
    else:
        a_spec = pl.BlockSpec((tk, tm), lambda i, j, kk: (kk, i))
        b_spec = pl.BlockSpec((tk, tn), lambda i, j, kk: (kk, j))
    o_spec = pl.BlockSpec((tm, tn), lambda i, j, kk: (i, j))
    in_specs, args = [a_spec, b_spec], [a, b]
    if residual is not None:
        in_specs.append(o_spec)
        args.append(residual)
    return pl.pallas_call(
        body, name=name, grid=(m // tm, n // tn, nk), in_specs=in_specs, out_specs=o_spec,
        out_shape=jax.ShapeDtypeStruct((m, n), out_dtype),
        scratch_shapes=[pltpu.VMEM((tm, tn), F32)] if nk > 1 else [],
        compiler_params=_params("parallel", "parallel", "arbitrary"),
    )(*args)


def _row_tile(rows, want):
    t = min(rows, want)
    assert rows % t == 0, (rows, t)
    return t


def _rmsnorm_fwd(x, g, *, name, tr=1024):
    s, d = x.shape
    tr = _row_tile(s, tr)

    def body(x_ref, g_ref, o_ref):
        xf = x_ref[...]
        r = lax.rsqrt(jnp.mean(xf * xf, axis=-1, keepdims=True) + EPS)
        o_ref[...] = (xf * r * g_ref[...]).astype(o_ref.dtype)

    return pl.pallas_call(
        body, name=name, grid=(s // tr,),
        in_specs=[pl.BlockSpec((tr, d), lambda i: (i, 0)), pl.BlockSpec((1, d), lambda i: (0, 0))],
        out_specs=pl.BlockSpec((tr, d), lambda i: (i, 0)),
        out_shape=jax.ShapeDtypeStruct((s, d), BF16),
        compiler_params=_params("parallel"),
    )(x, g.reshape(1, d))


def _rmsnorm_bwd(x, g, dh, dres, *, name, tr=512):
    s, d = x.shape
    tr = _row_tile(s, tr)

    def body(x_ref, g_ref, dh_ref, *rest):
        if dres is None:
            dx_ref, dg_ref = rest
        else:
            dres_ref, dx_ref, dg_ref = rest
        xf = x_ref[...]
        r = lax.rsqrt(jnp.mean(xf * xf, axis=-1, keepdims=True) + EPS)
        xhat = xf * r
        dhf = dh_ref[...].astype(F32)
        dxhat = dhf * g_ref[...]
        dx = r * (dxhat - xhat * jnp.mean(dxhat * xhat, axis=-1, keepdims=True))
        if dres is not None:
            dx = dx + dres_ref[...]
        dx_ref[...] = dx

        @pl.when(pl.program_id(0) == 0)
        def _():
            dg_ref[...] = jnp.zeros_like(dg_ref)

        dg_ref[...] += jnp.sum(dhf * xhat, axis=0, keepdims=True)

    row = pl.BlockSpec((tr, d), lambda i: (i, 0))
    vec = pl.BlockSpec((1, d), lambda i: (0, 0))
    in_specs, args = [row, vec, row], [x, g.reshape(1, d), dh]
    if dres is not None:
        in_specs.append(row)
        args.append(dres)
    dx, dg = pl.pallas_call(
        body, name=name, grid=(s // tr,), in_specs=in_specs, out_specs=[row, vec],
        out_shape=[jax.ShapeDtypeStruct((s, d), F32), jax.ShapeDtypeStruct((1, d), F32)],
        compiler_params=_params("arbitrary"),
    )(*args)
    return dx, dg.reshape(d)


def _swiglu_fwd(gu, *, name, tr=512):
    s, f2 = gu.shape
    f = f2 // 2
    tr = _row_tile(s, tr)

    def body(gu_ref, a_ref):
        gate = gu_ref[:, :f].astype(F32)
        up = gu_ref[:, f:].astype(F32)
        a_ref[...] = (gate * jax.nn.sigmoid(gate) * up).astype(a_ref.dtype)

    return pl.pallas_call(
        body, name=name, grid=(s // tr,),
        in_specs=[pl.BlockSpec((tr, f2), lambda i: (i, 0))],
        out_specs=pl.BlockSpec((tr, f), lambda i: (i, 0)),
        out_shape=jax.ShapeDtypeStruct((s, f), BF16),
        compiler_params=_params("parallel"),
    )(gu)


def _swiglu_bwd(da, gu, *, name, tr=512):
    s, f2 = gu.shape
    f = f2 // 2
    tr = _row_tile(s, tr)

    def body(da_ref, gu_ref, dgu_ref):
        gate = gu_ref[:, :f].astype(F32)
        up = gu_ref[:, f:].astype(F32)
        daf = da_ref[...].astype(F32)
        sig = jax.nn.sigmoid(gate)
        silu = gate * sig
        dgu_ref[:, :f] = (daf * up * (sig + silu * (1.0 - sig))).astype(dgu_ref.dtype)
        dgu_ref[:, f:] = (daf * silu).astype(dgu_ref.dtype)

    return pl.pallas_call(
        body, name=name, grid=(s // tr,),
        in_specs=[pl.BlockSpec((tr, f), lambda i: (i, 0)), pl.BlockSpec((tr, f2), lambda i: (i, 0))],
        out_specs=pl.BlockSpec((tr, f2), lambda i: (i, 0)),
        out_shape=jax.ShapeDtypeStruct((s, f2), BF16),
        compiler_params=_params("parallel"),
    )(da, gu)


_NT = (((1,), (1,)), ((), ()))
_NN = (((1,), (0,)), ((), ()))
_TN = (((0,), (0,)), ((), ()))
_QK_SCALE = HEAD_DIM ** -0.5


def _dot(a, b, dims):
    return lax.dot_general(a, b, dims, preferred_element_type=F32)


def _split_heads(v2, is_a):
    zero = jnp.zeros_like(v2)
    return jnp.where(is_a, v2, zero), jnp.where(is_a, zero, v2)


SKIP_BELOW = 107.0
_SMEM = pl.BlockSpec(memory_space=pltpu.SMEM)


def _attn_bounds(z, c, *, qcol_units, kcol_units, aw, t, name):
    s = z.shape[0]
    nq = s // t
    nh = aw // HEAD_DIM

    def body(q_ref, k_ref, o_ref):
        d = lax.broadcasted_iota(jnp.int32, (aw, LANES), 0)
        hh = lax.broadcasted_iota(jnp.int32, (aw, LANES), 1)
        onehot = ((d >= hh * HEAD_DIM) & (d < (hh + 1) * HEAD_DIM)).astype(F32)
        for r, ref in enumerate((q_ref, k_ref)):
            v = ref[...].astype(F32)
            sq = lax.dot_general(v * v, onehot, _NN, precision=lax.Precision.HIGHEST, preferred_element_type=F32)
            o_ref[r:r + 1, :] = jnp.max(sq, axis=0, keepdims=True)
        o_ref[2:, :] = jnp.zeros((HALO - 2, LANES), F32)

    sq = pl.pallas_call(
        body, name=name, grid=(nq,),
        in_specs=[pl.BlockSpec((t, aw), lambda i: (i, qcol_units)), pl.BlockSpec((t, aw), lambda i: (i, kcol_units))],
        out_specs=pl.BlockSpec((None, HALO, LANES), lambda i: (i, 0, 0)),
        out_shape=jax.ShapeDtypeStruct((nq, HALO, LANES), F32),
        compiler_params=_params("parallel"),
    )(z, z)
    norms = jnp.sqrt(sq[:, :2, :nh]) * 1.01
    qn = (norms[:, 0, :] * _QK_SCALE).T.reshape(-1)
    kn = norms[:, 1, :].T.reshape(-1)
    cs = c[0::t, :nh].T.reshape(-1)
    ce = c[t - 1::t, :nh].T.reshape(-1)
    return qn, kn, cs, ce


def _block_active(bounds, head, i, j, nq):
    qn_ref, kn_ref, cs_ref, ce_ref = bounds
    qi = qn_ref[head * nq + i]
    upper = qi * kn_ref[head * nq + j] + (cs_ref[head * nq + i] - ce_ref[head * nq + j])
    lower = -(qi * kn_ref[head * nq + i])
    return upper - lower > -SKIP_BELOW


def _fattn_fwd(z, ccol, crow, bounds, *, qcol, kcol, vcol, t, name):
    s = z.shape[0]
    npairs, nq = crow.shape[0], crow.shape[1]
    assert nq * t == s

    def body(qn_ref, kn_ref, cs_ref, ce_ref, q_ref, k_ref, v_ref, cc_ref, cr_ref, o_ref, lse_ref, m_scr, l_scr, acc_scr):
        pair, i = pl.program_id(0), pl.program_id(1)
        is_a = lax.broadcasted_iota(jnp.int32, (t, LANES), 1) < HEAD_DIM
        qh = _split_heads(q_ref[...] * jnp.asarray(_QK_SCALE, BF16), is_a)
        cq = (cc_ref[:, 0:1], cc_ref[:, 1:2])
        m_scr[...] = jnp.full(m_scr.shape, NEG_BIG, F32)
        l_scr[...] = jnp.zeros(l_scr.shape, F32)
        acc_scr[...] = jnp.zeros(acc_scr.shape, F32)

        def head_step(h, j, diagonal):
            off = pl.multiple_of(j * t, t)
            k2 = k_ref[pl.ds(off, t), :]
            vh = _split_heads(v_ref[pl.ds(off, t), :], is_a)[h]
            sc = _dot(qh[h], k2, _NT) + (cq[h] - cr_ref[j][h:h + 1, :])
            if diagonal:
                row = lax.broadcasted_iota(jnp.int32, (t, t), 0)
                col = lax.broadcasted_iota(jnp.int32, (t, t), 1)
                sc = jnp.where(row >= col, sc, NEG_BIG)
            m_old = m_scr[h]
            m_new = jnp.maximum(m_old, jnp.max(sc, axis=1, keepdims=True))
            alpha = jnp.exp(m_old - m_new)
            p = jnp.exp(sc - m_new)
            l_scr[h] = alpha * l_scr[h] + jnp.sum(p, axis=1, keepdims=True)
            m_scr[h] = m_new
            acc_scr[h] = acc_scr[h] * alpha + _dot(p.astype(BF16), vh, _NN)

        def loop_body(j, carry):
            for h in range(2):
                pl.when(_block_active((qn_ref, kn_ref, cs_ref, ce_ref), 2 * pair + h, i, j, nq))(
                    functools.partial(head_step, h, j, False))
            return carry

        lax.fori_loop(0, i, loop_body, 0)
        for h in range(2):
            head_step(h, i, True)
        o_ref[...] = jnp.where(is_a, acc_scr[0] / l_scr[0], acc_scr[1] / l_scr[1])
        lse_ref[:, 0:1] = m_scr[0] + jnp.log(l_scr[0])
        lse_ref[:, 1:2] = m_scr[1] + jnp.log(l_scr[1])

    col_vec = pl.BlockSpec((None, t, 2), lambda p, i: (p, i, 0))
    return pl.pallas_call(
        body, name=name, grid=(npairs, nq),
        in_specs=[
            _SMEM, _SMEM, _SMEM, _SMEM,
            pl.BlockSpec((t, LANES), lambda p, i: (i, qcol + p)),
            pl.BlockSpec((s, LANES), lambda p, i: (0, kcol + p)),
            pl.BlockSpec((s, LANES), lambda p, i: (0, vcol + p)),
            col_vec,
            pl.BlockSpec((None, nq, 2, t), lambda p, i: (p, 0, 0, 0)),
        ],
        out_specs=[pl.BlockSpec((t, LANES), lambda p, i: (i, p)), col_vec],
        out_shape=[jax.ShapeDtypeStruct((s, npairs * LANES), F32), jax.ShapeDtypeStruct((npairs, s, 2), F32)],
        scratch_shapes=[pltpu.VMEM((2, t, 1), F32), pltpu.VMEM((2, t, 1), F32), pltpu.VMEM((2, t, LANES), F32)],
        compiler_params=_params("parallel", "arbitrary"),
    )(*bounds, z, z, z, ccol, crow)


def _fattn_dq(z, dy, y, lse, ccol, crow, bounds, *, qcol, kcol, vcol, t, name):
    s = z.shape[0]
    npairs, nq = crow.shape[0], crow.shape[1]

    def body(qn_ref, kn_ref, cs_ref, ce_ref, q_ref, k_ref, v_ref, dy_ref, y_ref, lse_ref, cc_ref, cr_ref,
             dq_ref, delta_ref, dcq_ref, acc_scr, rs_scr):
        pair, i = pl.program_id(0), pl.program_id(1)
        is_a = lax.broadcasted_iota(jnp.int32, (t, LANES), 1) < HEAD_DIM
        qh = _split_heads(q_ref[...] * jnp.asarray(_QK_SCALE, BF16), is_a)
        cq = (cc_ref[:, 0:1], cc_ref[:, 1:2])
        lse = (lse_ref[:, 0:1], lse_ref[:, 1:2])
        dyf = dy_ref[...]
        prod = _split_heads(dyf * y_ref[...], is_a)
        delta = (jnp.sum(prod[0], axis=1, keepdims=True), jnp.sum(prod[1], axis=1, keepdims=True))
        delta_ref[:, 0:1] = delta[0]
        delta_ref[:, 1:2] = delta[1]
        dyh = _split_heads(dyf.astype(BF16), is_a)
        acc_scr[...] = jnp.zeros(acc_scr.shape, F32)
        rs_scr[...] = jnp.zeros(rs_scr.shape, F32)

        def head_step(h, j, diagonal):
            off = pl.multiple_of(j * t, t)
            k2 = k_ref[pl.ds(off, t), :]
            v2 = v_ref[pl.ds(off, t), :]
            sc = _dot(qh[h], k2, _NT) + (cq[h] - cr_ref[j][h:h + 1, :])
            if diagonal:
                row = lax.broadcasted_iota(jnp.int32, (t, t), 0)
                col = lax.broadcasted_iota(jnp.int32, (t, t), 1)
                sc = jnp.where(row >= col, sc, NEG_BIG)
            p = jnp.exp(sc - lse[h])
            dp = _dot(dyh[h], v2, _NT)
            ds = p * (dp - delta[h])
            rs_scr[h] += jnp.sum(ds, axis=1, keepdims=True)
            acc_scr[...] += _dot(ds.astype(BF16), _split_heads(k2, is_a)[h], _NN)

        def loop_body(j, carry):
            for h in range(2):
                pl.when(_block_active((qn_ref, kn_ref, cs_ref, ce_ref), 2 * pair + h, i, j, nq))(
                    functools.partial(head_step, h, j, False))
            return carry

        lax.fori_loop(0, i, loop_body, 0)
        for h in range(2):
            head_step(h, i, True)
        dq_ref[...] = (acc_scr[...] * _QK_SCALE).astype(dq_ref.dtype)
        dcq_ref[:, 0:1] = rs_scr[0]
        dcq_ref[:, 1:2] = rs_scr[1]

    col_vec = pl.BlockSpec((None, t, 2), lambda p, i: (p, i, 0))
    tile = pl.BlockSpec((t, LANES), lambda p, i: (i, p))
    vec_shape = jax.ShapeDtypeStruct((npairs, s, 2), F32)
    return pl.pallas_call(
        body, name=name, grid=(npairs, nq),
        in_specs=[
            _SMEM, _SMEM, _SMEM, _SMEM,
            pl.BlockSpec((t, LANES), lambda p, i: (i, qcol + p)),
            pl.BlockSpec((s, LANES), lambda p, i: (0, kcol + p)),
            pl.BlockSpec((s, LANES), lambda p, i: (0, vcol + p)),
            tile, tile, col_vec, col_vec,
            pl.BlockSpec((None, nq, 2, t), lambda p, i: (p, 0, 0, 0)),
        ],
        out_specs=[tile, col_vec, col_vec],
        out_shape=[jax.ShapeDtypeStruct((s, npairs * LANES), BF16), vec_shape, vec_shape],
        scratch_shapes=[pltpu.VMEM((t, LANES), F32), pltpu.VMEM((2, t, 1), F32)],
        compiler_params=_params("parallel", "arbitrary"),
    )(*bounds, z, z, z, dy, y, lse, ccol, crow)


def _fattn_dkv(z, dyb, lse_row, delta_row, ccol, crow, bounds, *, qcol, kcol, vcol, t, name):
    s = z.shape[0]
    npairs, nq = crow.shape[0], crow.shape[1]

    def body(qn_ref, kn_ref, cs_ref, ce_ref, k_ref, v_ref, q_ref, dy_ref, lse_ref, dl_ref, cc_ref, cr_ref,
             dk_ref, dv_ref, dc_ref, dk_scr, dv_scr, dc_scr):
        pair, j = pl.program_id(0), pl.program_id(1)
        is_a = lax.broadcasted_iota(jnp.int32, (t, LANES), 1) < HEAD_DIM
        kh = _split_heads(k_ref[...], is_a)
        vh = _split_heads(v_ref[...], is_a)
        ck = (cc_ref[:, 0:1], cc_ref[:, 1:2])
        dk_scr[...] = jnp.zeros(dk_scr.shape, F32)
        dv_scr[...] = jnp.zeros(dv_scr.shape, F32)
        dc_scr[...] = jnp.zeros(dc_scr.shape, F32)

        def head_step(h, i, diagonal):
            off = pl.multiple_of(i * t, t)
            q2 = q_ref[pl.ds(off, t), :] * jnp.asarray(_QK_SCALE, BF16)
            dy2 = dy_ref[pl.ds(off, t), :]
            st = _dot(kh[h], q2, _NT) + (cr_ref[i][h:h + 1, :] - ck[h])
            if diagonal:
                row = lax.broadcasted_iota(jnp.int32, (t, t), 0)
                col = lax.broadcasted_iota(jnp.int32, (t, t), 1)
                st = jnp.where(col >= row, st, NEG_BIG)
            pt = jnp.exp(st - lse_ref[i][h:h + 1, :])
            dv_scr[...] += _dot(pt.astype(BF16), _split_heads(dy2, is_a)[h], _NN)
            dpt = _dot(vh[h], dy2, _NT)
            dst = pt * (dpt - dl_ref[i][h:h + 1, :])
            dk_scr[...] += _dot(dst.astype(BF16), _split_heads(q2, is_a)[h], _NN)
            dc_scr[h] += jnp.sum(dst, axis=1, keepdims=True)

        def loop_body(i, carry):
            for h in range(2):
                pl.when(_block_active((qn_ref, kn_ref, cs_ref, ce_ref), 2 * pair + h, i, j, nq))(
                    functools.partial(head_step, h, i, False))
            return carry

        for h in range(2):
            head_step(h, j, True)
        lax.fori_loop(j + 1, nq, loop_body, 0)
        dk_ref[...] = dk_scr[...].astype(dk_ref.dtype)
        dv_ref[...] = dv_scr[...].astype(dv_ref.dtype)
        dc_ref[:, 0:1] = -dc_scr[0]
        dc_ref[:, 1:2] = -dc_scr[1]

    col_vec = pl.BlockSpec((None, t, 2), lambda p, j: (p, j, 0))
    rows = pl.BlockSpec((None, nq, 2, t), lambda p, j: (p, 0, 0, 0))
    tile = pl.BlockSpec((t, LANES), lambda p, j: (j, p))
    return pl.pallas_call(
        body, name=name, grid=(npairs, nq),
        in_specs=[
            _SMEM, _SMEM, _SMEM, _SMEM,
            pl.BlockSpec((t, LANES), lambda p, j: (j, kcol + p)),
            pl.BlockSpec((t, LANES), lambda p, j: (j, vcol + p)),
            pl.BlockSpec((s, LANES), lambda p, j: (0, qcol + p)),
            pl.BlockSpec((s, LANES), lambda p, j: (0, p)),
            rows, rows, col_vec, rows,
        ],
        out_specs=[tile, tile, col_vec],
        out_shape=[jax.ShapeDtypeStruct((s, npairs * LANES), BF16), jax.ShapeDtypeStruct((s, npairs * LANES), BF16),
                   jax.ShapeDtypeStruct((npairs, s, 2), F32)],
        scratch_shapes=[pltpu.VMEM((t, LANES), F32), pltpu.VMEM((t, LANES), F32), pltpu.VMEM((2, t, 1), F32)],
        compiler_params=_params("parallel", "arbitrary"),
    )(*bounds, z, z, z, dyb, lse_row, delta_row, ccol, crow)


def _log_sigmoid(x):
    return jnp.minimum(x, 0.0) - jnp.log(1.0 + jnp.exp(-jnp.abs(x)))


def _cumsum_fwd(zf, b, *, name, t=512):
    s, w = zf.shape
    t = _row_tile(s, t)

    def body(zf_ref, b_ref, c_ref, carry):
        @pl.when(pl.program_id(0) == 0)
        def _():
            carry[...] = jnp.zeros(carry.shape, F32)

        lf = _log_sigmoid(zf_ref[...] + b_ref[...])
        row = lax.broadcasted_iota(jnp.int32, (t, t), 0)
        col = lax.broadcasted_iota(jnp.int32, (t, t), 1)
        tri = (row >= col).astype(F32)
        c = lax.dot_general(tri, lf, _NN, precision=lax.Precision.HIGHEST, preferred_element_type=F32) + carry[...]
        c_ref[...] = c
        carry[...] = c[t - 1:t, :]

    return pl.pallas_call(
        body, name=name, grid=(s // t,),
        in_specs=[pl.BlockSpec((t, w), lambda i: (i, 0)), pl.BlockSpec((1, w), lambda i: (0, 0))],
        out_specs=pl.BlockSpec((t, w), lambda i: (i, 0)),
        out_shape=jax.ShapeDtypeStruct((s, w), F32),
        scratch_shapes=[pltpu.VMEM((1, w), F32)],
        compiler_params=_params("arbitrary"),
    )(zf, b)


def _cumsum_bwd(dcq, dck, zf, b, *, name, t=512):
    s, w = zf.shape
    t = _row_tile(s, t)
    nb = s // t

    def body(dcq_ref, dck_ref, zf_ref, b_ref, dzf_ref, db_ref, carry):
        @pl.when(pl.program_id(0) == 0)
        def _():
            carry[...] = jnp.zeros(carry.shape, F32)
            db_ref[...] = jnp.zeros(db_ref.shape, F32)

        row = lax.broadcasted_iota(jnp.int32, (t, t), 0)
        col = lax.broadcasted_iota(jnp.int32, (t, t), 1)
        tri = (row <= col).astype(F32)
        dc = dcq_ref[...] + dck_ref[...]
        dlf = lax.dot_general(tri, dc, _NN, precision=lax.Precision.HIGHEST, preferred_element_type=F32) + carry[...]
        carry[...] = dlf[0:1, :]
        dzf = dlf * jax.nn.sigmoid(-(zf_ref[...] + b_ref[...]))
        dzf_ref[...] = dzf
        db_ref[...] += jnp.sum(dzf, axis=0, keepdims=True)

    blk = pl.BlockSpec((t, w), lambda i: (nb - 1 - i, 0))
    vec = pl.BlockSpec((1, w), lambda i: (0, 0))
    return pl.pallas_call(
        body, name=name, grid=(nb,), in_specs=[blk, blk, blk, vec], out_specs=[blk, vec],
        out_shape=[jax.ShapeDtypeStruct((s, w), F32), jax.ShapeDtypeStruct((1, w), F32)],
        scratch_shapes=[pltpu.VMEM((1, w), F32)],
        compiler_params=_params("arbitrary"),
    )(dcq, dck, zf, b)


HALO = 8


def _rms_rows(v):
    return lax.rsqrt(jnp.mean(v * v, axis=-1, keepdims=True) + EPS)


def _mixpost_fwd(z, yatt, wconv, gc, ga, *, name, tr=512):
    s = z.shape[0]
    cw, aw = gc.shape[-1], ga.shape[-1]
    tr = _row_tile(s, tr)

    def body(zb_ref, zc_ref, zv_ref, ya_ref, w_ref, gc_ref, ga_ref, ycat_ref, cv_ref, u_scr):
        @pl.when(pl.program_id(0) == 0)
        def _():
            u_scr[0:HALO, :] = jnp.zeros((HALO, cw), F32)

        u = zc_ref[...].astype(F32) * zv_ref[...].astype(F32)
        u_scr[HALO:HALO + tr, :] = u
        cv = w_ref[0:1, :] * u_scr[HALO - 2:HALO - 2 + tr, :] + w_ref[1:2, :] * u_scr[HALO - 1:HALO - 1 + tr, :] + w_ref[2:3, :] * u
        u_scr[0:HALO, :] = u_scr[tr:tr + HALO, :]
        cv_ref[...] = cv
        yc = zb_ref[...].astype(F32) * cv
        ya = ya_ref[...]
        ycat_ref[:, :cw] = (yc * _rms_rows(yc) * gc_ref[...]).astype(ycat_ref.dtype)
        ycat_ref[:, cw:] = (ya * _rms_rows(ya) * ga_ref[...]).astype(ycat_ref.dtype)

    return pl.pallas_call(
        body, name=name, grid=(s // tr,),
        in_specs=[
            pl.BlockSpec((tr, cw), lambda i: (i, 0)), pl.BlockSpec((tr, cw), lambda i: (i, 1)),
            pl.BlockSpec((tr, cw), lambda i: (i, 2)), pl.BlockSpec((tr, aw), lambda i: (i, 0)),
            pl.BlockSpec((HALO, cw), lambda i: (0, 0)), pl.BlockSpec((1, cw), lambda i: (0, 0)),
            pl.BlockSpec((1, aw), lambda i: (0, 0)),
        ],
        out_specs=[pl.BlockSpec((tr, cw + aw), lambda i: (i, 0)), pl.BlockSpec((tr, cw), lambda i: (i, 0))],
        out_shape=[jax.ShapeDtypeStruct((s, cw + aw), BF16), jax.ShapeDtypeStruct((s, cw), F32)],
        scratch_shapes=[pltpu.VMEM((tr + HALO, cw), F32)],
        compiler_params=_params("arbitrary"),
    )(z, z, z, yatt, wconv, gc.reshape(1, cw), ga.reshape(1, aw))


def _mixpost_bwd(dycat, z, yatt, cv, wconv, gc, ga, *, name, tr=512):
    s = z.shape[0]
    cw, aw = gc.shape[-1], ga.shape[-1]
    tr = _row_tile(s, tr)
    nb = s // tr

    def body(dy_ref, zb_ref, zc_ref, zv_ref, ya_ref, cv_ref, w_ref, gc_ref, ga_ref,
             dz_ref, dya_ref, dw_ref, dgc_ref, dga_ref, d_scr):
        @pl.when(pl.program_id(0) == 0)
        def _():
            d_scr[tr:tr + HALO, :] = jnp.zeros((HALO, cw), F32)
            dw_ref[...] = jnp.zeros(dw_ref.shape, F32)
            dgc_ref[...] = jnp.zeros(dgc_ref.shape, F32)
            dga_ref[...] = jnp.zeros(dga_ref.shape, F32)

        zb, zc, zv = zb_ref[...].astype(F32), zc_ref[...].astype(F32), zv_ref[...].astype(F32)
        cvv = cv_ref[...]

        def norm_bwd(v, dn, g):
            r = _rms_rows(v)
            vh = v * r
            dvh = dn * g
            return r * (dvh - vh * jnp.mean(dvh * vh, axis=-1, keepdims=True)), jnp.sum(dn * vh, axis=0, keepdims=True)

        dyc, dgc = norm_bwd(zb * cvv, dy_ref[:, :cw], gc_ref[...])
        dya, dga = norm_bwd(ya_ref[...], dy_ref[:, cw:], ga_ref[...])
        dgc_ref[...] += dgc
        dga_ref[...] += dga
        dya_ref[...] = dya
        dcv = dyc * zb
        d_scr[0:tr, :] = dcv
        d1 = d_scr[1:tr + 1, :]
        d2 = d_scr[2:tr + 2, :]
        du = w_ref[2:3, :] * dcv + w_ref[1:2, :] * d1 + w_ref[0:1, :] * d2
        u = zc * zv
        dw_ref[0:1, :] += jnp.sum(u * d2, axis=0, keepdims=True)
        dw_ref[1:2, :] += jnp.sum(u * d1, axis=0, keepdims=True)
        dw_ref[2:3, :] += jnp.sum(u * dcv, axis=0, keepdims=True)
        d_scr[tr:tr + HALO, :] = d_scr[0:HALO, :]
        dz_ref[:, :cw] = (dyc * cvv).astype(dz_ref.dtype)
        dz_ref[:, cw:2 * cw] = (du * zv).astype(dz_ref.dtype)
        dz_ref[:, 2 * cw:] = (du * zc).astype(dz_ref.dtype)

    def rows(width, colblk=0):
        return pl.BlockSpec((tr, width), lambda i: (nb - 1 - i, colblk))

    def fixed(r, width):
        return pl.BlockSpec((r, width), lambda i: (0, 0))

    return pl.pallas_call(
        body, name=name, grid=(nb,),
        in_specs=[rows(cw + aw), rows(cw, 0), rows(cw, 1), rows(cw, 2), rows(aw), rows(cw),
                  fixed(HALO, cw), fixed(1, cw), fixed(1, aw)],
        out_specs=[rows(3 * cw), rows(aw), fixed(HALO, cw), fixed(1, cw), fixed(1, aw)],
        out_shape=[jax.ShapeDtypeStruct((s, 3 * cw), BF16), jax.ShapeDtypeStruct((s, aw), F32),
                   jax.ShapeDtypeStruct((HALO, cw), F32), jax.ShapeDtypeStruct((1, cw), F32),
                   jax.ShapeDtypeStruct((1, aw), F32)],
        scratch_shapes=[pltpu.VMEM((tr + HALO, cw), F32)],
        compiler_params=_params("arbitrary"),
    )(dycat, z, z, z, yatt, cv, wconv, gc.reshape(1, cw), ga.reshape(1, aw))


def _xattn_fwd(q, kv, *, name, tq=1024):
    s, d = q.shape
    m = kv.shape[0]
    dh = d // N_XHEADS
    scale = dh ** -0.5
    tq = _row_tile(s, tq)

    def body(q_ref, kv_ref, o_ref):
        for h in range(N_XHEADS):
            lo, hi = h * dh, (h + 1) * dh
            sc = _dot(q_ref[:, lo:hi], kv_ref[:, lo:hi], _NT) * scale
            p = jnp.exp(sc - jnp.max(sc, axis=1, keepdims=True))
            o = _dot(p.astype(BF16), kv_ref[:, d + lo:d + hi], _NN) / jnp.sum(p, axis=1, keepdims=True)
            o_ref[:, lo:hi] = o.astype(o_ref.dtype)

    return pl.pallas_call(
        body, name=name, grid=(s // tq,),
        in_specs=[pl.BlockSpec((tq, d), lambda i: (i, 0)), pl.BlockSpec((m, 2 * d), lambda i: (0, 0))],
        out_specs=pl.BlockSpec((tq, d), lambda i: (i, 0)),
        out_shape=jax.ShapeDtypeStruct((s, d), BF16),
        compiler_params=_params("parallel"),
    )(q, kv)


def _xattn_bwd(q, kv, do, *, name, tq=1024):
    s, d = q.shape
    m = kv.shape[0]
    dh = d // N_XHEADS
    scale = dh ** -0.5
    tq = _row_tile(s, tq)

    def body(q_ref, kv_ref, do_ref, dq_ref, dkv_ref):
        @pl.when(pl.program_id(0) == 0)
        def _():
            dkv_ref[...] = jnp.zeros(dkv_ref.shape, F32)

        for h in range(N_XHEADS):
            lo, hi = h * dh, (h + 1) * dh
            qh, kh, vh, doh = q_ref[:, lo:hi], kv_ref[:, lo:hi], kv_ref[:, d + lo:d + hi], do_ref[:, lo:hi]
            sc = _dot(qh, kh, _NT) * scale
            e = jnp.exp(sc - jnp.max(sc, axis=1, keepdims=True))
            p = e / jnp.sum(e, axis=1, keepdims=True)
            dp = _dot(doh, vh, _NT)
            ds = p * (dp - jnp.sum(dp * p, axis=1, keepdims=True))
            dsb = ds.astype(BF16)
            dq_ref[:, lo:hi] = (_dot(dsb, kh, _NN) * scale).astype(dq_ref.dtype)
            dkv_ref[:, lo:hi] += _dot(dsb, qh, _TN) * scale
            dkv_ref[:, d + lo:d + hi] += _dot(p.astype(BF16), doh, _TN)

    return pl.pallas_call(
        body, name=name, grid=(s // tq,),
        in_specs=[pl.BlockSpec((tq, d), lambda i: (i, 0)), pl.BlockSpec((m, 2 * d), lambda i: (0, 0)),
                  pl.BlockSpec((tq, d), lambda i: (i, 0))],
        out_specs=[pl.BlockSpec((tq, d), lambda i: (i, 0)), pl.BlockSpec((m, 2 * d), lambda i: (0, 0))],
        out_shape=[jax.ShapeDtypeStruct((s, d), BF16), jax.ShapeDtypeStruct((m, 2 * d), F32)],
        compiler_params=_params("arbitrary"),
    )(q, kv, do)


def _final_loss(x, g, target, *, name, tr=512):
    s, d = x.shape
    tr = _row_tile(s, tr)

    def body(x_ref, g_ref, t_ref, dx_ref, dg_ref, sq_ref):
        @pl.when(pl.program_id(0) == 0)
        def _():
            dg_ref[...] = jnp.zeros(dg_ref.shape, F32)
            sq_ref[...] = jnp.zeros(sq_ref.shape, F32)

        xf = x_ref[...]
        r = _rms_rows(xf)
        xhat = xf * r
        err = xhat * g_ref[...] - t_ref[...]
        sq_ref[...] += jnp.sum(err * err, axis=0, keepdims=True)
        dy = err * (1.0 / d)
        dg_ref[...] += jnp.sum(dy * xhat, axis=0, keepdims=True)
        dxhat = dy * g_ref[...]
        dx_ref[...] = r * (dxhat - xhat * jnp.mean(dxhat * xhat, axis=-1, keepdims=True))

    row = pl.BlockSpec((tr, d), lambda i: (i, 0))
    vec = pl.BlockSpec((1, d), lambda i: (0, 0))
    return pl.pallas_call(
        body, name=name, grid=(s // tr,), in_specs=[row, vec, row], out_specs=[row, vec, vec],
        out_shape=[jax.ShapeDtypeStruct((s, d), F32), jax.ShapeDtypeStruct((1, d), F32), jax.ShapeDtypeStruct((1, d), F32)],
        compiler_params=_params("arbitrary"),
    )(x, g.reshape(1, d), target)


def _adamw(w, g, m, v, *, name, tr=512):
    shape = w.shape
    cols = shape[-1]
    rows = w.size // cols
    tr = tr if rows % tr == 0 else rows

    def body(w_ref, g_ref, m_ref, v_ref, d_ref, nm_ref, nv_ref):
        gf = g_ref[...]
        nm = ADAM_B1 * m_ref[...] + (1.0 - ADAM_B1) * gf
        nv = ADAM_B2 * v_ref[...] + (1.0 - ADAM_B2) * (gf * gf)
        m_hat = nm / (1.0 - ADAM_B1 ** ADAM_STEP)
        v_hat = nv / (1.0 - ADAM_B2 ** ADAM_STEP)
        d_ref[...] = -ADAM_LR * (m_hat / (jnp.sqrt(v_hat) + ADAM_EPS) + ADAM_WD * w_ref[...])
        nm_ref[...] = nm
        nv_ref[...] = nv

    blk = pl.BlockSpec((tr, cols), lambda i: (i, 0))
    out = jax.ShapeDtypeStruct((rows, cols), F32)
    outs = pl.pallas_call(
        body, name=name, grid=(rows // tr,), in_specs=[blk] * 4, out_specs=[blk] * 3, out_shape=[out] * 3,
        compiler_params=_params("parallel"),
    )(*[t.reshape(rows, cols) for t in (w, g, m, v)])
    return tuple(o.reshape(shape) for o in outs)


_HBM = pl.BlockSpec(memory_space=pl.ANY)
_MESH_ID = pl.DeviceIdType.MESH


def _place():
    x, y, c = (lax.axis_index(a) for a in MESH_AXES)
    return x, y, c, [(1 - x, y), (x, 1 - y), (1 - x, 1 - y)]


def _remote(src, dst, send_sems, recv_sems, k, to):
    return pltpu.make_async_remote_copy(src_ref=src, dst_ref=dst, send_sem=send_sems.at[k], recv_sem=recv_sems.at[k],
                                        device_id=to, device_id_type=_MESH_ID)


def _comm_call(body, arrays, out_shapes, n_remote, name):
    return pl.pallas_call(
        body, name=name, in_specs=[_HBM] * len(arrays), out_specs=[_HBM] * len(out_shapes), out_shape=out_shapes,
        scratch_shapes=[pltpu.SemaphoreType.DMA((n_remote,)), pltpu.SemaphoreType.DMA((n_remote,)),
                        pltpu.SemaphoreType.DMA((len(arrays),))],
    )(*arrays)


def _allgather_weights(halves, *, name):
    n = len(halves)

    def body(*refs):
        w, out = refs[:n], refs[n:2 * n]
        send_sems, recv_sems, local_sems = refs[2 * n:]
        x, y, c, chips = _place()
        me = 2 * x + y
        sibling = (x, y, 1 - c)
        slots = [2 * px + py for px, py in chips]

        def copy(t, k, slot, half, to, src=None):
            dst = out[t].at[slot, half]
            return _remote(dst if src is None else src, dst, send_sems, recv_sems, 7 * t + k, to)

        local = [pltpu.make_async_copy(w[t].at[c], out[t].at[me, c], local_sems.at[t]) for t in range(n)]
        first = []
        for t in range(n):
            local[t].start()
            first.append(copy(t, 0, me, c, sibling, src=w[t].at[c]))
            first += [copy(t, 1 + j, me, c, (px, py, c), src=w[t].at[c]) for j, (px, py) in enumerate(chips)]
        for cp in first:
            cp.start()
        passed = []
        for j, (px, py) in enumerate(chips):
            for t in range(n):
                copy(t, 1 + j, slots[j], c, (px, py, c)).wait_recv()
                passed.append(copy(t, 4 + j, slots[j], c, sibling))
                passed[-1].start()
        for t in range(n):
            copy(t, 0, me, 1 - c, sibling).wait_recv()
            for j in range(3):
                copy(t, 4 + j, slots[j], 1 - c, sibling).wait_recv()
        for cp in first + passed:
            cp.wait_send()
        for cp in local:
            cp.wait()

    shapes = [jax.ShapeDtypeStruct((N_CHIPS,) + h.shape, h.dtype) for h in halves]
    return _comm_call(body, halves, shapes, 7 * n, name)


def _swap_halves(gs, *, name):
    n = len(gs)

    def body(*refs):
        g, out = refs[:n], refs[n:2 * n]
        send_sems, recv_sems, _ = refs[2 * n:]
        x, y, c, _ = _place()
        copies = [_remote(g[t].at[1 - c], out[t], send_sems, recv_sems, t, (x, y, 1 - c)) for t in range(n)]
        for cp in copies:
            cp.start()
        for cp in copies:
            cp.wait()

    return _comm_call(body, gs, [jax.ShapeDtypeStruct(g.shape[1:], g.dtype) for g in gs], n, name)


def _scatter_chips(ps, *, name):
    n = len(ps)

    def body(*refs):
        p, out = refs[:n], refs[n:2 * n]
        send_sems, recv_sems, local_sems = refs[2 * n:]
        x, y, c, chips = _place()
        me = 2 * x + y
        local = [pltpu.make_async_copy(p[t].at[me], out[t].at[me], local_sems.at[t]) for t in range(n)]
        sends = [_remote(p[t].at[2 * px + py], out[t].at[me], send_sems, recv_sems, 3 * t + j, (px, py, c))
                 for t in range(n) for j, (px, py) in enumerate(chips)]
        for cp in local + sends:
            cp.start()
        for t in range(n):
            for j, (px, py) in enumerate(chips):
                slot = out[t].at[2 * px + py]
                _remote(slot, slot, send_sems, recv_sems, 3 * t + j, (px, py, c)).wait_recv()
        for cp in sends:
            cp.wait_send()
        for cp in local:
            cp.wait()

    return _comm_call(body, ps, [jax.ShapeDtypeStruct(p.shape, p.dtype) for p in ps], 3 * n, name)


def _join_halves(rs, *, name):
    n = len(rs)

    def body(*refs):
        r, out = refs[:n], refs[n:2 * n]
        send_sems, recv_sems, local_sems = refs[2 * n:]
        x, y, c, _ = _place()
        local = [pltpu.make_async_copy(r[t], out[t].at[c], local_sems.at[t]) for t in range(n)]
        sends = [_remote(r[t], out[t].at[c], send_sems, recv_sems, t, (x, y, 1 - c)) for t in range(n)]
        for cp in local + sends:
            cp.start()
        for t in range(n):
            slot = out[t].at[1 - c]
            _remote(slot, slot, send_sems, recv_sems, t, (x, y, 1 - c)).wait_recv()
        for cp in sends:
            cp.wait_send()
        for cp in local:
            cp.wait()

    return _comm_call(body, rs, [jax.ShapeDtypeStruct((2,) + r.shape, r.dtype) for r in rs], n, name)


def _pick_rows(rows, cap=512):
    for t in range(min(rows, cap), 0, -1):
        if rows % t == 0 and t % 16 == 0:
            return t
    return rows


def _add_halves(g, recv, c, out_dtype, *, name):
    cols = g.shape[-1]
    rows = recv.size // (N_CHIPS * cols)
    tr = _pick_rows(rows)

    def body(c_ref, g_ref, r_ref, o_ref):
        o_ref[...] = (g_ref[...] + r_ref[...]).astype(o_ref.dtype)

    blk = pl.BlockSpec((None, tr, cols), lambda b, i, c_ref: (b, i, 0))
    out = pl.pallas_call(
        body, name=name,
        grid_spec=pltpu.PrefetchScalarGridSpec(
            num_scalar_prefetch=1, grid=(N_CHIPS, rows // tr),
            in_specs=[pl.BlockSpec((None, None, tr, cols), lambda b, i, c_ref: (c_ref[0], b, i, 0)), blk],
            out_specs=blk),
        out_shape=jax.ShapeDtypeStruct((N_CHIPS, rows, cols), out_dtype),
        compiler_params=_params("parallel", "parallel"),
    )(c.reshape(1).astype(jnp.int32), g.reshape(2, N_CHIPS, rows, cols), recv.reshape(N_CHIPS, rows, cols))
    return out.reshape(recv.shape)


def _sum_slots(p, *, name):
    cols = p.shape[-1]
    rows = p.size // (N_CHIPS * cols)
    tr = _pick_rows(rows)

    def body(p_ref, o_ref):
        acc = p_ref[0].astype(F32)
        for q in range(1, N_CHIPS):
            acc = acc + p_ref[q].astype(F32)
        o_ref[...] = acc

    out = pl.pallas_call(
        body, name=name, grid=(rows // tr,),
        in_specs=[pl.BlockSpec((N_CHIPS, tr, cols), lambda i: (0, i, 0))],
        out_specs=pl.BlockSpec((tr, cols), lambda i: (i, 0)),
        out_shape=jax.ShapeDtypeStruct((rows, cols), F32),
        compiler_params=_params("parallel"),
    )(p.reshape(N_CHIPS, rows, cols))
    return out.reshape(p.shape[1:])


GROUPS = (("gu", ("w_ffn1_gu", "w_ffn2_gu"), "col"), ("down", ("w_ffn1_down", "w_ffn2_down"), "row"),
          ("square", ("w_mix_out", "w_xq", "w_xo"), "row"), ("mix_in", ("w_mix_in",), "col"), ("xkv", ("w_xkv",), "col"))
REPLICATED = ("g_ffn1", "g_mix", "b_f", "g_conv_out", "g_att_out", "g_xattn", "g_mem", "g_ffn2", "g_final")
WEIGHTS = ("g_ffn1", "w_ffn1_gu", "w_ffn1_down", "g_mix", "w_mix_in", "w_conv", "b_f", "g_conv_out", "g_att_out",
           "w_mix_out", "g_xattn", "g_mem", "w_xq", "w_xkv", "w_xo", "g_ffn2", "w_ffn2_gu", "w_ffn2_down", "g_final")
SMALL_COLS = 1024
SMALL_ROW_UNIT = 16


def _gather_weights(shards):
    packs = []
    for _, members, _ in GROUPS:
        hs = []
        for name in members:
            l, a, b = shards[name].shape
            hs.append(shards[name].astype(BF16).reshape(l, 2, a // 2, b).transpose(1, 0, 2, 3))
        packs.append(jnp.stack(hs, axis=1))
    wc = shards["w_conv"]
    packs.append(jnp.stack([wc, wc]))
    gathered = _allgather_weights(packs, name="allgather_weights")
    full = {}
    for (_, members, kind), got in zip(GROUPS, gathered):
        _, _, g, l, a2, b = got.shape
        if kind == "col":
            whole = got.transpose(2, 3, 1, 4, 0, 5).reshape(g, l, 2 * a2, N_CHIPS * b)
        else:
            whole = got.transpose(2, 3, 0, 1, 4, 5).reshape(g, l, N_CHIPS * 2 * a2, b)
        for gi, name in enumerate(members):
            full[name] = whole[gi]
    l, k, b = wc.shape
    full["w_conv"] = gathered[-1][:, 0].transpose(1, 2, 0, 3).reshape(l, k, N_CHIPS * b)
    return full


def _small_rows(v):
    flat = v.reshape(-1)
    return jnp.pad(flat, (0, -flat.shape[0] % SMALL_COLS)).reshape(-1, SMALL_COLS)


def _reduce_gradients(grads, c):
    packs = []
    for _, members, kind in GROUPS:
        cut = []
        for name in members:
            l, a, b = grads[name].shape
            if kind == "col":
                cut.append(grads[name].reshape(l, 2, a // 2, N_CHIPS, b // N_CHIPS).transpose(1, 3, 0, 2, 4))
            else:
                cut.append(grads[name].reshape(l, N_CHIPS, 2, a // (2 * N_CHIPS), b).transpose(2, 1, 0, 3, 4))
        packs.append(jnp.stack(cut, axis=2))
    rep = jnp.concatenate([_small_rows(grads[n]) for n in REPLICATED])
    l, k, b = grads["w_conv"].shape
    conv = grads["w_conv"].reshape(l, k, N_CHIPS, b // N_CHIPS).transpose(2, 0, 1, 3)
    conv_rows = [_small_rows(conv[q]) for q in range(N_CHIPS)]
    n_rows = rep.shape[0] + conv_rows[0].shape[0]
    fill = jnp.zeros((-n_rows % SMALL_ROW_UNIT, SMALL_COLS), F32)
    small = jnp.stack([jnp.concatenate([rep, conv_rows[q], fill]) for q in range(N_CHIPS)])
    half_rows = small.shape[1] // 2
    packs.append(small.reshape(N_CHIPS, 2, half_rows, SMALL_COLS).transpose(1, 0, 2, 3))

    from_sibling = _swap_halves(packs, name="grad_swap_halves")
    wire = [BF16] * len(GROUPS) + [F32]
    chip_sums = [_add_halves(g, r, c, dt, name=f"grad_add_halves_{i}")
                 for i, (g, r, dt) in enumerate(zip(packs, from_sibling, wire))]
    from_chips = _scatter_chips(chip_sums, name="grad_scatter_chips")
    halves = [_sum_slots(p, name=f"grad_sum_chips_{i}") for i, p in enumerate(from_chips)]
    reduced = _join_halves(halves, name="grad_join_halves")

    out = {}
    for (_, members, _), r in zip(GROUPS, reduced):
        _, g, l, a2, b = r.shape
        whole = r.transpose(1, 2, 0, 3, 4).reshape(g, l, 2 * a2, b)
        for gi, name in enumerate(members):
            out[name] = whole[gi]
    rows = reduced[-1].reshape(-1, SMALL_COLS)
    off = 0
    for name in REPLICATED:
        n = -(-grads[name].size // SMALL_COLS)
        out[name] = rows[off:off + n].reshape(-1)[:grads[name].size].reshape(grads[name].shape)
        off += n
    l, k, b = grads["w_conv"].shape
    out["w_conv"] = rows[off:off + conv_rows[0].shape[0]].reshape(-1)[:l * k * b // N_CHIPS].reshape(l, k, b // N_CHIPS)
    return out


def _ffn_fwd(x, g, w_gu, w_down, tag):
    h = _rmsnorm_fwd(x, g, name=f"{tag}_norm")
    gu = _matmul(h, w_gu, "nn", BF16, tm=2048, tn=512, tk=w_gu.shape[0], name=f"{tag}_gu")
    a = _swiglu_fwd(gu, name=f"{tag}_act")
    y = _matmul(a, w_down, "nn", F32, tm=1024, tn=1024, tk=w_down.shape[0], alpha=0.5, residual=x, name=f"{tag}_down")
    return y, (x, h, gu, a)


def _ffn_bwd(dy, saved, g, w_gu, w_down, tag):
    x, h, gu, a = saved
    f = w_down.shape[0]
    da = _matmul(dy, w_down, "nt", BF16, tm=1024, tn=f // 2, tk=w_down.shape[1], alpha=0.5, name=f"{tag}_da")
    dw_down = _matmul(a, dy, "tn", F32, tm=f // 2, tn=1024, tk=1024, alpha=0.5, name=f"{tag}_dwdown")
    dgu = _swiglu_bwd(da, gu, name=f"{tag}_dact")
    dw_gu = _matmul(h, dgu, "tn", F32, tm=1024, tn=512, tk=2048, name=f"{tag}_dwgu")
    dh = _matmul(dgu, w_gu, "nt", F32, tm=1024, tn=1024, tk=f, name=f"{tag}_dh")
    dx, dg = _rmsnorm_bwd(x, g, dh, dy, name=f"{tag}_dnorm")
    return dx, dg, dw_gu, dw_down


def _tokens_on_lanes(v, t):
    p, s, _ = v.shape
    return v.transpose(0, 2, 1).reshape(p, 2, s // t, t).transpose(0, 2, 1, 3)


def _mix_fwd(x, w, l, tag):
    s, d = x.shape
    gc, ga = w["g_conv_out"][l], w["g_att_out"][l]
    cw, aw = gc.shape[0], ga.shape[0]
    nh = aw // HEAD_DIM
    zw = 3 * cw + 3 * aw
    t = 512 if s >= 2048 else s // 4
    cols = dict(qcol=3 * cw // LANES, kcol=(3 * cw + aw) // LANES, vcol=(3 * cw + 2 * aw) // LANES, t=t)
    h = _rmsnorm_fwd(x, w["g_mix"][l], name=f"{tag}_norm")
    w_in = w["w_mix_in"][l]
    w_main = w_in[:, :zw]
    w_f = jnp.pad(w_in[:, zw:], ((0, 0), (0, LANES - nh)))
    z = _matmul(h, w_main, "nn", BF16, tm=2048, tn=512, tk=d, name=f"{tag}_in")
    zf = _matmul(h, w_f, "nn", F32, tm=2048, tn=LANES, tk=d, name=f"{tag}_in_f")
    b = jnp.pad(w["b_f"][l], (0, LANES - nh)).reshape(1, LANES)
    c = _cumsum_fwd(zf, b, name=f"{tag}_cumsum")
    ccol = c[:, :nh].reshape(s, nh // 2, 2).transpose(1, 0, 2)
    crow = _tokens_on_lanes(ccol, t)
    assert (3 * cw) % aw == 0
    bounds = _attn_bounds(z, c, qcol_units=3 * cw // aw, kcol_units=3 * cw // aw + 1, aw=aw, t=t, name=f"{tag}_bounds")
    yatt, lse = _fattn_fwd(z, ccol, crow, bounds, name=f"{tag}_attn", **cols)
    wc = jnp.pad(w["w_conv"][l], ((0, HALO - CONV_K), (0, 0)))
    ycat, cv = _mixpost_fwd(z, yatt, wc, gc, ga, name=f"{tag}_post")
    y = _matmul(ycat, w["w_mix_out"][l], "nn", F32, tm=1024, tn=1024, tk=d, residual=x, name=f"{tag}_out")
    return y, (x, h, w_main, w_f, z, zf, b, ccol, crow, bounds, yatt, lse, wc, ycat, cv, cols)


def _mix_bwd(dy, saved, w, l, tag):
    x, h, w_main, w_f, z, zf, b, ccol, crow, bounds, yatt, lse, wc, ycat, cv, cols = saved
    s, d = x.shape
    gc, ga = w["g_conv_out"][l], w["g_att_out"][l]
    nh = ga.shape[0] // HEAD_DIM
    zw = w_main.shape[1]
    t = cols["t"]
    dycat = _matmul(dy, w["w_mix_out"][l], "nt", F32, tm=1024, tn=1024, tk=d, name=f"{tag}_dycat")
    dw_out = _matmul(ycat, dy, "tn", F32, tm=1024, tn=1024, tk=1024, name=f"{tag}_dwout")
    dz_conv, dyatt, dwc, dgc, dga = _mixpost_bwd(dycat, z, yatt, cv, wc, gc, ga, name=f"{tag}_dpost")
    dq, delta, dcq = _fattn_dq(z, dyatt, yatt, lse, ccol, crow, bounds, name=f"{tag}_attn_dq", **cols)
    dk, dv, dck = _fattn_dkv(z, dyatt.astype(BF16), _tokens_on_lanes(lse, t), _tokens_on_lanes(delta, t), ccol, crow,
                             bounds, name=f"{tag}_attn_dkv", **cols)
    def heads_on_lanes(v):
        return jnp.pad(v.transpose(1, 0, 2).reshape(s, nh), ((0, 0), (0, LANES - nh)))

    dzf, db = _cumsum_bwd(heads_on_lanes(dcq), heads_on_lanes(dck), zf, b, name=f"{tag}_dcumsum")
    dz = jnp.concatenate([dz_conv, dq, dk, dv], axis=1)
    dw_main = _matmul(h, dz, "tn", F32, tm=1024, tn=512, tk=2048, name=f"{tag}_dwin")
    dw_f = _matmul(h, dzf, "tn", F32, tm=1024, tn=LANES, tk=2048, name=f"{tag}_dwin_f")
    dh = _matmul(dz, w_main, "nt", F32, tm=1024, tn=1024, tk=zw // 2, name=f"{tag}_dh")
    dh = _matmul(dzf, w_f, "nt", F32, tm=1024, tn=1024, tk=LANES, residual=dh, name=f"{tag}_dh_f")
    dx, dg = _rmsnorm_bwd(x, w["g_mix"][l], dh, dy, name=f"{tag}_dnorm")
    grads = dict(g_mix=dg, w_mix_in=jnp.concatenate([dw_main, dw_f[:, :nh]], axis=1), w_conv=dwc[:CONV_K], b_f=db[0, :nh],
                 g_conv_out=dgc[0], g_att_out=dga[0], w_mix_out=dw_out)
    return dx, grads


def _xattn_block_fwd(x, mem, w, l, tag):
    d = x.shape[1]
    h = _rmsnorm_fwd(x, w["g_xattn"][l], name=f"{tag}_norm")
    mn = _rmsnorm_fwd(mem, w["g_mem"][l], name=f"{tag}_mem_norm")
    q = _matmul(h, w["w_xq"][l], "nn", BF16, tm=1024, tn=1024, tk=d, name=f"{tag}_q")
    kv = _matmul(mn, w["w_xkv"][l], "nn", BF16, tm=1024, tn=1024, tk=d, name=f"{tag}_kv")
    o = _xattn_fwd(q, kv, name=f"{tag}_core")
    y = _matmul(o, w["w_xo"][l], "nn", F32, tm=1024, tn=1024, tk=d, residual=x, name=f"{tag}_o")
    return y, (x, h, mn, q, kv, o)


def _xattn_block_bwd(dy, saved, mem, w, l, tag):
    x, h, mn, q, kv, o = saved
    d = x.shape[1]
    do = _matmul(dy, w["w_xo"][l], "nt", BF16, tm=1024, tn=1024, tk=d, name=f"{tag}_do")
    dw_xo = _matmul(o, dy, "tn", F32, tm=1024, tn=1024, tk=1024, name=f"{tag}_dwo")
    dq, dkv = _xattn_bwd(q, kv, do, name=f"{tag}_dcore")
    dw_xq = _matmul(h, dq, "tn", F32, tm=1024, tn=1024, tk=2048, name=f"{tag}_dwq")
    dh = _matmul(dq, w["w_xq"][l], "nt", F32, tm=1024, tn=1024, tk=d, name=f"{tag}_dh")
    dx, dg = _rmsnorm_bwd(x, w["g_xattn"][l], dh, dy, name=f"{tag}_dnorm")
    dw_xkv = _matmul(mn, dkv, "tn", F32, tm=1024, tn=1024, tk=1024, name=f"{tag}_dwkv")
    dmn = _matmul(dkv, w["w_xkv"][l], "nt", F32, tm=1024, tn=1024, tk=1024, name=f"{tag}_dmem")
    _, dg_mem = _rmsnorm_bwd(mem, w["g_mem"][l], dmn, None, name=f"{tag}_dmem_norm")
    return dx, dict(g_xattn=dg, g_mem=dg_mem, w_xq=dw_xq, w_xkv=dw_xkv, w_xo=dw_xo)


def kernel(x, mem, g_ffn1, w_ffn1_gu, w_ffn1_down, g_mix, w_mix_in, w_conv, b_f, g_conv_out, g_att_out, w_mix_out, g_xattn, g_mem, w_xq, w_xkv, w_xo, g_ffn2, w_ffn2_gu, w_ffn2_down, g_final, loss_target, m_g_ffn1, m_w_ffn1_gu, m_w_ffn1_down, m_g_mix, m_w_mix_in, m_w_conv, m_b_f, m_g_conv_out, m_g_att_out, m_w_mix_out, m_g_xattn, m_g_mem, m_w_xq, m_w_xkv, m_w_xo, m_g_ffn2, m_w_ffn2_gu, m_w_ffn2_down, m_g_final, v_g_ffn1, v_w_ffn1_gu, v_w_ffn1_down, v_g_mix, v_w_mix_in, v_w_conv, v_b_f, v_g_conv_out, v_g_att_out, v_w_mix_out, v_g_xattn, v_g_mem, v_w_xq, v_w_xkv, v_w_xo, v_g_ffn2, v_w_ffn2_gu, v_w_ffn2_down, v_g_final):
    local = dict(zip(WEIGHTS, (g_ffn1, w_ffn1_gu, w_ffn1_down, g_mix, w_mix_in, w_conv, b_f, g_conv_out, g_att_out, w_mix_out,
                               g_xattn, g_mem, w_xq, w_xkv, w_xo, g_ffn2, w_ffn2_gu, w_ffn2_down, g_final)))
    mom1 = dict(zip(WEIGHTS, (m_g_ffn1, m_w_ffn1_gu, m_w_ffn1_down, m_g_mix, m_w_mix_in, m_w_conv, m_b_f, m_g_conv_out,
                              m_g_att_out, m_w_mix_out, m_g_xattn, m_g_mem, m_w_xq, m_w_xkv, m_w_xo, m_g_ffn2, m_w_ffn2_gu,
                              m_w_ffn2_down, m_g_final)))
    mom2 = dict(zip(WEIGHTS, (v_g_ffn1, v_w_ffn1_gu, v_w_ffn1_down, v_g_mix, v_w_mix_in, v_w_conv, v_b_f, v_g_conv_out,
                              v_g_att_out, v_w_mix_out, v_g_xattn, v_g_mem, v_w_xq, v_w_xkv, v_w_xo, v_g_ffn2, v_w_ffn2_gu,
                              v_w_ffn2_down, v_g_final)))
    depth = g_ffn1.shape[0]
    s, d = x.shape[1], x.shape[2]
    w = dict(local)
    w.update(_gather_weights(local))

    xs = x.reshape(s, d)
    mems = mem.reshape(mem.shape[1], d)
    saved = []
    for l in range(depth):
        xs, s1 = _ffn_fwd(xs, w["g_ffn1"][l], w["w_ffn1_gu"][l], w["w_ffn1_down"][l], f"l{l}_ffn1")
        xs, s2 = _mix_fwd(xs, w, l, f"l{l}_mix")
        xs, s3 = _xattn_block_fwd(xs, mems, w, l, f"l{l}_xattn")
        xs, s4 = _ffn_fwd(xs, w["g_ffn2"][l], w["w_ffn2_gu"][l], w["w_ffn2_down"][l], f"l{l}_ffn2")
        saved.append((s1, s2, s3, s4))

    dx, dg_final, sq = _final_loss(xs, g_final, loss_target.reshape(s, d), name="loss_head")
    loss = lax.psum(jnp.sum(sq) * (0.5 / d), MESH_AXES)

    per_layer = []
    for l in reversed(range(depth)):
        s1, s2, s3, s4 = saved[l]
        grads = {}
        dx, grads["g_ffn2"], grads["w_ffn2_gu"], grads["w_ffn2_down"] = _ffn_bwd(
            dx, s4, w["g_ffn2"][l], w["w_ffn2_gu"][l], w["w_ffn2_down"][l], f"l{l}_ffn2")
        dx, g3 = _xattn_block_bwd(dx, s3, mems, w, l, f"l{l}_xattn")
        dx, g2 = _mix_bwd(dx, s2, w, l, f"l{l}_mix")
        dx, grads["g_ffn1"], grads["w_ffn1_gu"], grads["w_ffn1_down"] = _ffn_bwd(
            dx, s1, w["g_ffn1"][l], w["w_ffn1_gu"][l], w["w_ffn1_down"][l], f"l{l}_ffn1")
        grads.update(g2)
        grads.update(g3)
        per_layer.append(grads)
    per_layer.reverse()
    grads = {name: jnp.stack([per_layer[l][name] for l in range(depth)]) for name in WEIGHTS if name != "g_final"}
    grads["g_final"] = dg_final.reshape(d)

    reduced = _reduce_gradients(grads, lax.axis_index("c"))
    deltas, new_m, new_v = {}, {}, {}
    for name in WEIGHTS:
        deltas[name], new_m[name], new_v[name] = _adamw(local[name], reduced[name], mom1[name], mom2[name], name=f"adamw_{name}")
    return (loss, dx.reshape(x.shape), *[reduced[n] for n in WEIGHTS], *[deltas[n] for n in WEIGHTS],
            *[new_m[n] for n in WEIGHTS], *[new_v[n] for n in WEIGHTS])
```

```python
import functools

import jax
import jax.numpy as jnp
from jax import lax
from jax.experimental import pallas as pl
from jax.experimental.pallas import tpu as pltpu

F32 = jnp.float32
BF16 = jnp.bfloat16

EPS = 1e-6
HEAD_DIM = 64
LANES = 128
N_XHEADS = 4
CONV_K = 3
ADAM_LR, ADAM_B1, ADAM_B2, ADAM_EPS, ADAM_WD, ADAM_STEP = 0.001, 0.9, 0.999, 1e-08, 0.01, 10
VMEM_LIMIT_BYTES = 56 * 1024 * 1024
NEG_BIG = -1e30
MESH_AXES = ("x", "y", "c")
N_CHIPS = 4


def _params(*sem):
    return pltpu.CompilerParams(dimension_semantics=sem, vmem_limit_bytes=VMEM_LIMIT_BYTES)


_DIMS = {"nn": (((1,), (0,)), ((), ())), "nt": (((1,), (1,)), ((), ())), "tn": (((0,), (0,)), ((), ()))}


def _matmul(a, b, mode, out_dtype, *, tm, tn, tk, name, alpha=1.0, residual=None):
    if mode == "nn":
        (m, k), (k2, n) = a.shape, b.shape
    elif mode == "nt":
        (m, k), (n, k2) = a.shape, b.shape
    else:
        (k, m), (k2, n) = a.shape, b.shape
    assert k == k2, (a.shape, b.shape, mode)
    tm, tn, tk = min(tm, m), min(tn, n), min(tk, k)
    assert m % tm == 0 and n % tn == 0 and k % tk == 0, (m, n, k, tm, tn, tk)
    nk = k // tk
    dims = _DIMS[mode]

    def body(*refs):
        if residual is None:
            a_ref, b_ref, o_ref, *scratch = refs
            r_ref = None
        else:
            a_ref, b_ref, r_ref, o_ref, *scratch = refs
        prod = lax.dot_general(a_ref[...].astype(BF16), b_ref[...].astype(BF16), dims, preferred_element_type=F32)

        def finish(acc):
            if alpha != 1.0:
                acc = acc * alpha
            if r_ref is not None:
                acc = acc + r_ref[...].astype(F32)
            o_ref[...] = acc.astype(o_ref.dtype)

        if nk == 1:
            finish(prod)
        else:
            acc_ref = scratch[0]
            kk = pl.program_id(2)

            @pl.when(kk == 0)
            def _():
                acc_ref[...] = prod

            @pl.when(kk > 0)
            def _():
                acc_ref[...] += prod

            @pl.when(kk == nk - 1)
            def _():
                finish(acc_ref[...])

    if mode == "nn":
        a_spec = pl.BlockSpec((tm, tk), lambda i, j, kk: (i, kk))
        b_spec = pl.BlockSpec((tk, tn), lambda i, j, kk: (kk, j))
    elif mode == "nt":
        a_spec = pl.BlockSpec((tm, tk), lambda i, j, kk: (i, kk))
        b_spec = pl.BlockSpec((tn, tk), lambda i, j, kk: (j, kk))
    else:
        a_spec = pl.BlockSpec((tk, tm), lambda i, j, kk: (kk, i))
        b_spec = pl.BlockSpec((tk, tn), lambda i, j, kk: (kk, j))
    o_spec = pl.BlockSpec((tm, tn), lambda i, j, kk: (i, j))
    in_specs, args = [a_spec, b_spec], [a, b]
    if residual is not None:
        in_specs.append(o_spec)
        args.append(residual)
    return pl.pallas_call(
        body, name=name, grid=(m // tm, n // tn, nk), in_specs=in_specs, out_specs=o_spec,
        out_shape=jax.ShapeDtypeStruct((m, n), out_dtype),
        scratch_shapes=[pltpu.VMEM((tm, tn), F32)] if nk > 1 else [],
        compiler_params=_params("parallel", "parallel", "arbitrary"),
    )(*args)


def _row_tile(rows, want):
    t = min(rows, want)
    assert rows % t == 0, (rows, t)
    return t


def _rmsnorm_fwd(x, g, *, name, tr=1024):
    s, d = x.shape
    tr = _row_tile(s, tr)

    def body(x_ref, g_ref, o_ref):
        xf = x_ref[...]
        r = lax.rsqrt(jnp.mean(xf * xf, axis=-1, keepdims=True) + EPS)
        o_ref[...] = (xf * r * g_ref[...]).astype(o_ref.dtype)

    return pl.pallas_call(
        body, name=name, grid=(s // tr,),
        in_specs=[pl.BlockSpec((tr, d), lambda i: (i, 0)), pl.BlockSpec((1, d), lambda i: (0, 0))],
        out_specs=pl.BlockSpec((tr, d), lambda i: (i, 0)),
        out_shape=jax.ShapeDtypeStruct((s, d), BF16),
        compiler_params=_params("parallel"),
    )(x, g.reshape(1, d))


def _rmsnorm_bwd(x, g, dh, dres, *, name, tr=512):
    s, d = x.shape
    tr = _row_tile(s, tr)

    def body(x_ref, g_ref, dh_ref, *rest):
        if dres is None:
            dx_ref, dg_ref = rest
        else:
            dres_ref, dx_ref, dg_ref = rest
        xf = x_ref[...]
        r = lax.rsqrt(jnp.mean(xf * xf, axis=-1, keepdims=True) + EPS)
        xhat = xf * r
        dhf = dh_ref[...].astype(F32)
        dxhat = dhf * g_ref[...]
        dx = r * (dxhat - xhat * jnp.mean(dxhat * xhat, axis=-1, keepdims=True))
        if dres is not None:
            dx = dx + dres_ref[...]
        dx_ref[...] = dx

        @pl.when(pl.program_id(0) == 0)
        def _():
            dg_ref[...] = jnp.zeros_like(dg_ref)

        dg_ref[...] += jnp.sum(dhf * xhat, axis=0, keepdims=True)

    row = pl.BlockSpec((tr, d), lambda i: (i, 0))
    vec = pl.BlockSpec((1, d), lambda i: (0, 0))
    in_specs, args = [row, vec, row], [x, g.reshape(1, d), dh]
    if dres is not None:
        in_specs.append(row)
        args.append(dres)
    dx, dg = pl.pallas_call(
        body, name=name, grid=(s // tr,), in_specs=in_specs, out_specs=[row, vec],
        out_shape=[jax.ShapeDtypeStruct((s, d), F32), jax.ShapeDtypeStruct((1, d), F32)],
        compiler_params=_params("arbitrary"),
    )(*args)
    return dx, dg.reshape(d)


def _swiglu_fwd(gu, *, name, tr=512):
    s, f2 = gu.shape
    f = f2 // 2
    tr = _row_tile(s, tr)

    def body(gu_ref, a_ref):
        gate = gu_ref[:, :f].astype(F32)
        up = gu_ref[:, f:].astype(F32)
        a_ref[...] = (gate * jax.nn.sigmoid(gate) * up).astype(a_ref.dtype)

    return pl.pallas_call(
        body, name=name, grid=(s // tr,),
        in_specs=[pl.BlockSpec((tr, f2), lambda i: (i, 0))],
        out_specs=pl.BlockSpec((tr, f), lambda i: (i, 0)),
        out_shape=jax.ShapeDtypeStruct((s, f), BF16),
        compiler_params=_params("parallel"),
    )(gu)


def _swiglu_bwd(da, gu, *, name, tr=512):
    s, f2 = gu.shape
    f = f2 // 2
    tr = _row_tile(s, tr)

    def body(da_ref, gu_ref, dgu_ref):
        gate = gu_ref[:, :f].astype(F32)
        up = gu_ref[:, f:].astype(F32)
        daf = da_ref[...].astype(F32)
        sig = jax.nn.sigmoid(gate)
        silu = gate * sig
        dgu_ref[:, :f] = (daf * up * (sig + silu * (1.0 - sig))).astype(dgu_ref.dtype)
        dgu_ref[:, f:] = (daf * silu).astype(dgu_ref.dtype)

    return pl.pallas_call(
        body, name=name, grid=(s // tr,),
        in_specs=[pl.BlockSpec((tr, f), lambda i: (i, 0)), pl.BlockSpec((tr, f2), lambda i: (i, 0))],
        out_specs=pl.BlockSpec((tr, f2), lambda i: (i, 0)),
        out_shape=jax.ShapeDtypeStruct((s, f2), BF16),
        compiler_params=_params("parallel"),
    )(da, gu)


_NT = (((1,), (1,)), ((), ()))
_NN = (((1,), (0,)), ((), ()))
_TN = (((0,), (0,)), ((), ()))
_QK_SCALE = HEAD_DIM ** -0.5


def _dot(a, b, dims):
    return lax.dot_general(a, b, dims, preferred_element_type=F32)


def _split_heads(v2, is_a):
    zero = jnp.zeros_like(v2)
    return jnp.where(is_a, v2, zero), jnp.where(is_a, zero, v2)


SKIP_BELOW = 107.0
_SMEM = pl.BlockSpec(memory_space=pltpu.SMEM)


def _attn_bounds(z, c, *, qcol_units, kcol_units, aw, t, name):
    s = z.shape[0]
    nq = s // t
    nh = aw // HEAD_DIM

    def body(q_ref, k_ref, o_ref):
        d = lax.broadcasted_iota(jnp.int32, (aw, LANES), 0)
        hh = lax.broadcasted_iota(jnp.int32, (aw, LANES), 1)
        onehot = ((d >= hh * HEAD_DIM) & (d < (hh + 1) * HEAD_DIM)).astype(F32)
        for r, ref in enumerate((q_ref, k_ref)):
            v = ref[...].astype(F32)
            sq = lax.dot_general(v * v, onehot, _NN, precision=lax.Precision.HIGHEST, preferred_element_type=F32)
            o_ref[r:r + 1, :] = jnp.max(sq, axis=0, keepdims=True)
        o_ref[2:, :] = jnp.zeros((HALO - 2, LANES), F32)

    sq = pl.pallas_call(
        body, name=name, grid=(nq,),
        in_specs=[pl.BlockSpec((t, aw), lambda i: (i, qcol_units)), pl.BlockSpec((t, aw), lambda i: (i, kcol_units))],
        out_specs=pl.BlockSpec((None, HALO, LANES), lambda i: (i, 0, 0)),
        out_shape=jax.ShapeDtypeStruct((nq, HALO, LANES), F32),
        compiler_params=_params("parallel"),
    )(z, z)
    norms = jnp.sqrt(sq[:, :2, :nh]) * 1.01
    qn = (norms[:, 0, :] * _QK_SCALE).T.reshape(-1)
    kn = norms[:, 1, :].T.reshape(-1)
    cs = c[0::t, :nh].T.reshape(-1)
    ce = c[t - 1::t, :nh].T.reshape(-1)
    return qn, kn, cs, ce


def _block_active(bounds, head, i, j, nq):
    qn_ref, kn_ref, cs_ref, ce_ref = bounds
    qi = qn_ref[head * nq + i]
    upper = qi * kn_ref[head * nq + j] + (cs_ref[head * nq + i] - ce_ref[head * nq + j])
    lower = -(qi * kn_ref[head * nq + i])
    return upper - lower > -SKIP_BELOW


def _for_active_heads(bounds, pair, i, j, nq, head_step):
    act = [_block_active(bounds, 2 * pair + h, i, j, nq) for h in range(2)]

    def run(heads):
        for h in heads:
            head_step(h)

    pl.when(act[0] & act[1])(functools.partial(run, (0, 1)))
    pl.when(act[0] & jnp.logical_not(act[1]))(functools.partial(run, (0,)))
    pl.when(jnp.logical_not(act[0]) & act[1])(functools.partial(run, (1,)))


def _fattn_fwd(z, ccol, crow, bounds, *, qcol, kcol, vcol, t, name):
    s = z.shape[0]
    npairs, nq = crow.shape[0], crow.shape[1]
    assert nq * t == s

    def body(qn_ref, kn_ref, cs_ref, ce_ref, q_ref, k_ref, v_ref, cc_ref, cr_ref, o_ref, lse_ref, m_scr, l_scr, acc_scr):
        pair, i = pl.program_id(0), pl.program_id(1)
        is_a = lax.broadcasted_iota(jnp.int32, (t, LANES), 1) < HEAD_DIM
        qh = _split_heads(q_ref[...] * jnp.asarray(_QK_SCALE, BF16), is_a)
        cq = (cc_ref[:, 0:1], cc_ref[:, 1:2])
        m_scr[...] = jnp.full(m_scr.shape, NEG_BIG, F32)
        l_scr[...] = jnp.zeros(l_scr.shape, F32)
        acc_scr[...] = jnp.zeros(acc_scr.shape, F32)

        def head_step(h, j, diagonal):
            off = pl.multiple_of(j * t, t)
            k2 = k_ref[pl.ds(off, t), :]
            vh = _split_heads(v_ref[pl.ds(off, t), :], is_a)[h]
            sc = _dot(qh[h], k2, _NT) + (cq[h] - cr_ref[j][h:h + 1, :])
            if diagonal:
                row = lax.broadcasted_iota(jnp.int32, (t, t), 0)
                col = lax.broadcasted_iota(jnp.int32, (t, t), 1)
                sc = jnp.where(row >= col, sc, NEG_BIG)
            m_old = m_scr[h]
            m_new = jnp.maximum(m_old, jnp.max(sc, axis=1, keepdims=True))
            alpha = jnp.exp(m_old - m_new)
            p = jnp.exp(sc - m_new)
            l_scr[h] = alpha * l_scr[h] + jnp.sum(p, axis=1, keepdims=True)
            m_scr[h] = m_new
            acc_scr[h] = acc_scr[h] * alpha + _dot(p.astype(BF16), vh, _NN)

        def loop_body(j, carry):
            _for_active_heads((qn_ref, kn_ref, cs_ref, ce_ref), pair, i, j, nq, lambda h: head_step(h, j, False))
            return carry

        lax.fori_loop(0, i, loop_body, 0)
        for h in range(2):
            head_step(h, i, True)
        o_ref[...] = jnp.where(is_a, acc_scr[0] / l_scr[0], acc_scr[1] / l_scr[1])
        lse_ref[:, 0:1] = m_scr[0] + jnp.log(l_scr[0])
        lse_ref[:, 1:2] = m_scr[1] + jnp.log(l_scr[1])

    col_vec = pl.BlockSpec((None, t, 2), lambda p, i: (p, i, 0))
    return pl.pallas_call(
        body, name=name, grid=(npairs, nq),
        in_specs=[
            _SMEM, _SMEM, _SMEM, _SMEM,
            pl.BlockSpec((t, LANES), lambda p, i: (i, qcol + p)),
            pl.BlockSpec((s, LANES), lambda p, i: (0, kcol + p)),
            pl.BlockSpec((s, LANES), lambda p, i: (0, vcol + p)),
            col_vec,
            pl.BlockSpec((None, nq, 2, t), lambda p, i: (p, 0, 0, 0)),
        ],
        out_specs=[pl.BlockSpec((t, LANES), lambda p, i: (i, p)), col_vec],
        out_shape=[jax.ShapeDtypeStruct((s, npairs * LANES), F32), jax.ShapeDtypeStruct((npairs, s, 2), F32)],
        scratch_shapes=[pltpu.VMEM((2, t, 1), F32), pltpu.VMEM((2, t, 1), F32), pltpu.VMEM((2, t, LANES), F32)],
        compiler_params=_params("parallel", "arbitrary"),
    )(*bounds, z, z, z, ccol, crow)


def _fattn_dq(z, dy, y, lse, ccol, crow, bounds, *, qcol, kcol, vcol, t, name):
    s = z.shape[0]
    npairs, nq = crow.shape[0], crow.shape[1]

    def body(qn_ref, kn_ref, cs_ref, ce_ref, q_ref, k_ref, v_ref, dy_ref, y_ref, lse_ref, cc_ref, cr_ref,
             dq_ref, delta_ref, dcq_ref, acc_scr, rs_scr):
        pair, i = pl.program_id(0), pl.program_id(1)
        is_a = lax.broadcasted_iota(jnp.int32, (t, LANES), 1) < HEAD_DIM
        qh = _split_heads(q_ref[...] * jnp.asarray(_QK_SCALE, BF16), is_a)
        cq = (cc_ref[:, 0:1], cc_ref[:, 1:2])
        lse = (lse_ref[:, 0:1], lse_ref[:, 1:2])
        dyf = dy_ref[...]
        prod = _split_heads(dyf * y_ref[...], is_a)
        delta = (jnp.sum(prod[0], axis=1, keepdims=True), jnp.sum(prod[1], axis=1, keepdims=True))
        delta_ref[:, 0:1] = delta[0]
        delta_ref[:, 1:2] = delta[1]
        dyh = _split_heads(dyf.astype(BF16), is_a)
        acc_scr[...] = jnp.zeros(acc_scr.shape, F32)
        rs_scr[...] = jnp.zeros(rs_scr.shape, F32)

        def head_step(h, j, diagonal):
            off = pl.multiple_of(j * t, t)
            k2 = k_ref[pl.ds(off, t), :]
            v2 = v_ref[pl.ds(off, t), :]
            sc = _dot(qh[h], k2, _NT) + (cq[h] - cr_ref[j][h:h + 1, :])
            if diagonal:
                row = lax.broadcasted_iota(jnp.int32, (t, t), 0)
                col = lax.broadcasted_iota(jnp.int32, (t, t), 1)
                sc = jnp.where(row >= col, sc, NEG_BIG)
            p = jnp.exp(sc - lse[h])
            dp = _dot(dyh[h], v2, _NT)
            ds = p * (dp - delta[h])
            rs_scr[h] += jnp.sum(ds, axis=1, keepdims=True)
            acc_scr[...] += _dot(ds.astype(BF16), _split_heads(k2, is_a)[h], _NN)

        def loop_body(j, carry):
            _for_active_heads((qn_ref, kn_ref, cs_ref, ce_ref), pair, i, j, nq, lambda h: head_step(h, j, False))
            return carry

        lax.fori_loop(0, i, loop_body, 0)
        for h in range(2):
            head_step(h, i, True)
        dq_ref[...] = (acc_scr[...] * _QK_SCALE).astype(dq_ref.dtype)
        dcq_ref[:, 0:1] = rs_scr[0]
        dcq_ref[:, 1:2] = rs_scr[1]

    col_vec = pl.BlockSpec((None, t, 2), lambda p, i: (p, i, 0))
    tile = pl.BlockSpec((t, LANES), lambda p, i: (i, p))
    vec_shape = jax.ShapeDtypeStruct((npairs, s, 2), F32)
    return pl.pallas_call(
        body, name=name, grid=(npairs, nq),
        in_specs=[
            _SMEM, _SMEM, _SMEM, _SMEM,
            pl.BlockSpec((t, LANES), lambda p, i: (i, qcol + p)),
            pl.BlockSpec((s, LANES), lambda p, i: (0, kcol + p)),
            pl.BlockSpec((s, LANES), lambda p, i: (0, vcol + p)),
            tile, tile, col_vec, col_vec,
            pl.BlockSpec((None, nq, 2, t), lambda p, i: (p, 0, 0, 0)),
        ],
        out_specs=[tile, col_vec, col_vec],
        out_shape=[jax.ShapeDtypeStruct((s, npairs * LANES), BF16), vec_shape, vec_shape],
        scratch_shapes=[pltpu.VMEM((t, LANES), F32), pltpu.VMEM((2, t, 1), F32)],
        compiler_params=_params("parallel", "arbitrary"),
    )(*bounds, z, z, z, dy, y, lse, ccol, crow)


def _fattn_dkv(z, dyb, lse_row, delta_row, ccol, crow, bounds, *, qcol, kcol, vcol, t, name):
    s = z.shape[0]
    npairs, nq = crow.shape[0], crow.shape[1]

    def body(qn_ref, kn_ref, cs_ref, ce_ref, k_ref, v_ref, q_ref, dy_ref, lse_ref, dl_ref, cc_ref, cr_ref,
             dk_ref, dv_ref, dc_ref, dk_scr, dv_scr, dc_scr):
        pair, j = pl.program_id(0), pl.program_id(1)
        is_a = lax.broadcasted_iota(jnp.int32, (t, LANES), 1) < HEAD_DIM
        kh = _split_heads(k_ref[...], is_a)
        vh = _split_heads(v_ref[...], is_a)
        ck = (cc_ref[:, 0:1], cc_ref[:, 1:2])
        dk_scr[...] = jnp.zeros(dk_scr.shape, F32)
        dv_scr[...] = jnp.zeros(dv_scr.shape, F32)
        dc_scr[...] = jnp.zeros(dc_scr.shape, F32)

        def head_step(h, i, diagonal):
            off = pl.multiple_of(i * t, t)
            q2 = q_ref[pl.ds(off, t), :] * jnp.asarray(_QK_SCALE, BF16)
            dy2 = dy_ref[pl.ds(off, t), :]
            st = _dot(kh[h], q2, _NT) + (cr_ref[i][h:h + 1, :] - ck[h])
            if diagonal:
                row = lax.broadcasted_iota(jnp.int32, (t, t), 0)
                col = lax.broadcasted_iota(jnp.int32, (t, t), 1)
                st = jnp.where(col >= row, st, NEG_BIG)
            pt = jnp.exp(st - lse_ref[i][h:h + 1, :])
            dv_scr[...] += _dot(pt.astype(BF16), _split_heads(dy2, is_a)[h], _NN)
            dpt = _dot(vh[h], dy2, _NT)
            dst = pt * (dpt - dl_ref[i][h:h + 1, :])
            dk_scr[...] += _dot(dst.astype(BF16), _split_heads(q2, is_a)[h], _NN)
            dc_scr[h] += jnp.sum(dst, axis=1, keepdims=True)

        def loop_body(i, carry):
            _for_active_heads((qn_ref, kn_ref, cs_ref, ce_ref), pair, i, j, nq, lambda h: head_step(h, i, False))
            return carry

        for h in range(2):
            head_step(h, j, True)
        lax.fori_loop(j + 1, nq, loop_body, 0)
        dk_ref[...] = dk_scr[...].astype(dk_ref.dtype)
        dv_ref[...] = dv_scr[...].astype(dv_ref.dtype)
        dc_ref[:, 0:1] = -dc_scr[0]
        dc_ref[:, 1:2] = -dc_scr[1]

    col_vec = pl.BlockSpec((None, t, 2), lambda p, j: (p, j, 0))
    rows = pl.BlockSpec((None, nq, 2, t), lambda p, j: (p, 0, 0, 0))
    tile = pl.BlockSpec((t, LANES), lambda p, j: (j, p))
    return pl.pallas_call(
        body, name=name, grid=(npairs, nq),
        in_specs=[
            _SMEM, _SMEM, _SMEM, _SMEM,
            pl.BlockSpec((t, LANES), lambda p, j: (j, kcol + p)),
            pl.BlockSpec((t, LANES), lambda p, j: (j, vcol + p)),
            pl.BlockSpec((s, LANES), lambda p, j: (0, qcol + p)),
            pl.BlockSpec((s, LANES), lambda p, j: (0, p)),
            rows, rows, col_vec, rows,
        ],
        out_specs=[tile, tile, col_vec],
        out_shape=[jax.ShapeDtypeStruct((s, npairs * LANES), BF16), jax.ShapeDtypeStruct((s, npairs * LANES), BF16),
                   jax.ShapeDtypeStruct((npairs, s, 2), F32)],
        scratch_shapes=[pltpu.VMEM((t, LANES), F32), pltpu.VMEM((t, LANES), F32), pltpu.VMEM((2, t, 1), F32)],
        compiler_params=_params("parallel", "arbitrary"),
    )(*bounds, z, z, z, dyb, lse_row, delta_row, ccol, crow)


def _log_sigmoid(x):
    return jnp.minimum(x, 0.0) - jnp.log(1.0 + jnp.exp(-jnp.abs(x)))


def _cumsum_fwd(zf, b, *, name, t=512):
    s, w = zf.shape
    t = _row_tile(s, t)

    def body(zf_ref, b_ref, c_ref, carry):
        @pl.when(pl.program_id(0) == 0)
        def _():
            carry[...] = jnp.zeros(carry.shape, F32)

        lf = _log_sigmoid(zf_ref[...] + b_ref[...])
        row = lax.broadcasted_iota(jnp.int32, (t, t), 0)
        col = lax.broadcasted_iota(jnp.int32, (t, t), 1)
        tri = (row >= col).astype(F32)
        c = lax.dot_general(tri, lf, _NN, precision=lax.Precision.HIGHEST, preferred_element_type=F32) + carry[...]
        c_ref[...] = c
        carry[...] = c[t - 1:t, :]

    return pl.pallas_call(
        body, name=name, grid=(s // t,),
        in_specs=[pl.BlockSpec((t, w), lambda i: (i, 0)), pl.BlockSpec((1, w), lambda i: (0, 0))],
        out_specs=pl.BlockSpec((t, w), lambda i: (i, 0)),
        out_shape=jax.ShapeDtypeStruct((s, w), F32),
        scratch_shapes=[pltpu.VMEM((1, w), F32)],
        compiler_params=_params("arbitrary"),
    )(zf, b)


def _cumsum_bwd(dcq, dck, zf, b, *, name, t=512):
    s, w = zf.shape
    t = _row_tile(s, t)
    nb = s // t

    def body(dcq_ref, dck_ref, zf_ref, b_ref, dzf_ref, db_ref, carry):
        @pl.when(pl.program_id(0) == 0)
        def _():
            carry[...] = jnp.zeros(carry.shape, F32)
            db_ref[...] = jnp.zeros(db_ref.shape, F32)

        row = lax.broadcasted_iota(jnp.int32, (t, t), 0)
        col = lax.broadcasted_iota(jnp.int32, (t, t), 1)
        tri = (row <= col).astype(F32)
        dc = dcq_ref[...] + dck_ref[...]
        dlf = lax.dot_general(tri, dc, _NN, precision=lax.Precision.HIGHEST, preferred_element_type=F32) + carry[...]
        carry[...] = dlf[0:1, :]
        dzf = dlf * jax.nn.sigmoid(-(zf_ref[...] + b_ref[...]))
        dzf_ref[...] = dzf
        db_ref[...] += jnp.sum(dzf, axis=0, keepdims=True)

    blk = pl.BlockSpec((t, w), lambda i: (nb - 1 - i, 0))
    vec = pl.BlockSpec((1, w), lambda i: (0, 0))
    return pl.pallas_call(
        body, name=name, grid=(nb,), in_specs=[blk, blk, blk, vec], out_specs=[blk, vec],
        out_shape=[jax.ShapeDtypeStruct((s, w), F32), jax.ShapeDtypeStruct((1, w), F32)],
        scratch_shapes=[pltpu.VMEM((1, w), F32)],
        compiler_params=_params("arbitrary"),
    )(dcq, dck, zf, b)


HALO = 8


def _rms_rows(v):
    return lax.rsqrt(jnp.mean(v * v, axis=-1, keepdims=True) + EPS)


def _mixpost_fwd(z, yatt, wconv, gc, ga, *, name, tr=512):
    s = z.shape[0]
    cw, aw = gc.shape[-1], ga.shape[-1]
    tr = _row_tile(s, tr)

    def body(zb_ref, zc_ref, zv_ref, ya_ref, w_ref, gc_ref, ga_ref, ycat_ref, cv_ref, u_scr):
        @pl.when(pl.program_id(0) == 0)
        def _():
            u_scr[0:HALO, :] = jnp.zeros((HALO, cw), F32)

        u = zc_ref[...].astype(F32) * zv_ref[...].astype(F32)
        u_scr[HALO:HALO + tr, :] = u
        cv = w_ref[0:1, :] * u_scr[HALO - 2:HALO - 2 + tr, :] + w_ref[1:2, :] * u_scr[HALO - 1:HALO - 1 + tr, :] + w_ref[2:3, :] * u
        u_scr[0:HALO, :] = u_scr[tr:tr + HALO, :]
        cv_ref[...] = cv
        yc = zb_ref[...].astype(F32) * cv
        ya = ya_ref[...]
        ycat_ref[:, :cw] = (yc * _rms_rows(yc) * gc_ref[...]).astype(ycat_ref.dtype)
        ycat_ref[:, cw:] = (ya * _rms_rows(ya) * ga_ref[...]).astype(ycat_ref.dtype)

    return pl.pallas_call(
        body, name=name, grid=(s // tr,),
        in_specs=[
            pl.BlockSpec((tr, cw), lambda i: (i, 0)), pl.BlockSpec((tr, cw), lambda i: (i, 1)),
            pl.BlockSpec((tr, cw), lambda i: (i, 2)), pl.BlockSpec((tr, aw), lambda i: (i, 0)),
            pl.BlockSpec((HALO, cw), lambda i: (0, 0)), pl.BlockSpec((1, cw), lambda i: (0, 0)),
            pl.BlockSpec((1, aw), lambda i: (0, 0)),
        ],
        out_specs=[pl.BlockSpec((tr, cw + aw), lambda i: (i, 0)), pl.BlockSpec((tr, cw), lambda i: (i, 0))],
        out_shape=[jax.ShapeDtypeStruct((s, cw + aw), BF16), jax.ShapeDtypeStruct((s, cw), F32)],
        scratch_shapes=[pltpu.VMEM((tr + HALO, cw), F32)],
        compiler_params=_params("arbitrary"),
    )(z, z, z, yatt, wconv, gc.reshape(1, cw), ga.reshape(1, aw))


def _mixpost_bwd(dycat, z, yatt, cv, wconv, gc, ga, *, name, tr=512):
    s = z.shape[0]
    cw, aw = gc.shape[-1], ga.shape[-1]
    tr = _row_tile(s, tr)
    nb = s // tr

    def body(dy_ref, zb_ref, zc_ref, zv_ref, ya_ref, cv_ref, w_ref, gc_ref, ga_ref,
             dz_ref, dya_ref, dw_ref, dgc_ref, dga_ref, d_scr):
        @pl.when(pl.program_id(0) == 0)
        def _():
            d_scr[tr:tr + HALO, :] = jnp.zeros((HALO, cw), F32)
            dw_ref[...] = jnp.zeros(dw_ref.shape, F32)
            dgc_ref[...] = jnp.zeros(dgc_ref.shape, F32)
            dga_ref[...] = jnp.zeros(dga_ref.shape, F32)

        zb, zc, zv = zb_ref[...].astype(F32), zc_ref[...].astype(F32), zv_ref[...].astype(F32)
        cvv = cv_ref[...]

        def norm_bwd(v, dn, g):
            r = _rms_rows(v)
            vh = v * r
            dvh = dn * g
            return r * (dvh - vh * jnp.mean(dvh * vh, axis=-1, keepdims=True)), jnp.sum(dn * vh, axis=0, keepdims=True)

        dyc, dgc = norm_bwd(zb * cvv, dy_ref[:, :cw], gc_ref[...])
        dya, dga = norm_bwd(ya_ref[...], dy_ref[:, cw:], ga_ref[...])
        dgc_ref[...] += dgc
        dga_ref[...] += dga
        dya_ref[...] = dya
        dcv = dyc * zb
        d_scr[0:tr, :] = dcv
        d1 = d_scr[1:tr + 1, :]
        d2 = d_scr[2:tr + 2, :]
        du = w_ref[2:3, :] * dcv + w_ref[1:2, :] * d1 + w_ref[0:1, :] * d2
        u = zc * zv
        dw_ref[0:1, :] += jnp.sum(u * d2, axis=0, keepdims=True)
        dw_ref[1:2, :] += jnp.sum(u * d1, axis=0, keepdims=True)
        dw_ref[2:3, :] += jnp.sum(u * dcv, axis=0, keepdims=True)
        d_scr[tr:tr + HALO, :] = d_scr[0:HALO, :]
        dz_ref[:, :cw] = (dyc * cvv).astype(dz_ref.dtype)
        dz_ref[:, cw:2 * cw] = (du * zv).astype(dz_ref.dtype)
        dz_ref[:, 2 * cw:] = (du * zc).astype(dz_ref.dtype)

    def rows(width, colblk=0):
        return pl.BlockSpec((tr, width), lambda i: (nb - 1 - i, colblk))

    def fixed(r, width):
        return pl.BlockSpec((r, width), lambda i: (0, 0))

    return pl.pallas_call(
        body, name=name, grid=(nb,),
        in_specs=[rows(cw + aw), rows(cw, 0), rows(cw, 1), rows(cw, 2), rows(aw), rows(cw),
                  fixed(HALO, cw), fixed(1, cw), fixed(1, aw)],
        out_specs=[rows(3 * cw), rows(aw), fixed(HALO, cw), fixed(1, cw), fixed(1, aw)],
        out_shape=[jax.ShapeDtypeStruct((s, 3 * cw), BF16), jax.ShapeDtypeStruct((s, aw), F32),
                   jax.ShapeDtypeStruct((HALO, cw), F32), jax.ShapeDtypeStruct((1, cw), F32),
                   jax.ShapeDtypeStruct((1, aw), F32)],
        scratch_shapes=[pltpu.VMEM((tr + HALO, cw), F32)],
        compiler_params=_params("arbitrary"),
    )(dycat, z, z, z, yatt, cv, wconv, gc.reshape(1, cw), ga.reshape(1, aw))


def _xattn_fwd(q, kv, *, name, tq=1024):
    s, d = q.shape
    m = kv.shape[0]
    dh = d // N_XHEADS
    scale = dh ** -0.5
    tq = _row_tile(s, tq)

    def body(q_ref, kv_ref, o_ref):
        for h in range(N_XHEADS):
            lo, hi = h * dh, (h + 1) * dh
            sc = _dot(q_ref[:, lo:hi], kv_ref[:, lo:hi], _NT) * scale
            p = jnp.exp(sc - jnp.max(sc, axis=1, keepdims=True))
            o = _dot(p.astype(BF16), kv_ref[:, d + lo:d + hi], _NN) / jnp.sum(p, axis=1, keepdims=True)
            o_ref[:, lo:hi] = o.astype(o_ref.dtype)

    return pl.pallas_call(
        body, name=name, grid=(s // tq,),
        in_specs=[pl.BlockSpec((tq, d), lambda i: (i, 0)), pl.BlockSpec((m, 2 * d), lambda i: (0, 0))],
        out_specs=pl.BlockSpec((tq, d), lambda i: (i, 0)),
        out_shape=jax.ShapeDtypeStruct((s, d), BF16),
        compiler_params=_params("parallel"),
    )(q, kv)


def _xattn_bwd(q, kv, do, *, name, tq=1024):
    s, d = q.shape
    m = kv.shape[0]
    dh = d // N_XHEADS
    scale = dh ** -0.5
    tq = _row_tile(s, tq)

    def body(q_ref, kv_ref, do_ref, dq_ref, dkv_ref):
        @pl.when(pl.program_id(0) == 0)
        def _():
            dkv_ref[...] = jnp.zeros(dkv_ref.shape, F32)

        for h in range(N_XHEADS):
            lo, hi = h * dh, (h + 1) * dh
            qh, kh, vh, doh = q_ref[:, lo:hi], kv_ref[:, lo:hi], kv_ref[:, d + lo:d + hi], do_ref[:, lo:hi]
            sc = _dot(qh, kh, _NT) * scale
            e = jnp.exp(sc - jnp.max(sc, axis=1, keepdims=True))
            p = e / jnp.sum(e, axis=1, keepdims=True)
            dp = _dot(doh, vh, _NT)
            ds = p * (dp - jnp.sum(dp * p, axis=1, keepdims=True))
            dsb = ds.astype(BF16)
            dq_ref[:, lo:hi] = (_dot(dsb, kh, _NN) * scale).astype(dq_ref.dtype)
            dkv_ref[:, lo:hi] += _dot(dsb, qh, _TN) * scale
            dkv_ref[:, d + lo:d + hi] += _dot(p.astype(BF16), doh, _TN)

    return pl.pallas_call(
        body, name=name, grid=(s // tq,),
        in_specs=[pl.BlockSpec((tq, d), lambda i: (i, 0)), pl.BlockSpec((m, 2 * d), lambda i: (0, 0)),
                  pl.BlockSpec((tq, d), lambda i: (i, 0))],
        out_specs=[pl.BlockSpec((tq, d), lambda i: (i, 0)), pl.BlockSpec((m, 2 * d), lambda i: (0, 0))],
        out_shape=[jax.ShapeDtypeStruct((s, d), BF16), jax.ShapeDtypeStruct((m, 2 * d), F32)],
        compiler_params=_params("arbitrary"),
    )(q, kv, do)


def _final_loss(x, g, target, *, name, tr=512):
    s, d = x.shape
    tr = _row_tile(s, tr)

    def body(x_ref, g_ref, t_ref, dx_ref, dg_ref, sq_ref):
        @pl.when(pl.program_id(0) == 0)
        def _():
            dg_ref[...] = jnp.zeros(dg_ref.shape, F32)
            sq_ref[...] = jnp.zeros(sq_ref.shape, F32)

        xf = x_ref[...]
        r = _rms_rows(xf)
        xhat = xf * r
        err = xhat * g_ref[...] - t_ref[...]
        sq_ref[...] += jnp.sum(err * err, axis=0, keepdims=True)
        dy = err * (1.0 / d)
        dg_ref[...] += jnp.sum(dy * xhat, axis=0, keepdims=True)
        dxhat = dy * g_ref[...]
        dx_ref[...] = r * (dxhat - xhat * jnp.mean(dxhat * xhat, axis=-1, keepdims=True))

    row = pl.BlockSpec((tr, d), lambda i: (i, 0))
    vec = pl.BlockSpec((1, d), lambda i: (0, 0))
    return pl.pallas_call(
        body, name=name, grid=(s // tr,), in_specs=[row, vec, row], out_specs=[row, vec, vec],
        out_shape=[jax.ShapeDtypeStruct((s, d), F32), jax.ShapeDtypeStruct((1, d), F32), jax.ShapeDtypeStruct((1, d), F32)],
        compiler_params=_params("arbitrary"),
    )(x, g.reshape(1, d), target)


def _adamw(w, g, m, v, *, name, tr=512):
    shape = w.shape
    cols = shape[-1]
    rows = w.size // cols
    tr = tr if rows % tr == 0 else rows

    def body(w_ref, g_ref, m_ref, v_ref, d_ref, nm_ref, nv_ref):
        gf = g_ref[...]
        nm = ADAM_B1 * m_ref[...] + (1.0 - ADAM_B1) * gf
        nv = ADAM_B2 * v_ref[...] + (1.0 - ADAM_B2) * (gf * gf)
        m_hat = nm / (1.0 - ADAM_B1 ** ADAM_STEP)
        v_hat = nv / (1.0 - ADAM_B2 ** ADAM_STEP)
        d_ref[...] = -ADAM_LR * (m_hat / (jnp.sqrt(v_hat) + ADAM_EPS) + ADAM_WD * w_ref[...])
        nm_ref[...] = nm
        nv_ref[...] = nv

    blk = pl.BlockSpec((tr, cols), lambda i: (i, 0))
    out = jax.ShapeDtypeStruct((rows, cols), F32)
    outs = pl.pallas_call(
        body, name=name, grid=(rows // tr,), in_specs=[blk] * 4, out_specs=[blk] * 3, out_shape=[out] * 3,
        compiler_params=_params("parallel"),
    )(*[t.reshape(rows, cols) for t in (w, g, m, v)])
    return tuple(o.reshape(shape) for o in outs)


_HBM = pl.BlockSpec(memory_space=pl.ANY)
_MESH_ID = pl.DeviceIdType.MESH


def _place():
    x, y, c = (lax.axis_index(a) for a in MESH_AXES)
    return x, y, c, [(1 - x, y), (x, 1 - y), (1 - x, 1 - y)]


def _remote(src, dst, send_sems, recv_sems, k, to):
    return pltpu.make_async_remote_copy(src_ref=src, dst_ref=dst, send_sem=send_sems.at[k], recv_sem=recv_sems.at[k],
                                        device_id=to, device_id_type=_MESH_ID)


def _comm_call(body, arrays, out_shapes, n_remote, name):
    return pl.pallas_call(
        body, name=name, in_specs=[_HBM] * len(arrays), out_specs=[_HBM] * len(out_shapes), out_shape=out_shapes,
        scratch_shapes=[pltpu.SemaphoreType.DMA((n_remote,)), pltpu.SemaphoreType.DMA((n_remote,)),
                        pltpu.SemaphoreType.DMA((len(arrays),))],
    )(*arrays)


def _allgather_weights(halves, *, name):
    n = len(halves)

    def body(*refs):
        w, out = refs[:n], refs[n:2 * n]
        send_sems, recv_sems, local_sems = refs[2 * n:]
        x, y, c, chips = _place()
        me = 2 * x + y
        sibling = (x, y, 1 - c)
        slots = [2 * px + py for px, py in chips]

        def copy(t, k, slot, half, to, src=None):
            dst = out[t].at[slot, half]
            return _remote(dst if src is None else src, dst, send_sems, recv_sems, 7 * t + k, to)

        local = [pltpu.make_async_copy(w[t].at[c], out[t].at[me, c], local_sems.at[t]) for t in range(n)]
        first = []
        for t in range(n):
            local[t].start()
            first.append(copy(t, 0, me, c, sibling, src=w[t].at[c]))
            first += [copy(t, 1 + j, me, c, (px, py, c), src=w[t].at[c]) for j, (px, py) in enumerate(chips)]
        for cp in first:
            cp.start()
        passed = []
        for j, (px, py) in enumerate(chips):
            for t in range(n):
                copy(t, 1 + j, slots[j], c, (px, py, c)).wait_recv()
                passed.append(copy(t, 4 + j, slots[j], c, sibling))
                passed[-1].start()
        for t in range(n):
            copy(t, 0, me, 1 - c, sibling).wait_recv()
            for j in range(3):
                copy(t, 4 + j, slots[j], 1 - c, sibling).wait_recv()
        for cp in first + passed:
            cp.wait_send()
        for cp in local:
            cp.wait()

    shapes = [jax.ShapeDtypeStruct((N_CHIPS,) + h.shape, h.dtype) for h in halves]
    return _comm_call(body, halves, shapes, 7 * n, name)


def _swap_halves(gs, *, name):
    n = len(gs)

    def body(*refs):
        g, out = refs[:n], refs[n:2 * n]
        send_sems, recv_sems, _ = refs[2 * n:]
        x, y, c, _ = _place()
        copies = [_remote(g[t].at[1 - c], out[t], send_sems, recv_sems, t, (x, y, 1 - c)) for t in range(n)]
        for cp in copies:
            cp.start()
        for cp in copies:
            cp.wait()

    return _comm_call(body, gs, [jax.ShapeDtypeStruct(g.shape[1:], g.dtype) for g in gs], n, name)


def _scatter_chips(ps, *, name):
    n = len(ps)

    def body(*refs):
        p, out = refs[:n], refs[n:2 * n]
        send_sems, recv_sems, local_sems = refs[2 * n:]
        x, y, c, chips = _place()
        me = 2 * x + y
        local = [pltpu.make_async_copy(p[t].at[me], out[t].at[me], local_sems.at[t]) for t in range(n)]
        sends = [_remote(p[t].at[2 * px + py], out[t].at[me], send_sems, recv_sems, 3 * t + j, (px, py, c))
                 for t in range(n) for j, (px, py) in enumerate(chips)]
        for cp in local + sends:
            cp.start()
        for t in range(n):
            for j, (px, py) in enumerate(chips):
                slot = out[t].at[2 * px + py]
                _remote(slot, slot, send_sems, recv_sems, 3 * t + j, (px, py, c)).wait_recv()
        for cp in sends:
            cp.wait_send()
        for cp in local:
            cp.wait()

    return _comm_call(body, ps, [jax.ShapeDtypeStruct(p.shape, p.dtype) for p in ps], 3 * n, name)


def _join_halves(rs, *, name):
    n = len(rs)

    def body(*refs):
        r, out = refs[:n], refs[n:2 * n]
        send_sems, recv_sems, _ = refs[2 * n:]
        x, y, c, _ = _place()
        sends = [_remote(r[t].at[c], out[t].at[c], send_sems, recv_sems, t, (x, y, 1 - c)) for t in range(n)]
        for cp in sends:
            cp.start()
        for t in range(n):
            slot = out[t].at[1 - c]
            _remote(slot, slot, send_sems, recv_sems, t, (x, y, 1 - c)).wait_recv()
        for cp in sends:
            cp.wait_send()

    return pl.pallas_call(
        body, name=name, in_specs=[_HBM] * n, out_specs=[_HBM] * n,
        out_shape=[jax.ShapeDtypeStruct(r.shape, r.dtype) for r in rs],
        input_output_aliases={t: t for t in range(n)},
        scratch_shapes=[pltpu.SemaphoreType.DMA((n,)), pltpu.SemaphoreType.DMA((n,)), pltpu.SemaphoreType.DMA((n,))],
    )(*rs)


def _pick_rows(rows, cap=512):
    for t in range(min(rows, cap), 0, -1):
        if rows % t == 0 and t % 16 == 0:
            return t
    return rows


def _add_halves(g, recv, c, out_dtype, *, name):
    cols = g.shape[-1]
    rows = recv.size // (N_CHIPS * cols)
    tr = _pick_rows(rows)

    def body(c_ref, g_ref, r_ref, o_ref):
        o_ref[...] = (g_ref[...] + r_ref[...]).astype(o_ref.dtype)

    blk = pl.BlockSpec((None, tr, cols), lambda b, i, c_ref: (b, i, 0))
    out = pl.pallas_call(
        body, name=name,
        grid_spec=pltpu.PrefetchScalarGridSpec(
            num_scalar_prefetch=1, grid=(N_CHIPS, rows // tr),
            in_specs=[pl.BlockSpec((None, None, tr, cols), lambda b, i, c_ref: (c_ref[0], b, i, 0)), blk],
            out_specs=blk),
        out_shape=jax.ShapeDtypeStruct((N_CHIPS, rows, cols), out_dtype),
        compiler_params=_params("parallel", "parallel"),
    )(c.reshape(1).astype(jnp.int32), g.reshape(2, N_CHIPS, rows, cols), recv.reshape(N_CHIPS, rows, cols))
    return out.reshape(recv.shape)


def _sum_slots(p, c, *, name):
    cols = p.shape[-1]
    rows = p.size // (N_CHIPS * cols)
    tr = _pick_rows(rows)

    def body(c_ref, p_ref, o_ref):
        acc = p_ref[0].astype(F32)
        for q in range(1, N_CHIPS):
            acc = acc + p_ref[q].astype(F32)
        o_ref[...] = acc

    out = pl.pallas_call(
        body, name=name,
        grid_spec=pltpu.PrefetchScalarGridSpec(
            num_scalar_prefetch=1, grid=(rows // tr,),
            in_specs=[pl.BlockSpec((N_CHIPS, tr, cols), lambda i, c_ref: (0, i, 0))],
            out_specs=pl.BlockSpec((None, tr, cols), lambda i, c_ref: (c_ref[0], i, 0))),
        out_shape=jax.ShapeDtypeStruct((2, rows, cols), F32),
        compiler_params=_params("parallel"),
    )(c.reshape(1).astype(jnp.int32), p.reshape(N_CHIPS, rows, cols))
    return out.reshape((2,) + p.shape[1:])


GROUPS = (("gu", ("w_ffn1_gu", "w_ffn2_gu"), "col"), ("down", ("w_ffn1_down", "w_ffn2_down"), "row"),
          ("square", ("w_mix_out", "w_xq", "w_xo"), "row"), ("mix_in", ("w_mix_in",), "col"), ("xkv", ("w_xkv",), "col"))
REPLICATED = ("g_ffn1", "g_mix", "b_f", "g_conv_out", "g_att_out", "g_xattn", "g_mem", "g_ffn2", "g_final")
WEIGHTS = ("g_ffn1", "w_ffn1_gu", "w_ffn1_down", "g_mix", "w_mix_in", "w_conv", "b_f", "g_conv_out", "g_att_out",
           "w_mix_out", "g_xattn", "g_mem", "w_xq", "w_xkv", "w_xo", "g_ffn2", "w_ffn2_gu", "w_ffn2_down", "g_final")
SMALL_COLS = 1024
SMALL_ROW_UNIT = 16


def _gather_weights(shards):
    packs = []
    for _, members, _ in GROUPS:
        hs = []
        for name in members:
            l, a, b = shards[name].shape
            hs.append(shards[name].astype(BF16).reshape(l, 2, a // 2, b).transpose(1, 0, 2, 3))
        packs.append(jnp.stack(hs, axis=1))
    wc = shards["w_conv"]
    packs.append(jnp.stack([wc, wc]))
    gathered = _allgather_weights(packs, name="allgather_weights")
    full = {}
    for (_, members, kind), got in zip(GROUPS, gathered):
        _, _, g, l, a2, b = got.shape
        if kind == "col":
            whole = got.transpose(2, 3, 1, 4, 0, 5).reshape(g, l, 2 * a2, N_CHIPS * b)
        else:
            whole = got.transpose(2, 3, 0, 1, 4, 5).reshape(g, l, N_CHIPS * 2 * a2, b)
        for gi, name in enumerate(members):
            full[name] = whole[gi]
    l, k, b = wc.shape
    full["w_conv"] = gathered[-1][:, 0].transpose(1, 2, 0, 3).reshape(l, k, N_CHIPS * b)
    return full


def _small_rows(v):
    flat = v.reshape(-1)
    return jnp.pad(flat, (0, -flat.shape[0] % SMALL_COLS)).reshape(-1, SMALL_COLS)


def _reduce_gradients(grads, c):
    packs = []
    for _, members, kind in GROUPS:
        cut = []
        for name in members:
            l, a, b = grads[name].shape
            if kind == "col":
                cut.append(grads[name].reshape(l, 2, a // 2, N_CHIPS, b // N_CHIPS).transpose(1, 3, 0, 2, 4))
            else:
                cut.append(grads[name].reshape(l, N_CHIPS, 2, a // (2 * N_CHIPS), b).transpose(2, 1, 0, 3, 4))
        packs.append(jnp.stack(cut, axis=2))
    rep = jnp.concatenate([_small_rows(grads[n]) for n in REPLICATED])
    l, k, b = grads["w_conv"].shape
    conv = grads["w_conv"].reshape(l, k, N_CHIPS, b // N_CHIPS).transpose(2, 0, 1, 3)
    conv_rows = [_small_rows(conv[q]) for q in range(N_CHIPS)]
    n_rows = rep.shape[0] + conv_rows[0].shape[0]
    fill = jnp.zeros((-n_rows % SMALL_ROW_UNIT, SMALL_COLS), F32)
    small = jnp.stack([jnp.concatenate([rep, conv_rows[q], fill]) for q in range(N_CHIPS)])
    half_rows = small.shape[1] // 2
    packs.append(small.reshape(N_CHIPS, 2, half_rows, SMALL_COLS).transpose(1, 0, 2, 3))

    from_sibling = _swap_halves(packs, name="grad_swap_halves")
    wire = [BF16] * len(GROUPS) + [F32]
    chip_sums = [_add_halves(g, r, c, dt, name=f"grad_add_halves_{i}")
                 for i, (g, r, dt) in enumerate(zip(packs, from_sibling, wire))]
    from_chips = _scatter_chips(chip_sums, name="grad_scatter_chips")
    halves = [_sum_slots(p, c, name=f"grad_sum_chips_{i}") for i, p in enumerate(from_chips)]
    reduced = _join_halves(halves, name="grad_join_halves")

    out = {}
    for (_, members, _), r in zip(GROUPS, reduced):
        _, g, l, a2, b = r.shape
        whole = r.transpose(1, 2, 0, 3, 4).reshape(g, l, 2 * a2, b)
        for gi, name in enumerate(members):
            out[name] = whole[gi]
    rows = reduced[-1].reshape(-1, SMALL_COLS)
    off = 0
    for name in REPLICATED:
        n = -(-grads[name].size // SMALL_COLS)
        out[name] = rows[off:off + n].reshape(-1)[:grads[name].size].reshape(grads[name].shape)
        off += n
    l, k, b = grads["w_conv"].shape
    out["w_conv"] = rows[off:off + conv_rows[0].shape[0]].reshape(-1)[:l * k * b // N_CHIPS].reshape(l, k, b // N_CHIPS)
    return out


def _ffn_fwd(x, g, w_gu, w_down, tag):
    h = _rmsnorm_fwd(x, g, name=f"{tag}_norm")
    gu = _matmul(h, w_gu, "nn", BF16, tm=2048, tn=512, tk=w_gu.shape[0], name=f"{tag}_gu")
    a = _swiglu_fwd(gu, name=f"{tag}_act")
    y = _matmul(a, w_down, "nn", F32, tm=1024, tn=1024, tk=w_down.shape[0], alpha=0.5, residual=x, name=f"{tag}_down")
    return y, (x, h, gu, a)


def _ffn_bwd(dy, saved, g, w_gu, w_down, tag):
    x, h, gu, a = saved
    f = w_down.shape[0]
    da = _matmul(dy, w_down, "nt", BF16, tm=1024, tn=f // 2, tk=w_down.shape[1], alpha=0.5, name=f"{tag}_da")
    dw_down = _matmul(a, dy, "tn", F32, tm=f // 2, tn=1024, tk=1024, alpha=0.5, name=f"{tag}_dwdown")
    dgu = _swiglu_bwd(da, gu, name=f"{tag}_dact")
    dw_gu = _matmul(h, dgu, "tn", F32, tm=1024, tn=512, tk=2048, name=f"{tag}_dwgu")
    dh = _matmul(dgu, w_gu, "nt", F32, tm=1024, tn=1024, tk=f, name=f"{tag}_dh")
    dx, dg = _rmsnorm_bwd(x, g, dh, dy, name=f"{tag}_dnorm")
    return dx, dg, dw_gu, dw_down


def _tokens_on_lanes(v, t):
    p, s, _ = v.shape
    return v.transpose(0, 2, 1).reshape(p, 2, s // t, t).transpose(0, 2, 1, 3)


def _mix_fwd(x, w, l, tag):
    s, d = x.shape
    gc, ga = w["g_conv_out"][l], w["g_att_out"][l]
    cw, aw = gc.shape[0], ga.shape[0]
    nh = aw // HEAD_DIM
    zw = 3 * cw + 3 * aw
    t = 512 if s >= 2048 else s // 4
    cols = dict(qcol=3 * cw // LANES, kcol=(3 * cw + aw) // LANES, vcol=(3 * cw + 2 * aw) // LANES, t=t)
    h = _rmsnorm_fwd(x, w["g_mix"][l], name=f"{tag}_norm")
    w_in = w["w_mix_in"][l]
    w_main = w_in[:, :zw]
    w_f = jnp.pad(w_in[:, zw:], ((0, 0), (0, LANES - nh)))
    z = _matmul(h, w_main, "nn", BF16, tm=2048, tn=512, tk=d, name=f"{tag}_in")
    zf = _matmul(h, w_f, "nn", F32, tm=2048, tn=LANES, tk=d, name=f"{tag}_in_f")
    b = jnp.pad(w["b_f"][l], (0, LANES - nh)).reshape(1, LANES)
    c = _cumsum_fwd(zf, b, name=f"{tag}_cumsum")
    ccol = c[:, :nh].reshape(s, nh // 2, 2).transpose(1, 0, 2)
    crow = _tokens_on_lanes(ccol, t)
    assert (3 * cw) % aw == 0
    bounds = _attn_bounds(z, c, qcol_units=3 * cw // aw, kcol_units=3 * cw // aw + 1, aw=aw, t=t, name=f"{tag}_bounds")
    yatt, lse = _fattn_fwd(z, ccol, crow, bounds, name=f"{tag}_attn", **cols)
    wc = jnp.pad(w["w_conv"][l], ((0, HALO - CONV_K), (0, 0)))
    ycat, cv = _mixpost_fwd(z, yatt, wc, gc, ga, name=f"{tag}_post")
    y = _matmul(ycat, w["w_mix_out"][l], "nn", F32, tm=1024, tn=1024, tk=d, residual=x, name=f"{tag}_out")
    return y, (x, h, w_main, w_f, z, zf, b, ccol, crow, bounds, yatt, lse, wc, ycat, cv, cols)


def _mix_bwd(dy, saved, w, l, tag):
    x, h, w_main, w_f, z, zf, b, ccol, crow, bounds, yatt, lse, wc, ycat, cv, cols = saved
    s, d = x.shape
    gc, ga = w["g_conv_out"][l], w["g_att_out"][l]
    nh = ga.shape[0] // HEAD_DIM
    zw = w_main.shape[1]
    t = cols["t"]
    dycat = _matmul(dy, w["w_mix_out"][l], "nt", F32, tm=1024, tn=1024, tk=d, name=f"{tag}_dycat")
    dw_out = _matmul(ycat, dy, "tn", F32, tm=1024, tn=1024, tk=1024, name=f"{tag}_dwout")
    dz_conv, dyatt, dwc, dgc, dga = _mixpost_bwd(dycat, z, yatt, cv, wc, gc, ga, name=f"{tag}_dpost")
    dq, delta, dcq = _fattn_dq(z, dyatt, yatt, lse, ccol, crow, bounds, name=f"{tag}_attn_dq", **cols)
    dk, dv, dck = _fattn_dkv(z, dyatt.astype(BF16), _tokens_on_lanes(lse, t), _tokens_on_lanes(delta, t), ccol, crow,
                             bounds, name=f"{tag}_attn_dkv", **cols)
    def heads_on_lanes(v):
        return jnp.pad(v.transpose(1, 0, 2).reshape(s, nh), ((0, 0), (0, LANES - nh)))

    dzf, db = _cumsum_bwd(heads_on_lanes(dcq), heads_on_lanes(dck), zf, b, name=f"{tag}_dcumsum")
    dz = jnp.concatenate([dz_conv, dq, dk, dv], axis=1)
    dw_main = _matmul(h, dz, "tn", F32, tm=1024, tn=512, tk=2048, name=f"{tag}_dwin")
    dw_f = _matmul(h, dzf, "tn", F32, tm=1024, tn=LANES, tk=2048, name=f"{tag}_dwin_f")
    dh = _matmul(dz, w_main, "nt", F32, tm=1024, tn=1024, tk=zw // 2, name=f"{tag}_dh")
    dh = _matmul(dzf, w_f, "nt", F32, tm=1024, tn=1024, tk=LANES, residual=dh, name=f"{tag}_dh_f")
    dx, dg = _rmsnorm_bwd(x, w["g_mix"][l], dh, dy, name=f"{tag}_dnorm")
    grads = dict(g_mix=dg, w_mix_in=jnp.concatenate([dw_main, dw_f[:, :nh]], axis=1), w_conv=dwc[:CONV_K], b_f=db[0, :nh],
                 g_conv_out=dgc[0], g_att_out=dga[0], w_mix_out=dw_out)
    return dx, grads


def _xattn_block_fwd(x, mem, w, l, tag):
    d = x.shape[1]
    h = _rmsnorm_fwd(x, w["g_xattn"][l], name=f"{tag}_norm")
    mn = _rmsnorm_fwd(mem, w["g_mem"][l], name=f"{tag}_mem_norm")
    q = _matmul(h, w["w_xq"][l], "nn", BF16, tm=1024, tn=1024, tk=d, name=f"{tag}_q")
    kv = _matmul(mn, w["w_xkv"][l], "nn", BF16, tm=1024, tn=1024, tk=d, name=f"{tag}_kv")
    o = _xattn_fwd(q, kv, name=f"{tag}_core")
    y = _matmul(o, w["w_xo"][l], "nn", F32, tm=1024, tn=1024, tk=d, residual=x, name=f"{tag}_o")
    return y, (x, h, mn, q, kv, o)


def _xattn_block_bwd(dy, saved, mem, w, l, tag):
    x, h, mn, q, kv, o = saved
    d = x.shape[1]
    do = _matmul(dy, w["w_xo"][l], "nt", BF16, tm=1024, tn=1024, tk=d, name=f"{tag}_do")
    dw_xo = _matmul(o, dy, "tn", F32, tm=1024, tn=1024, tk=1024, name=f"{tag}_dwo")
    dq, dkv = _xattn_bwd(q, kv, do, name=f"{tag}_dcore")
    dw_xq = _matmul(h, dq, "tn", F32, tm=1024, tn=1024, tk=2048, name=f"{tag}_dwq")
    dh = _matmul(dq, w["w_xq"][l], "nt", F32, tm=1024, tn=1024, tk=d, name=f"{tag}_dh")
    dx, dg = _rmsnorm_bwd(x, w["g_xattn"][l], dh, dy, name=f"{tag}_dnorm")
    dw_xkv = _matmul(mn, dkv, "tn", F32, tm=1024, tn=1024, tk=1024, name=f"{tag}_dwkv")
    dmn = _matmul(dkv, w["w_xkv"][l], "nt", F32, tm=1024, tn=1024, tk=1024, name=f"{tag}_dmem")
    _, dg_mem = _rmsnorm_bwd(mem, w["g_mem"][l], dmn, None, name=f"{tag}_dmem_norm")
    return dx, dict(g_xattn=dg, g_mem=dg_mem, w_xq=dw_xq, w_xkv=dw_xkv, w_xo=dw_xo)


def kernel(x, mem, g_ffn1, w_ffn1_gu, w_ffn1_down, g_mix, w_mix_in, w_conv, b_f, g_conv_out, g_att_out, w_mix_out, g_xattn, g_mem, w_xq, w_xkv, w_xo, g_ffn2, w_ffn2_gu, w_ffn2_down, g_final, loss_target, m_g_ffn1, m_w_ffn1_gu, m_w_ffn1_down, m_g_mix, m_w_mix_in, m_w_conv, m_b_f, m_g_conv_out, m_g_att_out, m_w_mix_out, m_g_xattn, m_g_mem, m_w_xq, m_w_xkv, m_w_xo, m_g_ffn2, m_w_ffn2_gu, m_w_ffn2_down, m_g_final, v_g_ffn1, v_w_ffn1_gu, v_w_ffn1_down, v_g_mix, v_w_mix_in, v_w_conv, v_b_f, v_g_conv_out, v_g_att_out, v_w_mix_out, v_g_xattn, v_g_mem, v_w_xq, v_w_xkv, v_w_xo, v_g_ffn2, v_w_ffn2_gu, v_w_ffn2_down, v_g_final):
    local = dict(zip(WEIGHTS, (g_ffn1, w_ffn1_gu, w_ffn1_down, g_mix, w_mix_in, w_conv, b_f, g_conv_out, g_att_out, w_mix_out,
                               g_xattn, g_mem, w_xq, w_xkv, w_xo, g_ffn2, w_ffn2_gu, w_ffn2_down, g_final)))
    mom1 = dict(zip(WEIGHTS, (m_g_ffn1, m_w_ffn1_gu, m_w_ffn1_down, m_g_mix, m_w_mix_in, m_w_conv, m_b_f, m_g_conv_out,
                              m_g_att_out, m_w_mix_out, m_g_xattn, m_g_mem, m_w_xq, m_w_xkv, m_w_xo, m_g_ffn2, m_w_ffn2_gu,
                              m_w_ffn2_down, m_g_final)))
    mom2 = dict(zip(WEIGHTS, (v_g_ffn1, v_w_ffn1_gu, v_w_ffn1_down, v_g_mix, v_w_mix_in, v_w_conv, v_b_f, v_g_conv_out,
                              v_g_att_out, v_w_mix_out, v_g_xattn, v_g_mem, v_w_xq, v_w_xkv, v_w_xo, v_g_ffn2, v_w_ffn2_gu,
                              v_w_ffn2_down, v_g_final)))
    depth = g_ffn1.shape[0]
    s, d = x.shape[1], x.shape[2]
    w = dict(local)
    w.update(_gather_weights(local))

    xs = x.reshape(s, d)
    mems = mem.reshape(mem.shape[1], d)
    saved = []
    for l in range(depth):
        xs, s1 = _ffn_fwd(xs, w["g_ffn1"][l], w["w_ffn1_gu"][l], w["w_ffn1_down"][l], f"l{l}_ffn1")
        xs, s2 = _mix_fwd(xs, w, l, f"l{l}_mix")
        xs, s3 = _xattn_block_fwd(xs, mems, w, l, f"l{l}_xattn")
        xs, s4 = _ffn_fwd(xs, w["g_ffn2"][l], w["w_ffn2_gu"][l], w["w_ffn2_down"][l], f"l{l}_ffn2")
        saved.append((s1, s2, s3, s4))

    dx, dg_final, sq = _final_loss(xs, g_final, loss_target.reshape(s, d), name="loss_head")
    loss = lax.psum(jnp.sum(sq) * (0.5 / d), MESH_AXES)

    per_layer = []
    for l in reversed(range(depth)):
        s1, s2, s3, s4 = saved[l]
        grads = {}
        dx, grads["g_ffn2"], grads["w_ffn2_gu"], grads["w_ffn2_down"] = _ffn_bwd(
            dx, s4, w["g_ffn2"][l], w["w_ffn2_gu"][l], w["w_ffn2_down"][l], f"l{l}_ffn2")
        dx, g3 = _xattn_block_bwd(dx, s3, mems, w, l, f"l{l}_xattn")
        dx, g2 = _mix_bwd(dx, s2, w, l, f"l{l}_mix")
        dx, grads["g_ffn1"], grads["w_ffn1_gu"], grads["w_ffn1_down"] = _ffn_bwd(
            dx, s1, w["g_ffn1"][l], w["w_ffn1_gu"][l], w["w_ffn1_down"][l], f"l{l}_ffn1")
        grads.update(g2)
        grads.update(g3)
        per_layer.append(grads)
    per_layer.reverse()
    grads = {name: jnp.stack([per_layer[l][name] for l in range(depth)]) for name in WEIGHTS if name != "g_final"}
    grads["g_final"] = dg_final.reshape(d)

    reduced = _reduce_gradients(grads, lax.axis_index("c"))
    deltas, new_m, new_v = {}, {}, {}
    for name in WEIGHTS:
        deltas[name], new_m[name], new_v[name] = _adamw(local[name], reduced[name], mom1[name], mom2[name], name=f"adamw_{name}")
    return (loss, dx.reshape(x.shape), *[reduced[n] for n in WEIGHTS], *[deltas[n] for n in WEIGHTS],
            *[new_m[n] for n in WEIGHTS], *[new_v[n] for n in WEIGHTS])
```

```python
import functools

import jax
import jax.numpy as jnp
from jax import lax
from jax.experimental import pallas as pl
from jax.experimental.pallas import tpu as pltpu

F32 = jnp.float32
BF16 = jnp.bfloat16

EPS = 1e-6
HEAD_DIM = 64
LANES = 128
N_XHEADS = 4
CONV_K = 3
ADAM_LR, ADAM_B1, ADAM_B2, ADAM_EPS, ADAM_WD, ADAM_STEP = 0.001, 0.9, 0.999, 1e-08, 0.01, 10
VMEM_LIMIT_BYTES = 56 * 1024 * 1024
NEG_BIG = -1e30
MESH_AXES = ("x", "y", "c")
N_CHIPS = 4


def _params(*sem):
    return pltpu.CompilerParams(dimension_semantics=sem, vmem_limit_bytes=VMEM_LIMIT_BYTES)


_DIMS = {"nn": (((1,), (0,)), ((), ())), "nt": (((1,), (1,)), ((), ())), "tn": (((0,), (0,)), ((), ()))}


def _matmul(a, b, mode, out_dtype, *, tm, tn, tk, name, alpha=1.0, residual=None):
    if mode == "nn":
        (m, k), (k2, n) = a.shape, b.shape
    elif mode == "nt":
        (m, k), (n, k2) = a.shape, b.shape
    else:
        (k, m), (k2, n) = a.shape, b.shape
    assert k == k2, (a.shape, b.shape, mode)
    tm, tn, tk = min(tm, m), min(tn, n), min(tk, k)
    assert m % tm == 0 and n % tn == 0 and k % tk == 0, (m, n, k, tm, tn, tk)
    nk = k // tk
    dims = _DIMS[mode]

    def body(*refs):
        if residual is None:
            a_ref, b_ref, o_ref, *scratch = refs
            r_ref = None
        else:
            a_ref, b_ref, r_ref, o_ref, *scratch = refs
        prod = lax.dot_general(a_ref[...].astype(BF16), b_ref[...].astype(BF16), dims, preferred_element_type=F32)

        def finish(acc):
            if alpha != 1.0:
                acc = acc * alpha
            if r_ref is not None:
                acc = acc + r_ref[...].astype(F32)
            o_ref[...] = acc.astype(o_ref.dtype)

        if nk == 1:
            finish(prod)
        else:
            acc_ref = scratch[0]
            kk = pl.program_id(2)

            @pl.when(kk == 0)
            def _():
                acc_ref[...] = prod

            @pl.when(kk > 0)
            def _():
                acc_ref[...] += prod

            @pl.when(kk == nk - 1)
            def _():
                finish(acc_ref[...])

    if mode == "nn":
        a_spec = pl.BlockSpec((tm, tk), lambda i, j, kk: (i, kk))
        b_spec = pl.BlockSpec((tk, tn), lambda i, j, kk: (kk, j))
    elif mode == "nt":
        a_spec = pl.BlockSpec((tm, tk), lambda i, j, kk: (i, kk))
        b_spec = pl.BlockSpec((tn, tk), lambda i, j, kk: (j, kk))
    else:
        a_spec = pl.BlockSpec((tk, tm), lambda i, j, kk: (kk, i))
        b_spec = pl.BlockSpec((tk, tn), lambda i, j, kk: (kk, j))
    o_spec = pl.BlockSpec((tm, tn), lambda i, j, kk: (i, j))
    in_specs, args = [a_spec, b_spec], [a, b]
    if residual is not None:
        in_specs.append(o_spec)
        args.append(residual)
    return pl.pallas_call(
        body, name=name, grid=(m // tm, n // tn, nk), in_specs=in_specs, out_specs=o_spec,
        out_shape=jax.ShapeDtypeStruct((m, n), out_dtype),
        scratch_shapes=[pltpu.VMEM((tm, tn), F32)] if nk > 1 else [],
        compiler_params=_params("parallel", "parallel", "arbitrary"),
    )(*args)


def _row_tile(rows, want):
    t = min(rows, want)
    assert rows % t == 0, (rows, t)
    return t


def _rmsnorm_fwd(x, g, *, name, tr=1024):
    s, d = x.shape
    tr = _row_tile(s, tr)

    def body(x_ref, g_ref, o_ref):
        xf = x_ref[...]
        r = lax.rsqrt(jnp.mean(xf * xf, axis=-1, keepdims=True) + EPS)
        o_ref[...] = (xf * r * g_ref[...]).astype(o_ref.dtype)

    return pl.pallas_call(
        body, name=name, grid=(s // tr,),
        in_specs=[pl.BlockSpec((tr, d), lambda i: (i, 0)), pl.BlockSpec((1, d), lambda i: (0, 0))],
        out_specs=pl.BlockSpec((tr, d), lambda i: (i, 0)),
        out_shape=jax.ShapeDtypeStruct((s, d), BF16),
        compiler_params=_params("parallel"),
    )(x, g.reshape(1, d))


def _rmsnorm_bwd(x, g, dh, dres, *, name, tr=512):
    s, d = x.shape
    tr = _row_tile(s, tr)

    def body(x_ref, g_ref, dh_ref, *rest):
        if dres is None:
            dx_ref, dg_ref = rest
        else:
            dres_ref, dx_ref, dg_ref = rest
        xf = x_ref[...]
        r = lax.rsqrt(jnp.mean(xf * xf, axis=-1, keepdims=True) + EPS)
        xhat = xf * r
        dhf = dh_ref[...].astype(F32)
        dxhat = dhf * g_ref[...]
        dx = r * (dxhat - xhat * jnp.mean(dxhat * xhat, axis=-1, keepdims=True))
        if dres is not None:
            dx = dx + dres_ref[...]
        dx_ref[...] = dx

        @pl.when(pl.program_id(0) == 0)
        def _():
            dg_ref[...] = jnp.zeros_like(dg_ref)

        dg_ref[...] += jnp.sum(dhf * xhat, axis=0, keepdims=True)

    row = pl.BlockSpec((tr, d), lambda i: (i, 0))
    vec = pl.BlockSpec((1, d), lambda i: (0, 0))
    in_specs, args = [row, vec, row], [x, g.reshape(1, d), dh]
    if dres is not None:
        in_specs.append(row)
        args.append(dres)
    dx, dg = pl.pallas_call(
        body, name=name, grid=(s // tr,), in_specs=in_specs, out_specs=[row, vec],
        out_shape=[jax.ShapeDtypeStruct((s, d), F32), jax.ShapeDtypeStruct((1, d), F32)],
        compiler_params=_params("arbitrary"),
    )(*args)
    return dx, dg.reshape(d)


def _swiglu_fwd(gu, *, name, tr=512):
    s, f2 = gu.shape
    f = f2 // 2
    tr = _row_tile(s, tr)

    def body(gu_ref, a_ref):
        gate = gu_ref[:, :f].astype(F32)
        up = gu_ref[:, f:].astype(F32)
        a_ref[...] = (gate * jax.nn.sigmoid(gate) * up).astype(a_ref.dtype)

    return pl.pallas_call(
        body, name=name, grid=(s // tr,),
        in_specs=[pl.BlockSpec((tr, f2), lambda i: (i, 0))],
        out_specs=pl.BlockSpec((tr, f), lambda i: (i, 0)),
        out_shape=jax.ShapeDtypeStruct((s, f), BF16),
        compiler_params=_params("parallel"),
    )(gu)


def _swiglu_bwd(da, gu, *, name, tr=512):
    s, f2 = gu.shape
    f = f2 // 2
    tr = _row_tile(s, tr)

    def body(da_ref, gu_ref, dgu_ref):
        gate = gu_ref[:, :f].astype(F32)
        up = gu_ref[:, f:].astype(F32)
        daf = da_ref[...].astype(F32)
        sig = jax.nn.sigmoid(gate)
        silu = gate * sig
        dgu_ref[:, :f] = (daf * up * (sig + silu * (1.0 - sig))).astype(dgu_ref.dtype)
        dgu_ref[:, f:] = (daf * silu).astype(dgu_ref.dtype)

    return pl.pallas_call(
        body, name=name, grid=(s // tr,),
        in_specs=[pl.BlockSpec((tr, f), lambda i: (i, 0)), pl.BlockSpec((tr, f2), lambda i: (i, 0))],
        out_specs=pl.BlockSpec((tr, f2), lambda i: (i, 0)),
        out_shape=jax.ShapeDtypeStruct((s, f2), BF16),
        compiler_params=_params("parallel"),
    )(da, gu)


_NT = (((1,), (1,)), ((), ()))
_NN = (((1,), (0,)), ((), ()))
_TN = (((0,), (0,)), ((), ()))
_QK_SCALE = HEAD_DIM ** -0.5


def _dot(a, b, dims):
    return lax.dot_general(a, b, dims, preferred_element_type=F32)


SKIP_BELOW = 107.0
_SMEM = pl.BlockSpec(memory_space=pltpu.SMEM)


def _attn_bounds(z, c, *, qcol_units, kcol_units, aw, t, name):
    s = z.shape[0]
    nq = s // t
    nh = aw // HEAD_DIM

    def body(q_ref, k_ref, o_ref):
        d = lax.broadcasted_iota(jnp.int32, (aw, LANES), 0)
        hh = lax.broadcasted_iota(jnp.int32, (aw, LANES), 1)
        onehot = ((d >= hh * HEAD_DIM) & (d < (hh + 1) * HEAD_DIM)).astype(BF16)
        for r, ref in enumerate((q_ref, k_ref)):
            v = ref[...].astype(F32)
            sq = _dot((v * v).astype(BF16), onehot, _NN)
            o_ref[r:r + 1, :] = jnp.max(sq, axis=0, keepdims=True)
        o_ref[2:, :] = jnp.zeros((HALO - 2, LANES), F32)

    sq = pl.pallas_call(
        body, name=name, grid=(nq,),
        in_specs=[pl.BlockSpec((t, aw), lambda i: (i, qcol_units)), pl.BlockSpec((t, aw), lambda i: (i, kcol_units))],
        out_specs=pl.BlockSpec((None, HALO, LANES), lambda i: (i, 0, 0)),
        out_shape=jax.ShapeDtypeStruct((nq, HALO, LANES), F32),
        compiler_params=_params("parallel"),
    )(z, z)
    norms = jnp.sqrt(sq[:, :2, :nh]) * 1.01
    qn = (norms[:, 0, :] * _QK_SCALE).T.reshape(-1)
    kn = norms[:, 1, :].T.reshape(-1)
    cs = c[0::t, :nh].T.reshape(-1)
    ce = c[t - 1::t, :nh].T.reshape(-1)
    return qn, kn, cs, ce


def _block_active(bounds, head, i, j, nq):
    qn_ref, kn_ref, cs_ref, ce_ref = bounds
    qi = qn_ref[head * nq + i]
    upper = qi * kn_ref[head * nq + j] + (cs_ref[head * nq + i] - ce_ref[head * nq + j])
    lower = -(qi * kn_ref[head * nq + i])
    return upper - lower > -SKIP_BELOW


def _for_active_heads(bounds, pair, i, j, nq, head_step):
    act = [_block_active(bounds, 2 * pair + h, i, j, nq) for h in range(2)]

    def run(heads):
        for h in heads:
            head_step(h)

    pl.when(act[0] & act[1])(functools.partial(run, (0, 1)))
    pl.when(act[0] & jnp.logical_not(act[1]))(functools.partial(run, (0,)))
    pl.when(jnp.logical_not(act[0]) & act[1])(functools.partial(run, (1,)))


def _attn_operands(z, ccol, *, qcol, kcol, vcol, t, name):
    s = z.shape[0]
    npairs = ccol.shape[0]

    def body(q_ref, k_ref, v_ref, c_ref, qa_ref, ka_ref, va_ref):
        lane = lax.broadcasted_iota(jnp.int32, (t, LANES), 1)
        q2 = q_ref[...] * jnp.asarray(_QK_SCALE, BF16)
        k2, v2 = k_ref[...], v_ref[...]
        one, zero = jnp.ones((t, LANES), BF16), jnp.zeros((t, LANES), BF16)
        for h in range(2):
            base = HEAD_DIM * (1 - h)
            mine = (lane < HEAD_DIM) if h == 0 else (lane >= HEAD_DIM)
            c = c_ref[:, h:h + 1]
            hi = c.astype(BF16)
            r1 = c - hi.astype(F32)
            mid = r1.astype(BF16)
            lo = (r1 - mid.astype(F32)).astype(BF16)
            qa, ka = jnp.where(mine, q2, zero), jnp.where(mine, k2, zero)
            for r, word in enumerate((hi, mid, lo)):
                qa = jnp.where(lane == base + r, word, qa)
                qa = jnp.where(lane == base + 3 + r, one, qa)
                ka = jnp.where(lane == base + r, one, ka)
                ka = jnp.where(lane == base + 3 + r, -word, ka)
            qa_ref[h] = qa
            ka_ref[h] = ka
            va_ref[h] = jnp.where((lane >= base) & (lane < base + 3), one, jnp.where(mine, v2, zero))

    out = jax.ShapeDtypeStruct((npairs, 2, s, LANES), BF16)
    blk = pl.BlockSpec((None, 2, t, LANES), lambda p, i: (p, 0, i, 0))
    return pl.pallas_call(
        body, name=name, grid=(npairs, s // t),
        in_specs=[pl.BlockSpec((t, LANES), lambda p, i: (i, qcol + p)), pl.BlockSpec((t, LANES), lambda p, i: (i, kcol + p)),
                  pl.BlockSpec((t, LANES), lambda p, i: (i, vcol + p)), pl.BlockSpec((None, t, 2), lambda p, i: (p, i, 0))],
        out_specs=[blk, blk, blk], out_shape=[out, out, out],
        compiler_params=_params("parallel", "parallel"),
    )(z, z, z, ccol)


def _fattn_fwd(qa, ka, va, bounds, *, t, name):
    npairs, _, s, _ = qa.shape
    nq = s // t
    reps = t // LANES

    def body(qn_ref, kn_ref, cs_ref, ce_ref, q_ref, k_ref, v_ref, o_ref, lse_ref, m_scr, acc_scr):
        pair, i = pl.program_id(0), pl.program_id(1)
        m_scr[...] = jnp.full(m_scr.shape, NEG_BIG, F32)
        acc_scr[...] = jnp.zeros(acc_scr.shape, F32)

        def head_step(h, j, diagonal):
            off = pl.multiple_of(j * t, t)
            sc = _dot(q_ref[h], k_ref[h, pl.ds(off, t), :], _NT)
            if diagonal:
                row = lax.broadcasted_iota(jnp.int32, (t, t), 0)
                col = lax.broadcasted_iota(jnp.int32, (t, t), 1)
                sc = jnp.where(row >= col, sc, NEG_BIG)
            m_old = m_scr[h]
            m_new = jnp.maximum(m_old, jnp.max(sc, axis=1, keepdims=True))
            p = jnp.exp(sc - jnp.tile(m_new, (1, reps)))
            acc_scr[h] = acc_scr[h] * jnp.exp(m_old - m_new) + _dot(p.astype(BF16), v_ref[h, pl.ds(off, t), :], _NN)
            m_scr[h] = m_new

        def loop_body(j, carry):
            _for_active_heads((qn_ref, kn_ref, cs_ref, ce_ref), pair, i, j, nq, lambda h: head_step(h, j, False))
            return carry

        lax.fori_loop(0, i, loop_body, 0)
        for h in range(2):
            head_step(h, i, True)
        is_a = lax.broadcasted_iota(jnp.int32, (t, LANES), 1) < HEAD_DIM
        l = (acc_scr[0][:, HEAD_DIM:HEAD_DIM + 1], acc_scr[1][:, 0:1])
        o_ref[...] = jnp.where(is_a, acc_scr[0] / l[0], acc_scr[1] / l[1])
        lse_ref[:, 0:1] = m_scr[0][:, 0:1] + jnp.log(l[0])
        lse_ref[:, 1:2] = m_scr[1][:, 0:1] + jnp.log(l[1])

    return pl.pallas_call(
        body, name=name, grid=(npairs, nq),
        in_specs=[_SMEM, _SMEM, _SMEM, _SMEM,
                  pl.BlockSpec((None, 2, t, LANES), lambda p, i: (p, 0, i, 0)),
                  pl.BlockSpec((None, 2, s, LANES), lambda p, i: (p, 0, 0, 0)),
                  pl.BlockSpec((None, 2, s, LANES), lambda p, i: (p, 0, 0, 0))],
        out_specs=[pl.BlockSpec((t, LANES), lambda p, i: (i, p)), pl.BlockSpec((None, t, 2), lambda p, i: (p, i, 0))],
        out_shape=[jax.ShapeDtypeStruct((s, npairs * LANES), F32), jax.ShapeDtypeStruct((npairs, s, 2), F32)],
        scratch_shapes=[pltpu.VMEM((2, t, LANES), F32), pltpu.VMEM((2, t, LANES), F32)],
        compiler_params=_params("parallel", "arbitrary"),
    )(*bounds, qa, ka, va)


def _split3(x):
    hi = x.astype(BF16)
    r1 = x - hi.astype(F32)
    mid = r1.astype(BF16)
    return hi, mid, (r1 - mid.astype(F32)).astype(BF16)


def _attn_bwd_operands(qa, ka, dy, y, lse, *, t, name):
    npairs, _, s, _ = qa.shape

    def body(qa_ref, ka_ref, dy_ref, y_ref, lse_ref, qb_ref, kb_ref, dyb_ref):
        lane = lax.broadcasted_iota(jnp.int32, (t, LANES), 1)
        dyf = dy_ref[...]
        prod = dyf * y_ref[...]
        dyh = dyf.astype(BF16)
        one, zero = jnp.ones((t, LANES), BF16), jnp.zeros((t, LANES), BF16)
        for h in range(2):
            base = HEAD_DIM * (1 - h)
            mine = (lane < HEAD_DIM) if h == 0 else (lane >= HEAD_DIM)
            delta = jnp.sum(jnp.where(mine, prod, 0.0), axis=1, keepdims=True)
            qb, kb, dyb = qa_ref[h], ka_ref[h], jnp.where(mine, dyh, zero)
            for r, (lw, dw) in enumerate(zip(_split3(lse_ref[:, h:h + 1]), _split3(delta))):
                qb = jnp.where(lane == base + 6 + r, -lw, qb)
                kb = jnp.where(lane == base + 6 + r, one, kb)
                dyb = jnp.where(lane == base + r, -dw, dyb)
            qb_ref[h] = qb
            kb_ref[h] = kb
            dyb_ref[h] = dyb

    out = jax.ShapeDtypeStruct((npairs, 2, s, LANES), BF16)
    blk = pl.BlockSpec((None, 2, t, LANES), lambda p, i: (p, 0, i, 0))
    tile = pl.BlockSpec((t, LANES), lambda p, i: (i, p))
    return pl.pallas_call(
        body, name=name, grid=(npairs, s // t),
        in_specs=[blk, blk, tile, tile, pl.BlockSpec((None, t, 2), lambda p, i: (p, i, 0))],
        out_specs=[blk, blk, blk], out_shape=[out, out, out],
        compiler_params=_params("parallel", "parallel"),
    )(qa, ka, dy, y, lse)


def _diag_mask(sc, t, queries_on_rows):
    row = lax.broadcasted_iota(jnp.int32, (t, t), 0)
    col = lax.broadcasted_iota(jnp.int32, (t, t), 1)
    return jnp.where((row >= col) if queries_on_rows else (col >= row), sc, NEG_BIG)


def _fattn_dq(qb, kb, va, dyb, bounds, *, t, name):
    npairs, _, s, _ = qb.shape
    nq = s // t

    def body(qn_ref, kn_ref, cs_ref, ce_ref, q_ref, k_ref, v_ref, dy_ref, dq_ref, dcq_ref, acc_scr):
        pair, i = pl.program_id(0), pl.program_id(1)
        acc_scr[...] = jnp.zeros(acc_scr.shape, F32)

        def head_step(h, j, diagonal):
            off = pl.multiple_of(j * t, t)
            kj = k_ref[h, pl.ds(off, t), :]
            sc = _dot(q_ref[h], kj, _NT)
            if diagonal:
                sc = _diag_mask(sc, t, True)
            ds = jnp.exp(sc) * _dot(dy_ref[h], v_ref[h, pl.ds(off, t), :], _NT)
            acc_scr[h] += _dot(ds.astype(BF16), kj, _NN)

        def loop_body(j, carry):
            _for_active_heads((qn_ref, kn_ref, cs_ref, ce_ref), pair, i, j, nq, lambda h: head_step(h, j, False))
            return carry

        lax.fori_loop(0, i, loop_body, 0)
        for h in range(2):
            head_step(h, i, True)
        is_a = lax.broadcasted_iota(jnp.int32, (t, LANES), 1) < HEAD_DIM
        dq_ref[...] = (jnp.where(is_a, acc_scr[0], acc_scr[1]) * _QK_SCALE).astype(dq_ref.dtype)
        dcq_ref[:, 0:1] = acc_scr[0][:, HEAD_DIM:HEAD_DIM + 1]
        dcq_ref[:, 1:2] = acc_scr[1][:, 0:1]

    tile2 = pl.BlockSpec((None, 2, t, LANES), lambda p, i: (p, 0, i, 0))
    whole = pl.BlockSpec((None, 2, s, LANES), lambda p, i: (p, 0, 0, 0))
    return pl.pallas_call(
        body, name=name, grid=(npairs, nq),
        in_specs=[_SMEM, _SMEM, _SMEM, _SMEM, tile2, whole, whole, tile2],
        out_specs=[pl.BlockSpec((t, LANES), lambda p, i: (i, p)), pl.BlockSpec((None, t, 2), lambda p, i: (p, i, 0))],
        out_shape=[jax.ShapeDtypeStruct((s, npairs * LANES), BF16), jax.ShapeDtypeStruct((npairs, s, 2), F32)],
        scratch_shapes=[pltpu.VMEM((2, t, LANES), F32)],
        compiler_params=_params("parallel", "arbitrary"),
    )(*bounds, qb, kb, va, dyb)


def _fattn_dkv(qb, kb, va, dyb, bounds, *, t, name):
    npairs, _, s, _ = qb.shape
    nq = s // t

    def body(qn_ref, kn_ref, cs_ref, ce_ref, k_ref, v_ref, q_ref, dy_ref, dk_ref, dv_ref, dc_ref, dk_scr, dv_scr):
        pair, j = pl.program_id(0), pl.program_id(1)
        dk_scr[...] = jnp.zeros(dk_scr.shape, F32)
        dv_scr[...] = jnp.zeros(dv_scr.shape, F32)

        def head_step(h, i, diagonal):
            off = pl.multiple_of(i * t, t)
            qi = q_ref[h, pl.ds(off, t), :]
            dyi = dy_ref[h, pl.ds(off, t), :]
            st = _dot(k_ref[h], qi, _NT)
            if diagonal:
                st = _diag_mask(st, t, False)
            pt = jnp.exp(st)
            dv_scr[h] += _dot(pt.astype(BF16), dyi, _NN)
            dst = pt * _dot(v_ref[h], dyi, _NT)
            dk_scr[h] += _dot(dst.astype(BF16), qi, _NN)

        def loop_body(i, carry):
            _for_active_heads((qn_ref, kn_ref, cs_ref, ce_ref), pair, i, j, nq, lambda h: head_step(h, i, False))
            return carry

        for h in range(2):
            head_step(h, j, True)
        lax.fori_loop(j + 1, nq, loop_body, 0)
        is_a = lax.broadcasted_iota(jnp.int32, (t, LANES), 1) < HEAD_DIM
        dk_ref[...] = jnp.where(is_a, dk_scr[0], dk_scr[1]).astype(dk_ref.dtype)
        dv_ref[...] = jnp.where(is_a, dv_scr[0], dv_scr[1]).astype(dv_ref.dtype)
        dc_ref[:, 0:1] = -dk_scr[0][:, HEAD_DIM + 3:HEAD_DIM + 4]
        dc_ref[:, 1:2] = -dk_scr[1][:, 3:4]

    tile2 = pl.BlockSpec((None, 2, t, LANES), lambda p, j: (p, 0, j, 0))
    whole = pl.BlockSpec((None, 2, s, LANES), lambda p, j: (p, 0, 0, 0))
    tile = pl.BlockSpec((t, LANES), lambda p, j: (j, p))
    return pl.pallas_call(
        body, name=name, grid=(npairs, nq),
        in_specs=[_SMEM, _SMEM, _SMEM, _SMEM, tile2, tile2, whole, whole],
        out_specs=[tile, tile, pl.BlockSpec((None, t, 2), lambda p, j: (p, j, 0))],
        out_shape=[jax.ShapeDtypeStruct((s, npairs * LANES), BF16), jax.ShapeDtypeStruct((s, npairs * LANES), BF16),
                   jax.ShapeDtypeStruct((npairs, s, 2), F32)],
        scratch_shapes=[pltpu.VMEM((2, t, LANES), F32), pltpu.VMEM((2, t, LANES), F32)],
        compiler_params=_params("parallel", "arbitrary"),
    )(*bounds, kb, va, qb, dyb)


def _log_sigmoid(x):
    return jnp.minimum(x, 0.0) - jnp.log(1.0 + jnp.exp(-jnp.abs(x)))


def _cumsum_fwd(zf, b, *, name, t=512):
    s, w = zf.shape
    t = _row_tile(s, t)

    def body(zf_ref, b_ref, c_ref, carry):
        @pl.when(pl.program_id(0) == 0)
        def _():
            carry[...] = jnp.zeros(carry.shape, F32)

        lf = _log_sigmoid(zf_ref[...] + b_ref[...])
        row = lax.broadcasted_iota(jnp.int32, (t, t), 0)
        col = lax.broadcasted_iota(jnp.int32, (t, t), 1)
        tri = (row >= col).astype(F32)
        c = lax.dot_general(tri, lf, _NN, precision=lax.Precision.HIGHEST, preferred_element_type=F32) + carry[...]
        c_ref[...] = c
        carry[...] = c[t - 1:t, :]

    return pl.pallas_call(
        body, name=name, grid=(s // t,),
        in_specs=[pl.BlockSpec((t, w), lambda i: (i, 0)), pl.BlockSpec((1, w), lambda i: (0, 0))],
        out_specs=pl.BlockSpec((t, w), lambda i: (i, 0)),
        out_shape=jax.ShapeDtypeStruct((s, w), F32),
        scratch_shapes=[pltpu.VMEM((1, w), F32)],
        compiler_params=_params("arbitrary"),
    )(zf, b)


def _cumsum_bwd(dcq, dck, zf, b, *, name, t=512):
    s, w = zf.shape
    t = _row_tile(s, t)
    nb = s // t

    def body(dcq_ref, dck_ref, zf_ref, b_ref, dzf_ref, db_ref, carry):
        @pl.when(pl.program_id(0) == 0)
        def _():
            carry[...] = jnp.zeros(carry.shape, F32)
            db_ref[...] = jnp.zeros(db_ref.shape, F32)

        row = lax.broadcasted_iota(jnp.int32, (t, t), 0)
        col = lax.broadcasted_iota(jnp.int32, (t, t), 1)
        tri = (row <= col).astype(F32)
        dc = dcq_ref[...] + dck_ref[...]
        dlf = lax.dot_general(tri, dc, _NN, precision=lax.Precision.HIGHEST, preferred_element_type=F32) + carry[...]
        carry[...] = dlf[0:1, :]
        dzf = dlf * jax.nn.sigmoid(-(zf_ref[...] + b_ref[...]))
        dzf_ref[...] = dzf
        db_ref[...] += jnp.sum(dzf, axis=0, keepdims=True)

    blk = pl.BlockSpec((t, w), lambda i: (nb - 1 - i, 0))
    vec = pl.BlockSpec((1, w), lambda i: (0, 0))
    return pl.pallas_call(
        body, name=name, grid=(nb,), in_specs=[blk, blk, blk, vec], out_specs=[blk, vec],
        out_shape=[jax.ShapeDtypeStruct((s, w), F32), jax.ShapeDtypeStruct((1, w), F32)],
        scratch_shapes=[pltpu.VMEM((1, w), F32)],
        compiler_params=_params("arbitrary"),
    )(dcq, dck, zf, b)


HALO = 8


def _rms_rows(v):
    return lax.rsqrt(jnp.mean(v * v, axis=-1, keepdims=True) + EPS)


def _mixpost_fwd(z, yatt, wconv, gc, ga, *, name, tr=512):
    s = z.shape[0]
    cw, aw = gc.shape[-1], ga.shape[-1]
    tr = _row_tile(s, tr)

    def body(zb_ref, zc_ref, zv_ref, ya_ref, w_ref, gc_ref, ga_ref, ycat_ref, cv_ref, u_scr):
        @pl.when(pl.program_id(0) == 0)
        def _():
            u_scr[0:HALO, :] = jnp.zeros((HALO, cw), F32)

        u = zc_ref[...].astype(F32) * zv_ref[...].astype(F32)
        u_scr[HALO:HALO + tr, :] = u
        cv = w_ref[0:1, :] * u_scr[HALO - 2:HALO - 2 + tr, :] + w_ref[1:2, :] * u_scr[HALO - 1:HALO - 1 + tr, :] + w_ref[2:3, :] * u
        u_scr[0:HALO, :] = u_scr[tr:tr + HALO, :]
        cv_ref[...] = cv
        yc = zb_ref[...].astype(F32) * cv
        ya = ya_ref[...]
        ycat_ref[:, :cw] = (yc * _rms_rows(yc) * gc_ref[...]).astype(ycat_ref.dtype)
        ycat_ref[:, cw:] = (ya * _rms_rows(ya) * ga_ref[...]).astype(ycat_ref.dtype)

    return pl.pallas_call(
        body, name=name, grid=(s // tr,),
        in_specs=[
            pl.BlockSpec((tr, cw), lambda i: (i, 0)), pl.BlockSpec((tr, cw), lambda i: (i, 1)),
            pl.BlockSpec((tr, cw), lambda i: (i, 2)), pl.BlockSpec((tr, aw), lambda i: (i, 0)),
            pl.BlockSpec((HALO, cw), lambda i: (0, 0)), pl.BlockSpec((1, cw), lambda i: (0, 0)),
            pl.BlockSpec((1, aw), lambda i: (0, 0)),
        ],
        out_specs=[pl.BlockSpec((tr, cw + aw), lambda i: (i, 0)), pl.BlockSpec((tr, cw), lambda i: (i, 0))],
        out_shape=[jax.ShapeDtypeStruct((s, cw + aw), BF16), jax.ShapeDtypeStruct((s, cw), F32)],
        scratch_shapes=[pltpu.VMEM((tr + HALO, cw), F32)],
        compiler_params=_params("arbitrary"),
    )(z, z, z, yatt, wconv, gc.reshape(1, cw), ga.reshape(1, aw))


def _mixpost_bwd(dycat, z, yatt, cv, wconv, gc, ga, *, name, tr=512):
    s = z.shape[0]
    cw, aw = gc.shape[-1], ga.shape[-1]
    tr = _row_tile(s, tr)
    nb = s // tr

    def body(dy_ref, zb_ref, zc_ref, zv_ref, ya_ref, cv_ref, w_ref, gc_ref, ga_ref,
             dz_ref, dya_ref, dw_ref, dgc_ref, dga_ref, d_scr):
        @pl.when(pl.program_id(0) == 0)
        def _():
            d_scr[tr:tr + HALO, :] = jnp.zeros((HALO, cw), F32)
            dw_ref[...] = jnp.zeros(dw_ref.shape, F32)
            dgc_ref[...] = jnp.zeros(dgc_ref.shape, F32)
            dga_ref[...] = jnp.zeros(dga_ref.shape, F32)

        zb, zc, zv = zb_ref[...].astype(F32), zc_ref[...].astype(F32), zv_ref[...].astype(F32)
        cvv = cv_ref[...]

        def norm_bwd(v, dn, g):
            r = _rms_rows(v)
            vh = v * r
            dvh = dn * g
            return r * (dvh - vh * jnp.mean(dvh * vh, axis=-1, keepdims=True)), jnp.sum(dn * vh, axis=0, keepdims=True)

        dyc, dgc = norm_bwd(zb * cvv, dy_ref[:, :cw], gc_ref[...])
        dya, dga = norm_bwd(ya_ref[...], dy_ref[:, cw:], ga_ref[...])
        dgc_ref[...] += dgc
        dga_ref[...] += dga
        dya_ref[...] = dya
        dcv = dyc * zb
        d_scr[0:tr, :] = dcv
        d1 = d_scr[1:tr + 1, :]
        d2 = d_scr[2:tr + 2, :]
        du = w_ref[2:3, :] * dcv + w_ref[1:2, :] * d1 + w_ref[0:1, :] * d2
        u = zc * zv
        dw_ref[0:1, :] += jnp.sum(u * d2, axis=0, keepdims=True)
        dw_ref[1:2, :] += jnp.sum(u * d1, axis=0, keepdims=True)
        dw_ref[2:3, :] += jnp.sum(u * dcv, axis=0, keepdims=True)
        d_scr[tr:tr + HALO, :] = d_scr[0:HALO, :]
        dz_ref[:, :cw] = (dyc * cvv).astype(dz_ref.dtype)
        dz_ref[:, cw:2 * cw] = (du * zv).astype(dz_ref.dtype)
        dz_ref[:, 2 * cw:] = (du * zc).astype(dz_ref.dtype)

    def rows(width, colblk=0):
        return pl.BlockSpec((tr, width), lambda i: (nb - 1 - i, colblk))

    def fixed(r, width):
        return pl.BlockSpec((r, width), lambda i: (0, 0))

    return pl.pallas_call(
        body, name=name, grid=(nb,),
        in_specs=[rows(cw + aw), rows(cw, 0), rows(cw, 1), rows(cw, 2), rows(aw), rows(cw),
                  fixed(HALO, cw), fixed(1, cw), fixed(1, aw)],
        out_specs=[rows(3 * cw), rows(aw), fixed(HALO, cw), fixed(1, cw), fixed(1, aw)],
        out_shape=[jax.ShapeDtypeStruct((s, 3 * cw), BF16), jax.ShapeDtypeStruct((s, aw), F32),
                   jax.ShapeDtypeStruct((HALO, cw), F32), jax.ShapeDtypeStruct((1, cw), F32),
                   jax.ShapeDtypeStruct((1, aw), F32)],
        scratch_shapes=[pltpu.VMEM((tr + HALO, cw), F32)],
        compiler_params=_params("arbitrary"),
    )(dycat, z, z, z, yatt, cv, wconv, gc.reshape(1, cw), ga.reshape(1, aw))


def _xattn_fwd(q, kv, *, name, tq=1024):
    s, d = q.shape
    m = kv.shape[0]
    dh = d // N_XHEADS
    scale = dh ** -0.5
    tq = _row_tile(s, tq)

    def body(q_ref, kv_ref, o_ref):
        for h in range(N_XHEADS):
            lo, hi = h * dh, (h + 1) * dh
            sc = _dot(q_ref[:, lo:hi], kv_ref[:, lo:hi], _NT) * scale
            p = jnp.exp(sc - jnp.max(sc, axis=1, keepdims=True))
            o = _dot(p.astype(BF16), kv_ref[:, d + lo:d + hi], _NN) / jnp.sum(p, axis=1, keepdims=True)
            o_ref[:, lo:hi] = o.astype(o_ref.dtype)

    return pl.pallas_call(
        body, name=name, grid=(s // tq,),
        in_specs=[pl.BlockSpec((tq, d), lambda i: (i, 0)), pl.BlockSpec((m, 2 * d), lambda i: (0, 0))],
        out_specs=pl.BlockSpec((tq, d), lambda i: (i, 0)),
        out_shape=jax.ShapeDtypeStruct((s, d), BF16),
        compiler_params=_params("parallel"),
    )(q, kv)


def _xattn_bwd(q, kv, do, *, name, tq=1024):
    s, d = q.shape
    m = kv.shape[0]
    dh = d // N_XHEADS
    scale = dh ** -0.5
    tq = _row_tile(s, tq)

    def body(q_ref, kv_ref, do_ref, dq_ref, dkv_ref):
        @pl.when(pl.program_id(0) == 0)
        def _():
            dkv_ref[...] = jnp.zeros(dkv_ref.shape, F32)

        for h in range(N_XHEADS):
            lo, hi = h * dh, (h + 1) * dh
            qh, kh, vh, doh = q_ref[:, lo:hi], kv_ref[:, lo:hi], kv_ref[:, d + lo:d + hi], do_ref[:, lo:hi]
            sc = _dot(qh, kh, _NT) * scale
            e = jnp.exp(sc - jnp.max(sc, axis=1, keepdims=True))
            p = e / jnp.sum(e, axis=1, keepdims=True)
            dp = _dot(doh, vh, _NT)
            ds = p * (dp - jnp.sum(dp * p, axis=1, keepdims=True))
            dsb = ds.astype(BF16)
            dq_ref[:, lo:hi] = (_dot(dsb, kh, _NN) * scale).astype(dq_ref.dtype)
            dkv_ref[:, lo:hi] += _dot(dsb, qh, _TN) * scale
            dkv_ref[:, d + lo:d + hi] += _dot(p.astype(BF16), doh, _TN)

    return pl.pallas_call(
        body, name=name, grid=(s // tq,),
        in_specs=[pl.BlockSpec((tq, d), lambda i: (i, 0)), pl.BlockSpec((m, 2 * d), lambda i: (0, 0)),
                  pl.BlockSpec((tq, d), lambda i: (i, 0))],
        out_specs=[pl.BlockSpec((tq, d), lambda i: (i, 0)), pl.BlockSpec((m, 2 * d), lambda i: (0, 0))],
        out_shape=[jax.ShapeDtypeStruct((s, d), BF16), jax.ShapeDtypeStruct((m, 2 * d), F32)],
        compiler_params=_params("arbitrary"),
    )(q, kv, do)


def _final_loss(x, g, target, *, name, tr=512):
    s, d = x.shape
    tr = _row_tile(s, tr)

    def body(x_ref, g_ref, t_ref, dx_ref, dg_ref, sq_ref):
        @pl.when(pl.program_id(0) == 0)
        def _():
            dg_ref[...] = jnp.zeros(dg_ref.shape, F32)
            sq_ref[...] = jnp.zeros(sq_ref.shape, F32)

        xf = x_ref[...]
        r = _rms_rows(xf)
        xhat = xf * r
        err = xhat * g_ref[...] - t_ref[...]
        sq_ref[...] += jnp.sum(err * err, axis=0, keepdims=True)
        dy = err * (1.0 / d)
        dg_ref[...] += jnp.sum(dy * xhat, axis=0, keepdims=True)
        dxhat = dy * g_ref[...]
        dx_ref[...] = r * (dxhat - xhat * jnp.mean(dxhat * xhat, axis=-1, keepdims=True))

    row = pl.BlockSpec((tr, d), lambda i: (i, 0))
    vec = pl.BlockSpec((1, d), lambda i: (0, 0))
    return pl.pallas_call(
        body, name=name, grid=(s // tr,), in_specs=[row, vec, row], out_specs=[row, vec, vec],
        out_shape=[jax.ShapeDtypeStruct((s, d), F32), jax.ShapeDtypeStruct((1, d), F32), jax.ShapeDtypeStruct((1, d), F32)],
        compiler_params=_params("arbitrary"),
    )(x, g.reshape(1, d), target)


def _adamw(w, g, m, v, *, name, tr=512):
    shape = w.shape
    cols = shape[-1]
    rows = w.size // cols
    tr = tr if rows % tr == 0 else rows

    def body(w_ref, g_ref, m_ref, v_ref, d_ref, nm_ref, nv_ref):
        gf = g_ref[...]
        nm = ADAM_B1 * m_ref[...] + (1.0 - ADAM_B1) * gf
        nv = ADAM_B2 * v_ref[...] + (1.0 - ADAM_B2) * (gf * gf)
        m_hat = nm / (1.0 - ADAM_B1 ** ADAM_STEP)
        v_hat = nv / (1.0 - ADAM_B2 ** ADAM_STEP)
        d_ref[...] = -ADAM_LR * (m_hat / (jnp.sqrt(v_hat) + ADAM_EPS) + ADAM_WD * w_ref[...])
        nm_ref[...] = nm
        nv_ref[...] = nv

    blk = pl.BlockSpec((tr, cols), lambda i: (i, 0))
    out = jax.ShapeDtypeStruct((rows, cols), F32)
    outs = pl.pallas_call(
        body, name=name, grid=(rows // tr,), in_specs=[blk] * 4, out_specs=[blk] * 3, out_shape=[out] * 3,
        compiler_params=_params("parallel"),
    )(*[t.reshape(rows, cols) for t in (w, g, m, v)])
    return tuple(o.reshape(shape) for o in outs)


_HBM = pl.BlockSpec(memory_space=pl.ANY)
_MESH_ID = pl.DeviceIdType.MESH


def _place():
    x, y, c = (lax.axis_index(a) for a in MESH_AXES)
    return x, y, c, [(1 - x, y), (x, 1 - y), (1 - x, 1 - y)]


def _remote(src, dst, send_sems, recv_sems, k, to):
    return pltpu.make_async_remote_copy(src_ref=src, dst_ref=dst, send_sem=send_sems.at[k], recv_sem=recv_sems.at[k],
                                        device_id=to, device_id_type=_MESH_ID)


def _comm_call(body, arrays, out_shapes, n_remote, name):
    return pl.pallas_call(
        body, name=name, in_specs=[_HBM] * len(arrays), out_specs=[_HBM] * len(out_shapes), out_shape=out_shapes,
        scratch_shapes=[pltpu.SemaphoreType.DMA((n_remote,)), pltpu.SemaphoreType.DMA((n_remote,)),
                        pltpu.SemaphoreType.DMA((len(arrays),))],
    )(*arrays)


def _allgather_weights(halves, *, name):
    n = len(halves)

    def body(*refs):
        w, out = refs[:n], refs[n:2 * n]
        send_sems, recv_sems, local_sems = refs[2 * n:]
        x, y, c, chips = _place()
        me = 2 * x + y
        sibling = (x, y, 1 - c)
        slots = [2 * px + py for px, py in chips]

        def copy(t, k, slot, half, to, src=None):
            dst = out[t].at[slot, half]
            return _remote(dst if src is None else src, dst, send_sems, recv_sems, 7 * t + k, to)

        local = [pltpu.make_async_copy(w[t].at[c], out[t].at[me, c], local_sems.at[t]) for t in range(n)]
        first = []
        for t in range(n):
            local[t].start()
            first.append(copy(t, 0, me, c, sibling, src=w[t].at[c]))
            first += [copy(t, 1 + j, me, c, (px, py, c), src=w[t].at[c]) for j, (px, py) in enumerate(chips)]
        for cp in first:
            cp.start()
        passed = []
        for j, (px, py) in enumerate(chips):
            for t in range(n):
                copy(t, 1 + j, slots[j], c, (px, py, c)).wait_recv()
                passed.append(copy(t, 4 + j, slots[j], c, sibling))
                passed[-1].start()
        for t in range(n):
            copy(t, 0, me, 1 - c, sibling).wait_recv()
            for j in range(3):
                copy(t, 4 + j, slots[j], 1 - c, sibling).wait_recv()
        for cp in first + passed:
            cp.wait_send()
        for cp in local:
            cp.wait()

    shapes = [jax.ShapeDtypeStruct((N_CHIPS,) + h.shape, h.dtype) for h in halves]
    return _comm_call(body, halves, shapes, 7 * n, name)


def _swap_halves(gs, *, name):
    n = len(gs)

    def body(*refs):
        g, out = refs[:n], refs[n:2 * n]
        send_sems, recv_sems, _ = refs[2 * n:]
        x, y, c, _ = _place()
        copies = [_remote(g[t].at[1 - c], out[t], send_sems, recv_sems, t, (x, y, 1 - c)) for t in range(n)]
        for cp in copies:
            cp.start()
        for cp in copies:
            cp.wait()

    return _comm_call(body, gs, [jax.ShapeDtypeStruct(g.shape[1:], g.dtype) for g in gs], n, name)


def _scatter_chips(ps, *, name):
    n = len(ps)

    def body(*refs):
        p, out = refs[:n], refs[n:2 * n]
        send_sems, recv_sems, local_sems = refs[2 * n:]
        x, y, c, chips = _place()
        me = 2 * x + y
        local = [pltpu.make_async_copy(p[t].at[me], out[t].at[me], local_sems.at[t]) for t in range(n)]
        sends = [_remote(p[t].at[2 * px + py], out[t].at[me], send_sems, recv_sems, 3 * t + j, (px, py, c))
                 for t in range(n) for j, (px, py) in enumerate(chips)]
        for cp in local + sends:
            cp.start()
        for t in range(n):
            for j, (px, py) in enumerate(chips):
                slot = out[t].at[2 * px + py]
                _remote(slot, slot, send_sems, recv_sems, 3 * t + j, (px, py, c)).wait_recv()
        for cp in sends:
            cp.wait_send()
        for cp in local:
            cp.wait()

    return _comm_call(body, ps, [jax.ShapeDtypeStruct(p.shape, p.dtype) for p in ps], 3 * n, name)


def _join_halves(rs, *, name):
    n = len(rs)

    def body(*refs):
        r, out = refs[:n], refs[n:2 * n]
        send_sems, recv_sems, _ = refs[2 * n:]
        x, y, c, _ = _place()
        sends = [_remote(r[t].at[c], out[t].at[c], send_sems, recv_sems, t, (x, y, 1 - c)) for t in range(n)]
        for cp in sends:
            cp.start()
        for t in range(n):
            slot = out[t].at[1 - c]
            _remote(slot, slot, send_sems, recv_sems, t, (x, y, 1 - c)).wait_recv()
        for cp in sends:
            cp.wait_send()

    return pl.pallas_call(
        body, name=name, in_specs=[_HBM] * n, out_specs=[_HBM] * n,
        out_shape=[jax.ShapeDtypeStruct(r.shape, r.dtype) for r in rs],
        input_output_aliases={t: t for t in range(n)},
        scratch_shapes=[pltpu.SemaphoreType.DMA((n,)), pltpu.SemaphoreType.DMA((n,)), pltpu.SemaphoreType.DMA((n,))],
    )(*rs)


def _pick_rows(rows, cap=512):
    for t in range(min(rows, cap), 0, -1):
        if rows % t == 0 and t % 16 == 0:
            return t
    return rows


def _add_halves(g, recv, c, out_dtype, *, name):
    cols = g.shape[-1]
    rows = recv.size // (N_CHIPS * cols)
    tr = _pick_rows(rows)

    def body(c_ref, g_ref, r_ref, o_ref):
        o_ref[...] = (g_ref[...] + r_ref[...]).astype(o_ref.dtype)

    blk = pl.BlockSpec((None, tr, cols), lambda b, i, c_ref: (b, i, 0))
    out = pl.pallas_call(
        body, name=name,
        grid_spec=pltpu.PrefetchScalarGridSpec(
            num_scalar_prefetch=1, grid=(N_CHIPS, rows // tr),
            in_specs=[pl.BlockSpec((None, None, tr, cols), lambda b, i, c_ref: (c_ref[0], b, i, 0)), blk],
            out_specs=blk),
        out_shape=jax.ShapeDtypeStruct((N_CHIPS, rows, cols), out_dtype),
        compiler_params=_params("parallel", "parallel"),
    )(c.reshape(1).astype(jnp.int32), g.reshape(2, N_CHIPS, rows, cols), recv.reshape(N_CHIPS, rows, cols))
    return out.reshape(recv.shape)


def _sum_slots(p, c, *, name):
    cols = p.shape[-1]
    rows = p.size // (N_CHIPS * cols)
    tr = _pick_rows(rows)

    def body(c_ref, p_ref, o_ref):
        acc = p_ref[0].astype(F32)
        for q in range(1, N_CHIPS):
            acc = acc + p_ref[q].astype(F32)
        o_ref[...] = acc

    out = pl.pallas_call(
        body, name=name,
        grid_spec=pltpu.PrefetchScalarGridSpec(
            num_scalar_prefetch=1, grid=(rows // tr,),
            in_specs=[pl.BlockSpec((N_CHIPS, tr, cols), lambda i, c_ref: (0, i, 0))],
            out_specs=pl.BlockSpec((None, tr, cols), lambda i, c_ref: (c_ref[0], i, 0))),
        out_shape=jax.ShapeDtypeStruct((2, rows, cols), F32),
        compiler_params=_params("parallel"),
    )(c.reshape(1).astype(jnp.int32), p.reshape(N_CHIPS, rows, cols))
    return out.reshape((2,) + p.shape[1:])


GROUPS = (("gu", ("w_ffn1_gu", "w_ffn2_gu"), "col"), ("down", ("w_ffn1_down", "w_ffn2_down"), "row"),
          ("square", ("w_mix_out", "w_xq", "w_xo"), "row"), ("mix_in", ("w_mix_in",), "col"), ("xkv", ("w_xkv",), "col"))
REPLICATED = ("g_ffn1", "g_mix", "b_f", "g_conv_out", "g_att_out", "g_xattn", "g_mem", "g_ffn2", "g_final")
WEIGHTS = ("g_ffn1", "w_ffn1_gu", "w_ffn1_down", "g_mix", "w_mix_in", "w_conv", "b_f", "g_conv_out", "g_att_out",
           "w_mix_out", "g_xattn", "g_mem", "w_xq", "w_xkv", "w_xo", "g_ffn2", "w_ffn2_gu", "w_ffn2_down", "g_final")
SMALL_COLS = 1024
SMALL_ROW_UNIT = 16


def _gather_weights(shards):
    packs = []
    for _, members, _ in GROUPS:
        hs = []
        for name in members:
            l, a, b = shards[name].shape
            hs.append(shards[name].astype(BF16).reshape(l, 2, a // 2, b).transpose(1, 0, 2, 3))
        packs.append(jnp.stack(hs, axis=1))
    wc = shards["w_conv"]
    packs.append(jnp.stack([wc, wc]))
    gathered = _allgather_weights(packs, name="allgather_weights")
    full = {}
    for (_, members, kind), got in zip(GROUPS, gathered):
        _, _, g, l, a2, b = got.shape
        if kind == "col":
            whole = got.transpose(2, 3, 1, 4, 0, 5).reshape(g, l, 2 * a2, N_CHIPS * b)
        else:
            whole = got.transpose(2, 3, 0, 1, 4, 5).reshape(g, l, N_CHIPS * 2 * a2, b)
        for gi, name in enumerate(members):
            full[name] = whole[gi]
    l, k, b = wc.shape
    full["w_conv"] = gathered[-1][:, 0].transpose(1, 2, 0, 3).reshape(l, k, N_CHIPS * b)
    return full


def _small_rows(v):
    flat = v.reshape(-1)
    return jnp.pad(flat, (0, -flat.shape[0] % SMALL_COLS)).reshape(-1, SMALL_COLS)


def _reduce_gradients(grads, c):
    packs = []
    for _, members, kind in GROUPS:
        cut = []
        for name in members:
            l, a, b = grads[name].shape
            if kind == "col":
                cut.append(grads[name].reshape(l, 2, a // 2, N_CHIPS, b // N_CHIPS).transpose(1, 3, 0, 2, 4))
            else:
                cut.append(grads[name].reshape(l, N_CHIPS, 2, a // (2 * N_CHIPS), b).transpose(2, 1, 0, 3, 4))
        packs.append(jnp.stack(cut, axis=2))
    rep = jnp.concatenate([_small_rows(grads[n]) for n in REPLICATED])
    l, k, b = grads["w_conv"].shape
    conv = grads["w_conv"].reshape(l, k, N_CHIPS, b // N_CHIPS).transpose(2, 0, 1, 3)
    conv_rows = [_small_rows(conv[q]) for q in range(N_CHIPS)]
    n_rows = rep.shape[0] + conv_rows[0].shape[0]
    fill = jnp.zeros((-n_rows % SMALL_ROW_UNIT, SMALL_COLS), F32)
    small = jnp.stack([jnp.concatenate([rep, conv_rows[q], fill]) for q in range(N_CHIPS)])
    half_rows = small.shape[1] // 2
    packs.append(small.reshape(N_CHIPS, 2, half_rows, SMALL_COLS).transpose(1, 0, 2, 3))

    from_sibling = _swap_halves(packs, name="grad_swap_halves")
    wire = [BF16] * len(GROUPS) + [F32]
    chip_sums = [_add_halves(g, r, c, dt, name=f"grad_add_halves_{i}")
                 for i, (g, r, dt) in enumerate(zip(packs, from_sibling, wire))]
    from_chips = _scatter_chips(chip_sums, name="grad_scatter_chips")
    halves = [_sum_slots(p, c, name=f"grad_sum_chips_{i}") for i, p in enumerate(from_chips)]
    reduced = _join_halves(halves, name="grad_join_halves")

    out = {}
    for (_, members, _), r in zip(GROUPS, reduced):
        _, g, l, a2, b = r.shape
        whole = r.transpose(1, 2, 0, 3, 4).reshape(g, l, 2 * a2, b)
        for gi, name in enumerate(members):
            out[name] = whole[gi]
    rows = reduced[-1].reshape(-1, SMALL_COLS)
    off = 0
    for name in REPLICATED:
        n = -(-grads[name].size // SMALL_COLS)
        out[name] = rows[off:off + n].reshape(-1)[:grads[name].size].reshape(grads[name].shape)
        off += n
    l, k, b = grads["w_conv"].shape
    out["w_conv"] = rows[off:off + conv_rows[0].shape[0]].reshape(-1)[:l * k * b // N_CHIPS].reshape(l, k, b // N_CHIPS)
    return out


def _ffn_fwd(x, g, w_gu, w_down, tag):
    h = _rmsnorm_fwd(x, g, name=f"{tag}_norm")
    gu = _matmul(h, w_gu, "nn", BF16, tm=2048, tn=512, tk=w_gu.shape[0], name=f"{tag}_gu")
    a = _swiglu_fwd(gu, name=f"{tag}_act")
    y = _matmul(a, w_down, "nn", F32, tm=1024, tn=1024, tk=w_down.shape[0], alpha=0.5, residual=x, name=f"{tag}_down")
    return y, (x, h, gu, a)


def _ffn_bwd(dy, saved, g, w_gu, w_down, tag):
    x, h, gu, a = saved
    f = w_down.shape[0]
    da = _matmul(dy, w_down, "nt", BF16, tm=1024, tn=f // 2, tk=w_down.shape[1], alpha=0.5, name=f"{tag}_da")
    dw_down = _matmul(a, dy, "tn", F32, tm=f // 2, tn=1024, tk=1024, alpha=0.5, name=f"{tag}_dwdown")
    dgu = _swiglu_bwd(da, gu, name=f"{tag}_dact")
    dw_gu = _matmul(h, dgu, "tn", F32, tm=1024, tn=512, tk=2048, name=f"{tag}_dwgu")
    dh = _matmul(dgu, w_gu, "nt", F32, tm=1024, tn=1024, tk=f, name=f"{tag}_dh")
    dx, dg = _rmsnorm_bwd(x, g, dh, dy, name=f"{tag}_dnorm")
    return dx, dg, dw_gu, dw_down


def _mix_fwd(x, w, l, tag):
    s, d = x.shape
    gc, ga = w["g_conv_out"][l], w["g_att_out"][l]
    cw, aw = gc.shape[0], ga.shape[0]
    nh = aw // HEAD_DIM
    zw = 3 * cw + 3 * aw
    t = 512 if s >= 2048 else s // 4
    cols = dict(qcol=3 * cw // LANES, kcol=(3 * cw + aw) // LANES, vcol=(3 * cw + 2 * aw) // LANES)
    h = _rmsnorm_fwd(x, w["g_mix"][l], name=f"{tag}_norm")
    w_in = w["w_mix_in"][l]
    w_main = w_in[:, :zw]
    w_f = jnp.pad(w_in[:, zw:], ((0, 0), (0, LANES - nh)))
    z = _matmul(h, w_main, "nn", BF16, tm=2048, tn=512, tk=d, name=f"{tag}_in")
    zf = _matmul(h, w_f, "nn", F32, tm=2048, tn=LANES, tk=d, name=f"{tag}_in_f")
    b = jnp.pad(w["b_f"][l], (0, LANES - nh)).reshape(1, LANES)
    c = _cumsum_fwd(zf, b, name=f"{tag}_cumsum")
    ccol = c[:, :nh].reshape(s, nh // 2, 2).transpose(1, 0, 2)
    assert (3 * cw) % aw == 0
    bounds = _attn_bounds(z, c, qcol_units=3 * cw // aw, kcol_units=3 * cw // aw + 1, aw=aw, t=t, name=f"{tag}_bounds")
    qa, ka, va = _attn_operands(z, ccol, t=t, name=f"{tag}_attn_operands", **cols)
    yatt, lse = _fattn_fwd(qa, ka, va, bounds, t=t, name=f"{tag}_attn")
    wc = jnp.pad(w["w_conv"][l], ((0, HALO - CONV_K), (0, 0)))
    ycat, cv = _mixpost_fwd(z, yatt, wc, gc, ga, name=f"{tag}_post")
    y = _matmul(ycat, w["w_mix_out"][l], "nn", F32, tm=1024, tn=1024, tk=d, residual=x, name=f"{tag}_out")
    return y, (x, h, w_main, w_f, z, zf, b, qa, ka, va, bounds, yatt, lse, wc, ycat, cv, t)


def _mix_bwd(dy, saved, w, l, tag):
    x, h, w_main, w_f, z, zf, b, qa, ka, va, bounds, yatt, lse, wc, ycat, cv, t = saved
    s, d = x.shape
    gc, ga = w["g_conv_out"][l], w["g_att_out"][l]
    nh = ga.shape[0] // HEAD_DIM
    zw = w_main.shape[1]
    dycat = _matmul(dy, w["w_mix_out"][l], "nt", F32, tm=1024, tn=1024, tk=d, name=f"{tag}_dycat")
    dw_out = _matmul(ycat, dy, "tn", F32, tm=1024, tn=1024, tk=1024, name=f"{tag}_dwout")
    dz_conv, dyatt, dwc, dgc, dga = _mixpost_bwd(dycat, z, yatt, cv, wc, gc, ga, name=f"{tag}_dpost")
    qb, kb, dyb = _attn_bwd_operands(qa, ka, dyatt, yatt, lse, t=t, name=f"{tag}_attn_bwd_operands")
    dq, dcq = _fattn_dq(qb, kb, va, dyb, bounds, t=t, name=f"{tag}_attn_dq")
    dk, dv, dck = _fattn_dkv(qb, kb, va, dyb, bounds, t=t, name=f"{tag}_attn_dkv")
    def heads_on_lanes(v):
        return jnp.pad(v.transpose(1, 0, 2).reshape(s, nh), ((0, 0), (0, LANES - nh)))

    dzf, db = _cumsum_bwd(heads_on_lanes(dcq), heads_on_lanes(dck), zf, b, name=f"{tag}_dcumsum")
    dz = jnp.concatenate([dz_conv, dq, dk, dv], axis=1)
    dw_main = _matmul(h, dz, "tn", F32, tm=1024, tn=512, tk=2048, name=f"{tag}_dwin")
    dw_f = _matmul(h, dzf, "tn", F32, tm=1024, tn=LANES, tk=2048, name=f"{tag}_dwin_f")
    dh = _matmul(dz, w_main, "nt", F32, tm=1024, tn=1024, tk=zw // 2, name=f"{tag}_dh")
    dh = _matmul(dzf, w_f, "nt", F32, tm=1024, tn=1024, tk=LANES, residual=dh, name=f"{tag}_dh_f")
    dx, dg = _rmsnorm_bwd(x, w["g_mix"][l], dh, dy, name=f"{tag}_dnorm")
    grads = dict(g_mix=dg, w_mix_in=jnp.concatenate([dw_main, dw_f[:, :nh]], axis=1), w_conv=dwc[:CONV_K], b_f=db[0, :nh],
                 g_conv_out=dgc[0], g_att_out=dga[0], w_mix_out=dw_out)
    return dx, grads


def _xattn_block_fwd(x, mem, w, l, tag):
    d = x.shape[1]
    h = _rmsnorm_fwd(x, w["g_xattn"][l], name=f"{tag}_norm")
    mn = _rmsnorm_fwd(mem, w["g_mem"][l], name=f"{tag}_mem_norm")
    q = _matmul(h, w["w_xq"][l], "nn", BF16, tm=1024, tn=1024, tk=d, name=f"{tag}_q")
    kv = _matmul(mn, w["w_xkv"][l], "nn", BF16, tm=1024, tn=1024, tk=d, name=f"{tag}_kv")
    o = _xattn_fwd(q, kv, name=f"{tag}_core")
    y = _matmul(o, w["w_xo"][l], "nn", F32, tm=1024, tn=1024, tk=d, residual=x, name=f"{tag}_o")
    return y, (x, h, mn, q, kv, o)


def _xattn_block_bwd(dy, saved, mem, w, l, tag):
    x, h, mn, q, kv, o = saved
    d = x.shape[1]
    do = _matmul(dy, w["w_xo"][l], "nt", BF16, tm=1024, tn=1024, tk=d, name=f"{tag}_do")
    dw_xo = _matmul(o, dy, "tn", F32, tm=1024, tn=1024, tk=1024, name=f"{tag}_dwo")
    dq, dkv = _xattn_bwd(q, kv, do, name=f"{tag}_dcore")
    dw_xq = _matmul(h, dq, "tn", F32, tm=1024, tn=1024, tk=2048, name=f"{tag}_dwq")
    dh = _matmul(dq, w["w_xq"][l], "nt", F32, tm=1024, tn=1024, tk=d, name=f"{tag}_dh")
    dx, dg = _rmsnorm_bwd(x, w["g_xattn"][l], dh, dy, name=f"{tag}_dnorm")
    dw_xkv = _matmul(mn, dkv, "tn", F32, tm=1024, tn=1024, tk=1024, name=f"{tag}_dwkv")
    dmn = _matmul(dkv, w["w_xkv"][l], "nt", F32, tm=1024, tn=1024, tk=1024, name=f"{tag}_dmem")
    _, dg_mem = _rmsnorm_bwd(mem, w["g_mem"][l], dmn, None, name=f"{tag}_dmem_norm")
    return dx, dict(g_xattn=dg, g_mem=dg_mem, w_xq=dw_xq, w_xkv=dw_xkv, w_xo=dw_xo)


def kernel(x, mem, g_ffn1, w_ffn1_gu, w_ffn1_down, g_mix, w_mix_in, w_conv, b_f, g_conv_out, g_att_out, w_mix_out, g_xattn, g_mem, w_xq, w_xkv, w_xo, g_ffn2, w_ffn2_gu, w_ffn2_down, g_final, loss_target, m_g_ffn1, m_w_ffn1_gu, m_w_ffn1_down, m_g_mix, m_w_mix_in, m_w_conv, m_b_f, m_g_conv_out, m_g_att_out, m_w_mix_out, m_g_xattn, m_g_mem, m_w_xq, m_w_xkv, m_w_xo, m_g_ffn2, m_w_ffn2_gu, m_w_ffn2_down, m_g_final, v_g_ffn1, v_w_ffn1_gu, v_w_ffn1_down, v_g_mix, v_w_mix_in, v_w_conv, v_b_f, v_g_conv_out, v_g_att_out, v_w_mix_out, v_g_xattn, v_g_mem, v_w_xq, v_w_xkv, v_w_xo, v_g_ffn2, v_w_ffn2_gu, v_w_ffn2_down, v_g_final):
    local = dict(zip(WEIGHTS, (g_ffn1, w_ffn1_gu, w_ffn1_down, g_mix, w_mix_in, w_conv, b_f, g_conv_out, g_att_out, w_mix_out,
                               g_xattn, g_mem, w_xq, w_xkv, w_xo, g_ffn2, w_ffn2_gu, w_ffn2_down, g_final)))
    mom1 = dict(zip(WEIGHTS, (m_g_ffn1, m_w_ffn1_gu, m_w_ffn1_down, m_g_mix, m_w_mix_in, m_w_conv, m_b_f, m_g_conv_out,
                              m_g_att_out, m_w_mix_out, m_g_xattn, m_g_mem, m_w_xq, m_w_xkv, m_w_xo, m_g_ffn2, m_w_ffn2_gu,
                              m_w_ffn2_down, m_g_final)))
    mom2 = dict(zip(WEIGHTS, (v_g_ffn1, v_w_ffn1_gu, v_w_ffn1_down, v_g_mix, v_w_mix_in, v_w_conv, v_b_f, v_g_conv_out,
                              v_g_att_out, v_w_mix_out, v_g_xattn, v_g_mem, v_w_xq, v_w_xkv, v_w_xo, v_g_ffn2, v_w_ffn2_gu,
                              v_w_ffn2_down, v_g_final)))
    depth = g_ffn1.shape[0]
    s, d = x.shape[1], x.shape[2]
    w = dict(local)
    w.update(_gather_weights(local))

    xs = x.reshape(s, d)
    mems = mem.reshape(mem.shape[1], d)
    saved = []
    for l in range(depth):
        xs, s1 = _ffn_fwd(xs, w["g_ffn1"][l], w["w_ffn1_gu"][l], w["w_ffn1_down"][l], f"l{l}_ffn1")
        xs, s2 = _mix_fwd(xs, w, l, f"l{l}_mix")
        xs, s3 = _xattn_block_fwd(xs, mems, w, l, f"l{l}_xattn")
        xs, s4 = _ffn_fwd(xs, w["g_ffn2"][l], w["w_ffn2_gu"][l], w["w_ffn2_down"][l], f"l{l}_ffn2")
        saved.append((s1, s2, s3, s4))

    dx, dg_final, sq = _final_loss(xs, g_final, loss_target.reshape(s, d), name="loss_head")
    loss = lax.psum(jnp.sum(sq) * (0.5 / d), MESH_AXES)

    per_layer = []
    for l in reversed(range(depth)):
        s1, s2, s3, s4 = saved[l]
        grads = {}
        dx, grads["g_ffn2"], grads["w_ffn2_gu"], grads["w_ffn2_down"] = _ffn_bwd(
            dx, s4, w["g_ffn2"][l], w["w_ffn2_gu"][l], w["w_ffn2_down"][l], f"l{l}_ffn2")
        dx, g3 = _xattn_block_bwd(dx, s3, mems, w, l, f"l{l}_xattn")
        dx, g2 = _mix_bwd(dx, s2, w, l, f"l{l}_mix")
        dx, grads["g_ffn1"], grads["w_ffn1_gu"], grads["w_ffn1_down"] = _ffn_bwd(
            dx, s1, w["g_ffn1"][l], w["w_ffn1_gu"][l], w["w_ffn1_down"][l], f"l{l}_ffn1")
        grads.update(g2)
        grads.update(g3)
        per_layer.append(grads)
    per_layer.reverse()
    grads = {name: jnp.stack([per_layer[l][name] for l in range(depth)]) for name in WEIGHTS if name != "g_final"}
    grads["g_final"] = dg_final.reshape(d)

    reduced = _reduce_gradients(grads, lax.axis_index("c"))
    deltas, new_m, new_v = {}, {}, {}
    for name in WEIGHTS:
        deltas[name], new_m[name], new_v[name] = _adamw(local[name], reduced[name], mom1[name], mom2[name], name=f"adamw_{name}")
    return (loss, dx.reshape(x.shape), *[reduced[n] for n in WEIGHTS], *[deltas[n] for n in WEIGHTS],
            *[new_m[n] for n in WEIGHTS], *[new_v[n] for n in WEIGHTS])
```

```python
import functools

import jax
import jax.numpy as jnp
from jax import lax
from jax.experimental import pallas as pl
from jax.experimental.pallas import tpu as pltpu

F32 = jnp.float32
BF16 = jnp.bfloat16

EPS = 1e-6
HEAD_DIM = 64
LANES = 128
N_XHEADS = 4
CONV_K = 3
ADAM_LR, ADAM_B1, ADAM_B2, ADAM_EPS, ADAM_WD, ADAM_STEP = 0.001, 0.9, 0.999, 1e-08, 0.01, 10
VMEM_LIMIT_BYTES = 56 * 1024 * 1024
NEG_BIG = -1e30
MESH_AXES = ("x", "y", "c")
N_CHIPS = 4


def _params(*sem):
    return pltpu.CompilerParams(dimension_semantics=sem, vmem_limit_bytes=VMEM_LIMIT_BYTES)


_DIMS = {"nn": (((1,), (0,)), ((), ())), "nt": (((1,), (1,)), ((), ())), "tn": (((0,), (0,)), ((), ()))}


def _matmul(a, b, mode, out_dtype, *, tm, tn, tk, name, alpha=1.0, addend=None, residual=None, norm_gain=None, dnorm=None):
    if mode == "nn":
        (m, k), (k2, n) = a.shape, b.shape
    elif mode == "nt":
        (m, k), (n, k2) = a.shape, b.shape
    else:
        (k, m), (k2, n) = a.shape, b.shape
    assert k == k2, (a.shape, b.shape, mode)
    tm, tn, tk = min(tm, m), min(tn, n), min(tk, k)
    assert m % tm == 0 and n % tn == 0 and k % tk == 0, (m, n, k, tm, tn, tk)
    assert (norm_gain is None and dnorm is None) or tn == n
    nk = k // tk
    dims = _DIMS[mode]

    def body(*refs):
        refs = list(refs)
        a_ref, b_ref = refs[:2]
        del refs[:2]
        add_ref = refs.pop(0) if addend is not None else None
        r_ref = refs.pop(0) if residual is not None else None
        g_ref = refs.pop(0) if (norm_gain is not None or dnorm is not None) else None
        x_ref = refs.pop(0) if dnorm is not None else None
        o_ref = refs.pop(0)
        h_ref = refs.pop(0) if norm_gain is not None else None
        dg_ref = refs.pop(0) if dnorm is not None else None
        scratch = refs
        prod = lax.dot_general(a_ref[...].astype(BF16), b_ref[...].astype(BF16), dims, preferred_element_type=F32)

        def finish(acc):
            if alpha != 1.0:
                acc = acc * alpha
            if add_ref is not None:
                acc = acc + add_ref[...]
            if dnorm is not None:
                xf = x_ref[...]
                r = _rms_rows(xf)
                xhat = xf * r
                dxhat = acc * g_ref[...]

                @pl.when((pl.program_id(0) == 0))
                def _():
                    dg_ref[...] = jnp.zeros(dg_ref.shape, F32)

                dg_ref[...] += jnp.sum(acc * xhat, axis=0, keepdims=True)
                acc = r * (dxhat - xhat * jnp.mean(dxhat * xhat, axis=-1, keepdims=True))
            if r_ref is not None:
                acc = acc + r_ref[...].astype(F32)
            o_ref[...] = acc.astype(o_ref.dtype)
            if norm_gain is not None:
                h_ref[...] = (acc * _rms_rows(acc) * g_ref[...]).astype(h_ref.dtype)

        if nk == 1:
            finish(prod)
        else:
            acc_ref = scratch[0]
            kk = pl.program_id(2)

            @pl.when(kk == 0)
            def _():
                acc_ref[...] = prod

            @pl.when(kk > 0)
            def _():
                acc_ref[...] += prod

            @pl.when(kk == nk - 1)
            def _():
                finish(acc_ref[...])

    if mode == "nn":
        a_spec = pl.BlockSpec((tm, tk), lambda i, j, kk: (i, kk))
        b_spec = pl.BlockSpec((tk, tn), lambda i, j, kk: (kk, j))
    elif mode == "nt":
        a_spec = pl.BlockSpec((tm, tk), lambda i, j, kk: (i, kk))
        b_spec = pl.BlockSpec((tn, tk), lambda i, j, kk: (j, kk))
    else:
        a_spec = pl.BlockSpec((tk, tm), lambda i, j, kk: (kk, i))
        b_spec = pl.BlockSpec((tk, tn), lambda i, j, kk: (kk, j))
    o_spec = pl.BlockSpec((tm, tn), lambda i, j, kk: (i, j))
    vec_spec = pl.BlockSpec((1, n), lambda i, j, kk: (0, 0))
    in_specs, args = [a_spec, b_spec], [a, b]
    out_specs, out_shape = [o_spec], [jax.ShapeDtypeStruct((m, n), out_dtype)]
    if addend is not None:
        in_specs.append(o_spec)
        args.append(addend)
    if residual is not None:
        in_specs.append(o_spec)
        args.append(residual)
    if norm_gain is not None:
        in_specs.append(vec_spec)
        args.append(norm_gain.reshape(1, n))
        out_specs.append(o_spec)
        out_shape.append(jax.ShapeDtypeStruct((m, n), BF16))
    if dnorm is not None:
        in_specs += [vec_spec, o_spec]
        args += [dnorm[1].reshape(1, n), dnorm[0]]
        out_specs.append(vec_spec)
        out_shape.append(jax.ShapeDtypeStruct((1, n), F32))
    outs = pl.pallas_call(
        body, name=name, grid=(m // tm, n // tn, nk), in_specs=in_specs, out_specs=out_specs, out_shape=out_shape,
        scratch_shapes=[pltpu.VMEM((tm, tn), F32)] if nk > 1 else [],
        compiler_params=_params(*(("arbitrary",) * 3 if dnorm is not None else ("parallel", "parallel", "arbitrary"))),
    )(*args)
    return outs[0] if len(outs) == 1 else tuple(outs)


def _row_tile(rows, want):
    t = min(rows, want)
    assert rows % t == 0, (rows, t)
    return t


def _rmsnorm_fwd(x, g, *, name, tr=1024):
    s, d = x.shape
    tr = _row_tile(s, tr)

    def body(x_ref, g_ref, o_ref):
        xf = x_ref[...]
        r = lax.rsqrt(jnp.mean(xf * xf, axis=-1, keepdims=True) + EPS)
        o_ref[...] = (xf * r * g_ref[...]).astype(o_ref.dtype)

    return pl.pallas_call(
        body, name=name, grid=(s // tr,),
        in_specs=[pl.BlockSpec((tr, d), lambda i: (i, 0)), pl.BlockSpec((1, d), lambda i: (0, 0))],
        out_specs=pl.BlockSpec((tr, d), lambda i: (i, 0)),
        out_shape=jax.ShapeDtypeStruct((s, d), BF16),
        compiler_params=_params("parallel"),
    )(x, g.reshape(1, d))


def _rmsnorm_bwd(x, g, dh, dres, *, name, tr=512):
    s, d = x.shape
    tr = _row_tile(s, tr)

    def body(x_ref, g_ref, dh_ref, *rest):
        if dres is None:
            dx_ref, dg_ref = rest
        else:
            dres_ref, dx_ref, dg_ref = rest
        xf = x_ref[...]
        r = lax.rsqrt(jnp.mean(xf * xf, axis=-1, keepdims=True) + EPS)
        xhat = xf * r
        dhf = dh_ref[...].astype(F32)
        dxhat = dhf * g_ref[...]
        dx = r * (dxhat - xhat * jnp.mean(dxhat * xhat, axis=-1, keepdims=True))
        if dres is not None:
            dx = dx + dres_ref[...]
        dx_ref[...] = dx

        @pl.when(pl.program_id(0) == 0)
        def _():
            dg_ref[...] = jnp.zeros_like(dg_ref)

        dg_ref[...] += jnp.sum(dhf * xhat, axis=0, keepdims=True)

    row = pl.BlockSpec((tr, d), lambda i: (i, 0))
    vec = pl.BlockSpec((1, d), lambda i: (0, 0))
    in_specs, args = [row, vec, row], [x, g.reshape(1, d), dh]
    if dres is not None:
        in_specs.append(row)
        args.append(dres)
    dx, dg = pl.pallas_call(
        body, name=name, grid=(s // tr,), in_specs=in_specs, out_specs=[row, vec],
        out_shape=[jax.ShapeDtypeStruct((s, d), F32), jax.ShapeDtypeStruct((1, d), F32)],
        compiler_params=_params("arbitrary"),
    )(*args)
    return dx, dg.reshape(d)


def _gate_up_act(h, w_gu, *, name, tm=2048, tn=256):
    s, d = h.shape
    f = w_gu.shape[1] // 2
    tm, tn = min(tm, s), min(tn, f)
    assert s % tm == 0 and f % tn == 0
    nf = f // tn

    def body(h_ref, wg_ref, wu_ref, gu_ref, a_ref):
        hb = h_ref[...]
        gate = _dot(hb, wg_ref[...], _NN)
        up = _dot(hb, wu_ref[...], _NN)
        gu_ref[0] = gate.astype(gu_ref.dtype)
        gu_ref[1] = up.astype(gu_ref.dtype)
        a_ref[...] = (gate * jax.nn.sigmoid(gate) * up).astype(a_ref.dtype)

    return pl.pallas_call(
        body, name=name, grid=(s // tm, nf),
        in_specs=[pl.BlockSpec((tm, d), lambda i, j: (i, 0)), pl.BlockSpec((d, tn), lambda i, j: (0, j)),
                  pl.BlockSpec((d, tn), lambda i, j: (0, j + nf))],
        out_specs=[pl.BlockSpec((2, tm, tn), lambda i, j: (0, i, j)), pl.BlockSpec((tm, tn), lambda i, j: (i, j))],
        out_shape=[jax.ShapeDtypeStruct((2, s, f), BF16), jax.ShapeDtypeStruct((s, f), BF16)],
        compiler_params=_params("parallel", "parallel"),
    )(h, w_gu, w_gu)


def _swiglu_bwd(da, gu, *, name, tr=512):
    _, s, f = gu.shape
    tr = _row_tile(s, tr)

    def body(da_ref, gu_ref, dgu_ref):
        gate = gu_ref[0].astype(F32)
        up = gu_ref[1].astype(F32)
        daf = da_ref[...].astype(F32)
        sig = jax.nn.sigmoid(gate)
        silu = gate * sig
        dgu_ref[:, :f] = (daf * up * (sig + silu * (1.0 - sig))).astype(dgu_ref.dtype)
        dgu_ref[:, f:] = (daf * silu).astype(dgu_ref.dtype)

    return pl.pallas_call(
        body, name=name, grid=(s // tr,),
        in_specs=[pl.BlockSpec((tr, f), lambda i: (i, 0)), pl.BlockSpec((2, tr, f), lambda i: (0, i, 0))],
        out_specs=pl.BlockSpec((tr, 2 * f), lambda i: (i, 0)),
        out_shape=jax.ShapeDtypeStruct((s, 2 * f), BF16),
        compiler_params=_params("parallel"),
    )(da, gu)


_NT = (((1,), (1,)), ((), ()))
_NN = (((1,), (0,)), ((), ()))
_TN = (((0,), (0,)), ((), ()))
_QK_SCALE = HEAD_DIM ** -0.5


def _dot(a, b, dims):
    return lax.dot_general(a, b, dims, preferred_element_type=F32)


SKIP_BELOW = 107.0
_SMEM = pl.BlockSpec(memory_space=pltpu.SMEM)


def _attn_bounds(z, c, *, qcol_units, kcol_units, aw, t, name):
    s = z.shape[0]
    nq = s // t
    nh = aw // HEAD_DIM

    def body(q_ref, k_ref, o_ref):
        d = lax.broadcasted_iota(jnp.int32, (aw, LANES), 0)
        hh = lax.broadcasted_iota(jnp.int32, (aw, LANES), 1)
        onehot = ((d >= hh * HEAD_DIM) & (d < (hh + 1) * HEAD_DIM)).astype(BF16)
        for r, ref in enumerate((q_ref, k_ref)):
            v = ref[...].astype(F32)
            sq = _dot((v * v).astype(BF16), onehot, _NN)
            o_ref[r:r + 1, :] = jnp.max(sq, axis=0, keepdims=True)
        o_ref[2:, :] = jnp.zeros((HALO - 2, LANES), F32)

    sq = pl.pallas_call(
        body, name=name, grid=(nq,),
        in_specs=[pl.BlockSpec((t, aw), lambda i: (i, qcol_units)), pl.BlockSpec((t, aw), lambda i: (i, kcol_units))],
        out_specs=pl.BlockSpec((None, HALO, LANES), lambda i: (i, 0, 0)),
        out_shape=jax.ShapeDtypeStruct((nq, HALO, LANES), F32),
        compiler_params=_params("parallel"),
    )(z, z)
    norms = jnp.sqrt(sq[:, :2, :nh]) * 1.01
    qn = (norms[:, 0, :] * _QK_SCALE).T.reshape(-1)
    kn = norms[:, 1, :].T.reshape(-1)
    cs = c[0::t, :nh].T.reshape(-1)
    ce = c[t - 1::t, :nh].T.reshape(-1)
    return qn, kn, cs, ce


def _block_active(bounds, head, i, j, nq):
    qn_ref, kn_ref, cs_ref, ce_ref = bounds
    qi = qn_ref[head * nq + i]
    upper = qi * kn_ref[head * nq + j] + (cs_ref[head * nq + i] - ce_ref[head * nq + j])
    lower = -(qi * kn_ref[head * nq + i])
    return upper - lower > -SKIP_BELOW


def _for_active_heads(bounds, pair, i, j, nq, head_step):
    act = [_block_active(bounds, 2 * pair + h, i, j, nq) for h in range(2)]

    def run(heads):
        for h in heads:
            head_step(h)

    pl.when(act[0] & act[1])(functools.partial(run, (0, 1)))
    pl.when(act[0] & jnp.logical_not(act[1]))(functools.partial(run, (0,)))
    pl.when(jnp.logical_not(act[0]) & act[1])(functools.partial(run, (1,)))


def _attn_operands(z, ccol, *, qcol, kcol, vcol, t, name):
    s = z.shape[0]
    npairs = ccol.shape[0]

    def body(q_ref, k_ref, v_ref, c_ref, qa_ref, ka_ref, va_ref):
        lane = lax.broadcasted_iota(jnp.int32, (t, LANES), 1)
        q2 = q_ref[...] * jnp.asarray(_QK_SCALE, BF16)
        k2, v2 = k_ref[...], v_ref[...]
        one, zero = jnp.ones((t, LANES), BF16), jnp.zeros((t, LANES), BF16)
        for h in range(2):
            base = HEAD_DIM * (1 - h)
            mine = (lane < HEAD_DIM) if h == 0 else (lane >= HEAD_DIM)
            c = c_ref[:, h:h + 1]
            hi = c.astype(BF16)
            r1 = c - hi.astype(F32)
            mid = r1.astype(BF16)
            lo = (r1 - mid.astype(F32)).astype(BF16)
            qa, ka = jnp.where(mine, q2, zero), jnp.where(mine, k2, zero)
            for r, word in enumerate((hi, mid, lo)):
                qa = jnp.where(lane == base + r, word, qa)
                qa = jnp.where(lane == base + 3 + r, one, qa)
                ka = jnp.where(lane == base + r, one, ka)
                ka = jnp.where(lane == base + 3 + r, -word, ka)
            qa_ref[h] = qa
            ka_ref[h] = ka
            va_ref[h] = jnp.where((lane >= base) & (lane < base + 3), one, jnp.where(mine, v2, zero))

    out = jax.ShapeDtypeStruct((npairs, 2, s, LANES), BF16)
    blk = pl.BlockSpec((None, 2, t, LANES), lambda p, i: (p, 0, i, 0))
    return pl.pallas_call(
        body, name=name, grid=(npairs, s // t),
        in_specs=[pl.BlockSpec((t, LANES), lambda p, i: (i, qcol + p)), pl.BlockSpec((t, LANES), lambda p, i: (i, kcol + p)),
                  pl.BlockSpec((t, LANES), lambda p, i: (i, vcol + p)), pl.BlockSpec((None, t, 2), lambda p, i: (p, i, 0))],
        out_specs=[blk, blk, blk], out_shape=[out, out, out],
        compiler_params=_params("parallel", "parallel"),
    )(z, z, z, ccol)


def _fattn_fwd(qa, ka, va, bounds, *, t, name):
    npairs, _, s, _ = qa.shape
    nq = s // t
    reps = t // LANES

    def body(qn_ref, kn_ref, cs_ref, ce_ref, q_ref, k_ref, v_ref, o_ref, lse_ref, m_scr, acc_scr):
        pair, i = pl.program_id(0), pl.program_id(1)
        m_scr[...] = jnp.full(m_scr.shape, NEG_BIG, F32)
        acc_scr[...] = jnp.zeros(acc_scr.shape, F32)

        def head_step(h, j, diagonal):
            off = pl.multiple_of(j * t, t)
            sc = _dot(q_ref[h], k_ref[h, pl.ds(off, t), :], _NT)
            if diagonal:
                row = lax.broadcasted_iota(jnp.int32, (t, t), 0)
                col = lax.broadcasted_iota(jnp.int32, (t, t), 1)
                sc = jnp.where(row >= col, sc, NEG_BIG)
            m_old = m_scr[h]
            m_new = jnp.maximum(m_old, jnp.max(sc, axis=1, keepdims=True))
            p = jnp.exp(sc - jnp.tile(m_new, (1, reps)))
            acc_scr[h] = acc_scr[h] * jnp.exp(m_old - m_new) + _dot(p.astype(BF16), v_ref[h, pl.ds(off, t), :], _NN)
            m_scr[h] = m_new

        def loop_body(j, carry):
            _for_active_heads((qn_ref, kn_ref, cs_ref, ce_ref), pair, i, j, nq, lambda h: head_step(h, j, False))
            return carry

        lax.fori_loop(0, i, loop_body, 0)
        for h in range(2):
            head_step(h, i, True)
        is_a = lax.broadcasted_iota(jnp.int32, (t, LANES), 1) < HEAD_DIM
        l = (acc_scr[0][:, HEAD_DIM:HEAD_DIM + 1], acc_scr[1][:, 0:1])
        o_ref[...] = jnp.where(is_a, acc_scr[0] / l[0], acc_scr[1] / l[1])
        lse_ref[:, 0:1] = m_scr[0][:, 0:1] + jnp.log(l[0])
        lse_ref[:, 1:2] = m_scr[1][:, 0:1] + jnp.log(l[1])

    return pl.pallas_call(
        body, name=name, grid=(npairs, nq),
        in_specs=[_SMEM, _SMEM, _SMEM, _SMEM,
                  pl.BlockSpec((None, 2, t, LANES), lambda p, i: (p, 0, i, 0)),
                  pl.BlockSpec((None, 2, s, LANES), lambda p, i: (p, 0, 0, 0)),
                  pl.BlockSpec((None, 2, s, LANES), lambda p, i: (p, 0, 0, 0))],
        out_specs=[pl.BlockSpec((t, LANES), lambda p, i: (i, p)), pl.BlockSpec((None, t, 2), lambda p, i: (p, i, 0))],
        out_shape=[jax.ShapeDtypeStruct((s, npairs * LANES), F32), jax.ShapeDtypeStruct((npairs, s, 2), F32)],
        scratch_shapes=[pltpu.VMEM((2, t, LANES), F32), pltpu.VMEM((2, t, LANES), F32)],
        compiler_params=_params("parallel", "arbitrary"),
    )(*bounds, qa, ka, va)


def _split3(x):
    hi = x.astype(BF16)
    r1 = x - hi.astype(F32)
    mid = r1.astype(BF16)
    return hi, mid, (r1 - mid.astype(F32)).astype(BF16)


def _attn_bwd_operands(qa, ka, dy, y, lse, *, t, name):
    npairs, _, s, _ = qa.shape

    def body(qa_ref, ka_ref, dy_ref, y_ref, lse_ref, qb_ref, kb_ref, dyb_ref):
        lane = lax.broadcasted_iota(jnp.int32, (t, LANES), 1)
        dyf = dy_ref[...]
        prod = dyf * y_ref[...]
        dyh = dyf.astype(BF16)
        one, zero = jnp.ones((t, LANES), BF16), jnp.zeros((t, LANES), BF16)
        for h in range(2):
            base = HEAD_DIM * (1 - h)
            mine = (lane < HEAD_DIM) if h == 0 else (lane >= HEAD_DIM)
            delta = jnp.sum(jnp.where(mine, prod, 0.0), axis=1, keepdims=True)
            qb, kb, dyb = qa_ref[h], ka_ref[h], jnp.where(mine, dyh, zero)
            for r, (lw, dw) in enumerate(zip(_split3(lse_ref[:, h:h + 1]), _split3(delta))):
                qb = jnp.where(lane == base + 6 + r, -lw, qb)
                kb = jnp.where(lane == base + 6 + r, one, kb)
                dyb = jnp.where(lane == base + r, -dw, dyb)
            qb_ref[h] = qb
            kb_ref[h] = kb
            dyb_ref[h] = dyb

    out = jax.ShapeDtypeStruct((npairs, 2, s, LANES), BF16)
    blk = pl.BlockSpec((None, 2, t, LANES), lambda p, i: (p, 0, i, 0))
    tile = pl.BlockSpec((t, LANES), lambda p, i: (i, p))
    return pl.pallas_call(
        body, name=name, grid=(npairs, s // t),
        in_specs=[blk, blk, tile, tile, pl.BlockSpec((None, t, 2), lambda p, i: (p, i, 0))],
        out_specs=[blk, blk, blk], out_shape=[out, out, out],
        compiler_params=_params("parallel", "parallel"),
    )(qa, ka, dy, y, lse)


def _diag_mask(sc, t, queries_on_rows):
    row = lax.broadcasted_iota(jnp.int32, (t, t), 0)
    col = lax.broadcasted_iota(jnp.int32, (t, t), 1)
    return jnp.where((row >= col) if queries_on_rows else (col >= row), sc, NEG_BIG)


def _fattn_dq(qb, kb, va, dyb, bounds, *, t, name):
    npairs, _, s, _ = qb.shape
    nq = s // t

    def body(qn_ref, kn_ref, cs_ref, ce_ref, q_ref, k_ref, v_ref, dy_ref, dq_ref, dcq_ref, acc_scr):
        pair, i = pl.program_id(0), pl.program_id(1)
        acc_scr[...] = jnp.zeros(acc_scr.shape, F32)

        def head_step(h, j, diagonal):
            off = pl.multiple_of(j * t, t)
            kj = k_ref[h, pl.ds(off, t), :]
            sc = _dot(q_ref[h], kj, _NT)
            if diagonal:
                sc = _diag_mask(sc, t, True)
            ds = jnp.exp(sc) * _dot(dy_ref[h], v_ref[h, pl.ds(off, t), :], _NT)
            acc_scr[h] += _dot(ds.astype(BF16), kj, _NN)

        def loop_body(j, carry):
            _for_active_heads((qn_ref, kn_ref, cs_ref, ce_ref), pair, i, j, nq, lambda h: head_step(h, j, False))
            return carry

        lax.fori_loop(0, i, loop_body, 0)
        for h in range(2):
            head_step(h, i, True)
        is_a = lax.broadcasted_iota(jnp.int32, (t, LANES), 1) < HEAD_DIM
        dq_ref[...] = (jnp.where(is_a, acc_scr[0], acc_scr[1]) * _QK_SCALE).astype(dq_ref.dtype)
        dcq_ref[:, 0:1] = acc_scr[0][:, HEAD_DIM:HEAD_DIM + 1]
        dcq_ref[:, 1:2] = acc_scr[1][:, 0:1]

    tile2 = pl.BlockSpec((None, 2, t, LANES), lambda p, i: (p, 0, i, 0))
    whole = pl.BlockSpec((None, 2, s, LANES), lambda p, i: (p, 0, 0, 0))
    return pl.pallas_call(
        body, name=name, grid=(npairs, nq),
        in_specs=[_SMEM, _SMEM, _SMEM, _SMEM, tile2, whole, whole, tile2],
        out_specs=[pl.BlockSpec((t, LANES), lambda p, i: (i, p)), pl.BlockSpec((None, t, 2), lambda p, i: (p, i, 0))],
        out_shape=[jax.ShapeDtypeStruct((s, npairs * LANES), BF16), jax.ShapeDtypeStruct((npairs, s, 2), F32)],
        scratch_shapes=[pltpu.VMEM((2, t, LANES), F32)],
        compiler_params=_params("parallel", "arbitrary"),
    )(*bounds, qb, kb, va, dyb)


def _fattn_dkv(qb, kb, va, dyb, bounds, *, t, name):
    npairs, _, s, _ = qb.shape
    nq = s // t

    def body(qn_ref, kn_ref, cs_ref, ce_ref, k_ref, v_ref, q_ref, dy_ref, dk_ref, dv_ref, dc_ref, dk_scr, dv_scr):
        pair, j = pl.program_id(0), pl.program_id(1)
        dk_scr[...] = jnp.zeros(dk_scr.shape, F32)
        dv_scr[...] = jnp.zeros(dv_scr.shape, F32)

        def head_step(h, i, diagonal):
            off = pl.multiple_of(i * t, t)
            qi = q_ref[h, pl.ds(off, t), :]
            dyi = dy_ref[h, pl.ds(off, t), :]
            st = _dot(k_ref[h], qi, _NT)
            if diagonal:
                st = _diag_mask(st, t, False)
            pt = jnp.exp(st)
            dv_scr[h] += _dot(pt.astype(BF16), dyi, _NN)
            dst = pt * _dot(v_ref[h], dyi, _NT)
            dk_scr[h] += _dot(dst.astype(BF16), qi, _NN)

        def loop_body(i, carry):
            _for_active_heads((qn_ref, kn_ref, cs_ref, ce_ref), pair, i, j, nq, lambda h: head_step(h, i, False))
            return carry

        for h in range(2):
            head_step(h, j, True)
        lax.fori_loop(j + 1, nq, loop_body, 0)
        is_a = lax.broadcasted_iota(jnp.int32, (t, LANES), 1) < HEAD_DIM
        dk_ref[...] = jnp.where(is_a, dk_scr[0], dk_scr[1]).astype(dk_ref.dtype)
        dv_ref[...] = jnp.where(is_a, dv_scr[0], dv_scr[1]).astype(dv_ref.dtype)
        dc_ref[:, 0:1] = -dk_scr[0][:, HEAD_DIM + 3:HEAD_DIM + 4]
        dc_ref[:, 1:2] = -dk_scr[1][:, 3:4]

    tile2 = pl.BlockSpec((None, 2, t, LANES), lambda p, j: (p, 0, j, 0))
    whole = pl.BlockSpec((None, 2, s, LANES), lambda p, j: (p, 0, 0, 0))
    tile = pl.BlockSpec((t, LANES), lambda p, j: (j, p))
    return pl.pallas_call(
        body, name=name, grid=(npairs, nq),
        in_specs=[_SMEM, _SMEM, _SMEM, _SMEM, tile2, tile2, whole, whole],
        out_specs=[tile, tile, pl.BlockSpec((None, t, 2), lambda p, j: (p, j, 0))],
        out_shape=[jax.ShapeDtypeStruct((s, npairs * LANES), BF16), jax.ShapeDtypeStruct((s, npairs * LANES), BF16),
                   jax.ShapeDtypeStruct((npairs, s, 2), F32)],
        scratch_shapes=[pltpu.VMEM((2, t, LANES), F32), pltpu.VMEM((2, t, LANES), F32)],
        compiler_params=_params("parallel", "arbitrary"),
    )(*bounds, kb, va, qb, dyb)


def _log_sigmoid(x):
    return jnp.minimum(x, 0.0) - jnp.log(1.0 + jnp.exp(-jnp.abs(x)))


def _cumsum_fwd(zf, b, *, name, t=512):
    s, w = zf.shape
    t = _row_tile(s, t)

    def body(zf_ref, b_ref, c_ref, carry):
        @pl.when(pl.program_id(0) == 0)
        def _():
            carry[...] = jnp.zeros(carry.shape, F32)

        lf = _log_sigmoid(zf_ref[...] + b_ref[...])
        row = lax.broadcasted_iota(jnp.int32, (t, t), 0)
        col = lax.broadcasted_iota(jnp.int32, (t, t), 1)
        tri = (row >= col).astype(F32)
        c = lax.dot_general(tri, lf, _NN, precision=lax.Precision.HIGHEST, preferred_element_type=F32) + carry[...]
        c_ref[...] = c
        carry[...] = c[t - 1:t, :]

    return pl.pallas_call(
        body, name=name, grid=(s // t,),
        in_specs=[pl.BlockSpec((t, w), lambda i: (i, 0)), pl.BlockSpec((1, w), lambda i: (0, 0))],
        out_specs=pl.BlockSpec((t, w), lambda i: (i, 0)),
        out_shape=jax.ShapeDtypeStruct((s, w), F32),
        scratch_shapes=[pltpu.VMEM((1, w), F32)],
        compiler_params=_params("arbitrary"),
    )(zf, b)


def _cumsum_bwd(dcq, dck, zf, b, *, name, t=512):
    s, w = zf.shape
    t = _row_tile(s, t)
    nb = s // t

    def body(dcq_ref, dck_ref, zf_ref, b_ref, dzf_ref, db_ref, carry):
        @pl.when(pl.program_id(0) == 0)
        def _():
            carry[...] = jnp.zeros(carry.shape, F32)
            db_ref[...] = jnp.zeros(db_ref.shape, F32)

        row = lax.broadcasted_iota(jnp.int32, (t, t), 0)
        col = lax.broadcasted_iota(jnp.int32, (t, t), 1)
        tri = (row <= col).astype(F32)
        dc = dcq_ref[...] + dck_ref[...]
        dlf = lax.dot_general(tri, dc, _NN, precision=lax.Precision.HIGHEST, preferred_element_type=F32) + carry[...]
        carry[...] = dlf[0:1, :]
        dzf = dlf * jax.nn.sigmoid(-(zf_ref[...] + b_ref[...]))
        dzf_ref[...] = dzf
        db_ref[...] += jnp.sum(dzf, axis=0, keepdims=True)

    blk = pl.BlockSpec((t, w), lambda i: (nb - 1 - i, 0))
    vec = pl.BlockSpec((1, w), lambda i: (0, 0))
    return pl.pallas_call(
        body, name=name, grid=(nb,), in_specs=[blk, blk, blk, vec], out_specs=[blk, vec],
        out_shape=[jax.ShapeDtypeStruct((s, w), F32), jax.ShapeDtypeStruct((1, w), F32)],
        scratch_shapes=[pltpu.VMEM((1, w), F32)],
        compiler_params=_params("arbitrary"),
    )(dcq, dck, zf, b)


HALO = 8


def _rms_rows(v):
    return lax.rsqrt(jnp.mean(v * v, axis=-1, keepdims=True) + EPS)


def _mixpost_fwd(z, yatt, wconv, gc, ga, *, name, tr=512):
    s = z.shape[0]
    cw, aw = gc.shape[-1], ga.shape[-1]
    tr = _row_tile(s, tr)

    def body(zb_ref, zc_ref, zv_ref, ya_ref, w_ref, gc_ref, ga_ref, ycat_ref, cv_ref, u_scr):
        @pl.when(pl.program_id(0) == 0)
        def _():
            u_scr[0:HALO, :] = jnp.zeros((HALO, cw), F32)

        u = zc_ref[...].astype(F32) * zv_ref[...].astype(F32)
        u_scr[HALO:HALO + tr, :] = u
        cv = w_ref[0:1, :] * u_scr[HALO - 2:HALO - 2 + tr, :] + w_ref[1:2, :] * u_scr[HALO - 1:HALO - 1 + tr, :] + w_ref[2:3, :] * u
        u_scr[0:HALO, :] = u_scr[tr:tr + HALO, :]
        cv_ref[...] = cv
        yc = zb_ref[...].astype(F32) * cv
        ya = ya_ref[...]
        ycat_ref[:, :cw] = (yc * _rms_rows(yc) * gc_ref[...]).astype(ycat_ref.dtype)
        ycat_ref[:, cw:] = (ya * _rms_rows(ya) * ga_ref[...]).astype(ycat_ref.dtype)

    return pl.pallas_call(
        body, name=name, grid=(s // tr,),
        in_specs=[
            pl.BlockSpec((tr, cw), lambda i: (i, 0)), pl.BlockSpec((tr, cw), lambda i: (i, 1)),
            pl.BlockSpec((tr, cw), lambda i: (i, 2)), pl.BlockSpec((tr, aw), lambda i: (i, 0)),
            pl.BlockSpec((HALO, cw), lambda i: (0, 0)), pl.BlockSpec((1, cw), lambda i: (0, 0)),
            pl.BlockSpec((1, aw), lambda i: (0, 0)),
        ],
        out_specs=[pl.BlockSpec((tr, cw + aw), lambda i: (i, 0)), pl.BlockSpec((tr, cw), lambda i: (i, 0))],
        out_shape=[jax.ShapeDtypeStruct((s, cw + aw), BF16), jax.ShapeDtypeStruct((s, cw), F32)],
        scratch_shapes=[pltpu.VMEM((tr + HALO, cw), F32)],
        compiler_params=_params("arbitrary"),
    )(z, z, z, yatt, wconv, gc.reshape(1, cw), ga.reshape(1, aw))


def _mixpost_bwd(dycat, z, yatt, cv, wconv, gc, ga, *, name, tr=512):
    s = z.shape[0]
    cw, aw = gc.shape[-1], ga.shape[-1]
    tr = _row_tile(s, tr)
    nb = s // tr

    def body(dy_ref, zb_ref, zc_ref, zv_ref, ya_ref, cv_ref, w_ref, gc_ref, ga_ref,
             dz_ref, dya_ref, dw_ref, dgc_ref, dga_ref, d_scr):
        @pl.when(pl.program_id(0) == 0)
        def _():
            d_scr[tr:tr + HALO, :] = jnp.zeros((HALO, cw), F32)
            dw_ref[...] = jnp.zeros(dw_ref.shape, F32)
            dgc_ref[...] = jnp.zeros(dgc_ref.shape, F32)
            dga_ref[...] = jnp.zeros(dga_ref.shape, F32)

        zb, zc, zv = zb_ref[...].astype(F32), zc_ref[...].astype(F32), zv_ref[...].astype(F32)
        cvv = cv_ref[...]

        def norm_bwd(v, dn, g):
            r = _rms_rows(v)
            vh = v * r
            dvh = dn * g
            return r * (dvh - vh * jnp.mean(dvh * vh, axis=-1, keepdims=True)), jnp.sum(dn * vh, axis=0, keepdims=True)

        dyc, dgc = norm_bwd(zb * cvv, dy_ref[:, :cw], gc_ref[...])
        dya, dga = norm_bwd(ya_ref[...], dy_ref[:, cw:], ga_ref[...])
        dgc_ref[...] += dgc
        dga_ref[...] += dga
        dya_ref[...] = dya
        dcv = dyc * zb
        d_scr[0:tr, :] = dcv
        d1 = d_scr[1:tr + 1, :]
        d2 = d_scr[2:tr + 2, :]
        du = w_ref[2:3, :] * dcv + w_ref[1:2, :] * d1 + w_ref[0:1, :] * d2
        u = zc * zv
        dw_ref[0:1, :] += jnp.sum(u * d2, axis=0, keepdims=True)
        dw_ref[1:2, :] += jnp.sum(u * d1, axis=0, keepdims=True)
        dw_ref[2:3, :] += jnp.sum(u * dcv, axis=0, keepdims=True)
        d_scr[tr:tr + HALO, :] = d_scr[0:HALO, :]
        dz_ref[:, :cw] = (dyc * cvv).astype(dz_ref.dtype)
        dz_ref[:, cw:2 * cw] = (du * zv).astype(dz_ref.dtype)
        dz_ref[:, 2 * cw:] = (du * zc).astype(dz_ref.dtype)

    def rows(width, colblk=0):
        return pl.BlockSpec((tr, width), lambda i: (nb - 1 - i, colblk))

    def fixed(r, width):
        return pl.BlockSpec((r, width), lambda i: (0, 0))

    return pl.pallas_call(
        body, name=name, grid=(nb,),
        in_specs=[rows(cw + aw), rows(cw, 0), rows(cw, 1), rows(cw, 2), rows(aw), rows(cw),
                  fixed(HALO, cw), fixed(1, cw), fixed(1, aw)],
        out_specs=[rows(3 * cw), rows(aw), fixed(HALO, cw), fixed(1, cw), fixed(1, aw)],
        out_shape=[jax.ShapeDtypeStruct((s, 3 * cw), BF16), jax.ShapeDtypeStruct((s, aw), F32),
                   jax.ShapeDtypeStruct((HALO, cw), F32), jax.ShapeDtypeStruct((1, cw), F32),
                   jax.ShapeDtypeStruct((1, aw), F32)],
        scratch_shapes=[pltpu.VMEM((tr + HALO, cw), F32)],
        compiler_params=_params("arbitrary"),
    )(dycat, z, z, z, yatt, cv, wconv, gc.reshape(1, cw), ga.reshape(1, aw))


def _xattn_fwd(q, kv, *, name, tq=1024):
    s, d = q.shape
    m = kv.shape[0]
    dh = d // N_XHEADS
    scale = dh ** -0.5
    tq = _row_tile(s, tq)

    def body(q_ref, kv_ref, o_ref):
        for h in range(N_XHEADS):
            lo, hi = h * dh, (h + 1) * dh
            sc = _dot(q_ref[:, lo:hi], kv_ref[:, lo:hi], _NT) * scale
            p = jnp.exp(sc - jnp.max(sc, axis=1, keepdims=True))
            o = _dot(p.astype(BF16), kv_ref[:, d + lo:d + hi], _NN) / jnp.sum(p, axis=1, keepdims=True)
            o_ref[:, lo:hi] = o.astype(o_ref.dtype)

    return pl.pallas_call(
        body, name=name, grid=(s // tq,),
        in_specs=[pl.BlockSpec((tq, d), lambda i: (i, 0)), pl.BlockSpec((m, 2 * d), lambda i: (0, 0))],
        out_specs=pl.BlockSpec((tq, d), lambda i: (i, 0)),
        out_shape=jax.ShapeDtypeStruct((s, d), BF16),
        compiler_params=_params("parallel"),
    )(q, kv)


def _xattn_bwd(q, kv, do, *, name, tq=1024):
    s, d = q.shape
    m = kv.shape[0]
    dh = d // N_XHEADS
    scale = dh ** -0.5
    tq = _row_tile(s, tq)

    def body(q_ref, kv_ref, do_ref, dq_ref, dkv_ref):
        @pl.when(pl.program_id(0) == 0)
        def _():
            dkv_ref[...] = jnp.zeros(dkv_ref.shape, F32)

        for h in range(N_XHEADS):
            lo, hi = h * dh, (h + 1) * dh
            qh, kh, vh, doh = q_ref[:, lo:hi], kv_ref[:, lo:hi], kv_ref[:, d + lo:d + hi], do_ref[:, lo:hi]
            sc = _dot(qh, kh, _NT) * scale
            e = jnp.exp(sc - jnp.max(sc, axis=1, keepdims=True))
            p = e / jnp.sum(e, axis=1, keepdims=True)
            dp = _dot(doh, vh, _NT)
            ds = p * (dp - jnp.sum(dp * p, axis=1, keepdims=True))
            dsb = ds.astype(BF16)
            dq_ref[:, lo:hi] = (_dot(dsb, kh, _NN) * scale).astype(dq_ref.dtype)
            dkv_ref[:, lo:hi] += _dot(dsb, qh, _TN) * scale
            dkv_ref[:, d + lo:d + hi] += _dot(p.astype(BF16), doh, _TN)

    return pl.pallas_call(
        body, name=name, grid=(s // tq,),
        in_specs=[pl.BlockSpec((tq, d), lambda i: (i, 0)), pl.BlockSpec((m, 2 * d), lambda i: (0, 0)),
                  pl.BlockSpec((tq, d), lambda i: (i, 0))],
        out_specs=[pl.BlockSpec((tq, d), lambda i: (i, 0)), pl.BlockSpec((m, 2 * d), lambda i: (0, 0))],
        out_shape=[jax.ShapeDtypeStruct((s, d), BF16), jax.ShapeDtypeStruct((m, 2 * d), F32)],
        compiler_params=_params("arbitrary"),
    )(q, kv, do)


def _final_loss(x, g, target, *, name, tr=512):
    s, d = x.shape
    tr = _row_tile(s, tr)

    def body(x_ref, g_ref, t_ref, dx_ref, dg_ref, sq_ref):
        @pl.when(pl.program_id(0) == 0)
        def _():
            dg_ref[...] = jnp.zeros(dg_ref.shape, F32)
            sq_ref[...] = jnp.zeros(sq_ref.shape, F32)

        xf = x_ref[...]
        r = _rms_rows(xf)
        xhat = xf * r
        err = xhat * g_ref[...] - t_ref[...]
        sq_ref[...] += jnp.sum(err * err, axis=0, keepdims=True)
        dy = err * (1.0 / d)
        dg_ref[...] += jnp.sum(dy * xhat, axis=0, keepdims=True)
        dxhat = dy * g_ref[...]
        dx_ref[...] = r * (dxhat - xhat * jnp.mean(dxhat * xhat, axis=-1, keepdims=True))

    row = pl.BlockSpec((tr, d), lambda i: (i, 0))
    vec = pl.BlockSpec((1, d), lambda i: (0, 0))
    return pl.pallas_call(
        body, name=name, grid=(s // tr,), in_specs=[row, vec, row], out_specs=[row, vec, vec],
        out_shape=[jax.ShapeDtypeStruct((s, d), F32), jax.ShapeDtypeStruct((1, d), F32), jax.ShapeDtypeStruct((1, d), F32)],
        compiler_params=_params("arbitrary"),
    )(x, g.reshape(1, d), target)


def _adamw(w, g, m, v, *, name, tr=512):
    shape = w.shape
    cols = shape[-1]
    rows = w.size // cols
    tr = tr if rows % tr == 0 else rows

    def body(w_ref, g_ref, m_ref, v_ref, d_ref, nm_ref, nv_ref):
        gf = g_ref[...]
        nm = ADAM_B1 * m_ref[...] + (1.0 - ADAM_B1) * gf
        nv = ADAM_B2 * v_ref[...] + (1.0 - ADAM_B2) * (gf * gf)
        m_hat = nm / (1.0 - ADAM_B1 ** ADAM_STEP)
        v_hat = nv / (1.0 - ADAM_B2 ** ADAM_STEP)
        d_ref[...] = -ADAM_LR * (m_hat / (jnp.sqrt(v_hat) + ADAM_EPS) + ADAM_WD * w_ref[...])
        nm_ref[...] = nm
        nv_ref[...] = nv

    blk = pl.BlockSpec((tr, cols), lambda i: (i, 0))
    out = jax.ShapeDtypeStruct((rows, cols), F32)
    outs = pl.pallas_call(
        body, name=name, grid=(rows // tr,), in_specs=[blk] * 4, out_specs=[blk] * 3, out_shape=[out] * 3,
        compiler_params=_params("parallel"),
    )(*[t.reshape(rows, cols) for t in (w, g, m, v)])
    return tuple(o.reshape(shape) for o in outs)


_HBM = pl.BlockSpec(memory_space=pl.ANY)
_MESH_ID = pl.DeviceIdType.MESH


def _place():
    x, y, c = (lax.axis_index(a) for a in MESH_AXES)
    return x, y, c, [(1 - x, y), (x, 1 - y), (1 - x, 1 - y)]


def _remote(src, dst, send_sems, recv_sems, k, to):
    return pltpu.make_async_remote_copy(src_ref=src, dst_ref=dst, send_sem=send_sems.at[k], recv_sem=recv_sems.at[k],
                                        device_id=to, device_id_type=_MESH_ID)


def _comm_call(body, arrays, out_shapes, n_remote, name):
    return pl.pallas_call(
        body, name=name, in_specs=[_HBM] * len(arrays), out_specs=[_HBM] * len(out_shapes), out_shape=out_shapes,
        scratch_shapes=[pltpu.SemaphoreType.DMA((n_remote,)), pltpu.SemaphoreType.DMA((n_remote,)),
                        pltpu.SemaphoreType.DMA((len(arrays),))],
    )(*arrays)


def _allgather_weights(halves, *, name):
    n = len(halves)

    def body(*refs):
        w, out = refs[:n], refs[n:2 * n]
        send_sems, recv_sems, local_sems = refs[2 * n:]
        x, y, c, chips = _place()
        me = 2 * x + y
        sibling = (x, y, 1 - c)
        slots = [2 * px + py for px, py in chips]

        def copy(t, k, slot, half, to, src=None):
            dst = out[t].at[slot, half]
            return _remote(dst if src is None else src, dst, send_sems, recv_sems, 7 * t + k, to)

        local = [pltpu.make_async_copy(w[t].at[c], out[t].at[me, c], local_sems.at[t]) for t in range(n)]
        first = []
        for t in range(n):
            local[t].start()
            first.append(copy(t, 0, me, c, sibling, src=w[t].at[c]))
            first += [copy(t, 1 + j, me, c, (px, py, c), src=w[t].at[c]) for j, (px, py) in enumerate(chips)]
        for cp in first:
            cp.start()
        passed = []
        for j, (px, py) in enumerate(chips):
            for t in range(n):
                copy(t, 1 + j, slots[j], c, (px, py, c)).wait_recv()
                passed.append(copy(t, 4 + j, slots[j], c, sibling))
                passed[-1].start()
        for t in range(n):
            copy(t, 0, me, 1 - c, sibling).wait_recv()
            for j in range(3):
                copy(t, 4 + j, slots[j], 1 - c, sibling).wait_recv()
        for cp in first + passed:
            cp.wait_send()
        for cp in local:
            cp.wait()

    shapes = [jax.ShapeDtypeStruct((N_CHIPS,) + h.shape, h.dtype) for h in halves]
    return _comm_call(body, halves, shapes, 7 * n, name)


def _swap_halves(gs, *, name):
    n = len(gs)

    def body(*refs):
        g, out = refs[:n], refs[n:2 * n]
        send_sems, recv_sems, _ = refs[2 * n:]
        x, y, c, _ = _place()
        copies = [_remote(g[t].at[1 - c], out[t], send_sems, recv_sems, t, (x, y, 1 - c)) for t in range(n)]
        for cp in copies:
            cp.start()
        for cp in copies:
            cp.wait()

    return _comm_call(body, gs, [jax.ShapeDtypeStruct(g.shape[1:], g.dtype) for g in gs], n, name)


def _scatter_chips(ps, *, name):
    n = len(ps)

    def body(*refs):
        p, out = refs[:n], refs[n:2 * n]
        send_sems, recv_sems, local_sems = refs[2 * n:]
        x, y, c, chips = _place()
        me = 2 * x + y
        local = [pltpu.make_async_copy(p[t].at[me], out[t].at[me], local_sems.at[t]) for t in range(n)]
        sends = [_remote(p[t].at[2 * px + py], out[t].at[me], send_sems, recv_sems, 3 * t + j, (px, py, c))
                 for t in range(n) for j, (px, py) in enumerate(chips)]
        for cp in local + sends:
            cp.start()
        for t in range(n):
            for j, (px, py) in enumerate(chips):
                slot = out[t].at[2 * px + py]
                _remote(slot, slot, send_sems, recv_sems, 3 * t + j, (px, py, c)).wait_recv()
        for cp in sends:
            cp.wait_send()
        for cp in local:
            cp.wait()

    return _comm_call(body, ps, [jax.ShapeDtypeStruct(p.shape, p.dtype) for p in ps], 3 * n, name)


def _join_halves(rs, *, name):
    n = len(rs)

    def body(*refs):
        r, out = refs[:n], refs[n:2 * n]
        send_sems, recv_sems, _ = refs[2 * n:]
        x, y, c, _ = _place()
        sends = [_remote(r[t].at[c], out[t].at[c], send_sems, recv_sems, t, (x, y, 1 - c)) for t in range(n)]
        for cp in sends:
            cp.start()
        for t in range(n):
            slot = out[t].at[1 - c]
            _remote(slot, slot, send_sems, recv_sems, t, (x, y, 1 - c)).wait_recv()
        for cp in sends:
            cp.wait_send()

    return pl.pallas_call(
        body, name=name, in_specs=[_HBM] * n, out_specs=[_HBM] * n,
        out_shape=[jax.ShapeDtypeStruct(r.shape, r.dtype) for r in rs],
        input_output_aliases={t: t for t in range(n)},
        scratch_shapes=[pltpu.SemaphoreType.DMA((n,)), pltpu.SemaphoreType.DMA((n,)), pltpu.SemaphoreType.DMA((n,))],
    )(*rs)


def _pick_rows(rows, cap=512):
    for t in range(min(rows, cap), 0, -1):
        if rows % t == 0 and t % 16 == 0:
            return t
    return rows


def _add_halves(g, recv, c, out_dtype, *, name):
    cols = g.shape[-1]
    rows = recv.size // (N_CHIPS * cols)
    tr = _pick_rows(rows)

    def body(c_ref, g_ref, r_ref, o_ref):
        o_ref[...] = (g_ref[...] + r_ref[...]).astype(o_ref.dtype)

    blk = pl.BlockSpec((None, tr, cols), lambda b, i, c_ref: (b, i, 0))
    out = pl.pallas_call(
        body, name=name,
        grid_spec=pltpu.PrefetchScalarGridSpec(
            num_scalar_prefetch=1, grid=(N_CHIPS, rows // tr),
            in_specs=[pl.BlockSpec((None, None, tr, cols), lambda b, i, c_ref: (c_ref[0], b, i, 0)), blk],
            out_specs=blk),
        out_shape=jax.ShapeDtypeStruct((N_CHIPS, rows, cols), out_dtype),
        compiler_params=_params("parallel", "parallel"),
    )(c.reshape(1).astype(jnp.int32), g.reshape(2, N_CHIPS, rows, cols), recv.reshape(N_CHIPS, rows, cols))
    return out.reshape(recv.shape)


def _sum_slots(p, c, *, name):
    cols = p.shape[-1]
    rows = p.size // (N_CHIPS * cols)
    tr = _pick_rows(rows)

    def body(c_ref, p_ref, o_ref):
        acc = p_ref[0].astype(F32)
        for q in range(1, N_CHIPS):
            acc = acc + p_ref[q].astype(F32)
        o_ref[...] = acc

    out = pl.pallas_call(
        body, name=name,
        grid_spec=pltpu.PrefetchScalarGridSpec(
            num_scalar_prefetch=1, grid=(rows // tr,),
            in_specs=[pl.BlockSpec((N_CHIPS, tr, cols), lambda i, c_ref: (0, i, 0))],
            out_specs=pl.BlockSpec((None, tr, cols), lambda i, c_ref: (c_ref[0], i, 0))),
        out_shape=jax.ShapeDtypeStruct((2, rows, cols), F32),
        compiler_params=_params("parallel"),
    )(c.reshape(1).astype(jnp.int32), p.reshape(N_CHIPS, rows, cols))
    return out.reshape((2,) + p.shape[1:])


GROUPS = (("gu", ("w_ffn1_gu", "w_ffn2_gu"), "col"), ("down", ("w_ffn1_down", "w_ffn2_down"), "row"),
          ("square", ("w_mix_out", "w_xq", "w_xo"), "row"), ("mix_in", ("w_mix_in",), "col"), ("xkv", ("w_xkv",), "col"))
REPLICATED = ("g_ffn1", "g_mix", "b_f", "g_conv_out", "g_att_out", "g_xattn", "g_mem", "g_ffn2", "g_final")
WEIGHTS = ("g_ffn1", "w_ffn1_gu", "w_ffn1_down", "g_mix", "w_mix_in", "w_conv", "b_f", "g_conv_out", "g_att_out",
           "w_mix_out", "g_xattn", "g_mem", "w_xq", "w_xkv", "w_xo", "g_ffn2", "w_ffn2_gu", "w_ffn2_down", "g_final")
SMALL_COLS = 1024
SMALL_ROW_UNIT = 16


def _gather_weights(shards):
    packs = []
    for _, members, _ in GROUPS:
        hs = []
        for name in members:
            l, a, b = shards[name].shape
            hs.append(shards[name].astype(BF16).reshape(l, 2, a // 2, b).transpose(1, 0, 2, 3))
        packs.append(jnp.stack(hs, axis=1))
    wc = shards["w_conv"]
    packs.append(jnp.stack([wc, wc]))
    gathered = _allgather_weights(packs, name="allgather_weights")
    full = {}
    for (_, members, kind), got in zip(GROUPS, gathered):
        _, _, g, l, a2, b = got.shape
        if kind == "col":
            whole = got.transpose(2, 3, 1, 4, 0, 5).reshape(g, l, 2 * a2, N_CHIPS * b)
        else:
            whole = got.transpose(2, 3, 0, 1, 4, 5).reshape(g, l, N_CHIPS * 2 * a2, b)
        for gi, name in enumerate(members):
            full[name] = whole[gi]
    l, k, b = wc.shape
    full["w_conv"] = gathered[-1][:, 0].transpose(1, 2, 0, 3).reshape(l, k, N_CHIPS * b)
    return full


def _small_rows(v):
    flat = v.reshape(-1)
    return jnp.pad(flat, (0, -flat.shape[0] % SMALL_COLS)).reshape(-1, SMALL_COLS)


def _reduce_gradients(grads, c):
    packs = []
    for _, members, kind in GROUPS:
        cut = []
        for name in members:
            l, a, b = grads[name].shape
            if kind == "col":
                cut.append(grads[name].reshape(l, 2, a // 2, N_CHIPS, b // N_CHIPS).transpose(1, 3, 0, 2, 4))
            else:
                cut.append(grads[name].reshape(l, N_CHIPS, 2, a // (2 * N_CHIPS), b).transpose(2, 1, 0, 3, 4))
        packs.append(jnp.stack(cut, axis=2))
    rep = jnp.concatenate([_small_rows(grads[n]) for n in REPLICATED])
    l, k, b = grads["w_conv"].shape
    conv = grads["w_conv"].reshape(l, k, N_CHIPS, b // N_CHIPS).transpose(2, 0, 1, 3)
    conv_rows = [_small_rows(conv[q]) for q in range(N_CHIPS)]
    n_rows = rep.shape[0] + conv_rows[0].shape[0]
    fill = jnp.zeros((-n_rows % SMALL_ROW_UNIT, SMALL_COLS), F32)
    small = jnp.stack([jnp.concatenate([rep, conv_rows[q], fill]) for q in range(N_CHIPS)])
    half_rows = small.shape[1] // 2
    packs.append(small.reshape(N_CHIPS, 2, half_rows, SMALL_COLS).transpose(1, 0, 2, 3))

    from_sibling = _swap_halves(packs, name="grad_swap_halves")
    wire = [BF16] * len(GROUPS) + [F32]
    chip_sums = [_add_halves(g, r, c, dt, name=f"grad_add_halves_{i}")
                 for i, (g, r, dt) in enumerate(zip(packs, from_sibling, wire))]
    from_chips = _scatter_chips(chip_sums, name="grad_scatter_chips")
    halves = [_sum_slots(p, c, name=f"grad_sum_chips_{i}") for i, p in enumerate(from_chips)]
    reduced = _join_halves(halves, name="grad_join_halves")

    out = {}
    for (_, members, _), r in zip(GROUPS, reduced):
        _, g, l, a2, b = r.shape
        whole = r.transpose(1, 2, 0, 3, 4).reshape(g, l, 2 * a2, b)
        for gi, name in enumerate(members):
            out[name] = whole[gi]
    rows = reduced[-1].reshape(-1, SMALL_COLS)
    off = 0
    for name in REPLICATED:
        n = -(-grads[name].size // SMALL_COLS)
        out[name] = rows[off:off + n].reshape(-1)[:grads[name].size].reshape(grads[name].shape)
        off += n
    l, k, b = grads["w_conv"].shape
    out["w_conv"] = rows[off:off + conv_rows[0].shape[0]].reshape(-1)[:l * k * b // N_CHIPS].reshape(l, k, b // N_CHIPS)
    return out


def _residual_out(a, w_out, x, g_next, name, alpha=1.0):
    kw = dict(tm=1024, tn=1024, tk=w_out.shape[0], alpha=alpha, residual=x, name=name)
    if g_next is None:
        return _matmul(a, w_out, "nn", F32, **kw), None
    return _matmul(a, w_out, "nn", F32, norm_gain=g_next, **kw)


def _ffn_fwd(x, h, w_gu, w_down, g_next, tag):
    gu, a = _gate_up_act(h, w_gu, name=f"{tag}_gu_act")
    y, h_next = _residual_out(a, w_down, x, g_next, f"{tag}_down", alpha=0.5)
    return y, h_next, (x, h, gu, a)


def _ffn_bwd(dy, saved, g, w_gu, w_down, tag):
    x, h, gu, a = saved
    f = w_down.shape[0]
    da = _matmul(dy, w_down, "nt", BF16, tm=1024, tn=f // 2, tk=w_down.shape[1], alpha=0.5, name=f"{tag}_da")
    dw_down = _matmul(a, dy, "tn", F32, tm=f // 2, tn=1024, tk=1024, alpha=0.5, name=f"{tag}_dwdown")
    dgu = _swiglu_bwd(da, gu, name=f"{tag}_dact")
    dw_gu = _matmul(h, dgu, "tn", F32, tm=1024, tn=512, tk=2048, name=f"{tag}_dwgu")
    dx, dg = _matmul(dgu, w_gu, "nt", F32, tm=512, tn=1024, tk=f, residual=dy, dnorm=(x, g), name=f"{tag}_dh_dnorm")
    return dx, dg[0], dw_gu, dw_down


def _mix_fwd(x, h, w, l, g_next, tag):
    s, d = x.shape
    gc, ga = w["g_conv_out"][l], w["g_att_out"][l]
    cw, aw = gc.shape[0], ga.shape[0]
    nh = aw // HEAD_DIM
    zw = 3 * cw + 3 * aw
    t = 512 if s >= 2048 else s // 4
    cols = dict(qcol=3 * cw // LANES, kcol=(3 * cw + aw) // LANES, vcol=(3 * cw + 2 * aw) // LANES)
    w_in = w["w_mix_in"][l]
    w_main = w_in[:, :zw]
    w_f = jnp.pad(w_in[:, zw:], ((0, 0), (0, LANES - nh)))
    z = _matmul(h, w_main, "nn", BF16, tm=2048, tn=512, tk=d, name=f"{tag}_in")
    zf = _matmul(h, w_f, "nn", F32, tm=2048, tn=LANES, tk=d, name=f"{tag}_in_f")
    b = jnp.pad(w["b_f"][l], (0, LANES - nh)).reshape(1, LANES)
    c = _cumsum_fwd(zf, b, name=f"{tag}_cumsum")
    ccol = c[:, :nh].reshape(s, nh // 2, 2).transpose(1, 0, 2)
    assert (3 * cw) % aw == 0
    bounds = _attn_bounds(z, c, qcol_units=3 * cw // aw, kcol_units=3 * cw // aw + 1, aw=aw, t=t, name=f"{tag}_bounds")
    qa, ka, va = _attn_operands(z, ccol, t=t, name=f"{tag}_attn_operands", **cols)
    yatt, lse = _fattn_fwd(qa, ka, va, bounds, t=t, name=f"{tag}_attn")
    wc = jnp.pad(w["w_conv"][l], ((0, HALO - CONV_K), (0, 0)))
    ycat, cv = _mixpost_fwd(z, yatt, wc, gc, ga, name=f"{tag}_post")
    y, h_next = _residual_out(ycat, w["w_mix_out"][l], x, g_next, f"{tag}_out")
    return y, h_next, (x, h, w_main, w_f, z, zf, b, qa, ka, va, bounds, yatt, lse, wc, ycat, cv, t)


def _mix_bwd(dy, saved, w, l, tag):
    x, h, w_main, w_f, z, zf, b, qa, ka, va, bounds, yatt, lse, wc, ycat, cv, t = saved
    s, d = x.shape
    gc, ga = w["g_conv_out"][l], w["g_att_out"][l]
    nh = ga.shape[0] // HEAD_DIM
    zw = w_main.shape[1]
    dycat = _matmul(dy, w["w_mix_out"][l], "nt", F32, tm=1024, tn=1024, tk=d, name=f"{tag}_dycat")
    dw_out = _matmul(ycat, dy, "tn", F32, tm=1024, tn=1024, tk=1024, name=f"{tag}_dwout")
    dz_conv, dyatt, dwc, dgc, dga = _mixpost_bwd(dycat, z, yatt, cv, wc, gc, ga, name=f"{tag}_dpost")
    qb, kb, dyb = _attn_bwd_operands(qa, ka, dyatt, yatt, lse, t=t, name=f"{tag}_attn_bwd_operands")
    dq, dcq = _fattn_dq(qb, kb, va, dyb, bounds, t=t, name=f"{tag}_attn_dq")
    dk, dv, dck = _fattn_dkv(qb, kb, va, dyb, bounds, t=t, name=f"{tag}_attn_dkv")
    def heads_on_lanes(v):
        return jnp.pad(v.transpose(1, 0, 2).reshape(s, nh), ((0, 0), (0, LANES - nh)))

    dzf, db = _cumsum_bwd(heads_on_lanes(dcq), heads_on_lanes(dck), zf, b, name=f"{tag}_dcumsum")
    dz = jnp.concatenate([dz_conv, dq, dk, dv], axis=1)
    dw_main = _matmul(h, dz, "tn", F32, tm=1024, tn=512, tk=2048, name=f"{tag}_dwin")
    dw_f = _matmul(h, dzf, "tn", F32, tm=1024, tn=LANES, tk=2048, name=f"{tag}_dwin_f")
    dh_f = _matmul(dzf, w_f, "nt", F32, tm=1024, tn=1024, tk=LANES, name=f"{tag}_dh_f")
    dx, dg = _matmul(dz, w_main, "nt", F32, tm=512, tn=1024, tk=zw // 2, addend=dh_f, residual=dy,
                     dnorm=(x, w["g_mix"][l]), name=f"{tag}_dh_dnorm")
    grads = dict(g_mix=dg[0], w_mix_in=jnp.concatenate([dw_main, dw_f[:, :nh]], axis=1), w_conv=dwc[:CONV_K], b_f=db[0, :nh],
                 g_conv_out=dgc[0], g_att_out=dga[0], w_mix_out=dw_out)
    return dx, grads


def _xattn_block_fwd(x, h, mem, w, l, g_next, tag):
    d = x.shape[1]
    mn = _rmsnorm_fwd(mem, w["g_mem"][l], name=f"{tag}_mem_norm")
    q = _matmul(h, w["w_xq"][l], "nn", BF16, tm=1024, tn=1024, tk=d, name=f"{tag}_q")
    kv = _matmul(mn, w["w_xkv"][l], "nn", BF16, tm=1024, tn=1024, tk=d, name=f"{tag}_kv")
    o = _xattn_fwd(q, kv, name=f"{tag}_core")
    y, h_next = _residual_out(o, w["w_xo"][l], x, g_next, f"{tag}_o")
    return y, h_next, (x, h, mn, q, kv, o)


def _xattn_block_bwd(dy, saved, mem, w, l, tag):
    x, h, mn, q, kv, o = saved
    d = x.shape[1]
    do = _matmul(dy, w["w_xo"][l], "nt", BF16, tm=1024, tn=1024, tk=d, name=f"{tag}_do")
    dw_xo = _matmul(o, dy, "tn", F32, tm=1024, tn=1024, tk=1024, name=f"{tag}_dwo")
    dq, dkv = _xattn_bwd(q, kv, do, name=f"{tag}_dcore")
    dw_xq = _matmul(h, dq, "tn", F32, tm=1024, tn=1024, tk=2048, name=f"{tag}_dwq")
    dx, dg = _matmul(dq, w["w_xq"][l], "nt", F32, tm=512, tn=1024, tk=d, residual=dy, dnorm=(x, w["g_xattn"][l]),
                     name=f"{tag}_dh_dnorm")
    dw_xkv = _matmul(mn, dkv, "tn", F32, tm=1024, tn=1024, tk=1024, name=f"{tag}_dwkv")
    dmn = _matmul(dkv, w["w_xkv"][l], "nt", F32, tm=1024, tn=1024, tk=1024, name=f"{tag}_dmem")
    _, dg_mem = _rmsnorm_bwd(mem, w["g_mem"][l], dmn, None, name=f"{tag}_dmem_norm")
    return dx, dict(g_xattn=dg[0], g_mem=dg_mem, w_xq=dw_xq, w_xkv=dw_xkv, w_xo=dw_xo)


def kernel(x, mem, g_ffn1, w_ffn1_gu, w_ffn1_down, g_mix, w_mix_in, w_conv, b_f, g_conv_out, g_att_out, w_mix_out, g_xattn, g_mem, w_xq, w_xkv, w_xo, g_ffn2, w_ffn2_gu, w_ffn2_down, g_final, loss_target, m_g_ffn1, m_w_ffn1_gu, m_w_ffn1_down, m_g_mix, m_w_mix_in, m_w_conv, m_b_f, m_g_conv_out, m_g_att_out, m_w_mix_out, m_g_xattn, m_g_mem, m_w_xq, m_w_xkv, m_w_xo, m_g_ffn2, m_w_ffn2_gu, m_w_ffn2_down, m_g_final, v_g_ffn1, v_w_ffn1_gu, v_w_ffn1_down, v_g_mix, v_w_mix_in, v_w_conv, v_b_f, v_g_conv_out, v_g_att_out, v_w_mix_out, v_g_xattn, v_g_mem, v_w_xq, v_w_xkv, v_w_xo, v_g_ffn2, v_w_ffn2_gu, v_w_ffn2_down, v_g_final):
    local = dict(zip(WEIGHTS, (g_ffn1, w_ffn1_gu, w_ffn1_down, g_mix, w_mix_in, w_conv, b_f, g_conv_out, g_att_out, w_mix_out,
                               g_xattn, g_mem, w_xq, w_xkv, w_xo, g_ffn2, w_ffn2_gu, w_ffn2_down, g_final)))
    mom1 = dict(zip(WEIGHTS, (m_g_ffn1, m_w_ffn1_gu, m_w_ffn1_down, m_g_mix, m_w_mix_in, m_w_conv, m_b_f, m_g_conv_out,
                              m_g_att_out, m_w_mix_out, m_g_xattn, m_g_mem, m_w_xq, m_w_xkv, m_w_xo, m_g_ffn2, m_w_ffn2_gu,
                              m_w_ffn2_down, m_g_final)))
    mom2 = dict(zip(WEIGHTS, (v_g_ffn1, v_w_ffn1_gu, v_w_ffn1_down, v_g_mix, v_w_mix_in, v_w_conv, v_b_f, v_g_conv_out,
                              v_g_att_out, v_w_mix_out, v_g_xattn, v_g_mem, v_w_xq, v_w_xkv, v_w_xo, v_g_ffn2, v_w_ffn2_gu,
                              v_w_ffn2_down, v_g_final)))
    depth = g_ffn1.shape[0]
    s, d = x.shape[1], x.shape[2]
    w = dict(local)
    w.update(_gather_weights(local))

    xs = x.reshape(s, d)
    mems = mem.reshape(mem.shape[1], d)
    saved = []
    h = _rmsnorm_fwd(xs, w["g_ffn1"][0], name="l0_ffn1_norm")
    for l in range(depth):
        g_after = w["g_ffn1"][l + 1] if l + 1 < depth else None
        xs, h, s1 = _ffn_fwd(xs, h, w["w_ffn1_gu"][l], w["w_ffn1_down"][l], w["g_mix"][l], f"l{l}_ffn1")
        xs, h, s2 = _mix_fwd(xs, h, w, l, w["g_xattn"][l], f"l{l}_mix")
        xs, h, s3 = _xattn_block_fwd(xs, h, mems, w, l, w["g_ffn2"][l], f"l{l}_xattn")
        xs, h, s4 = _ffn_fwd(xs, h, w["w_ffn2_gu"][l], w["w_ffn2_down"][l], g_after, f"l{l}_ffn2")
        saved.append((s1, s2, s3, s4))

    dx, dg_final, sq = _final_loss(xs, g_final, loss_target.reshape(s, d), name="loss_head")
    loss = lax.psum(jnp.sum(sq) * (0.5 / d), MESH_AXES)

    per_layer = []
    for l in reversed(range(depth)):
        s1, s2, s3, s4 = saved[l]
        grads = {}
        dx, grads["g_ffn2"], grads["w_ffn2_gu"], grads["w_ffn2_down"] = _ffn_bwd(
            dx, s4, w["g_ffn2"][l], w["w_ffn2_gu"][l], w["w_ffn2_down"][l], f"l{l}_ffn2")
        dx, g3 = _xattn_block_bwd(dx, s3, mems, w, l, f"l{l}_xattn")
        dx, g2 = _mix_bwd(dx, s2, w, l, f"l{l}_mix")
        dx, grads["g_ffn1"], grads["w_ffn1_gu"], grads["w_ffn1_down"] = _ffn_bwd(
            dx, s1, w["g_ffn1"][l], w["w_ffn1_gu"][l], w["w_ffn1_down"][l], f"l{l}_ffn1")
        grads.update(g2)
        grads.update(g3)
        per_layer.append(grads)
    per_layer.reverse()
    grads = {name: jnp.stack([per_layer[l][name] for l in range(depth)]) for name in WEIGHTS if name != "g_final"}
    grads["g_final"] = dg_final.reshape(d)

    reduced = _reduce_gradients(grads, lax.axis_index("c"))
    deltas, new_m, new_v = {}, {}, {}
    for name in WEIGHTS:
        deltas[name], new_m[name], new_v[name] = _adamw(local[name], reduced[name], mom1[name], mom2[name], name=f"adamw_{name}")
    return (loss, dx.reshape(x.shape), *[reduced[n] for n in WEIGHTS], *[deltas[n] for n in WEIGHTS],
            *[new_m[n] for n in WEIGHTS], *[new_v[n] for n in WEIGHTS])
```

```python
import functools

import jax
import jax.numpy as jnp
from jax import lax
from jax.experimental import pallas as pl
from jax.experimental.pallas import tpu as pltpu

F32 = jnp.float32
BF16 = jnp.bfloat16

EPS = 1e-6
HEAD_DIM = 64
LANES = 128
N_XHEADS = 4
CONV_K = 3
ADAM_LR, ADAM_B1, ADAM_B2, ADAM_EPS, ADAM_WD, ADAM_STEP = 0.001, 0.9, 0.999, 1e-08, 0.01, 10
VMEM_LIMIT_BYTES = 56 * 1024 * 1024
NEG_BIG = -1e30
MESH_AXES = ("x", "y", "c")
N_CHIPS = 4


def _params(*sem):
    return pltpu.CompilerParams(dimension_semantics=sem, vmem_limit_bytes=VMEM_LIMIT_BYTES)


_DIMS = {"nn": (((1,), (0,)), ((), ())), "nt": (((1,), (1,)), ((), ())), "tn": (((0,), (0,)), ((), ()))}


def _matmul(a, b, mode, out_dtype, *, tm, tn, tk, name, alpha=1.0, addend=None, residual=None, norm_gain=None, dnorm=None,
            a_slabs=False, b_slab=None):
    if a_slabs:
        assert mode == "nt"
        n_slab, m, k_slab = a.shape
        k, (n, k2) = n_slab * k_slab, b.shape
        tk = min(tk, k_slab)
        assert k_slab % tk == 0
    elif b_slab is not None:
        assert mode == "tn"
        (k, m), (_, k2, n) = a.shape, b.shape
    elif mode == "nn":
        (m, k), (k2, n) = a.shape, b.shape
    elif mode == "nt":
        (m, k), (n, k2) = a.shape, b.shape
    else:
        (k, m), (k2, n) = a.shape, b.shape
    assert k == k2, (a.shape, b.shape, mode)
    tm, tn, tk = min(tm, m), min(tn, n), min(tk, k)
    assert m % tm == 0 and n % tn == 0 and k % tk == 0, (m, n, k, tm, tn, tk)
    assert (norm_gain is None and dnorm is None) or tn == n
    nk = k // tk
    dims = _DIMS[mode]

    def body(*refs):
        refs = list(refs)
        a_ref, b_ref = refs[:2]
        del refs[:2]
        add_ref = refs.pop(0) if addend is not None else None
        r_ref = refs.pop(0) if residual is not None else None
        g_ref = refs.pop(0) if (norm_gain is not None or dnorm is not None) else None
        x_ref = refs.pop(0) if dnorm is not None else None
        o_ref = refs.pop(0)
        h_ref = refs.pop(0) if norm_gain is not None else None
        dg_ref = refs.pop(0) if dnorm is not None else None
        scratch = refs
        prod = lax.dot_general(a_ref[...].astype(BF16), b_ref[...].astype(BF16), dims, preferred_element_type=F32)

        def finish(acc):
            if alpha != 1.0:
                acc = acc * alpha
            if add_ref is not None:
                acc = acc + add_ref[...]
            if dnorm is not None:
                xf = x_ref[...]
                r = _rms_rows(xf)
                xhat = xf * r
                dxhat = acc * g_ref[...]

                @pl.when((pl.program_id(0) == 0))
                def _():
                    dg_ref[...] = jnp.zeros(dg_ref.shape, F32)

                dg_ref[...] += jnp.sum(acc * xhat, axis=0, keepdims=True)
                acc = r * (dxhat - xhat * jnp.mean(dxhat * xhat, axis=-1, keepdims=True))
            if r_ref is not None:
                acc = acc + r_ref[...].astype(F32)
            o_ref[...] = acc.astype(o_ref.dtype)
            if norm_gain is not None:
                h_ref[...] = (acc * _rms_rows(acc) * g_ref[...]).astype(h_ref.dtype)

        if nk == 1:
            finish(prod)
        else:
            acc_ref = scratch[0]
            kk = pl.program_id(2)

            @pl.when(kk == 0)
            def _():
                acc_ref[...] = prod

            @pl.when(kk > 0)
            def _():
                acc_ref[...] += prod

            @pl.when(kk == nk - 1)
            def _():
                finish(acc_ref[...])

    if mode == "nn":
        a_spec = pl.BlockSpec((tm, tk), lambda i, j, kk: (i, kk))
        b_spec = pl.BlockSpec((tk, tn), lambda i, j, kk: (kk, j))
    elif mode == "nt":
        a_spec = pl.BlockSpec((tm, tk), lambda i, j, kk: (i, kk))
        b_spec = pl.BlockSpec((tn, tk), lambda i, j, kk: (j, kk))
    else:
        a_spec = pl.BlockSpec((tk, tm), lambda i, j, kk: (kk, i))
        b_spec = pl.BlockSpec((tk, tn), lambda i, j, kk: (kk, j))
    if a_slabs:
        per_slab = k_slab // tk
        assert n_slab == 2

        def a_index(i, j, kk):
            second = (kk >= per_slab).astype(jnp.int32)
            return second, i, kk - second * per_slab

        a_spec = pl.BlockSpec((None, tm, tk), a_index)
    if b_slab is not None:
        b_spec = pl.BlockSpec((None, tk, tn), lambda i, j, kk: (b_slab, kk, j))
    o_spec = pl.BlockSpec((tm, tn), lambda i, j, kk: (i, j))
    vec_spec = pl.BlockSpec((1, n), lambda i, j, kk: (0, 0))
    in_specs, args = [a_spec, b_spec], [a, b]
    out_specs, out_shape = [o_spec], [jax.ShapeDtypeStruct((m, n), out_dtype)]
    if addend is not None:
        in_specs.append(o_spec)
        args.append(addend)
    if residual is not None:
        in_specs.append(o_spec)
        args.append(residual)
    if norm_gain is not None:
        in_specs.append(vec_spec)
        args.append(norm_gain.reshape(1, n))
        out_specs.append(o_spec)
        out_shape.append(jax.ShapeDtypeStruct((m, n), BF16))
    if dnorm is not None:
        in_specs += [vec_spec, o_spec]
        args += [dnorm[1].reshape(1, n), dnorm[0]]
        out_specs.append(vec_spec)
        out_shape.append(jax.ShapeDtypeStruct((1, n), F32))
    outs = pl.pallas_call(
        body, name=name, grid=(m // tm, n // tn, nk), in_specs=in_specs, out_specs=out_specs, out_shape=out_shape,
        scratch_shapes=[pltpu.VMEM((tm, tn), F32)] if nk > 1 else [],
        compiler_params=_params(*(("arbitrary",) * 3 if dnorm is not None else ("parallel", "parallel", "arbitrary"))),
    )(*args)
    return outs[0] if len(outs) == 1 else tuple(outs)


def _row_tile(rows, want):
    t = min(rows, want)
    assert rows % t == 0, (rows, t)
    return t


def _rmsnorm_fwd(x, g, *, name, tr=1024):
    s, d = x.shape
    tr = _row_tile(s, tr)

    def body(x_ref, g_ref, o_ref):
        xf = x_ref[...]
        r = lax.rsqrt(jnp.mean(xf * xf, axis=-1, keepdims=True) + EPS)
        o_ref[...] = (xf * r * g_ref[...]).astype(o_ref.dtype)

    return pl.pallas_call(
        body, name=name, grid=(s // tr,),
        in_specs=[pl.BlockSpec((tr, d), lambda i: (i, 0)), pl.BlockSpec((1, d), lambda i: (0, 0))],
        out_specs=pl.BlockSpec((tr, d), lambda i: (i, 0)),
        out_shape=jax.ShapeDtypeStruct((s, d), BF16),
        compiler_params=_params("parallel"),
    )(x, g.reshape(1, d))


def _rmsnorm_bwd(x, g, dh, dres, *, name, tr=512):
    s, d = x.shape
    tr = _row_tile(s, tr)

    def body(x_ref, g_ref, dh_ref, *rest):
        if dres is None:
            dx_ref, dg_ref = rest
        else:
            dres_ref, dx_ref, dg_ref = rest
        xf = x_ref[...]
        r = lax.rsqrt(jnp.mean(xf * xf, axis=-1, keepdims=True) + EPS)
        xhat = xf * r
        dhf = dh_ref[...].astype(F32)
        dxhat = dhf * g_ref[...]
        dx = r * (dxhat - xhat * jnp.mean(dxhat * xhat, axis=-1, keepdims=True))
        if dres is not None:
            dx = dx + dres_ref[...]
        dx_ref[...] = dx

        @pl.when(pl.program_id(0) == 0)
        def _():
            dg_ref[...] = jnp.zeros_like(dg_ref)

        dg_ref[...] += jnp.sum(dhf * xhat, axis=0, keepdims=True)

    row = pl.BlockSpec((tr, d), lambda i: (i, 0))
    vec = pl.BlockSpec((1, d), lambda i: (0, 0))
    in_specs, args = [row, vec, row], [x, g.reshape(1, d), dh]
    if dres is not None:
        in_specs.append(row)
        args.append(dres)
    dx, dg = pl.pallas_call(
        body, name=name, grid=(s // tr,), in_specs=in_specs, out_specs=[row, vec],
        out_shape=[jax.ShapeDtypeStruct((s, d), F32), jax.ShapeDtypeStruct((1, d), F32)],
        compiler_params=_params("arbitrary"),
    )(*args)
    return dx, dg.reshape(d)


def _gate_up_act(h, w_gu, *, name, tm=2048, tn=256):
    s, d = h.shape
    f = w_gu.shape[1] // 2
    tm, tn = min(tm, s), min(tn, f)
    assert s % tm == 0 and f % tn == 0
    nf = f // tn

    def body(h_ref, wg_ref, wu_ref, gu_ref, a_ref):
        hb = h_ref[...]
        gate = _dot(hb, wg_ref[...], _NN)
        up = _dot(hb, wu_ref[...], _NN)
        gu_ref[0] = gate.astype(gu_ref.dtype)
        gu_ref[1] = up.astype(gu_ref.dtype)
        a_ref[...] = (gate * jax.nn.sigmoid(gate) * up).astype(a_ref.dtype)

    return pl.pallas_call(
        body, name=name, grid=(s // tm, nf),
        in_specs=[pl.BlockSpec((tm, d), lambda i, j: (i, 0)), pl.BlockSpec((d, tn), lambda i, j: (0, j)),
                  pl.BlockSpec((d, tn), lambda i, j: (0, j + nf))],
        out_specs=[pl.BlockSpec((2, tm, tn), lambda i, j: (0, i, j)), pl.BlockSpec((tm, tn), lambda i, j: (i, j))],
        out_shape=[jax.ShapeDtypeStruct((2, s, f), BF16), jax.ShapeDtypeStruct((s, f), BF16)],
        compiler_params=_params("parallel", "parallel"),
    )(h, w_gu, w_gu)


def _down_bwd_act(dy, w_down, gu, alpha, *, name, tm=512):
    s, d = dy.shape
    f = w_down.shape[0]
    tm, tn = min(tm, s), f // 2
    assert s % tm == 0 and tn % LANES == 0

    def body(dy_ref, w_ref, gu_ref, dgu_ref):
        da = _dot(dy_ref[...].astype(BF16), w_ref[...], _NT) * alpha
        gate = gu_ref[0].astype(F32)
        up = gu_ref[1].astype(F32)
        sig = jax.nn.sigmoid(gate)
        silu = gate * sig
        dgu_ref[0] = (da * up * (sig + silu * (1.0 - sig))).astype(dgu_ref.dtype)
        dgu_ref[1] = (da * silu).astype(dgu_ref.dtype)

    slab = pl.BlockSpec((2, tm, tn), lambda i, j: (0, i, j))
    return pl.pallas_call(
        body, name=name, grid=(s // tm, f // tn),
        in_specs=[pl.BlockSpec((tm, d), lambda i, j: (i, 0)), pl.BlockSpec((tn, d), lambda i, j: (j, 0)), slab],
        out_specs=slab, out_shape=jax.ShapeDtypeStruct((2, s, f), BF16),
        compiler_params=_params("parallel", "parallel"),
    )(dy, w_down, gu)


_NT = (((1,), (1,)), ((), ()))
_NN = (((1,), (0,)), ((), ()))
_TN = (((0,), (0,)), ((), ()))
_QK_SCALE = HEAD_DIM ** -0.5


def _dot(a, b, dims):
    return lax.dot_general(a, b, dims, preferred_element_type=F32)


SKIP_BELOW = 107.0
_SMEM = pl.BlockSpec(memory_space=pltpu.SMEM)


def _attn_bounds(z, c, *, qcol_units, kcol_units, aw, t, name):
    s = z.shape[0]
    nq = s // t
    nh = aw // HEAD_DIM

    def body(q_ref, k_ref, o_ref):
        d = lax.broadcasted_iota(jnp.int32, (aw, LANES), 0)
        hh = lax.broadcasted_iota(jnp.int32, (aw, LANES), 1)
        onehot = ((d >= hh * HEAD_DIM) & (d < (hh + 1) * HEAD_DIM)).astype(BF16)
        for r, ref in enumerate((q_ref, k_ref)):
            v = ref[...].astype(F32)
            sq = _dot((v * v).astype(BF16), onehot, _NN)
            o_ref[r:r + 1, :] = jnp.max(sq, axis=0, keepdims=True)
        o_ref[2:, :] = jnp.zeros((HALO - 2, LANES), F32)

    sq = pl.pallas_call(
        body, name=name, grid=(nq,),
        in_specs=[pl.BlockSpec((t, aw), lambda i: (i, qcol_units)), pl.BlockSpec((t, aw), lambda i: (i, kcol_units))],
        out_specs=pl.BlockSpec((None, HALO, LANES), lambda i: (i, 0, 0)),
        out_shape=jax.ShapeDtypeStruct((nq, HALO, LANES), F32),
        compiler_params=_params("parallel"),
    )(z, z)
    norms = jnp.sqrt(sq[:, :2, :nh]) * 1.01
    qn = (norms[:, 0, :] * _QK_SCALE).T.reshape(-1)
    kn = norms[:, 1, :].T.reshape(-1)
    cs = c[0::t, :nh].T.reshape(-1)
    ce = c[t - 1::t, :nh].T.reshape(-1)
    return qn, kn, cs, ce


def _block_active(bounds, head, i, j, nq):
    qn_ref, kn_ref, cs_ref, ce_ref = bounds
    qi = qn_ref[head * nq + i]
    upper = qi * kn_ref[head * nq + j] + (cs_ref[head * nq + i] - ce_ref[head * nq + j])
    lower = -(qi * kn_ref[head * nq + i])
    return upper - lower > -SKIP_BELOW


def _for_active_heads(bounds, pair, i, j, nq, head_step):
    act = [_block_active(bounds, 2 * pair + h, i, j, nq) for h in range(2)]

    def run(heads):
        for h in heads:
            head_step(h)

    pl.when(act[0] & act[1])(functools.partial(run, (0, 1)))
    pl.when(act[0] & jnp.logical_not(act[1]))(functools.partial(run, (0,)))
    pl.when(jnp.logical_not(act[0]) & act[1])(functools.partial(run, (1,)))


def _attn_operands(z, ccol, *, qcol, kcol, vcol, t, name):
    s = z.shape[0]
    npairs = ccol.shape[0]

    def body(q_ref, k_ref, v_ref, c_ref, qa_ref, ka_ref, va_ref):
        lane = lax.broadcasted_iota(jnp.int32, (t, LANES), 1)
        q2 = q_ref[...] * jnp.asarray(_QK_SCALE, BF16)
        k2, v2 = k_ref[...], v_ref[...]
        one, zero = jnp.ones((t, LANES), BF16), jnp.zeros((t, LANES), BF16)
        for h in range(2):
            base = HEAD_DIM * (1 - h)
            mine = (lane < HEAD_DIM) if h == 0 else (lane >= HEAD_DIM)
            c = c_ref[:, h:h + 1]
            hi = c.astype(BF16)
            r1 = c - hi.astype(F32)
            mid = r1.astype(BF16)
            lo = (r1 - mid.astype(F32)).astype(BF16)
            qa, ka = jnp.where(mine, q2, zero), jnp.where(mine, k2, zero)
            for r, word in enumerate((hi, mid, lo)):
                qa = jnp.where(lane == base + r, word, qa)
                qa = jnp.where(lane == base + 3 + r, one, qa)
                ka = jnp.where(lane == base + r, one, ka)
                ka = jnp.where(lane == base + 3 + r, -word, ka)
            qa_ref[h] = qa
            ka_ref[h] = ka
            va_ref[h] = jnp.where((lane >= base) & (lane < base + 3), one, jnp.where(mine, v2, zero))

    out = jax.ShapeDtypeStruct((npairs, 2, s, LANES), BF16)
    blk = pl.BlockSpec((None, 2, t, LANES), lambda p, i: (p, 0, i, 0))
    return pl.pallas_call(
        body, name=name, grid=(npairs, s // t),
        in_specs=[pl.BlockSpec((t, LANES), lambda p, i: (i, qcol + p)), pl.BlockSpec((t, LANES), lambda p, i: (i, kcol + p)),
                  pl.BlockSpec((t, LANES), lambda p, i: (i, vcol + p)), pl.BlockSpec((None, t, 2), lambda p, i: (p, i, 0))],
        out_specs=[blk, blk, blk], out_shape=[out, out, out],
        compiler_params=_params("parallel", "parallel"),
    )(z, z, z, ccol)


def _fattn_fwd(qa, ka, va, bounds, *, t, name):
    npairs, _, s, _ = qa.shape
    nq = s // t
    reps = t // LANES

    def body(qn_ref, kn_ref, cs_ref, ce_ref, q_ref, k_ref, v_ref, o_ref, lse_ref, m_scr, acc_scr):
        pair, i = pl.program_id(0), pl.program_id(1)
        m_scr[...] = jnp.full(m_scr.shape, NEG_BIG, F32)
        acc_scr[...] = jnp.zeros(acc_scr.shape, F32)

        def head_step(h, j, diagonal):
            off = pl.multiple_of(j * t, t)
            sc = _dot(q_ref[h], k_ref[h, pl.ds(off, t), :], _NT)
            if diagonal:
                row = lax.broadcasted_iota(jnp.int32, (t, t), 0)
                col = lax.broadcasted_iota(jnp.int32, (t, t), 1)
                sc = jnp.where(row >= col, sc, NEG_BIG)
            m_old = m_scr[h]
            m_new = jnp.maximum(m_old, jnp.max(sc, axis=1, keepdims=True))
            p = jnp.exp(sc - jnp.tile(m_new, (1, reps)))
            acc_scr[h] = acc_scr[h] * jnp.exp(m_old - m_new) + _dot(p.astype(BF16), v_ref[h, pl.ds(off, t), :], _NN)
            m_scr[h] = m_new

        def loop_body(j, carry):
            _for_active_heads((qn_ref, kn_ref, cs_ref, ce_ref), pair, i, j, nq, lambda h: head_step(h, j, False))
            return carry

        lax.fori_loop(0, i, loop_body, 0)
        for h in range(2):
            head_step(h, i, True)
        is_a = lax.broadcasted_iota(jnp.int32, (t, LANES), 1) < HEAD_DIM
        l = (acc_scr[0][:, HEAD_DIM:HEAD_DIM + 1], acc_scr[1][:, 0:1])
        o_ref[...] = jnp.where(is_a, acc_scr[0] / l[0], acc_scr[1] / l[1])
        lse_ref[:, 0:1] = m_scr[0][:, 0:1] + jnp.log(l[0])
        lse_ref[:, 1:2] = m_scr[1][:, 0:1] + jnp.log(l[1])

    return pl.pallas_call(
        body, name=name, grid=(npairs, nq),
        in_specs=[_SMEM, _SMEM, _SMEM, _SMEM,
                  pl.BlockSpec((None, 2, t, LANES), lambda p, i: (p, 0, i, 0)),
                  pl.BlockSpec((None, 2, s, LANES), lambda p, i: (p, 0, 0, 0)),
                  pl.BlockSpec((None, 2, s, LANES), lambda p, i: (p, 0, 0, 0))],
        out_specs=[pl.BlockSpec((t, LANES), lambda p, i: (i, p)), pl.BlockSpec((None, t, 2), lambda p, i: (p, i, 0))],
        out_shape=[jax.ShapeDtypeStruct((s, npairs * LANES), F32), jax.ShapeDtypeStruct((npairs, s, 2), F32)],
        scratch_shapes=[pltpu.VMEM((2, t, LANES), F32), pltpu.VMEM((2, t, LANES), F32)],
        compiler_params=_params("parallel", "arbitrary"),
    )(*bounds, qa, ka, va)


def _split3(x):
    hi = x.astype(BF16)
    r1 = x - hi.astype(F32)
    mid = r1.astype(BF16)
    return hi, mid, (r1 - mid.astype(F32)).astype(BF16)


def _attn_bwd_operands(qa, ka, dy, y, lse, *, t, name):
    npairs, _, s, _ = qa.shape

    def body(qa_ref, ka_ref, dy_ref, y_ref, lse_ref, qb_ref, kb_ref, dyb_ref):
        lane = lax.broadcasted_iota(jnp.int32, (t, LANES), 1)
        dyf = dy_ref[...]
        prod = dyf * y_ref[...]
        dyh = dyf.astype(BF16)
        one, zero = jnp.ones((t, LANES), BF16), jnp.zeros((t, LANES), BF16)
        for h in range(2):
            base = HEAD_DIM * (1 - h)
            mine = (lane < HEAD_DIM) if h == 0 else (lane >= HEAD_DIM)
            delta = jnp.sum(jnp.where(mine, prod, 0.0), axis=1, keepdims=True)
            qb, kb, dyb = qa_ref[h], ka_ref[h], jnp.where(mine, dyh, zero)
            for r, (lw, dw) in enumerate(zip(_split3(lse_ref[:, h:h + 1]), _split3(delta))):
                qb = jnp.where(lane == base + 6 + r, -lw, qb)
                kb = jnp.where(lane == base + 6 + r, one, kb)
                dyb = jnp.where(lane == base + r, -dw, dyb)
            qb_ref[h] = qb
            kb_ref[h] = kb
            dyb_ref[h] = dyb

    out = jax.ShapeDtypeStruct((npairs, 2, s, LANES), BF16)
    blk = pl.BlockSpec((None, 2, t, LANES), lambda p, i: (p, 0, i, 0))
    tile = pl.BlockSpec((t, LANES), lambda p, i: (i, p))
    return pl.pallas_call(
        body, name=name, grid=(npairs, s // t),
        in_specs=[blk, blk, tile, tile, pl.BlockSpec((None, t, 2), lambda p, i: (p, i, 0))],
        out_specs=[blk, blk, blk], out_shape=[out, out, out],
        compiler_params=_params("parallel", "parallel"),
    )(qa, ka, dy, y, lse)


def _diag_mask(sc, t, queries_on_rows):
    row = lax.broadcasted_iota(jnp.int32, (t, t), 0)
    col = lax.broadcasted_iota(jnp.int32, (t, t), 1)
    return jnp.where((row >= col) if queries_on_rows else (col >= row), sc, NEG_BIG)


def _fattn_dq(qb, kb, va, dyb, bounds, *, t, name):
    npairs, _, s, _ = qb.shape
    nq = s // t

    def body(qn_ref, kn_ref, cs_ref, ce_ref, q_ref, k_ref, v_ref, dy_ref, dq_ref, dcq_ref, acc_scr):
        pair, i = pl.program_id(0), pl.program_id(1)
        acc_scr[...] = jnp.zeros(acc_scr.shape, F32)

        def head_step(h, j, diagonal):
            off = pl.multiple_of(j * t, t)
            kj = k_ref[h, pl.ds(off, t), :]
            sc = _dot(q_ref[h], kj, _NT)
            if diagonal:
                sc = _diag_mask(sc, t, True)
            ds = jnp.exp(sc) * _dot(dy_ref[h], v_ref[h, pl.ds(off, t), :], _NT)
            acc_scr[h] += _dot(ds.astype(BF16), kj, _NN)

        def loop_body(j, carry):
            _for_active_heads((qn_ref, kn_ref, cs_ref, ce_ref), pair, i, j, nq, lambda h: head_step(h, j, False))
            return carry

        lax.fori_loop(0, i, loop_body, 0)
        for h in range(2):
            head_step(h, i, True)
        is_a = lax.broadcasted_iota(jnp.int32, (t, LANES), 1) < HEAD_DIM
        dq_ref[...] = (jnp.where(is_a, acc_scr[0], acc_scr[1]) * _QK_SCALE).astype(dq_ref.dtype)
        dcq_ref[:, 0:1] = acc_scr[0][:, HEAD_DIM:HEAD_DIM + 1]
        dcq_ref[:, 1:2] = acc_scr[1][:, 0:1]

    tile2 = pl.BlockSpec((None, 2, t, LANES), lambda p, i: (p, 0, i, 0))
    whole = pl.BlockSpec((None, 2, s, LANES), lambda p, i: (p, 0, 0, 0))
    return pl.pallas_call(
        body, name=name, grid=(npairs, nq),
        in_specs=[_SMEM, _SMEM, _SMEM, _SMEM, tile2, whole, whole, tile2],
        out_specs=[pl.BlockSpec((t, LANES), lambda p, i: (i, p)), pl.BlockSpec((None, t, 2), lambda p, i: (p, i, 0))],
        out_shape=[jax.ShapeDtypeStruct((s, npairs * LANES), BF16), jax.ShapeDtypeStruct((npairs, s, 2), F32)],
        scratch_shapes=[pltpu.VMEM((2, t, LANES), F32)],
        compiler_params=_params("parallel", "arbitrary"),
    )(*bounds, qb, kb, va, dyb)


def _fattn_dkv(qb, kb, va, dyb, bounds, *, t, name):
    npairs, _, s, _ = qb.shape
    nq = s // t

    def body(qn_ref, kn_ref, cs_ref, ce_ref, k_ref, v_ref, q_ref, dy_ref, dk_ref, dv_ref, dc_ref, dk_scr, dv_scr):
        pair, j = pl.program_id(0), pl.program_id(1)
        dk_scr[...] = jnp.zeros(dk_scr.shape, F32)
        dv_scr[...] = jnp.zeros(dv_scr.shape, F32)

        def head_step(h, i, diagonal):
            off = pl.multiple_of(i * t, t)
            qi = q_ref[h, pl.ds(off, t), :]
            dyi = dy_ref[h, pl.ds(off, t), :]
            st = _dot(k_ref[h], qi, _NT)
            if diagonal:
                st = _diag_mask(st, t, False)
            pt = jnp.exp(st)
            dv_scr[h] += _dot(pt.astype(BF16), dyi, _NN)
            dst = pt * _dot(v_ref[h], dyi, _NT)
            dk_scr[h] += _dot(dst.astype(BF16), qi, _NN)

        def loop_body(i, carry):
            _for_active_heads((qn_ref, kn_ref, cs_ref, ce_ref), pair, i, j, nq, lambda h: head_step(h, i, False))
            return carry

        for h in range(2):
            head_step(h, j, True)
        lax.fori_loop(j + 1, nq, loop_body, 0)
        is_a = lax.broadcasted_iota(jnp.int32, (t, LANES), 1) < HEAD_DIM
        dk_ref[...] = jnp.where(is_a, dk_scr[0], dk_scr[1]).astype(dk_ref.dtype)
        dv_ref[...] = jnp.where(is_a, dv_scr[0], dv_scr[1]).astype(dv_ref.dtype)
        dc_ref[:, 0:1] = -dk_scr[0][:, HEAD_DIM + 3:HEAD_DIM + 4]
        dc_ref[:, 1:2] = -dk_scr[1][:, 3:4]

    tile2 = pl.BlockSpec((None, 2, t, LANES), lambda p, j: (p, 0, j, 0))
    whole = pl.BlockSpec((None, 2, s, LANES), lambda p, j: (p, 0, 0, 0))
    tile = pl.BlockSpec((t, LANES), lambda p, j: (j, p))
    return pl.pallas_call(
        body, name=name, grid=(npairs, nq),
        in_specs=[_SMEM, _SMEM, _SMEM, _SMEM, tile2, tile2, whole, whole],
        out_specs=[tile, tile, pl.BlockSpec((None, t, 2), lambda p, j: (p, j, 0))],
        out_shape=[jax.ShapeDtypeStruct((s, npairs * LANES), BF16), jax.ShapeDtypeStruct((s, npairs * LANES), BF16),
                   jax.ShapeDtypeStruct((npairs, s, 2), F32)],
        scratch_shapes=[pltpu.VMEM((2, t, LANES), F32), pltpu.VMEM((2, t, LANES), F32)],
        compiler_params=_params("parallel", "arbitrary"),
    )(*bounds, kb, va, qb, dyb)


def _log_sigmoid(x):
    return jnp.minimum(x, 0.0) - jnp.log(1.0 + jnp.exp(-jnp.abs(x)))


def _cumsum_fwd(zf, b, *, name, t=512):
    s, w = zf.shape
    t = _row_tile(s, t)

    def body(zf_ref, b_ref, c_ref, carry):
        @pl.when(pl.program_id(0) == 0)
        def _():
            carry[...] = jnp.zeros(carry.shape, F32)

        lf = _log_sigmoid(zf_ref[...] + b_ref[...])
        row = lax.broadcasted_iota(jnp.int32, (t, t), 0)
        col = lax.broadcasted_iota(jnp.int32, (t, t), 1)
        tri = (row >= col).astype(F32)
        c = lax.dot_general(tri, lf, _NN, precision=lax.Precision.HIGHEST, preferred_element_type=F32) + carry[...]
        c_ref[...] = c
        carry[...] = c[t - 1:t, :]

    return pl.pallas_call(
        body, name=name, grid=(s // t,),
        in_specs=[pl.BlockSpec((t, w), lambda i: (i, 0)), pl.BlockSpec((1, w), lambda i: (0, 0))],
        out_specs=pl.BlockSpec((t, w), lambda i: (i, 0)),
        out_shape=jax.ShapeDtypeStruct((s, w), F32),
        scratch_shapes=[pltpu.VMEM((1, w), F32)],
        compiler_params=_params("arbitrary"),
    )(zf, b)


def _cumsum_bwd(dcq, dck, zf, b, *, name, t=512):
    s, w = zf.shape
    t = _row_tile(s, t)
    nb = s // t

    def body(dcq_ref, dck_ref, zf_ref, b_ref, dzf_ref, db_ref, carry):
        @pl.when(pl.program_id(0) == 0)
        def _():
            carry[...] = jnp.zeros(carry.shape, F32)
            db_ref[...] = jnp.zeros(db_ref.shape, F32)

        row = lax.broadcasted_iota(jnp.int32, (t, t), 0)
        col = lax.broadcasted_iota(jnp.int32, (t, t), 1)
        tri = (row <= col).astype(F32)
        dc = dcq_ref[...] + dck_ref[...]
        dlf = lax.dot_general(tri, dc, _NN, precision=lax.Precision.HIGHEST, preferred_element_type=F32) + carry[...]
        carry[...] = dlf[0:1, :]
        dzf = dlf * jax.nn.sigmoid(-(zf_ref[...] + b_ref[...]))
        dzf_ref[...] = dzf
        db_ref[...] += jnp.sum(dzf, axis=0, keepdims=True)

    blk = pl.BlockSpec((t, w), lambda i: (nb - 1 - i, 0))
    vec = pl.BlockSpec((1, w), lambda i: (0, 0))
    return pl.pallas_call(
        body, name=name, grid=(nb,), in_specs=[blk, blk, blk, vec], out_specs=[blk, vec],
        out_shape=[jax.ShapeDtypeStruct((s, w), F32), jax.ShapeDtypeStruct((1, w), F32)],
        scratch_shapes=[pltpu.VMEM((1, w), F32)],
        compiler_params=_params("arbitrary"),
    )(dcq, dck, zf, b)


HALO = 8


def _rms_rows(v):
    return lax.rsqrt(jnp.mean(v * v, axis=-1, keepdims=True) + EPS)


def _mixpost_fwd(z, yatt, wconv, gc, ga, *, name, tr=512):
    s = z.shape[0]
    cw, aw = gc.shape[-1], ga.shape[-1]
    tr = _row_tile(s, tr)

    def body(zb_ref, zc_ref, zv_ref, ya_ref, w_ref, gc_ref, ga_ref, ycat_ref, cv_ref, u_scr):
        @pl.when(pl.program_id(0) == 0)
        def _():
            u_scr[0:HALO, :] = jnp.zeros((HALO, cw), F32)

        u = zc_ref[...].astype(F32) * zv_ref[...].astype(F32)
        u_scr[HALO:HALO + tr, :] = u
        cv = w_ref[0:1, :] * u_scr[HALO - 2:HALO - 2 + tr, :] + w_ref[1:2, :] * u_scr[HALO - 1:HALO - 1 + tr, :] + w_ref[2:3, :] * u
        u_scr[0:HALO, :] = u_scr[tr:tr + HALO, :]
        cv_ref[...] = cv
        yc = zb_ref[...].astype(F32) * cv
        ya = ya_ref[...]
        ycat_ref[:, :cw] = (yc * _rms_rows(yc) * gc_ref[...]).astype(ycat_ref.dtype)
        ycat_ref[:, cw:] = (ya * _rms_rows(ya) * ga_ref[...]).astype(ycat_ref.dtype)

    return pl.pallas_call(
        body, name=name, grid=(s // tr,),
        in_specs=[
            pl.BlockSpec((tr, cw), lambda i: (i, 0)), pl.BlockSpec((tr, cw), lambda i: (i, 1)),
            pl.BlockSpec((tr, cw), lambda i: (i, 2)), pl.BlockSpec((tr, aw), lambda i: (i, 0)),
            pl.BlockSpec((HALO, cw), lambda i: (0, 0)), pl.BlockSpec((1, cw), lambda i: (0, 0)),
            pl.BlockSpec((1, aw), lambda i: (0, 0)),
        ],
        out_specs=[pl.BlockSpec((tr, cw + aw), lambda i: (i, 0)), pl.BlockSpec((tr, cw), lambda i: (i, 0))],
        out_shape=[jax.ShapeDtypeStruct((s, cw + aw), BF16), jax.ShapeDtypeStruct((s, cw), F32)],
        scratch_shapes=[pltpu.VMEM((tr + HALO, cw), F32)],
        compiler_params=_params("arbitrary"),
    )(z, z, z, yatt, wconv, gc.reshape(1, cw), ga.reshape(1, aw))


def _mixpost_bwd(dycat, z, yatt, cv, wconv, gc, ga, *, name, tr=512):
    s = z.shape[0]
    cw, aw = gc.shape[-1], ga.shape[-1]
    tr = _row_tile(s, tr)
    nb = s // tr

    def body(dy_ref, zb_ref, zc_ref, zv_ref, ya_ref, cv_ref, w_ref, gc_ref, ga_ref,
             dz_ref, dya_ref, dw_ref, dgc_ref, dga_ref, d_scr):
        @pl.when(pl.program_id(0) == 0)
        def _():
            d_scr[tr:tr + HALO, :] = jnp.zeros((HALO, cw), F32)
            dw_ref[...] = jnp.zeros(dw_ref.shape, F32)
            dgc_ref[...] = jnp.zeros(dgc_ref.shape, F32)
            dga_ref[...] = jnp.zeros(dga_ref.shape, F32)

        zb, zc, zv = zb_ref[...].astype(F32), zc_ref[...].astype(F32), zv_ref[...].astype(F32)
        cvv = cv_ref[...]

        def norm_bwd(v, dn, g):
            r = _rms_rows(v)
            vh = v * r
            dvh = dn * g
            return r * (dvh - vh * jnp.mean(dvh * vh, axis=-1, keepdims=True)), jnp.sum(dn * vh, axis=0, keepdims=True)

        dyc, dgc = norm_bwd(zb * cvv, dy_ref[:, :cw], gc_ref[...])
        dya, dga = norm_bwd(ya_ref[...], dy_ref[:, cw:], ga_ref[...])
        dgc_ref[...] += dgc
        dga_ref[...] += dga
        dya_ref[...] = dya
        dcv = dyc * zb
        d_scr[0:tr, :] = dcv
        d1 = d_scr[1:tr + 1, :]
        d2 = d_scr[2:tr + 2, :]
        du = w_ref[2:3, :] * dcv + w_ref[1:2, :] * d1 + w_ref[0:1, :] * d2
        u = zc * zv
        dw_ref[0:1, :] += jnp.sum(u * d2, axis=0, keepdims=True)
        dw_ref[1:2, :] += jnp.sum(u * d1, axis=0, keepdims=True)
        dw_ref[2:3, :] += jnp.sum(u * dcv, axis=0, keepdims=True)
        d_scr[tr:tr + HALO, :] = d_scr[0:HALO, :]
        dz_ref[:, :cw] = (dyc * cvv).astype(dz_ref.dtype)
        dz_ref[:, cw:2 * cw] = (du * zv).astype(dz_ref.dtype)
        dz_ref[:, 2 * cw:] = (du * zc).astype(dz_ref.dtype)

    def rows(width, colblk=0):
        return pl.BlockSpec((tr, width), lambda i: (nb - 1 - i, colblk))

    def fixed(r, width):
        return pl.BlockSpec((r, width), lambda i: (0, 0))

    return pl.pallas_call(
        body, name=name, grid=(nb,),
        in_specs=[rows(cw + aw), rows(cw, 0), rows(cw, 1), rows(cw, 2), rows(aw), rows(cw),
                  fixed(HALO, cw), fixed(1, cw), fixed(1, aw)],
        out_specs=[rows(3 * cw), rows(aw), fixed(HALO, cw), fixed(1, cw), fixed(1, aw)],
        out_shape=[jax.ShapeDtypeStruct((s, 3 * cw), BF16), jax.ShapeDtypeStruct((s, aw), F32),
                   jax.ShapeDtypeStruct((HALO, cw), F32), jax.ShapeDtypeStruct((1, cw), F32),
                   jax.ShapeDtypeStruct((1, aw), F32)],
        scratch_shapes=[pltpu.VMEM((tr + HALO, cw), F32)],
        compiler_params=_params("arbitrary"),
    )(dycat, z, z, z, yatt, cv, wconv, gc.reshape(1, cw), ga.reshape(1, aw))


def _xattn_fwd(q, kv, *, name, tq=1024):
    s, d = q.shape
    m = kv.shape[0]
    dh = d // N_XHEADS
    scale = dh ** -0.5
    tq = _row_tile(s, tq)

    def body(q_ref, kv_ref, o_ref):
        for h in range(N_XHEADS):
            lo, hi = h * dh, (h + 1) * dh
            sc = _dot(q_ref[:, lo:hi], kv_ref[:, lo:hi], _NT) * scale
            p = jnp.exp(sc - jnp.max(sc, axis=1, keepdims=True))
            o = _dot(p.astype(BF16), kv_ref[:, d + lo:d + hi], _NN) / jnp.sum(p, axis=1, keepdims=True)
            o_ref[:, lo:hi] = o.astype(o_ref.dtype)

    return pl.pallas_call(
        body, name=name, grid=(s // tq,),
        in_specs=[pl.BlockSpec((tq, d), lambda i: (i, 0)), pl.BlockSpec((m, 2 * d), lambda i: (0, 0))],
        out_specs=pl.BlockSpec((tq, d), lambda i: (i, 0)),
        out_shape=jax.ShapeDtypeStruct((s, d), BF16),
        compiler_params=_params("parallel"),
    )(q, kv)


def _xattn_bwd(q, kv, do, *, name, tq=1024):
    s, d = q.shape
    m = kv.shape[0]
    dh = d // N_XHEADS
    scale = dh ** -0.5
    tq = _row_tile(s, tq)

    def body(q_ref, kv_ref, do_ref, dq_ref, dkv_ref):
        @pl.when(pl.program_id(0) == 0)
        def _():
            dkv_ref[...] = jnp.zeros(dkv_ref.shape, F32)

        for h in range(N_XHEADS):
            lo, hi = h * dh, (h + 1) * dh
            qh, kh, vh, doh = q_ref[:, lo:hi], kv_ref[:, lo:hi], kv_ref[:, d + lo:d + hi], do_ref[:, lo:hi]
            sc = _dot(qh, kh, _NT) * scale
            e = jnp.exp(sc - jnp.max(sc, axis=1, keepdims=True))
            p = e / jnp.sum(e, axis=1, keepdims=True)
            dp = _dot(doh, vh, _NT)
            ds = p * (dp - jnp.sum(dp * p, axis=1, keepdims=True))
            dsb = ds.astype(BF16)
            dq_ref[:, lo:hi] = (_dot(dsb, kh, _NN) * scale).astype(dq_ref.dtype)
            dkv_ref[:, lo:hi] += _dot(dsb, qh, _TN) * scale
            dkv_ref[:, d + lo:d + hi] += _dot(p.astype(BF16), doh, _TN)

    return pl.pallas_call(
        body, name=name, grid=(s // tq,),
        in_specs=[pl.BlockSpec((tq, d), lambda i: (i, 0)), pl.BlockSpec((m, 2 * d), lambda i: (0, 0)),
                  pl.BlockSpec((tq, d), lambda i: (i, 0))],
        out_specs=[pl.BlockSpec((tq, d), lambda i: (i, 0)), pl.BlockSpec((m, 2 * d), lambda i: (0, 0))],
        out_shape=[jax.ShapeDtypeStruct((s, d), BF16), jax.ShapeDtypeStruct((m, 2 * d), F32)],
        compiler_params=_params("arbitrary"),
    )(q, kv, do)


def _final_loss(x, g, target, *, name, tr=512):
    s, d = x.shape
    tr = _row_tile(s, tr)

    def body(x_ref, g_ref, t_ref, dx_ref, dg_ref, sq_ref):
        @pl.when(pl.program_id(0) == 0)
        def _():
            dg_ref[...] = jnp.zeros(dg_ref.shape, F32)
            sq_ref[...] = jnp.zeros(sq_ref.shape, F32)

        xf = x_ref[...]
        r = _rms_rows(xf)
        xhat = xf * r
        err = xhat * g_ref[...] - t_ref[...]
        sq_ref[...] += jnp.sum(err * err, axis=0, keepdims=True)
        dy = err * (1.0 / d)
        dg_ref[...] += jnp.sum(dy * xhat, axis=0, keepdims=True)
        dxhat = dy * g_ref[...]
        dx_ref[...] = r * (dxhat - xhat * jnp.mean(dxhat * xhat, axis=-1, keepdims=True))

    row = pl.BlockSpec((tr, d), lambda i: (i, 0))
    vec = pl.BlockSpec((1, d), lambda i: (0, 0))
    return pl.pallas_call(
        body, name=name, grid=(s // tr,), in_specs=[row, vec, row], out_specs=[row, vec, vec],
        out_shape=[jax.ShapeDtypeStruct((s, d), F32), jax.ShapeDtypeStruct((1, d), F32), jax.ShapeDtypeStruct((1, d), F32)],
        compiler_params=_params("arbitrary"),
    )(x, g.reshape(1, d), target)


def _adamw(w, g, m, v, *, name, tr=512):
    shape = w.shape
    cols = shape[-1]
    rows = w.size // cols
    tr = tr if rows % tr == 0 else rows

    def body(w_ref, g_ref, m_ref, v_ref, d_ref, nm_ref, nv_ref):
        gf = g_ref[...]
        nm = ADAM_B1 * m_ref[...] + (1.0 - ADAM_B1) * gf
        nv = ADAM_B2 * v_ref[...] + (1.0 - ADAM_B2) * (gf * gf)
        m_hat = nm / (1.0 - ADAM_B1 ** ADAM_STEP)
        v_hat = nv / (1.0 - ADAM_B2 ** ADAM_STEP)
        d_ref[...] = -ADAM_LR * (m_hat / (jnp.sqrt(v_hat) + ADAM_EPS) + ADAM_WD * w_ref[...])
        nm_ref[...] = nm
        nv_ref[...] = nv

    blk = pl.BlockSpec((tr, cols), lambda i: (i, 0))
    out = jax.ShapeDtypeStruct((rows, cols), F32)
    outs = pl.pallas_call(
        body, name=name, grid=(rows // tr,), in_specs=[blk] * 4, out_specs=[blk] * 3, out_shape=[out] * 3,
        compiler_params=_params("parallel"),
    )(*[t.reshape(rows, cols) for t in (w, g, m, v)])
    return tuple(o.reshape(shape) for o in outs)


_HBM = pl.BlockSpec(memory_space=pl.ANY)
_MESH_ID = pl.DeviceIdType.MESH


def _place():
    x, y, c = (lax.axis_index(a) for a in MESH_AXES)
    return x, y, c, [(1 - x, y), (x, 1 - y), (1 - x, 1 - y)]


def _remote(src, dst, send_sems, recv_sems, k, to):
    return pltpu.make_async_remote_copy(src_ref=src, dst_ref=dst, send_sem=send_sems.at[k], recv_sem=recv_sems.at[k],
                                        device_id=to, device_id_type=_MESH_ID)


def _comm_call(body, arrays, out_shapes, n_remote, name):
    return pl.pallas_call(
        body, name=name, in_specs=[_HBM] * len(arrays), out_specs=[_HBM] * len(out_shapes), out_shape=out_shapes,
        scratch_shapes=[pltpu.SemaphoreType.DMA((n_remote,)), pltpu.SemaphoreType.DMA((n_remote,)),
                        pltpu.SemaphoreType.DMA((len(arrays),))],
    )(*arrays)


def _allgather_weights(halves, *, name):
    n = len(halves)

    def body(*refs):
        w, out = refs[:n], refs[n:2 * n]
        send_sems, recv_sems, local_sems = refs[2 * n:]
        x, y, c, chips = _place()
        me = 2 * x + y
        sibling = (x, y, 1 - c)
        slots = [2 * px + py for px, py in chips]

        def copy(t, k, slot, half, to, src=None):
            dst = out[t].at[slot, half]
            return _remote(dst if src is None else src, dst, send_sems, recv_sems, 7 * t + k, to)

        local = [pltpu.make_async_copy(w[t].at[c], out[t].at[me, c], local_sems.at[t]) for t in range(n)]
        first = []
        for t in range(n):
            local[t].start()
            first.append(copy(t, 0, me, c, sibling, src=w[t].at[c]))
            first += [copy(t, 1 + j, me, c, (px, py, c), src=w[t].at[c]) for j, (px, py) in enumerate(chips)]
        for cp in first:
            cp.start()
        passed = []
        for j, (px, py) in enumerate(chips):
            for t in range(n):
                copy(t, 1 + j, slots[j], c, (px, py, c)).wait_recv()
                passed.append(copy(t, 4 + j, slots[j], c, sibling))
                passed[-1].start()
        for t in range(n):
            copy(t, 0, me, 1 - c, sibling).wait_recv()
            for j in range(3):
                copy(t, 4 + j, slots[j], 1 - c, sibling).wait_recv()
        for cp in first + passed:
            cp.wait_send()
        for cp in local:
            cp.wait()

    shapes = [jax.ShapeDtypeStruct((N_CHIPS,) + h.shape, h.dtype) for h in halves]
    return _comm_call(body, halves, shapes, 7 * n, name)


def _swap_halves(gs, *, name):
    n = len(gs)

    def body(*refs):
        g, out = refs[:n], refs[n:2 * n]
        send_sems, recv_sems, _ = refs[2 * n:]
        x, y, c, _ = _place()
        copies = [_remote(g[t].at[1 - c], out[t], send_sems, recv_sems, t, (x, y, 1 - c)) for t in range(n)]
        for cp in copies:
            cp.start()
        for cp in copies:
            cp.wait()

    return _comm_call(body, gs, [jax.ShapeDtypeStruct(g.shape[1:], g.dtype) for g in gs], n, name)


def _scatter_chips(ps, *, name):
    n = len(ps)

    def body(*refs):
        p, out = refs[:n], refs[n:2 * n]
        send_sems, recv_sems, local_sems = refs[2 * n:]
        x, y, c, chips = _place()
        me = 2 * x + y
        local = [pltpu.make_async_copy(p[t].at[me], out[t].at[me], local_sems.at[t]) for t in range(n)]
        sends = [_remote(p[t].at[2 * px + py], out[t].at[me], send_sems, recv_sems, 3 * t + j, (px, py, c))
                 for t in range(n) for j, (px, py) in enumerate(chips)]
        for cp in local + sends:
            cp.start()
        for t in range(n):
            for j, (px, py) in enumerate(chips):
                slot = out[t].at[2 * px + py]
                _remote(slot, slot, send_sems, recv_sems, 3 * t + j, (px, py, c)).wait_recv()
        for cp in sends:
            cp.wait_send()
        for cp in local:
            cp.wait()

    return _comm_call(body, ps, [jax.ShapeDtypeStruct(p.shape, p.dtype) for p in ps], 3 * n, name)


def _join_halves(rs, *, name):
    n = len(rs)

    def body(*refs):
        r, out = refs[:n], refs[n:2 * n]
        send_sems, recv_sems, _ = refs[2 * n:]
        x, y, c, _ = _place()
        sends = [_remote(r[t].at[c], out[t].at[c], send_sems, recv_sems, t, (x, y, 1 - c)) for t in range(n)]
        for cp in sends:
            cp.start()
        for t in range(n):
            slot = out[t].at[1 - c]
            _remote(slot, slot, send_sems, recv_sems, t, (x, y, 1 - c)).wait_recv()
        for cp in sends:
            cp.wait_send()

    return pl.pallas_call(
        body, name=name, in_specs=[_HBM] * n, out_specs=[_HBM] * n,
        out_shape=[jax.ShapeDtypeStruct(r.shape, r.dtype) for r in rs],
        input_output_aliases={t: t for t in range(n)},
        scratch_shapes=[pltpu.SemaphoreType.DMA((n,)), pltpu.SemaphoreType.DMA((n,)), pltpu.SemaphoreType.DMA((n,))],
    )(*rs)


def _pick_rows(rows, cap=512):
    for t in range(min(rows, cap), 0, -1):
        if rows % t == 0 and t % 16 == 0:
            return t
    return rows


def _add_halves(g, recv, c, out_dtype, *, name):
    cols = g.shape[-1]
    rows = recv.size // (N_CHIPS * cols)
    tr = _pick_rows(rows)

    def body(c_ref, g_ref, r_ref, o_ref):
        o_ref[...] = (g_ref[...] + r_ref[...]).astype(o_ref.dtype)

    blk = pl.BlockSpec((None, tr, cols), lambda b, i, c_ref: (b, i, 0))
    out = pl.pallas_call(
        body, name=name,
        grid_spec=pltpu.PrefetchScalarGridSpec(
            num_scalar_prefetch=1, grid=(N_CHIPS, rows // tr),
            in_specs=[pl.BlockSpec((None, None, tr, cols), lambda b, i, c_ref: (c_ref[0], b, i, 0)), blk],
            out_specs=blk),
        out_shape=jax.ShapeDtypeStruct((N_CHIPS, rows, cols), out_dtype),
        compiler_params=_params("parallel", "parallel"),
    )(c.reshape(1).astype(jnp.int32), g.reshape(2, N_CHIPS, rows, cols), recv.reshape(N_CHIPS, rows, cols))
    return out.reshape(recv.shape)


def _sum_slots(p, c, *, name):
    cols = p.shape[-1]
    rows = p.size // (N_CHIPS * cols)
    tr = _pick_rows(rows)

    def body(c_ref, p_ref, o_ref):
        acc = p_ref[0].astype(F32)
        for q in range(1, N_CHIPS):
            acc = acc + p_ref[q].astype(F32)
        o_ref[...] = acc

    out = pl.pallas_call(
        body, name=name,
        grid_spec=pltpu.PrefetchScalarGridSpec(
            num_scalar_prefetch=1, grid=(rows // tr,),
            in_specs=[pl.BlockSpec((N_CHIPS, tr, cols), lambda i, c_ref: (0, i, 0))],
            out_specs=pl.BlockSpec((None, tr, cols), lambda i, c_ref: (c_ref[0], i, 0))),
        out_shape=jax.ShapeDtypeStruct((2, rows, cols), F32),
        compiler_params=_params("parallel"),
    )(c.reshape(1).astype(jnp.int32), p.reshape(N_CHIPS, rows, cols))
    return out.reshape((2,) + p.shape[1:])


GROUPS = (("gu", ("w_ffn1_gu", "w_ffn2_gu"), "col"), ("down", ("w_ffn1_down", "w_ffn2_down"), "row"),
          ("square", ("w_mix_out", "w_xq", "w_xo"), "row"), ("mix_in", ("w_mix_in",), "col"), ("xkv", ("w_xkv",), "col"))
REPLICATED = ("g_ffn1", "g_mix", "b_f", "g_conv_out", "g_att_out", "g_xattn", "g_mem", "g_ffn2", "g_final")
WEIGHTS = ("g_ffn1", "w_ffn1_gu", "w_ffn1_down", "g_mix", "w_mix_in", "w_conv", "b_f", "g_conv_out", "g_att_out",
           "w_mix_out", "g_xattn", "g_mem", "w_xq", "w_xkv", "w_xo", "g_ffn2", "w_ffn2_gu", "w_ffn2_down", "g_final")
SMALL_COLS = 1024
SMALL_ROW_UNIT = 16


def _gather_weights(shards):
    packs = []
    for _, members, _ in GROUPS:
        hs = []
        for name in members:
            l, a, b = shards[name].shape
            hs.append(shards[name].astype(BF16).reshape(l, 2, a // 2, b).transpose(1, 0, 2, 3))
        packs.append(jnp.stack(hs, axis=1))
    wc = shards["w_conv"]
    packs.append(jnp.stack([wc, wc]))
    gathered = _allgather_weights(packs, name="allgather_weights")
    full = {}
    for (_, members, kind), got in zip(GROUPS, gathered):
        _, _, g, l, a2, b = got.shape
        if kind == "col":
            whole = got.transpose(2, 3, 1, 4, 0, 5).reshape(g, l, 2 * a2, N_CHIPS * b)
        else:
            whole = got.transpose(2, 3, 0, 1, 4, 5).reshape(g, l, N_CHIPS * 2 * a2, b)
        for gi, name in enumerate(members):
            full[name] = whole[gi]
    l, k, b = wc.shape
    full["w_conv"] = gathered[-1][:, 0].transpose(1, 2, 0, 3).reshape(l, k, N_CHIPS * b)
    return full


def _small_rows(v):
    flat = v.reshape(-1)
    return jnp.pad(flat, (0, -flat.shape[0] % SMALL_COLS)).reshape(-1, SMALL_COLS)


def _reduce_gradients(grads, c):
    packs = []
    for _, members, kind in GROUPS:
        cut = []
        for name in members:
            l, a, b = grads[name].shape
            if kind == "col":
                cut.append(grads[name].reshape(l, 2, a // 2, N_CHIPS, b // N_CHIPS).transpose(1, 3, 0, 2, 4))
            else:
                cut.append(grads[name].reshape(l, N_CHIPS, 2, a // (2 * N_CHIPS), b).transpose(2, 1, 0, 3, 4))
        packs.append(jnp.stack(cut, axis=2))
    rep = jnp.concatenate([_small_rows(grads[n]) for n in REPLICATED])
    l, k, b = grads["w_conv"].shape
    conv = grads["w_conv"].reshape(l, k, N_CHIPS, b // N_CHIPS).transpose(2, 0, 1, 3)
    conv_rows = [_small_rows(conv[q]) for q in range(N_CHIPS)]
    n_rows = rep.shape[0] + conv_rows[0].shape[0]
    fill = jnp.zeros((-n_rows % SMALL_ROW_UNIT, SMALL_COLS), F32)
    small = jnp.stack([jnp.concatenate([rep, conv_rows[q], fill]) for q in range(N_CHIPS)])
    half_rows = small.shape[1] // 2
    packs.append(small.reshape(N_CHIPS, 2, half_rows, SMALL_COLS).transpose(1, 0, 2, 3))

    from_sibling = _swap_halves(packs, name="grad_swap_halves")
    wire = [BF16] * len(GROUPS) + [F32]
    chip_sums = [_add_halves(g, r, c, dt, name=f"grad_add_halves_{i}")
                 for i, (g, r, dt) in enumerate(zip(packs, from_sibling, wire))]
    from_chips = _scatter_chips(chip_sums, name="grad_scatter_chips")
    halves = [_sum_slots(p, c, name=f"grad_sum_chips_{i}") for i, p in enumerate(from_chips)]
    reduced = _join_halves(halves, name="grad_join_halves")

    out = {}
    for (_, members, _), r in zip(GROUPS, reduced):
        _, g, l, a2, b = r.shape
        whole = r.transpose(1, 2, 0, 3, 4).reshape(g, l, 2 * a2, b)
        for gi, name in enumerate(members):
            out[name] = whole[gi]
    rows = reduced[-1].reshape(-1, SMALL_COLS)
    off = 0
    for name in REPLICATED:
        n = -(-grads[name].size // SMALL_COLS)
        out[name] = rows[off:off + n].reshape(-1)[:grads[name].size].reshape(grads[name].shape)
        off += n
    l, k, b = grads["w_conv"].shape
    out["w_conv"] = rows[off:off + conv_rows[0].shape[0]].reshape(-1)[:l * k * b // N_CHIPS].reshape(l, k, b // N_CHIPS)
    return out


def _residual_out(a, w_out, x, g_next, name, alpha=1.0):
    kw = dict(tm=1024, tn=1024, tk=w_out.shape[0], alpha=alpha, residual=x, name=name)
    if g_next is None:
        return _matmul(a, w_out, "nn", F32, **kw), None
    return _matmul(a, w_out, "nn", F32, norm_gain=g_next, **kw)


def _ffn_fwd(x, h, w_gu, w_down, g_next, tag):
    gu, a = _gate_up_act(h, w_gu, name=f"{tag}_gu_act")
    y, h_next = _residual_out(a, w_down, x, g_next, f"{tag}_down", alpha=0.5)
    return y, h_next, (x, h, gu, a)


def _ffn_bwd(dy, saved, g, w_gu, w_down, tag):
    x, h, gu, a = saved
    f = w_down.shape[0]
    dgu = _down_bwd_act(dy, w_down, gu, 0.5, name=f"{tag}_da_dact")
    dw_down = _matmul(a, dy, "tn", F32, tm=f // 2, tn=1024, tk=1024, alpha=0.5, name=f"{tag}_dwdown")
    dw_gu = jnp.concatenate([_matmul(h, dgu, "tn", F32, tm=1024, tn=f // 2, tk=2048, b_slab=half, name=f"{tag}_dwgu{half}")
                             for half in range(2)], axis=1)
    dx, dg = _matmul(dgu, w_gu, "nt", F32, tm=1024, tn=1024, tk=f // 2, residual=dy, dnorm=(x, g), a_slabs=True,
                     name=f"{tag}_dh_dnorm")
    return dx, dg[0], dw_gu, dw_down


def _mix_fwd(x, h, w, l, g_next, tag):
    s, d = x.shape
    gc, ga = w["g_conv_out"][l], w["g_att_out"][l]
    cw, aw = gc.shape[0], ga.shape[0]
    nh = aw // HEAD_DIM
    zw = 3 * cw + 3 * aw
    t = 512 if s >= 2048 else s // 4
    cols = dict(qcol=3 * cw // LANES, kcol=(3 * cw + aw) // LANES, vcol=(3 * cw + 2 * aw) // LANES)
    w_in = w["w_mix_in"][l]
    w_main = w_in[:, :zw]
    w_f = jnp.pad(w_in[:, zw:], ((0, 0), (0, LANES - nh)))
    z = _matmul(h, w_main, "nn", BF16, tm=2048, tn=512, tk=d, name=f"{tag}_in")
    zf = _matmul(h, w_f, "nn", F32, tm=2048, tn=LANES, tk=d, name=f"{tag}_in_f")
    b = jnp.pad(w["b_f"][l], (0, LANES - nh)).reshape(1, LANES)
    c = _cumsum_fwd(zf, b, name=f"{tag}_cumsum")
    ccol = c[:, :nh].reshape(s, nh // 2, 2).transpose(1, 0, 2)
    assert (3 * cw) % aw == 0
    bounds = _attn_bounds(z, c, qcol_units=3 * cw // aw, kcol_units=3 * cw // aw + 1, aw=aw, t=t, name=f"{tag}_bounds")
    qa, ka, va = _attn_operands(z, ccol, t=t, name=f"{tag}_attn_operands", **cols)
    yatt, lse = _fattn_fwd(qa, ka, va, bounds, t=t, name=f"{tag}_attn")
    wc = jnp.pad(w["w_conv"][l], ((0, HALO - CONV_K), (0, 0)))
    ycat, cv = _mixpost_fwd(z, yatt, wc, gc, ga, name=f"{tag}_post")
    y, h_next = _residual_out(ycat, w["w_mix_out"][l], x, g_next, f"{tag}_out")
    return y, h_next, (x, h, w_main, w_f, z, zf, b, qa, ka, va, bounds, yatt, lse, wc, ycat, cv, t)


def _mix_bwd(dy, saved, w, l, tag):
    x, h, w_main, w_f, z, zf, b, qa, ka, va, bounds, yatt, lse, wc, ycat, cv, t = saved
    s, d = x.shape
    gc, ga = w["g_conv_out"][l], w["g_att_out"][l]
    nh = ga.shape[0] // HEAD_DIM
    zw = w_main.shape[1]
    dycat = _matmul(dy, w["w_mix_out"][l], "nt", F32, tm=1024, tn=1024, tk=d, name=f"{tag}_dycat")
    dw_out = _matmul(ycat, dy, "tn", F32, tm=1024, tn=1024, tk=1024, name=f"{tag}_dwout")
    dz_conv, dyatt, dwc, dgc, dga = _mixpost_bwd(dycat, z, yatt, cv, wc, gc, ga, name=f"{tag}_dpost")
    qb, kb, dyb = _attn_bwd_operands(qa, ka, dyatt, yatt, lse, t=t, name=f"{tag}_attn_bwd_operands")
    dq, dcq = _fattn_dq(qb, kb, va, dyb, bounds, t=t, name=f"{tag}_attn_dq")
    dk, dv, dck = _fattn_dkv(qb, kb, va, dyb, bounds, t=t, name=f"{tag}_attn_dkv")
    def heads_on_lanes(v):
        return jnp.pad(v.transpose(1, 0, 2).reshape(s, nh), ((0, 0), (0, LANES - nh)))

    dzf, db = _cumsum_bwd(heads_on_lanes(dcq), heads_on_lanes(dck), zf, b, name=f"{tag}_dcumsum")
    dz = jnp.concatenate([dz_conv, dq, dk, dv], axis=1)
    dw_main = _matmul(h, dz, "tn", F32, tm=1024, tn=512, tk=2048, name=f"{tag}_dwin")
    dw_f = _matmul(h, dzf, "tn", F32, tm=1024, tn=LANES, tk=2048, name=f"{tag}_dwin_f")
    dh_f = _matmul(dzf, w_f, "nt", F32, tm=1024, tn=1024, tk=LANES, name=f"{tag}_dh_f")
    dx, dg = _matmul(dz, w_main, "nt", F32, tm=512, tn=1024, tk=zw // 2, addend=dh_f, residual=dy,
                     dnorm=(x, w["g_mix"][l]), name=f"{tag}_dh_dnorm")
    grads = dict(g_mix=dg[0], w_mix_in=jnp.concatenate([dw_main, dw_f[:, :nh]], axis=1), w_conv=dwc[:CONV_K], b_f=db[0, :nh],
                 g_conv_out=dgc[0], g_att_out=dga[0], w_mix_out=dw_out)
    return dx, grads


def _xattn_block_fwd(x, h, mem, w, l, g_next, tag):
    d = x.shape[1]
    mn = _rmsnorm_fwd(mem, w["g_mem"][l], name=f"{tag}_mem_norm")
    q = _matmul(h, w["w_xq"][l], "nn", BF16, tm=1024, tn=1024, tk=d, name=f"{tag}_q")
    kv = _matmul(mn, w["w_xkv"][l], "nn", BF16, tm=1024, tn=1024, tk=d, name=f"{tag}_kv")
    o = _xattn_fwd(q, kv, name=f"{tag}_core")
    y, h_next = _residual_out(o, w["w_xo"][l], x, g_next, f"{tag}_o")
    return y, h_next, (x, h, mn, q, kv, o)


def _xattn_block_bwd(dy, saved, mem, w, l, tag):
    x, h, mn, q, kv, o = saved
    d = x.shape[1]
    do = _matmul(dy, w["w_xo"][l], "nt", BF16, tm=1024, tn=1024, tk=d, name=f"{tag}_do")
    dw_xo = _matmul(o, dy, "tn", F32, tm=1024, tn=1024, tk=1024, name=f"{tag}_dwo")
    dq, dkv = _xattn_bwd(q, kv, do, name=f"{tag}_dcore")
    dw_xq = _matmul(h, dq, "tn", F32, tm=1024, tn=1024, tk=2048, name=f"{tag}_dwq")
    dx, dg = _matmul(dq, w["w_xq"][l], "nt", F32, tm=1024, tn=1024, tk=d, residual=dy, dnorm=(x, w["g_xattn"][l]),
                     name=f"{tag}_dh_dnorm")
    dw_xkv = _matmul(mn, dkv, "tn", F32, tm=1024, tn=1024, tk=1024, name=f"{tag}_dwkv")
    dmn = _matmul(dkv, w["w_xkv"][l], "nt", F32, tm=1024, tn=1024, tk=1024, name=f"{tag}_dmem")
    _, dg_mem = _rmsnorm_bwd(mem, w["g_mem"][l], dmn, None, name=f"{tag}_dmem_norm")
    return dx, dict(g_xattn=dg[0], g_mem=dg_mem, w_xq=dw_xq, w_xkv=dw_xkv, w_xo=dw_xo)


def kernel(x, mem, g_ffn1, w_ffn1_gu, w_ffn1_down, g_mix, w_mix_in, w_conv, b_f, g_conv_out, g_att_out, w_mix_out, g_xattn, g_mem, w_xq, w_xkv, w_xo, g_ffn2, w_ffn2_gu, w_ffn2_down, g_final, loss_target, m_g_ffn1, m_w_ffn1_gu, m_w_ffn1_down, m_g_mix, m_w_mix_in, m_w_conv, m_b_f, m_g_conv_out, m_g_att_out, m_w_mix_out, m_g_xattn, m_g_mem, m_w_xq, m_w_xkv, m_w_xo, m_g_ffn2, m_w_ffn2_gu, m_w_ffn2_down, m_g_final, v_g_ffn1, v_w_ffn1_gu, v_w_ffn1_down, v_g_mix, v_w_mix_in, v_w_conv, v_b_f, v_g_conv_out, v_g_att_out, v_w_mix_out, v_g_xattn, v_g_mem, v_w_xq, v_w_xkv, v_w_xo, v_g_ffn2, v_w_ffn2_gu, v_w_ffn2_down, v_g_final):
    local = dict(zip(WEIGHTS, (g_ffn1, w_ffn1_gu, w_ffn1_down, g_mix, w_mix_in, w_conv, b_f, g_conv_out, g_att_out, w_mix_out,
                               g_xattn, g_mem, w_xq, w_xkv, w_xo, g_ffn2, w_ffn2_gu, w_ffn2_down, g_final)))
    mom1 = dict(zip(WEIGHTS, (m_g_ffn1, m_w_ffn1_gu, m_w_ffn1_down, m_g_mix, m_w_mix_in, m_w_conv, m_b_f, m_g_conv_out,
                              m_g_att_out, m_w_mix_out, m_g_xattn, m_g_mem, m_w_xq, m_w_xkv, m_w_xo, m_g_ffn2, m_w_ffn2_gu,
                              m_w_ffn2_down, m_g_final)))
    mom2 = dict(zip(WEIGHTS, (v_g_ffn1, v_w_ffn1_gu, v_w_ffn1_down, v_g_mix, v_w_mix_in, v_w_conv, v_b_f, v_g_conv_out,
                              v_g_att_out, v_w_mix_out, v_g_xattn, v_g_mem, v_w_xq, v_w_xkv, v_w_xo, v_g_ffn2, v_w_ffn2_gu,
                              v_w_ffn2_down, v_g_final)))
    depth = g_ffn1.shape[0]
    s, d = x.shape[1], x.shape[2]
    w = dict(local)
    w.update(_gather_weights(local))

    xs = x.reshape(s, d)
    mems = mem.reshape(mem.shape[1], d)
    saved = []
    h = _rmsnorm_fwd(xs, w["g_ffn1"][0], name="l0_ffn1_norm")
    for l in range(depth):
        g_after = w["g_ffn1"][l + 1] if l + 1 < depth else None
        xs, h, s1 = _ffn_fwd(xs, h, w["w_ffn1_gu"][l], w["w_ffn1_down"][l], w["g_mix"][l], f"l{l}_ffn1")
        xs, h, s2 = _mix_fwd(xs, h, w, l, w["g_xattn"][l], f"l{l}_mix")
        xs, h, s3 = _xattn_block_fwd(xs, h, mems, w, l, w["g_ffn2"][l], f"l{l}_xattn")
        xs, h, s4 = _ffn_fwd(xs, h, w["w_ffn2_gu"][l], w["w_ffn2_down"][l], g_after, f"l{l}_ffn2")
        saved.append((s1, s2, s3, s4))

    dx, dg_final, sq = _final_loss(xs, g_final, loss_target.reshape(s, d), name="loss_head")
    loss = lax.psum(jnp.sum(sq) * (0.5 / d), MESH_AXES)

    per_layer = []
    for l in reversed(range(depth)):
        s1, s2, s3, s4 = saved[l]
        grads = {}
        dx, grads["g_ffn2"], grads["w_ffn2_gu"], grads["w_ffn2_down"] = _ffn_bwd(
            dx, s4, w["g_ffn2"][l], w["w_ffn2_gu"][l], w["w_ffn2_down"][l], f"l{l}_ffn2")
        dx, g3 = _xattn_block_bwd(dx, s3, mems, w, l, f"l{l}_xattn")
        dx, g2 = _mix_bwd(dx, s2, w, l, f"l{l}_mix")
        dx, grads["g_ffn1"], grads["w_ffn1_gu"], grads["w_ffn1_down"] = _ffn_bwd(
            dx, s1, w["g_ffn1"][l], w["w_ffn1_gu"][l], w["w_ffn1_down"][l], f"l{l}_ffn1")
        grads.update(g2)
        grads.update(g3)
        per_layer.append(grads)
    per_layer.reverse()
    grads = {name: jnp.stack([per_layer[l][name] for l in range(depth)]) for name in WEIGHTS if name != "g_final"}
    grads["g_final"] = dg_final.reshape(d)

    reduced = _reduce_gradients(grads, lax.axis_index("c"))
    deltas, new_m, new_v = {}, {}, {}
    for name in WEIGHTS:
        deltas[name], new_m[name], new_v[name] = _adamw(local[name], reduced[name], mom1[name], mom2[name], name=f"adamw_{name}")
    return (loss, dx.reshape(x.shape), *[reduced[n] for n in WEIGHTS], *[deltas[n] for n in WEIGHTS],
            *[new_m[n] for n in WEIGHTS], *[new_v[n] for n in WEIGHTS])
```

```python
import functools

import jax
import jax.numpy as jnp
from jax import lax
from jax.experimental import pallas as pl
from jax.experimental.pallas import tpu as pltpu

F32 = jnp.float32
BF16 = jnp.bfloat16

EPS = 1e-6
HEAD_DIM = 64
LANES = 128
N_XHEADS = 4
CONV_K = 3
ADAM_LR, ADAM_B1, ADAM_B2, ADAM_EPS, ADAM_WD, ADAM_STEP = 0.001, 0.9, 0.999, 1e-08, 0.01, 10
VMEM_LIMIT_BYTES = 56 * 1024 * 1024
NEG_BIG = -1e30
MESH_AXES = ("x", "y", "c")
N_CHIPS = 4


def _params(*sem):
    return pltpu.CompilerParams(dimension_semantics=sem, vmem_limit_bytes=VMEM_LIMIT_BYTES)


_DIMS = {"nn": (((1,), (0,)), ((), ())), "nt": (((1,), (1,)), ((), ())), "tn": (((0,), (0,)), ((), ()))}


def _matmul(a, b, mode, out_dtype, *, tm, tn, tk, name, alpha=1.0, addend=None, residual=None, norm_gain=None, dnorm=None,
            a_slabs=False, b_slab=None):
    if a_slabs:
        assert mode == "nt"
        n_slab, m, k_slab = a.shape
        k, (n, k2) = n_slab * k_slab, b.shape
        tk = min(tk, k_slab)
        assert k_slab % tk == 0
    elif b_slab is not None:
        assert mode == "tn"
        (k, m), (_, k2, n) = a.shape, b.shape
    elif mode == "nn":
        (m, k), (k2, n) = a.shape, b.shape
    elif mode == "nt":
        (m, k), (n, k2) = a.shape, b.shape
    else:
        (k, m), (k2, n) = a.shape, b.shape
    assert k == k2, (a.shape, b.shape, mode)
    tm, tn, tk = min(tm, m), min(tn, n), min(tk, k)
    assert m % tm == 0 and n % tn == 0 and k % tk == 0, (m, n, k, tm, tn, tk)
    assert (norm_gain is None and dnorm is None) or tn == n
    nk = k // tk
    dims = _DIMS[mode]

    def body(*refs):
        refs = list(refs)
        a_ref, b_ref = refs[:2]
        del refs[:2]
        add_ref = refs.pop(0) if addend is not None else None
        r_ref = refs.pop(0) if residual is not None else None
        g_ref = refs.pop(0) if (norm_gain is not None or dnorm is not None) else None
        x_ref = refs.pop(0) if dnorm is not None else None
        o_ref = refs.pop(0)
        h_ref = refs.pop(0) if norm_gain is not None else None
        dg_ref = refs.pop(0) if dnorm is not None else None
        scratch = refs
        prod = lax.dot_general(a_ref[...].astype(BF16), b_ref[...].astype(BF16), dims, preferred_element_type=F32)

        def finish(acc):
            if alpha != 1.0:
                acc = acc * alpha
            if add_ref is not None:
                acc = acc + add_ref[...]
            if dnorm is not None:
                xf = x_ref[...]
                r = _rms_rows(xf)
                xhat = xf * r
                dxhat = acc * g_ref[...]

                @pl.when((pl.program_id(0) == 0))
                def _():
                    dg_ref[...] = jnp.zeros(dg_ref.shape, F32)

                dg_ref[...] += jnp.sum(acc * xhat, axis=0, keepdims=True)
                acc = r * (dxhat - xhat * jnp.mean(dxhat * xhat, axis=-1, keepdims=True))
            if r_ref is not None:
                acc = acc + r_ref[...].astype(F32)
            o_ref[...] = acc.astype(o_ref.dtype)
            if norm_gain is not None:
                h_ref[...] = (acc * _rms_rows(acc) * g_ref[...]).astype(h_ref.dtype)

        if nk == 1:
            finish(prod)
        else:
            acc_ref = scratch[0]
            kk = pl.program_id(2)

            @pl.when(kk == 0)
            def _():
                acc_ref[...] = prod

            @pl.when(kk > 0)
            def _():
                acc_ref[...] += prod

            @pl.when(kk == nk - 1)
            def _():
                finish(acc_ref[...])

    if mode == "nn":
        a_spec = pl.BlockSpec((tm, tk), lambda i, j, kk: (i, kk))
        b_spec = pl.BlockSpec((tk, tn), lambda i, j, kk: (kk, j))
    elif mode == "nt":
        a_spec = pl.BlockSpec((tm, tk), lambda i, j, kk: (i, kk))
        b_spec = pl.BlockSpec((tn, tk), lambda i, j, kk: (j, kk))
    else:
        a_spec = pl.BlockSpec((tk, tm), lambda i, j, kk: (kk, i))
        b_spec = pl.BlockSpec((tk, tn), lambda i, j, kk: (kk, j))
    if a_slabs:
        per_slab = k_slab // tk
        assert n_slab == 2

        def a_index(i, j, kk):
            second = (kk >= per_slab).astype(jnp.int32)
            return second, i, kk - second * per_slab

        a_spec = pl.BlockSpec((None, tm, tk), a_index)
    if b_slab is not None:
        b_spec = pl.BlockSpec((None, tk, tn), lambda i, j, kk: (b_slab, kk, j))
    o_spec = pl.BlockSpec((tm, tn), lambda i, j, kk: (i, j))
    vec_spec = pl.BlockSpec((1, n), lambda i, j, kk: (0, 0))
    in_specs, args = [a_spec, b_spec], [a, b]
    out_specs, out_shape = [o_spec], [jax.ShapeDtypeStruct((m, n), out_dtype)]
    if addend is not None:
        in_specs.append(o_spec)
        args.append(addend)
    if residual is not None:
        in_specs.append(o_spec)
        args.append(residual)
    if norm_gain is not None:
        in_specs.append(vec_spec)
        args.append(norm_gain.reshape(1, n))
        out_specs.append(o_spec)
        out_shape.append(jax.ShapeDtypeStruct((m, n), BF16))
    if dnorm is not None:
        in_specs += [vec_spec, o_spec]
        args += [dnorm[1].reshape(1, n), dnorm[0]]
        out_specs.append(vec_spec)
        out_shape.append(jax.ShapeDtypeStruct((1, n), F32))
    outs = pl.pallas_call(
        body, name=name, grid=(m // tm, n // tn, nk), in_specs=in_specs, out_specs=out_specs, out_shape=out_shape,
        scratch_shapes=[pltpu.VMEM((tm, tn), F32)] if nk > 1 else [],
        compiler_params=_params(*(("arbitrary",) * 3 if dnorm is not None else ("parallel", "parallel", "arbitrary"))),
    )(*args)
    return outs[0] if len(outs) == 1 else tuple(outs)


def _row_tile(rows, want):
    t = min(rows, want)
    assert rows % t == 0, (rows, t)
    return t


def _rmsnorm_fwd(x, g, *, name, tr=1024):
    s, d = x.shape
    tr = _row_tile(s, tr)

    def body(x_ref, g_ref, o_ref):
        xf = x_ref[...]
        r = lax.rsqrt(jnp.mean(xf * xf, axis=-1, keepdims=True) + EPS)
        o_ref[...] = (xf * r * g_ref[...]).astype(o_ref.dtype)

    return pl.pallas_call(
        body, name=name, grid=(s // tr,),
        in_specs=[pl.BlockSpec((tr, d), lambda i: (i, 0)), pl.BlockSpec((1, d), lambda i: (0, 0))],
        out_specs=pl.BlockSpec((tr, d), lambda i: (i, 0)),
        out_shape=jax.ShapeDtypeStruct((s, d), BF16),
        compiler_params=_params("parallel"),
    )(x, g.reshape(1, d))


def _rmsnorm_bwd(x, g, dh, dres, *, name, tr=512):
    s, d = x.shape
    tr = _row_tile(s, tr)

    def body(x_ref, g_ref, dh_ref, *rest):
        if dres is None:
            dx_ref, dg_ref = rest
        else:
            dres_ref, dx_ref, dg_ref = rest
        xf = x_ref[...]
        r = lax.rsqrt(jnp.mean(xf * xf, axis=-1, keepdims=True) + EPS)
        xhat = xf * r
        dhf = dh_ref[...].astype(F32)
        dxhat = dhf * g_ref[...]
        dx = r * (dxhat - xhat * jnp.mean(dxhat * xhat, axis=-1, keepdims=True))
        if dres is not None:
            dx = dx + dres_ref[...]
        dx_ref[...] = dx

        @pl.when(pl.program_id(0) == 0)
        def _():
            dg_ref[...] = jnp.zeros_like(dg_ref)

        dg_ref[...] += jnp.sum(dhf * xhat, axis=0, keepdims=True)

    row = pl.BlockSpec((tr, d), lambda i: (i, 0))
    vec = pl.BlockSpec((1, d), lambda i: (0, 0))
    in_specs, args = [row, vec, row], [x, g.reshape(1, d), dh]
    if dres is not None:
        in_specs.append(row)
        args.append(dres)
    dx, dg = pl.pallas_call(
        body, name=name, grid=(s // tr,), in_specs=in_specs, out_specs=[row, vec],
        out_shape=[jax.ShapeDtypeStruct((s, d), F32), jax.ShapeDtypeStruct((1, d), F32)],
        compiler_params=_params("arbitrary"),
    )(*args)
    return dx, dg.reshape(d)


def _gate_up_act(h, w_gu, *, name, tm=2048, tn=256):
    s, d = h.shape
    f = w_gu.shape[1] // 2
    tm, tn = min(tm, s), min(tn, f)
    assert s % tm == 0 and f % tn == 0
    nf = f // tn

    def body(h_ref, wg_ref, wu_ref, gu_ref, a_ref):
        hb = h_ref[...]
        gate = _dot(hb, wg_ref[...], _NN)
        up = _dot(hb, wu_ref[...], _NN)
        gu_ref[0] = gate.astype(gu_ref.dtype)
        gu_ref[1] = up.astype(gu_ref.dtype)
        a_ref[...] = (gate * jax.nn.sigmoid(gate) * up).astype(a_ref.dtype)

    return pl.pallas_call(
        body, name=name, grid=(s // tm, nf),
        in_specs=[pl.BlockSpec((tm, d), lambda i, j: (i, 0)), pl.BlockSpec((d, tn), lambda i, j: (0, j)),
                  pl.BlockSpec((d, tn), lambda i, j: (0, j + nf))],
        out_specs=[pl.BlockSpec((2, tm, tn), lambda i, j: (0, i, j)), pl.BlockSpec((tm, tn), lambda i, j: (i, j))],
        out_shape=[jax.ShapeDtypeStruct((2, s, f), BF16), jax.ShapeDtypeStruct((s, f), BF16)],
        compiler_params=_params("parallel", "parallel"),
    )(h, w_gu, w_gu)


def _down_bwd_act(dy, w_down, gu, alpha, *, name, tm=512):
    s, d = dy.shape
    f = w_down.shape[0]
    tm, tn = min(tm, s), f // 2
    assert s % tm == 0 and tn % LANES == 0

    def body(dy_ref, w_ref, gu_ref, dgu_ref):
        da = _dot(dy_ref[...].astype(BF16), w_ref[...], _NT) * alpha
        gate = gu_ref[0].astype(F32)
        up = gu_ref[1].astype(F32)
        sig = jax.nn.sigmoid(gate)
        silu = gate * sig
        dgu_ref[0] = (da * up * (sig + silu * (1.0 - sig))).astype(dgu_ref.dtype)
        dgu_ref[1] = (da * silu).astype(dgu_ref.dtype)

    slab = pl.BlockSpec((2, tm, tn), lambda i, j: (0, i, j))
    return pl.pallas_call(
        body, name=name, grid=(s // tm, f // tn),
        in_specs=[pl.BlockSpec((tm, d), lambda i, j: (i, 0)), pl.BlockSpec((tn, d), lambda i, j: (j, 0)), slab],
        out_specs=slab, out_shape=jax.ShapeDtypeStruct((2, s, f), BF16),
        compiler_params=_params("parallel", "parallel"),
    )(dy, w_down, gu)


_NT = (((1,), (1,)), ((), ()))
_NN = (((1,), (0,)), ((), ()))
_TN = (((0,), (0,)), ((), ()))
_QK_SCALE = HEAD_DIM ** -0.5


def _dot(a, b, dims):
    return lax.dot_general(a, b, dims, preferred_element_type=F32)


SKIP_BELOW = 50.0
_SMEM = pl.BlockSpec(memory_space=pltpu.SMEM)


def _attn_bounds(z, c, *, qcol_units, kcol_units, aw, t, name):
    s = z.shape[0]
    nq = s // t
    nh = aw // HEAD_DIM

    def body(q_ref, k_ref, o_ref):
        d = lax.broadcasted_iota(jnp.int32, (aw, LANES), 0)
        hh = lax.broadcasted_iota(jnp.int32, (aw, LANES), 1)
        onehot = ((d >= hh * HEAD_DIM) & (d < (hh + 1) * HEAD_DIM)).astype(BF16)
        for r, ref in enumerate((q_ref, k_ref)):
            v = ref[...].astype(F32)
            sq = _dot((v * v).astype(BF16), onehot, _NN)
            o_ref[r:r + 1, :] = jnp.max(sq, axis=0, keepdims=True)
        o_ref[2:, :] = jnp.zeros((HALO - 2, LANES), F32)

    sq = pl.pallas_call(
        body, name=name, grid=(nq,),
        in_specs=[pl.BlockSpec((t, aw), lambda i: (i, qcol_units)), pl.BlockSpec((t, aw), lambda i: (i, kcol_units))],
        out_specs=pl.BlockSpec((None, HALO, LANES), lambda i: (i, 0, 0)),
        out_shape=jax.ShapeDtypeStruct((nq, HALO, LANES), F32),
        compiler_params=_params("parallel"),
    )(z, z)
    norms = jnp.sqrt(sq[:, :2, :nh]) * 1.01
    qn = (norms[:, 0, :] * _QK_SCALE).T.reshape(-1)
    kn = norms[:, 1, :].T.reshape(-1)
    cs = c[0::t, :nh].T.reshape(-1)
    ce = c[t - 1::t, :nh].T.reshape(-1)
    return qn, kn, cs, ce


def _block_active(bounds, head, i, j, nq):
    qn_ref, kn_ref, cs_ref, ce_ref = bounds
    qi = qn_ref[head * nq + i]
    upper = qi * kn_ref[head * nq + j] + (cs_ref[head * nq + i] - ce_ref[head * nq + j])
    lower = -(qi * kn_ref[head * nq + i])
    return upper - lower > -SKIP_BELOW


def _for_active_heads(bounds, pair, i, j, nq, head_step):
    act = [_block_active(bounds, 2 * pair + h, i, j, nq) for h in range(2)]

    def run(heads):
        for h in heads:
            head_step(h)

    pl.when(act[0] & act[1])(functools.partial(run, (0, 1)))
    pl.when(act[0] & jnp.logical_not(act[1]))(functools.partial(run, (0,)))
    pl.when(jnp.logical_not(act[0]) & act[1])(functools.partial(run, (1,)))


def _attn_operands(z, ccol, *, qcol, kcol, vcol, t, name):
    s = z.shape[0]
    npairs = ccol.shape[0]

    def body(q_ref, k_ref, v_ref, c_ref, qa_ref, ka_ref, va_ref):
        lane = lax.broadcasted_iota(jnp.int32, (t, LANES), 1)
        q2 = q_ref[...] * jnp.asarray(_QK_SCALE, BF16)
        k2, v2 = k_ref[...], v_ref[...]
        one, zero = jnp.ones((t, LANES), BF16), jnp.zeros((t, LANES), BF16)
        for h in range(2):
            base = HEAD_DIM * (1 - h)
            mine = (lane < HEAD_DIM) if h == 0 else (lane >= HEAD_DIM)
            c = c_ref[:, h:h + 1]
            hi = c.astype(BF16)
            r1 = c - hi.astype(F32)
            mid = r1.astype(BF16)
            lo = (r1 - mid.astype(F32)).astype(BF16)
            qa, ka = jnp.where(mine, q2, zero), jnp.where(mine, k2, zero)
            for r, word in enumerate((hi, mid, lo)):
                qa = jnp.where(lane == base + r, word, qa)
                qa = jnp.where(lane == base + 3 + r, one, qa)
                ka = jnp.where(lane == base + r, one, ka)
                ka = jnp.where(lane == base + 3 + r, -word, ka)
            qa_ref[h] = qa
            ka_ref[h] = ka
            va_ref[h] = jnp.where((lane >= base) & (lane < base + 3), one, jnp.where(mine, v2, zero))

    out = jax.ShapeDtypeStruct((npairs, 2, s, LANES), BF16)
    blk = pl.BlockSpec((None, 2, t, LANES), lambda p, i: (p, 0, i, 0))
    return pl.pallas_call(
        body, name=name, grid=(npairs, s // t),
        in_specs=[pl.BlockSpec((t, LANES), lambda p, i: (i, qcol + p)), pl.BlockSpec((t, LANES), lambda p, i: (i, kcol + p)),
                  pl.BlockSpec((t, LANES), lambda p, i: (i, vcol + p)), pl.BlockSpec((None, t, 2), lambda p, i: (p, i, 0))],
        out_specs=[blk, blk, blk], out_shape=[out, out, out],
        compiler_params=_params("parallel", "parallel"),
    )(z, z, z, ccol)


def _fattn_fwd(qa, ka, va, bounds, *, t, name):
    npairs, _, s, _ = qa.shape
    nq = s // t
    reps = t // LANES

    def body(qn_ref, kn_ref, cs_ref, ce_ref, q_ref, k_ref, v_ref, o_ref, lse_ref, m_scr, acc_scr):
        pair, i = pl.program_id(0), pl.program_id(1)
        m_scr[...] = jnp.full(m_scr.shape, NEG_BIG, F32)
        acc_scr[...] = jnp.zeros(acc_scr.shape, F32)

        def head_step(h, j, diagonal):
            off = pl.multiple_of(j * t, t)
            sc = _dot(q_ref[h], k_ref[h, pl.ds(off, t), :], _NT)
            if diagonal:
                row = lax.broadcasted_iota(jnp.int32, (t, t), 0)
                col = lax.broadcasted_iota(jnp.int32, (t, t), 1)
                sc = jnp.where(row >= col, sc, NEG_BIG)
            m_old = m_scr[h]
            m_new = jnp.maximum(m_old, jnp.max(sc, axis=1, keepdims=True))
            p = jnp.exp(sc - jnp.tile(m_new, (1, reps)))
            acc_scr[h] = acc_scr[h] * jnp.exp(m_old - m_new) + _dot(p.astype(BF16), v_ref[h, pl.ds(off, t), :], _NN)
            m_scr[h] = m_new

        def loop_body(j, carry):
            _for_active_heads((qn_ref, kn_ref, cs_ref, ce_ref), pair, i, j, nq, lambda h: head_step(h, j, False))
            return carry

        lax.fori_loop(0, i, loop_body, 0)
        for h in range(2):
            head_step(h, i, True)
        is_a = lax.broadcasted_iota(jnp.int32, (t, LANES), 1) < HEAD_DIM
        l = (acc_scr[0][:, HEAD_DIM:HEAD_DIM + 1], acc_scr[1][:, 0:1])
        o_ref[...] = jnp.where(is_a, acc_scr[0] / l[0], acc_scr[1] / l[1])
        lse_ref[:, 0:1] = m_scr[0][:, 0:1] + jnp.log(l[0])
        lse_ref[:, 1:2] = m_scr[1][:, 0:1] + jnp.log(l[1])

    return pl.pallas_call(
        body, name=name, grid=(npairs, nq),
        in_specs=[_SMEM, _SMEM, _SMEM, _SMEM,
                  pl.BlockSpec((None, 2, t, LANES), lambda p, i: (p, 0, i, 0)),
                  pl.BlockSpec((None, 2, s, LANES), lambda p, i: (p, 0, 0, 0)),
                  pl.BlockSpec((None, 2, s, LANES), lambda p, i: (p, 0, 0, 0))],
        out_specs=[pl.BlockSpec((t, LANES), lambda p, i: (i, p)), pl.BlockSpec((None, t, 2), lambda p, i: (p, i, 0))],
        out_shape=[jax.ShapeDtypeStruct((s, npairs * LANES), F32), jax.ShapeDtypeStruct((npairs, s, 2), F32)],
        scratch_shapes=[pltpu.VMEM((2, t, LANES), F32), pltpu.VMEM((2, t, LANES), F32)],
        compiler_params=_params("parallel", "arbitrary"),
    )(*bounds, qa, ka, va)


def _split3(x):
    hi = x.astype(BF16)
    r1 = x - hi.astype(F32)
    mid = r1.astype(BF16)
    return hi, mid, (r1 - mid.astype(F32)).astype(BF16)


def _attn_bwd_operands(qa, ka, dy, y, lse, *, t, name):
    npairs, _, s, _ = qa.shape

    def body(qa_ref, ka_ref, dy_ref, y_ref, lse_ref, qb_ref, kb_ref, dyb_ref):
        lane = lax.broadcasted_iota(jnp.int32, (t, LANES), 1)
        dyf = dy_ref[...]
        prod = dyf * y_ref[...]
        dyh = dyf.astype(BF16)
        one, zero = jnp.ones((t, LANES), BF16), jnp.zeros((t, LANES), BF16)
        for h in range(2):
            base = HEAD_DIM * (1 - h)
            mine = (lane < HEAD_DIM) if h == 0 else (lane >= HEAD_DIM)
            delta = jnp.sum(jnp.where(mine, prod, 0.0), axis=1, keepdims=True)
            qb, kb, dyb = qa_ref[h], ka_ref[h], jnp.where(mine, dyh, zero)
            for r, (lw, dw) in enumerate(zip(_split3(lse_ref[:, h:h + 1]), _split3(delta))):
                qb = jnp.where(lane == base + 6 + r, -lw, qb)
                kb = jnp.where(lane == base + 6 + r, one, kb)
                dyb = jnp.where(lane == base + r, -dw, dyb)
            qb_ref[h] = qb
            kb_ref[h] = kb
            dyb_ref[h] = dyb

    out = jax.ShapeDtypeStruct((npairs, 2, s, LANES), BF16)
    blk = pl.BlockSpec((None, 2, t, LANES), lambda p, i: (p, 0, i, 0))
    tile = pl.BlockSpec((t, LANES), lambda p, i: (i, p))
    return pl.pallas_call(
        body, name=name, grid=(npairs, s // t),
        in_specs=[blk, blk, tile, tile, pl.BlockSpec((None, t, 2), lambda p, i: (p, i, 0))],
        out_specs=[blk, blk, blk], out_shape=[out, out, out],
        compiler_params=_params("parallel", "parallel"),
    )(qa, ka, dy, y, lse)


def _diag_mask(sc, t, queries_on_rows):
    row = lax.broadcasted_iota(jnp.int32, (t, t), 0)
    col = lax.broadcasted_iota(jnp.int32, (t, t), 1)
    return jnp.where((row >= col) if queries_on_rows else (col >= row), sc, NEG_BIG)


def _fattn_dq(qb, kb, va, dyb, bounds, *, t, name):
    npairs, _, s, _ = qb.shape
    nq = s // t

    def body(qn_ref, kn_ref, cs_ref, ce_ref, q_ref, k_ref, v_ref, dy_ref, dq_ref, dcq_ref, acc_scr):
        pair, i = pl.program_id(0), pl.program_id(1)
        acc_scr[...] = jnp.zeros(acc_scr.shape, F32)

        def head_step(h, j, diagonal):
            off = pl.multiple_of(j * t, t)
            kj = k_ref[h, pl.ds(off, t), :]
            sc = _dot(q_ref[h], kj, _NT)
            if diagonal:
                sc = _diag_mask(sc, t, True)
            ds = jnp.exp(sc) * _dot(dy_ref[h], v_ref[h, pl.ds(off, t), :], _NT)
            acc_scr[h] += _dot(ds.astype(BF16), kj, _NN)

        def loop_body(j, carry):
            _for_active_heads((qn_ref, kn_ref, cs_ref, ce_ref), pair, i, j, nq, lambda h: head_step(h, j, False))
            return carry

        lax.fori_loop(0, i, loop_body, 0)
        for h in range(2):
            head_step(h, i, True)
        is_a = lax.broadcasted_iota(jnp.int32, (t, LANES), 1) < HEAD_DIM
        dq_ref[...] = (jnp.where(is_a, acc_scr[0], acc_scr[1]) * _QK_SCALE).astype(dq_ref.dtype)
        dcq_ref[:, 0:1] = acc_scr[0][:, HEAD_DIM:HEAD_DIM + 1]
        dcq_ref[:, 1:2] = acc_scr[1][:, 0:1]

    tile2 = pl.BlockSpec((None, 2, t, LANES), lambda p, i: (p, 0, i, 0))
    whole = pl.BlockSpec((None, 2, s, LANES), lambda p, i: (p, 0, 0, 0))
    return pl.pallas_call(
        body, name=name, grid=(npairs, nq),
        in_specs=[_SMEM, _SMEM, _SMEM, _SMEM, tile2, whole, whole, tile2],
        out_specs=[pl.BlockSpec((t, LANES), lambda p, i: (i, p)), pl.BlockSpec((None, t, 2), lambda p, i: (p, i, 0))],
        out_shape=[jax.ShapeDtypeStruct((s, npairs * LANES), BF16), jax.ShapeDtypeStruct((npairs, s, 2), F32)],
        scratch_shapes=[pltpu.VMEM((2, t, LANES), F32)],
        compiler_params=_params("parallel", "arbitrary"),
    )(*bounds, qb, kb, va, dyb)


def _fattn_dkv(qb, kb, va, dyb, bounds, *, t, name):
    npairs, _, s, _ = qb.shape
    nq = s // t

    def body(qn_ref, kn_ref, cs_ref, ce_ref, k_ref, v_ref, q_ref, dy_ref, dk_ref, dv_ref, dc_ref, dk_scr, dv_scr):
        pair, j = pl.program_id(0), pl.program_id(1)
        dk_scr[...] = jnp.zeros(dk_scr.shape, F32)
        dv_scr[...] = jnp.zeros(dv_scr.shape, F32)

        def head_step(h, i, diagonal):
            off = pl.multiple_of(i * t, t)
            qi = q_ref[h, pl.ds(off, t), :]
            dyi = dy_ref[h, pl.ds(off, t), :]
            st = _dot(k_ref[h], qi, _NT)
            if diagonal:
                st = _diag_mask(st, t, False)
            pt = jnp.exp(st)
            dv_scr[h] += _dot(pt.astype(BF16), dyi, _NN)
            dst = pt * _dot(v_ref[h], dyi, _NT)
            dk_scr[h] += _dot(dst.astype(BF16), qi, _NN)

        def loop_body(i, carry):
            _for_active_heads((qn_ref, kn_ref, cs_ref, ce_ref), pair, i, j, nq, lambda h: head_step(h, i, False))
            return carry

        for h in range(2):
            head_step(h, j, True)
        lax.fori_loop(j + 1, nq, loop_body, 0)
        is_a = lax.broadcasted_iota(jnp.int32, (t, LANES), 1) < HEAD_DIM
        dk_ref[...] = jnp.where(is_a, dk_scr[0], dk_scr[1]).astype(dk_ref.dtype)
        dv_ref[...] = jnp.where(is_a, dv_scr[0], dv_scr[1]).astype(dv_ref.dtype)
        dc_ref[:, 0:1] = -dk_scr[0][:, HEAD_DIM + 3:HEAD_DIM + 4]
        dc_ref[:, 1:2] = -dk_scr[1][:, 3:4]

    tile2 = pl.BlockSpec((None, 2, t, LANES), lambda p, j: (p, 0, j, 0))
    whole = pl.BlockSpec((None, 2, s, LANES), lambda p, j: (p, 0, 0, 0))
    tile = pl.BlockSpec((t, LANES), lambda p, j: (j, p))
    return pl.pallas_call(
        body, name=name, grid=(npairs, nq),
        in_specs=[_SMEM, _SMEM, _SMEM, _SMEM, tile2, tile2, whole, whole],
        out_specs=[tile, tile, pl.BlockSpec((None, t, 2), lambda p, j: (p, j, 0))],
        out_shape=[jax.ShapeDtypeStruct((s, npairs * LANES), BF16), jax.ShapeDtypeStruct((s, npairs * LANES), BF16),
                   jax.ShapeDtypeStruct((npairs, s, 2), F32)],
        scratch_shapes=[pltpu.VMEM((2, t, LANES), F32), pltpu.VMEM((2, t, LANES), F32)],
        compiler_params=_params("parallel", "arbitrary"),
    )(*bounds, kb, va, qb, dyb)


def _log_sigmoid(x):
    return jnp.minimum(x, 0.0) - jnp.log(1.0 + jnp.exp(-jnp.abs(x)))


def _cumsum_fwd(zf, b, *, name, t=512):
    s, w = zf.shape
    t = _row_tile(s, t)

    def body(zf_ref, b_ref, c_ref, carry):
        @pl.when(pl.program_id(0) == 0)
        def _():
            carry[...] = jnp.zeros(carry.shape, F32)

        lf = _log_sigmoid(zf_ref[...] + b_ref[...])
        row = lax.broadcasted_iota(jnp.int32, (t, t), 0)
        col = lax.broadcasted_iota(jnp.int32, (t, t), 1)
        tri = (row >= col).astype(F32)
        c = lax.dot_general(tri, lf, _NN, precision=lax.Precision.HIGHEST, preferred_element_type=F32) + carry[...]
        c_ref[...] = c
        carry[...] = c[t - 1:t, :]

    return pl.pallas_call(
        body, name=name, grid=(s // t,),
        in_specs=[pl.BlockSpec((t, w), lambda i: (i, 0)), pl.BlockSpec((1, w), lambda i: (0, 0))],
        out_specs=pl.BlockSpec((t, w), lambda i: (i, 0)),
        out_shape=jax.ShapeDtypeStruct((s, w), F32),
        scratch_shapes=[pltpu.VMEM((1, w), F32)],
        compiler_params=_params("arbitrary"),
    )(zf, b)


def _cumsum_bwd(dcq, dck, zf, b, *, name, t=512):
    s, w = zf.shape
    t = _row_tile(s, t)
    nb = s // t

    def body(dcq_ref, dck_ref, zf_ref, b_ref, dzf_ref, db_ref, carry):
        @pl.when(pl.program_id(0) == 0)
        def _():
            carry[...] = jnp.zeros(carry.shape, F32)
            db_ref[...] = jnp.zeros(db_ref.shape, F32)

        row = lax.broadcasted_iota(jnp.int32, (t, t), 0)
        col = lax.broadcasted_iota(jnp.int32, (t, t), 1)
        tri = (row <= col).astype(F32)
        dc = dcq_ref[...] + dck_ref[...]
        dlf = lax.dot_general(tri, dc, _NN, precision=lax.Precision.HIGHEST, preferred_element_type=F32) + carry[...]
        carry[...] = dlf[0:1, :]
        dzf = dlf * jax.nn.sigmoid(-(zf_ref[...] + b_ref[...]))
        dzf_ref[...] = dzf
        db_ref[...] += jnp.sum(dzf, axis=0, keepdims=True)

    blk = pl.BlockSpec((t, w), lambda i: (nb - 1 - i, 0))
    vec = pl.BlockSpec((1, w), lambda i: (0, 0))
    return pl.pallas_call(
        body, name=name, grid=(nb,), in_specs=[blk, blk, blk, vec], out_specs=[blk, vec],
        out_shape=[jax.ShapeDtypeStruct((s, w), F32), jax.ShapeDtypeStruct((1, w), F32)],
        scratch_shapes=[pltpu.VMEM((1, w), F32)],
        compiler_params=_params("arbitrary"),
    )(dcq, dck, zf, b)


HALO = 8


def _rms_rows(v):
    return lax.rsqrt(jnp.mean(v * v, axis=-1, keepdims=True) + EPS)


def _mixpost_fwd(z, yatt, wconv, gc, ga, *, name, tr=512):
    s = z.shape[0]
    cw, aw = gc.shape[-1], ga.shape[-1]
    tr = _row_tile(s, tr)

    def body(zb_ref, zc_ref, zv_ref, ya_ref, w_ref, gc_ref, ga_ref, ycat_ref, cv_ref, u_scr):
        @pl.when(pl.program_id(0) == 0)
        def _():
            u_scr[0:HALO, :] = jnp.zeros((HALO, cw), F32)

        u = zc_ref[...].astype(F32) * zv_ref[...].astype(F32)
        u_scr[HALO:HALO + tr, :] = u
        cv = w_ref[0:1, :] * u_scr[HALO - 2:HALO - 2 + tr, :] + w_ref[1:2, :] * u_scr[HALO - 1:HALO - 1 + tr, :] + w_ref[2:3, :] * u
        u_scr[0:HALO, :] = u_scr[tr:tr + HALO, :]
        cv_ref[...] = cv
        yc = zb_ref[...].astype(F32) * cv
        ya = ya_ref[...]
        ycat_ref[:, :cw] = (yc * _rms_rows(yc) * gc_ref[...]).astype(ycat_ref.dtype)
        ycat_ref[:, cw:] = (ya * _rms_rows(ya) * ga_ref[...]).astype(ycat_ref.dtype)

    return pl.pallas_call(
        body, name=name, grid=(s // tr,),
        in_specs=[
            pl.BlockSpec((tr, cw), lambda i: (i, 0)), pl.BlockSpec((tr, cw), lambda i: (i, 1)),
            pl.BlockSpec((tr, cw), lambda i: (i, 2)), pl.BlockSpec((tr, aw), lambda i: (i, 0)),
            pl.BlockSpec((HALO, cw), lambda i: (0, 0)), pl.BlockSpec((1, cw), lambda i: (0, 0)),
            pl.BlockSpec((1, aw), lambda i: (0, 0)),
        ],
        out_specs=[pl.BlockSpec((tr, cw + aw), lambda i: (i, 0)), pl.BlockSpec((tr, cw), lambda i: (i, 0))],
        out_shape=[jax.ShapeDtypeStruct((s, cw + aw), BF16), jax.ShapeDtypeStruct((s, cw), F32)],
        scratch_shapes=[pltpu.VMEM((tr + HALO, cw), F32)],
        compiler_params=_params("arbitrary"),
    )(z, z, z, yatt, wconv, gc.reshape(1, cw), ga.reshape(1, aw))


def _mixpost_bwd(dycat, z, yatt, cv, wconv, gc, ga, *, name, tr=512):
    s = z.shape[0]
    cw, aw = gc.shape[-1], ga.shape[-1]
    tr = _row_tile(s, tr)
    nb = s // tr

    def body(dy_ref, zb_ref, zc_ref, zv_ref, ya_ref, cv_ref, w_ref, gc_ref, ga_ref,
             dz_ref, dya_ref, dw_ref, dgc_ref, dga_ref, d_scr):
        @pl.when(pl.program_id(0) == 0)
        def _():
            d_scr[tr:tr + HALO, :] = jnp.zeros((HALO, cw), F32)
            dw_ref[...] = jnp.zeros(dw_ref.shape, F32)
            dgc_ref[...] = jnp.zeros(dgc_ref.shape, F32)
            dga_ref[...] = jnp.zeros(dga_ref.shape, F32)

        zb, zc, zv = zb_ref[...].astype(F32), zc_ref[...].astype(F32), zv_ref[...].astype(F32)
        cvv = cv_ref[...]

        def norm_bwd(v, dn, g):
            r = _rms_rows(v)
            vh = v * r
            dvh = dn * g
            return r * (dvh - vh * jnp.mean(dvh * vh, axis=-1, keepdims=True)), jnp.sum(dn * vh, axis=0, keepdims=True)

        dyc, dgc = norm_bwd(zb * cvv, dy_ref[:, :cw], gc_ref[...])
        dya, dga = norm_bwd(ya_ref[...], dy_ref[:, cw:], ga_ref[...])
        dgc_ref[...] += dgc
        dga_ref[...] += dga
        dya_ref[...] = dya
        dcv = dyc * zb
        d_scr[0:tr, :] = dcv
        d1 = d_scr[1:tr + 1, :]
        d2 = d_scr[2:tr + 2, :]
        du = w_ref[2:3, :] * dcv + w_ref[1:2, :] * d1 + w_ref[0:1, :] * d2
        u = zc * zv
        dw_ref[0:1, :] += jnp.sum(u * d2, axis=0, keepdims=True)
        dw_ref[1:2, :] += jnp.sum(u * d1, axis=0, keepdims=True)
        dw_ref[2:3, :] += jnp.sum(u * dcv, axis=0, keepdims=True)
        d_scr[tr:tr + HALO, :] = d_scr[0:HALO, :]
        dz_ref[:, :cw] = (dyc * cvv).astype(dz_ref.dtype)
        dz_ref[:, cw:2 * cw] = (du * zv).astype(dz_ref.dtype)
        dz_ref[:, 2 * cw:] = (du * zc).astype(dz_ref.dtype)

    def rows(width, colblk=0):
        return pl.BlockSpec((tr, width), lambda i: (nb - 1 - i, colblk))

    def fixed(r, width):
        return pl.BlockSpec((r, width), lambda i: (0, 0))

    return pl.pallas_call(
        body, name=name, grid=(nb,),
        in_specs=[rows(cw + aw), rows(cw, 0), rows(cw, 1), rows(cw, 2), rows(aw), rows(cw),
                  fixed(HALO, cw), fixed(1, cw), fixed(1, aw)],
        out_specs=[rows(3 * cw), rows(aw), fixed(HALO, cw), fixed(1, cw), fixed(1, aw)],
        out_shape=[jax.ShapeDtypeStruct((s, 3 * cw), BF16), jax.ShapeDtypeStruct((s, aw), F32),
                   jax.ShapeDtypeStruct((HALO, cw), F32), jax.ShapeDtypeStruct((1, cw), F32),
                   jax.ShapeDtypeStruct((1, aw), F32)],
        scratch_shapes=[pltpu.VMEM((tr + HALO, cw), F32)],
        compiler_params=_params("arbitrary"),
    )(dycat, z, z, z, yatt, cv, wconv, gc.reshape(1, cw), ga.reshape(1, aw))


def _xattn_fwd(q, kv, *, name, tq=1024):
    s, d = q.shape
    m = kv.shape[0]
    dh = d // N_XHEADS
    scale = dh ** -0.5
    tq = _row_tile(s, tq)

    def body(q_ref, kv_ref, o_ref):
        for h in range(N_XHEADS):
            lo, hi = h * dh, (h + 1) * dh
            sc = _dot(q_ref[:, lo:hi], kv_ref[:, lo:hi], _NT) * scale
            p = jnp.exp(sc - jnp.max(sc, axis=1, keepdims=True))
            o = _dot(p.astype(BF16), kv_ref[:, d + lo:d + hi], _NN) / jnp.sum(p, axis=1, keepdims=True)
            o_ref[:, lo:hi] = o.astype(o_ref.dtype)

    return pl.pallas_call(
        body, name=name, grid=(s // tq,),
        in_specs=[pl.BlockSpec((tq, d), lambda i: (i, 0)), pl.BlockSpec((m, 2 * d), lambda i: (0, 0))],
        out_specs=pl.BlockSpec((tq, d), lambda i: (i, 0)),
        out_shape=jax.ShapeDtypeStruct((s, d), BF16),
        compiler_params=_params("parallel"),
    )(q, kv)


def _xattn_bwd(q, kv, do, *, name, tq=1024):
    s, d = q.shape
    m = kv.shape[0]
    dh = d // N_XHEADS
    scale = dh ** -0.5
    tq = _row_tile(s, tq)

    def body(q_ref, kv_ref, do_ref, dq_ref, dkv_ref):
        @pl.when(pl.program_id(0) == 0)
        def _():
            dkv_ref[...] = jnp.zeros(dkv_ref.shape, F32)

        for h in range(N_XHEADS):
            lo, hi = h * dh, (h + 1) * dh
            qh, kh, vh, doh = q_ref[:, lo:hi], kv_ref[:, lo:hi], kv_ref[:, d + lo:d + hi], do_ref[:, lo:hi]
            sc = _dot(qh, kh, _NT) * scale
            e = jnp.exp(sc - jnp.max(sc, axis=1, keepdims=True))
            p = e / jnp.sum(e, axis=1, keepdims=True)
            dp = _dot(doh, vh, _NT)
            ds = p * (dp - jnp.sum(dp * p, axis=1, keepdims=True))
            dsb = ds.astype(BF16)
            dq_ref[:, lo:hi] = (_dot(dsb, kh, _NN) * scale).astype(dq_ref.dtype)
            dkv_ref[:, lo:hi] += _dot(dsb, qh, _TN) * scale
            dkv_ref[:, d + lo:d + hi] += _dot(p.astype(BF16), doh, _TN)

    return pl.pallas_call(
        body, name=name, grid=(s // tq,),
        in_specs=[pl.BlockSpec((tq, d), lambda i: (i, 0)), pl.BlockSpec((m, 2 * d), lambda i: (0, 0)),
                  pl.BlockSpec((tq, d), lambda i: (i, 0))],
        out_specs=[pl.BlockSpec((tq, d), lambda i: (i, 0)), pl.BlockSpec((m, 2 * d), lambda i: (0, 0))],
        out_shape=[jax.ShapeDtypeStruct((s, d), BF16), jax.ShapeDtypeStruct((m, 2 * d), F32)],
        compiler_params=_params("arbitrary"),
    )(q, kv, do)


def _final_loss(x, g, target, *, name, tr=512):
    s, d = x.shape
    tr = _row_tile(s, tr)

    def body(x_ref, g_ref, t_ref, dx_ref, dg_ref, sq_ref):
        @pl.when(pl.program_id(0) == 0)
        def _():
            dg_ref[...] = jnp.zeros(dg_ref.shape, F32)
            sq_ref[...] = jnp.zeros(sq_ref.shape, F32)

        xf = x_ref[...]
        r = _rms_rows(xf)
        xhat = xf * r
        err = xhat * g_ref[...] - t_ref[...]
        sq_ref[...] += jnp.sum(err * err, axis=0, keepdims=True)
        dy = err * (1.0 / d)
        dg_ref[...] += jnp.sum(dy * xhat, axis=0, keepdims=True)
        dxhat = dy * g_ref[...]
        dx_ref[...] = r * (dxhat - xhat * jnp.mean(dxhat * xhat, axis=-1, keepdims=True))

    row = pl.BlockSpec((tr, d), lambda i: (i, 0))
    vec = pl.BlockSpec((1, d), lambda i: (0, 0))
    return pl.pallas_call(
        body, name=name, grid=(s // tr,), in_specs=[row, vec, row], out_specs=[row, vec, vec],
        out_shape=[jax.ShapeDtypeStruct((s, d), F32), jax.ShapeDtypeStruct((1, d), F32), jax.ShapeDtypeStruct((1, d), F32)],
        compiler_params=_params("arbitrary"),
    )(x, g.reshape(1, d), target)


def _adamw(w, g, m, v, *, name, tr=512):
    shape = w.shape
    cols = shape[-1]
    rows = w.size // cols
    tr = tr if rows % tr == 0 else rows

    def body(w_ref, g_ref, m_ref, v_ref, d_ref, nm_ref, nv_ref):
        gf = g_ref[...]
        nm = ADAM_B1 * m_ref[...] + (1.0 - ADAM_B1) * gf
        nv = ADAM_B2 * v_ref[...] + (1.0 - ADAM_B2) * (gf * gf)
        m_hat = nm / (1.0 - ADAM_B1 ** ADAM_STEP)
        v_hat = nv / (1.0 - ADAM_B2 ** ADAM_STEP)
        d_ref[...] = -ADAM_LR * (m_hat / (jnp.sqrt(v_hat) + ADAM_EPS) + ADAM_WD * w_ref[...])
        nm_ref[...] = nm
        nv_ref[...] = nv

    blk = pl.BlockSpec((tr, cols), lambda i: (i, 0))
    out = jax.ShapeDtypeStruct((rows, cols), F32)
    outs = pl.pallas_call(
        body, name=name, grid=(rows // tr,), in_specs=[blk] * 4, out_specs=[blk] * 3, out_shape=[out] * 3,
        compiler_params=_params("parallel"),
    )(*[t.reshape(rows, cols) for t in (w, g, m, v)])
    return tuple(o.reshape(shape) for o in outs)


_HBM = pl.BlockSpec(memory_space=pl.ANY)
_MESH_ID = pl.DeviceIdType.MESH


def _place():
    x, y, c = (lax.axis_index(a) for a in MESH_AXES)
    return x, y, c, [(1 - x, y), (x, 1 - y), (1 - x, 1 - y)]


def _remote(src, dst, send_sems, recv_sems, k, to):
    return pltpu.make_async_remote_copy(src_ref=src, dst_ref=dst, send_sem=send_sems.at[k], recv_sem=recv_sems.at[k],
                                        device_id=to, device_id_type=_MESH_ID)


def _comm_call(body, arrays, out_shapes, n_remote, name):
    return pl.pallas_call(
        body, name=name, in_specs=[_HBM] * len(arrays), out_specs=[_HBM] * len(out_shapes), out_shape=out_shapes,
        scratch_shapes=[pltpu.SemaphoreType.DMA((n_remote,)), pltpu.SemaphoreType.DMA((n_remote,)),
                        pltpu.SemaphoreType.DMA((len(arrays),))],
    )(*arrays)


def _allgather_weights(halves, *, name):
    n = len(halves)

    def body(*refs):
        w, out = refs[:n], refs[n:2 * n]
        send_sems, recv_sems, local_sems = refs[2 * n:]
        x, y, c, chips = _place()
        me = 2 * x + y
        sibling = (x, y, 1 - c)
        slots = [2 * px + py for px, py in chips]

        def copy(t, k, slot, half, to, src=None):
            dst = out[t].at[slot, half]
            return _remote(dst if src is None else src, dst, send_sems, recv_sems, 7 * t + k, to)

        local = [pltpu.make_async_copy(w[t].at[c], out[t].at[me, c], local_sems.at[t]) for t in range(n)]
        first = []
        for t in range(n):
            local[t].start()
            first.append(copy(t, 0, me, c, sibling, src=w[t].at[c]))
            first += [copy(t, 1 + j, me, c, (px, py, c), src=w[t].at[c]) for j, (px, py) in enumerate(chips)]
        for cp in first:
            cp.start()
        passed = []
        for j, (px, py) in enumerate(chips):
            for t in range(n):
                copy(t, 1 + j, slots[j], c, (px, py, c)).wait_recv()
                passed.append(copy(t, 4 + j, slots[j], c, sibling))
                passed[-1].start()
        for t in range(n):
            copy(t, 0, me, 1 - c, sibling).wait_recv()
            for j in range(3):
                copy(t, 4 + j, slots[j], 1 - c, sibling).wait_recv()
        for cp in first + passed:
            cp.wait_send()
        for cp in local:
            cp.wait()

    shapes = [jax.ShapeDtypeStruct((N_CHIPS,) + h.shape, h.dtype) for h in halves]
    return _comm_call(body, halves, shapes, 7 * n, name)


def _swap_halves(gs, *, name):
    n = len(gs)

    def body(*refs):
        g, out = refs[:n], refs[n:2 * n]
        send_sems, recv_sems, _ = refs[2 * n:]
        x, y, c, _ = _place()
        copies = [_remote(g[t].at[1 - c], out[t], send_sems, recv_sems, t, (x, y, 1 - c)) for t in range(n)]
        for cp in copies:
            cp.start()
        for cp in copies:
            cp.wait()

    return _comm_call(body, gs, [jax.ShapeDtypeStruct(g.shape[1:], g.dtype) for g in gs], n, name)


def _scatter_chips(ps, *, name):
    n = len(ps)

    def body(*refs):
        p, out = refs[:n], refs[n:2 * n]
        send_sems, recv_sems, local_sems = refs[2 * n:]
        x, y, c, chips = _place()
        me = 2 * x + y
        local = [pltpu.make_async_copy(p[t].at[me], out[t].at[me], local_sems.at[t]) for t in range(n)]
        sends = [_remote(p[t].at[2 * px + py], out[t].at[me], send_sems, recv_sems, 3 * t + j, (px, py, c))
                 for t in range(n) for j, (px, py) in enumerate(chips)]
        for cp in local + sends:
            cp.start()
        for t in range(n):
            for j, (px, py) in enumerate(chips):
                slot = out[t].at[2 * px + py]
                _remote(slot, slot, send_sems, recv_sems, 3 * t + j, (px, py, c)).wait_recv()
        for cp in sends:
            cp.wait_send()
        for cp in local:
            cp.wait()

    return _comm_call(body, ps, [jax.ShapeDtypeStruct(p.shape, p.dtype) for p in ps], 3 * n, name)


def _join_halves(rs, *, name):
    n = len(rs)

    def body(*refs):
        r, out = refs[:n], refs[n:2 * n]
        send_sems, recv_sems, _ = refs[2 * n:]
        x, y, c, _ = _place()
        sends = [_remote(r[t].at[c], out[t].at[c], send_sems, recv_sems, t, (x, y, 1 - c)) for t in range(n)]
        for cp in sends:
            cp.start()
        for t in range(n):
            slot = out[t].at[1 - c]
            _remote(slot, slot, send_sems, recv_sems, t, (x, y, 1 - c)).wait_recv()
        for cp in sends:
            cp.wait_send()

    return pl.pallas_call(
        body, name=name, in_specs=[_HBM] * n, out_specs=[_HBM] * n,
        out_shape=[jax.ShapeDtypeStruct(r.shape, r.dtype) for r in rs],
        input_output_aliases={t: t for t in range(n)},
        scratch_shapes=[pltpu.SemaphoreType.DMA((n,)), pltpu.SemaphoreType.DMA((n,)), pltpu.SemaphoreType.DMA((n,))],
    )(*rs)


def _pick_rows(rows, cap=512):
    for t in range(min(rows, cap), 0, -1):
        if rows % t == 0 and t % 16 == 0:
            return t
    return rows


def _add_halves(g, recv, c, out_dtype, *, name):
    cols = g.shape[-1]
    rows = recv.size // (N_CHIPS * cols)
    tr = _pick_rows(rows)

    def body(c_ref, g_ref, r_ref, o_ref):
        o_ref[...] = (g_ref[...] + r_ref[...]).astype(o_ref.dtype)

    blk = pl.BlockSpec((None, tr, cols), lambda b, i, c_ref: (b, i, 0))
    out = pl.pallas_call(
        body, name=name,
        grid_spec=pltpu.PrefetchScalarGridSpec(
            num_scalar_prefetch=1, grid=(N_CHIPS, rows // tr),
            in_specs=[pl.BlockSpec((None, None, tr, cols), lambda b, i, c_ref: (c_ref[0], b, i, 0)), blk],
            out_specs=blk),
        out_shape=jax.ShapeDtypeStruct((N_CHIPS, rows, cols), out_dtype),
        compiler_params=_params("parallel", "parallel"),
    )(c.reshape(1).astype(jnp.int32), g.reshape(2, N_CHIPS, rows, cols), recv.reshape(N_CHIPS, rows, cols))
    return out.reshape(recv.shape)


def _sum_slots(p, c, *, name):
    cols = p.shape[-1]
    rows = p.size // (N_CHIPS * cols)
    tr = _pick_rows(rows)

    def body(c_ref, p_ref, o_ref):
        acc = p_ref[0].astype(F32)
        for q in range(1, N_CHIPS):
            acc = acc + p_ref[q].astype(F32)
        o_ref[...] = acc

    out = pl.pallas_call(
        body, name=name,
        grid_spec=pltpu.PrefetchScalarGridSpec(
            num_scalar_prefetch=1, grid=(rows // tr,),
            in_specs=[pl.BlockSpec((N_CHIPS, tr, cols), lambda i, c_ref: (0, i, 0))],
            out_specs=pl.BlockSpec((None, tr, cols), lambda i, c_ref: (c_ref[0], i, 0))),
        out_shape=jax.ShapeDtypeStruct((2, rows, cols), F32),
        compiler_params=_params("parallel"),
    )(c.reshape(1).astype(jnp.int32), p.reshape(N_CHIPS, rows, cols))
    return out.reshape((2,) + p.shape[1:])


GROUPS = (("gu", ("w_ffn1_gu", "w_ffn2_gu"), "col"), ("down", ("w_ffn1_down", "w_ffn2_down"), "row"),
          ("square", ("w_mix_out", "w_xq", "w_xo"), "row"), ("mix_in", ("w_mix_in",), "col"), ("xkv", ("w_xkv",), "col"))
REPLICATED = ("g_ffn1", "g_mix", "b_f", "g_conv_out", "g_att_out", "g_xattn", "g_mem", "g_ffn2", "g_final")
WEIGHTS = ("g_ffn1", "w_ffn1_gu", "w_ffn1_down", "g_mix", "w_mix_in", "w_conv", "b_f", "g_conv_out", "g_att_out",
           "w_mix_out", "g_xattn", "g_mem", "w_xq", "w_xkv", "w_xo", "g_ffn2", "w_ffn2_gu", "w_ffn2_down", "g_final")
SMALL_COLS = 1024
SMALL_ROW_UNIT = 16


def _gather_weights(shards):
    packs = []
    for _, members, _ in GROUPS:
        hs = []
        for name in members:
            l, a, b = shards[name].shape
            hs.append(shards[name].astype(BF16).reshape(l, 2, a // 2, b).transpose(1, 0, 2, 3))
        packs.append(jnp.stack(hs, axis=1))
    wc = shards["w_conv"]
    packs.append(jnp.stack([wc, wc]))
    gathered = _allgather_weights(packs, name="allgather_weights")
    full = {}
    for (_, members, kind), got in zip(GROUPS, gathered):
        _, _, g, l, a2, b = got.shape
        if kind == "col":
            whole = got.transpose(2, 3, 1, 4, 0, 5).reshape(g, l, 2 * a2, N_CHIPS * b)
        else:
            whole = got.transpose(2, 3, 0, 1, 4, 5).reshape(g, l, N_CHIPS * 2 * a2, b)
        for gi, name in enumerate(members):
            full[name] = whole[gi]
    l, k, b = wc.shape
    full["w_conv"] = gathered[-1][:, 0].transpose(1, 2, 0, 3).reshape(l, k, N_CHIPS * b)
    return full


def _small_rows(v):
    flat = v.reshape(-1)
    return jnp.pad(flat, (0, -flat.shape[0] % SMALL_COLS)).reshape(-1, SMALL_COLS)


def _reduce_gradients(grads, c):
    packs = []
    for _, members, kind in GROUPS:
        cut = []
        for name in members:
            l, a, b = grads[name].shape
            if kind == "col":
                cut.append(grads[name].reshape(l, 2, a // 2, N_CHIPS, b // N_CHIPS).transpose(1, 3, 0, 2, 4))
            else:
                cut.append(grads[name].reshape(l, N_CHIPS, 2, a // (2 * N_CHIPS), b).transpose(2, 1, 0, 3, 4))
        packs.append(jnp.stack(cut, axis=2))
    rep = jnp.concatenate([_small_rows(grads[n]) for n in REPLICATED])
    l, k, b = grads["w_conv"].shape
    conv = grads["w_conv"].reshape(l, k, N_CHIPS, b // N_CHIPS).transpose(2, 0, 1, 3)
    conv_rows = [_small_rows(conv[q]) for q in range(N_CHIPS)]
    n_rows = rep.shape[0] + conv_rows[0].shape[0]
    fill = jnp.zeros((-n_rows % SMALL_ROW_UNIT, SMALL_COLS), F32)
    small = jnp.stack([jnp.concatenate([rep, conv_rows[q], fill]) for q in range(N_CHIPS)])
    half_rows = small.shape[1] // 2
    packs.append(small.reshape(N_CHIPS, 2, half_rows, SMALL_COLS).transpose(1, 0, 2, 3))

    from_sibling = _swap_halves(packs, name="grad_swap_halves")
    wire = [BF16] * len(GROUPS) + [F32]
    chip_sums = [_add_halves(g, r, c, dt, name=f"grad_add_halves_{i}")
                 for i, (g, r, dt) in enumerate(zip(packs, from_sibling, wire))]
    from_chips = _scatter_chips(chip_sums, name="grad_scatter_chips")
    halves = [_sum_slots(p, c, name=f"grad_sum_chips_{i}") for i, p in enumerate(from_chips)]
    reduced = _join_halves(halves, name="grad_join_halves")

    out = {}
    for (_, members, _), r in zip(GROUPS, reduced):
        _, g, l, a2, b = r.shape
        whole = r.transpose(1, 2, 0, 3, 4).reshape(g, l, 2 * a2, b)
        for gi, name in enumerate(members):
            out[name] = whole[gi]
    rows = reduced[-1].reshape(-1, SMALL_COLS)
    off = 0
    for name in REPLICATED:
        n = -(-grads[name].size // SMALL_COLS)
        out[name] = rows[off:off + n].reshape(-1)[:grads[name].size].reshape(grads[name].shape)
        off += n
    l, k, b = grads["w_conv"].shape
    out["w_conv"] = rows[off:off + conv_rows[0].shape[0]].reshape(-1)[:l * k * b // N_CHIPS].reshape(l, k, b // N_CHIPS)
    return out


def _residual_out(a, w_out, x, g_next, name, alpha=1.0):
    kw = dict(tm=1024, tn=1024, tk=w_out.shape[0], alpha=alpha, residual=x, name=name)
    if g_next is None:
        return _matmul(a, w_out, "nn", F32, **kw), None
    return _matmul(a, w_out, "nn", F32, norm_gain=g_next, **kw)


def _ffn_fwd(x, h, w_gu, w_down, g_next, tag):
    gu, a = _gate_up_act(h, w_gu, name=f"{tag}_gu_act")
    y, h_next = _residual_out(a, w_down, x, g_next, f"{tag}_down", alpha=0.5)
    return y, h_next, (x, h, gu, a)


def _ffn_bwd(dy, saved, g, w_gu, w_down, tag):
    x, h, gu, a = saved
    f = w_down.shape[0]
    dgu = _down_bwd_act(dy, w_down, gu, 0.5, name=f"{tag}_da_dact")
    dw_down = _matmul(a, dy, "tn", F32, tm=f // 2, tn=1024, tk=1024, alpha=0.5, name=f"{tag}_dwdown")
    dw_gu = jnp.concatenate([_matmul(h, dgu, "tn", F32, tm=1024, tn=f // 2, tk=2048, b_slab=half, name=f"{tag}_dwgu{half}")
                             for half in range(2)], axis=1)
    dx, dg = _matmul(dgu, w_gu, "nt", F32, tm=1024, tn=1024, tk=f // 2, residual=dy, dnorm=(x, g), a_slabs=True,
                     name=f"{tag}_dh_dnorm")
    return dx, dg[0], dw_gu, dw_down


def _mix_fwd(x, h, w, l, g_next, tag):
    s, d = x.shape
    gc, ga = w["g_conv_out"][l], w["g_att_out"][l]
    cw, aw = gc.shape[0], ga.shape[0]
    nh = aw // HEAD_DIM
    zw = 3 * cw + 3 * aw
    t = 512 if s >= 2048 else s // 4
    cols = dict(qcol=3 * cw // LANES, kcol=(3 * cw + aw) // LANES, vcol=(3 * cw + 2 * aw) // LANES)
    w_in = w["w_mix_in"][l]
    w_main = w_in[:, :zw]
    w_f = jnp.pad(w_in[:, zw:], ((0, 0), (0, LANES - nh)))
    z = _matmul(h, w_main, "nn", BF16, tm=2048, tn=512, tk=d, name=f"{tag}_in")
    zf = _matmul(h, w_f, "nn", F32, tm=2048, tn=LANES, tk=d, name=f"{tag}_in_f")
    b = jnp.pad(w["b_f"][l], (0, LANES - nh)).reshape(1, LANES)
    c = _cumsum_fwd(zf, b, name=f"{tag}_cumsum")
    ccol = c[:, :nh].reshape(s, nh // 2, 2).transpose(1, 0, 2)
    assert (3 * cw) % aw == 0
    bounds = _attn_bounds(z, c, qcol_units=3 * cw // aw, kcol_units=3 * cw // aw + 1, aw=aw, t=t, name=f"{tag}_bounds")
    qa, ka, va = _attn_operands(z, ccol, t=t, name=f"{tag}_attn_operands", **cols)
    yatt, lse = _fattn_fwd(qa, ka, va, bounds, t=t, name=f"{tag}_attn")
    wc = jnp.pad(w["w_conv"][l], ((0, HALO - CONV_K), (0, 0)))
    ycat, cv = _mixpost_fwd(z, yatt, wc, gc, ga, name=f"{tag}_post")
    y, h_next = _residual_out(ycat, w["w_mix_out"][l], x, g_next, f"{tag}_out")
    return y, h_next, (x, h, w_main, w_f, z, zf, b, qa, ka, va, bounds, yatt, lse, wc, ycat, cv, t)


def _mix_bwd(dy, saved, w, l, tag):
    x, h, w_main, w_f, z, zf, b, qa, ka, va, bounds, yatt, lse, wc, ycat, cv, t = saved
    s, d = x.shape
    gc, ga = w["g_conv_out"][l], w["g_att_out"][l]
    nh = ga.shape[0] // HEAD_DIM
    zw = w_main.shape[1]
    dycat = _matmul(dy, w["w_mix_out"][l], "nt", F32, tm=1024, tn=1024, tk=d, name=f"{tag}_dycat")
    dw_out = _matmul(ycat, dy, "tn", F32, tm=1024, tn=1024, tk=1024, name=f"{tag}_dwout")
    dz_conv, dyatt, dwc, dgc, dga = _mixpost_bwd(dycat, z, yatt, cv, wc, gc, ga, name=f"{tag}_dpost")
    qb, kb, dyb = _attn_bwd_operands(qa, ka, dyatt, yatt, lse, t=t, name=f"{tag}_attn_bwd_operands")
    dq, dcq = _fattn_dq(qb, kb, va, dyb, bounds, t=t, name=f"{tag}_attn_dq")
    dk, dv, dck = _fattn_dkv(qb, kb, va, dyb, bounds, t=t, name=f"{tag}_attn_dkv")
    def heads_on_lanes(v):
        return jnp.pad(v.transpose(1, 0, 2).reshape(s, nh), ((0, 0), (0, LANES - nh)))

    dzf, db = _cumsum_bwd(heads_on_lanes(dcq), heads_on_lanes(dck), zf, b, name=f"{tag}_dcumsum")
    dz = jnp.concatenate([dz_conv, dq, dk, dv], axis=1)
    dw_main = _matmul(h, dz, "tn", F32, tm=1024, tn=512, tk=2048, name=f"{tag}_dwin")
    dw_f = _matmul(h, dzf, "tn", F32, tm=1024, tn=LANES, tk=2048, name=f"{tag}_dwin_f")
    dh_f = _matmul(dzf, w_f, "nt", F32, tm=1024, tn=1024, tk=LANES, name=f"{tag}_dh_f")
    dx, dg = _matmul(dz, w_main, "nt", F32, tm=512, tn=1024, tk=zw // 2, addend=dh_f, residual=dy,
                     dnorm=(x, w["g_mix"][l]), name=f"{tag}_dh_dnorm")
    grads = dict(g_mix=dg[0], w_mix_in=jnp.concatenate([dw_main, dw_f[:, :nh]], axis=1), w_conv=dwc[:CONV_K], b_f=db[0, :nh],
                 g_conv_out=dgc[0], g_att_out=dga[0], w_mix_out=dw_out)
    return dx, grads


def _xattn_block_fwd(x, h, mem, w, l, g_next, tag):
    d = x.shape[1]
    mn = _rmsnorm_fwd(mem, w["g_mem"][l], name=f"{tag}_mem_norm")
    q = _matmul(h, w["w_xq"][l], "nn", BF16, tm=1024, tn=1024, tk=d, name=f"{tag}_q")
    kv = _matmul(mn, w["w_xkv"][l], "nn", BF16, tm=1024, tn=1024, tk=d, name=f"{tag}_kv")
    o = _xattn_fwd(q, kv, name=f"{tag}_core")
    y, h_next = _residual_out(o, w["w_xo"][l], x, g_next, f"{tag}_o")
    return y, h_next, (x, h, mn, q, kv, o)


def _xattn_block_bwd(dy, saved, mem, w, l, tag):
    x, h, mn, q, kv, o = saved
    d = x.shape[1]
    do = _matmul(dy, w["w_xo"][l], "nt", BF16, tm=1024, tn=1024, tk=d, name=f"{tag}_do")
    dw_xo = _matmul(o, dy, "tn", F32, tm=1024, tn=1024, tk=1024, name=f"{tag}_dwo")
    dq, dkv = _xattn_bwd(q, kv, do, name=f"{tag}_dcore")
    dw_xq = _matmul(h, dq, "tn", F32, tm=1024, tn=1024, tk=2048, name=f"{tag}_dwq")
    dx, dg = _matmul(dq, w["w_xq"][l], "nt", F32, tm=1024, tn=1024, tk=d, residual=dy, dnorm=(x, w["g_xattn"][l]),
                     name=f"{tag}_dh_dnorm")
    dw_xkv = _matmul(mn, dkv, "tn", F32, tm=1024, tn=1024, tk=1024, name=f"{tag}_dwkv")
    dmn = _matmul(dkv, w["w_xkv"][l], "nt", F32, tm=1024, tn=1024, tk=1024, name=f"{tag}_dmem")
    _, dg_mem = _rmsnorm_bwd(mem, w["g_mem"][l], dmn, None, name=f"{tag}_dmem_norm")
    return dx, dict(g_xattn=dg[0], g_mem=dg_mem, w_xq=dw_xq, w_xkv=dw_xkv, w_xo=dw_xo)


def kernel(x, mem, g_ffn1, w_ffn1_gu, w_ffn1_down, g_mix, w_mix_in, w_conv, b_f, g_conv_out, g_att_out, w_mix_out, g_xattn, g_mem, w_xq, w_xkv, w_xo, g_ffn2, w_ffn2_gu, w_ffn2_down, g_final, loss_target, m_g_ffn1, m_w_ffn1_gu, m_w_ffn1_down, m_g_mix, m_w_mix_in, m_w_conv, m_b_f, m_g_conv_out, m_g_att_out, m_w_mix_out, m_g_xattn, m_g_mem, m_w_xq, m_w_xkv, m_w_xo, m_g_ffn2, m_w_ffn2_gu, m_w_ffn2_down, m_g_final, v_g_ffn1, v_w_ffn1_gu, v_w_ffn1_down, v_g_mix, v_w_mix_in, v_w_conv, v_b_f, v_g_conv_out, v_g_att_out, v_w_mix_out, v_g_xattn, v_g_mem, v_w_xq, v_w_xkv, v_w_xo, v_g_ffn2, v_w_ffn2_gu, v_w_ffn2_down, v_g_final):
    local = dict(zip(WEIGHTS, (g_ffn1, w_ffn1_gu, w_ffn1_down, g_mix, w_mix_in, w_conv, b_f, g_conv_out, g_att_out, w_mix_out,
                               g_xattn, g_mem, w_xq, w_xkv, w_xo, g_ffn2, w_ffn2_gu, w_ffn2_down, g_final)))
    mom1 = dict(zip(WEIGHTS, (m_g_ffn1, m_w_ffn1_gu, m_w_ffn1_down, m_g_mix, m_w_mix_in, m_w_conv, m_b_f, m_g_conv_out,
                              m_g_att_out, m_w_mix_out, m_g_xattn, m_g_mem, m_w_xq, m_w_xkv, m_w_xo, m_g_ffn2, m_w_ffn2_gu,
                              m_w_ffn2_down, m_g_final)))
    mom2 = dict(zip(WEIGHTS, (v_g_ffn1, v_w_ffn1_gu, v_w_ffn1_down, v_g_mix, v_w_mix_in, v_w_conv, v_b_f, v_g_conv_out,
                              v_g_att_out, v_w_mix_out, v_g_xattn, v_g_mem, v_w_xq, v_w_xkv, v_w_xo, v_g_ffn2, v_w_ffn2_gu,
                              v_w_ffn2_down, v_g_final)))
    depth = g_ffn1.shape[0]
    s, d = x.shape[1], x.shape[2]
    w = dict(local)
    w.update(_gather_weights(local))

    xs = x.reshape(s, d)
    mems = mem.reshape(mem.shape[1], d)
    saved = []
    h = _rmsnorm_fwd(xs, w["g_ffn1"][0], name="l0_ffn1_norm")
    for l in range(depth):
        g_after = w["g_ffn1"][l + 1] if l + 1 < depth else None
        xs, h, s1 = _ffn_fwd(xs, h, w["w_ffn1_gu"][l], w["w_ffn1_down"][l], w["g_mix"][l], f"l{l}_ffn1")
        xs, h, s2 = _mix_fwd(xs, h, w, l, w["g_xattn"][l], f"l{l}_mix")
        xs, h, s3 = _xattn_block_fwd(xs, h, mems, w, l, w["g_ffn2"][l], f"l{l}_xattn")
        xs, h, s4 = _ffn_fwd(xs, h, w["w_ffn2_gu"][l], w["w_ffn2_down"][l], g_after, f"l{l}_ffn2")
        saved.append((s1, s2, s3, s4))

    dx, dg_final, sq = _final_loss(xs, g_final, loss_target.reshape(s, d), name="loss_head")
    loss = lax.psum(jnp.sum(sq) * (0.5 / d), MESH_AXES)

    per_layer = []
    for l in reversed(range(depth)):
        s1, s2, s3, s4 = saved[l]
        grads = {}
        dx, grads["g_ffn2"], grads["w_ffn2_gu"], grads["w_ffn2_down"] = _ffn_bwd(
            dx, s4, w["g_ffn2"][l], w["w_ffn2_gu"][l], w["w_ffn2_down"][l], f"l{l}_ffn2")
        dx, g3 = _xattn_block_bwd(dx, s3, mems, w, l, f"l{l}_xattn")
        dx, g2 = _mix_bwd(dx, s2, w, l, f"l{l}_mix")
        dx, grads["g_ffn1"], grads["w_ffn1_gu"], grads["w_ffn1_down"] = _ffn_bwd(
            dx, s1, w["g_ffn1"][l], w["w_ffn1_gu"][l], w["w_ffn1_down"][l], f"l{l}_ffn1")
        grads.update(g2)
        grads.update(g3)
        per_layer.append(grads)
    per_layer.reverse()
    grads = {name: jnp.stack([per_layer[l][name] for l in range(depth)]) for name in WEIGHTS if name != "g_final"}
    grads["g_final"] = dg_final.reshape(d)

    reduced = _reduce_gradients(grads, lax.axis_index("c"))
    deltas, new_m, new_v = {}, {}, {}
    for name in WEIGHTS:
        deltas[name], new_m[name], new_v[name] = _adamw(local[name], reduced[name], mom1[name], mom2[name], name=f"adamw_{name}")
    return (loss, dx.reshape(x.shape), *[reduced[n] for n in WEIGHTS], *[deltas[n] for n in WEIGHTS],
            *[new_m[n] for n in WEIGHTS], *[new_v[n] for n in WEIGHTS])
```

```python
import functools

import jax
import jax.numpy as jnp
from jax import lax
from jax.experimental import pallas as pl
from jax.experimental.pallas import tpu as pltpu

F32 = jnp.float32
BF16 = jnp.bfloat16

EPS = 1e-6
HEAD_DIM = 64
LANES = 128
N_XHEADS = 4
CONV_K = 3
ADAM_LR, ADAM_B1, ADAM_B2, ADAM_EPS, ADAM_WD, ADAM_STEP = 0.001, 0.9, 0.999, 1e-08, 0.01, 10
VMEM_LIMIT_BYTES = 56 * 1024 * 1024
NEG_BIG = -1e30
MESH_AXES = ("x", "y", "c")
N_CHIPS = 4


def _params(*sem):
    return pltpu.CompilerParams(dimension_semantics=sem, vmem_limit_bytes=VMEM_LIMIT_BYTES)


_DIMS = {"nn": (((1,), (0,)), ((), ())), "nt": (((1,), (1,)), ((), ())), "tn": (((0,), (0,)), ((), ()))}


def _matmul(a, b, mode, out_dtype, *, tm, tn, tk, name, alpha=1.0, addend=None, residual=None, norm_gain=None, dnorm=None,
            a_slabs=False, b_slab=None):
    if a_slabs:
        assert mode == "nt"
        n_slab, m, k_slab = a.shape
        k, (n, k2) = n_slab * k_slab, b.shape
        tk = min(tk, k_slab)
        assert k_slab % tk == 0
    elif b_slab is not None:
        assert mode == "tn"
        (k, m), (_, k2, n) = a.shape, b.shape
    elif mode == "nn":
        (m, k), (k2, n) = a.shape, b.shape
    elif mode == "nt":
        (m, k), (n, k2) = a.shape, b.shape
    else:
        (k, m), (k2, n) = a.shape, b.shape
    assert k == k2, (a.shape, b.shape, mode)
    tm, tn, tk = min(tm, m), min(tn, n), min(tk, k)
    assert m % tm == 0 and n % tn == 0 and k % tk == 0, (m, n, k, tm, tn, tk)
    assert (norm_gain is None and dnorm is None) or tn == n
    nk = k // tk
    dims = _DIMS[mode]

    def body(*refs):
        refs = list(refs)
        a_ref, b_ref = refs[:2]
        del refs[:2]
        add_ref = refs.pop(0) if addend is not None else None
        r_ref = refs.pop(0) if residual is not None else None
        g_ref = refs.pop(0) if (norm_gain is not None or dnorm is not None) else None
        x_ref = refs.pop(0) if dnorm is not None else None
        o_ref = refs.pop(0)
        h_ref = refs.pop(0) if norm_gain is not None else None
        dg_ref = refs.pop(0) if dnorm is not None else None
        scratch = refs
        prod = lax.dot_general(a_ref[...].astype(BF16), b_ref[...].astype(BF16), dims, preferred_element_type=F32)

        def finish(acc):
            if alpha != 1.0:
                acc = acc * alpha
            if add_ref is not None:
                acc = acc + add_ref[...]
            if dnorm is not None:
                xf = x_ref[...]
                r = _rms_rows(xf)
                xhat = xf * r
                dxhat = acc * g_ref[...]

                @pl.when((pl.program_id(0) == 0))
                def _():
                    dg_ref[...] = jnp.zeros(dg_ref.shape, F32)

                dg_ref[...] += jnp.sum(acc * xhat, axis=0, keepdims=True)
                acc = r * (dxhat - xhat * jnp.mean(dxhat * xhat, axis=-1, keepdims=True))
            if r_ref is not None:
                acc = acc + r_ref[...].astype(F32)
            o_ref[...] = acc.astype(o_ref.dtype)
            if norm_gain is not None:
                h_ref[...] = (acc * _rms_rows(acc) * g_ref[...]).astype(h_ref.dtype)

        if nk == 1:
            finish(prod)
        else:
            acc_ref = scratch[0]
            kk = pl.program_id(2)

            @pl.when(kk == 0)
            def _():
                acc_ref[...] = prod

            @pl.when(kk > 0)
            def _():
                acc_ref[...] += prod

            @pl.when(kk == nk - 1)
            def _():
                finish(acc_ref[...])

    if mode == "nn":
        a_spec = pl.BlockSpec((tm, tk), lambda i, j, kk: (i, kk))
        b_spec = pl.BlockSpec((tk, tn), lambda i, j, kk: (kk, j))
    elif mode == "nt":
        a_spec = pl.BlockSpec((tm, tk), lambda i, j, kk: (i, kk))
        b_spec = pl.BlockSpec((tn, tk), lambda i, j, kk: (j, kk))
    else:
        a_spec = pl.BlockSpec((tk, tm), lambda i, j, kk: (kk, i))
        b_spec = pl.BlockSpec((tk, tn), lambda i, j, kk: (kk, j))
    if a_slabs:
        per_slab = k_slab // tk
        assert n_slab == 2

        def a_index(i, j, kk):
            second = (kk >= per_slab).astype(jnp.int32)
            return second, i, kk - second * per_slab

        a_spec = pl.BlockSpec((None, tm, tk), a_index)
    if b_slab is not None:
        b_spec = pl.BlockSpec((None, tk, tn), lambda i, j, kk: (b_slab, kk, j))
    o_spec = pl.BlockSpec((tm, tn), lambda i, j, kk: (i, j))
    vec_spec = pl.BlockSpec((1, n), lambda i, j, kk: (0, 0))
    in_specs, args = [a_spec, b_spec], [a, b]
    out_specs, out_shape = [o_spec], [jax.ShapeDtypeStruct((m, n), out_dtype)]
    if addend is not None:
        in_specs.append(o_spec)
        args.append(addend)
    if residual is not None:
        in_specs.append(o_spec)
        args.append(residual)
    if norm_gain is not None:
        in_specs.append(vec_spec)
        args.append(norm_gain.reshape(1, n))
        out_specs.append(o_spec)
        out_shape.append(jax.ShapeDtypeStruct((m, n), BF16))
    if dnorm is not None:
        in_specs += [vec_spec, o_spec]
        args += [dnorm[1].reshape(1, n), dnorm[0]]
        out_specs.append(vec_spec)
        out_shape.append(jax.ShapeDtypeStruct((1, n), F32))
    outs = pl.pallas_call(
        body, name=name, grid=(m // tm, n // tn, nk), in_specs=in_specs, out_specs=out_specs, out_shape=out_shape,
        scratch_shapes=[pltpu.VMEM((tm, tn), F32)] if nk > 1 else [],
        compiler_params=_params(*(("arbitrary",) * 3 if dnorm is not None else ("parallel", "parallel", "arbitrary"))),
    )(*args)
    return outs[0] if len(outs) == 1 else tuple(outs)


def _row_tile(rows, want):
    t = min(rows, want)
    assert rows % t == 0, (rows, t)
    return t


def _rmsnorm_fwd(x, g, *, name, tr=1024):
    s, d = x.shape
    tr = _row_tile(s, tr)

    def body(x_ref, g_ref, o_ref):
        xf = x_ref[...]
        r = lax.rsqrt(jnp.mean(xf * xf, axis=-1, keepdims=True) + EPS)
        o_ref[...] = (xf * r * g_ref[...]).astype(o_ref.dtype)

    return pl.pallas_call(
        body, name=name, grid=(s // tr,),
        in_specs=[pl.BlockSpec((tr, d), lambda i: (i, 0)), pl.BlockSpec((1, d), lambda i: (0, 0))],
        out_specs=pl.BlockSpec((tr, d), lambda i: (i, 0)),
        out_shape=jax.ShapeDtypeStruct((s, d), BF16),
        compiler_params=_params("parallel"),
    )(x, g.reshape(1, d))


def _rmsnorm_bwd(x, g, dh, dres, *, name, tr=512):
    s, d = x.shape
    tr = _row_tile(s, tr)

    def body(x_ref, g_ref, dh_ref, *rest):
        if dres is None:
            dx_ref, dg_ref = rest
        else:
            dres_ref, dx_ref, dg_ref = rest
        xf = x_ref[...]
        r = lax.rsqrt(jnp.mean(xf * xf, axis=-1, keepdims=True) + EPS)
        xhat = xf * r
        dhf = dh_ref[...].astype(F32)
        dxhat = dhf * g_ref[...]
        dx = r * (dxhat - xhat * jnp.mean(dxhat * xhat, axis=-1, keepdims=True))
        if dres is not None:
            dx = dx + dres_ref[...]
        dx_ref[...] = dx

        @pl.when(pl.program_id(0) == 0)
        def _():
            dg_ref[...] = jnp.zeros_like(dg_ref)

        dg_ref[...] += jnp.sum(dhf * xhat, axis=0, keepdims=True)

    row = pl.BlockSpec((tr, d), lambda i: (i, 0))
    vec = pl.BlockSpec((1, d), lambda i: (0, 0))
    in_specs, args = [row, vec, row], [x, g.reshape(1, d), dh]
    if dres is not None:
        in_specs.append(row)
        args.append(dres)
    dx, dg = pl.pallas_call(
        body, name=name, grid=(s // tr,), in_specs=in_specs, out_specs=[row, vec],
        out_shape=[jax.ShapeDtypeStruct((s, d), F32), jax.ShapeDtypeStruct((1, d), F32)],
        compiler_params=_params("arbitrary"),
    )(*args)
    return dx, dg.reshape(d)


def _gate_up_act(h, w_gu, *, name, tm=2048, tn=256):
    s, d = h.shape
    f = w_gu.shape[1] // 2
    tm, tn = min(tm, s), min(tn, f)
    assert s % tm == 0 and f % tn == 0
    nf = f // tn

    def body(h_ref, wg_ref, wu_ref, gu_ref, a_ref):
        hb = h_ref[...]
        gate = _dot(hb, wg_ref[...], _NN)
        up = _dot(hb, wu_ref[...], _NN)
        gu_ref[0] = gate.astype(gu_ref.dtype)
        gu_ref[1] = up.astype(gu_ref.dtype)
        a_ref[...] = (gate * jax.nn.sigmoid(gate) * up).astype(a_ref.dtype)

    return pl.pallas_call(
        body, name=name, grid=(s // tm, nf),
        in_specs=[pl.BlockSpec((tm, d), lambda i, j: (i, 0)), pl.BlockSpec((d, tn), lambda i, j: (0, j)),
                  pl.BlockSpec((d, tn), lambda i, j: (0, j + nf))],
        out_specs=[pl.BlockSpec((2, tm, tn), lambda i, j: (0, i, j)), pl.BlockSpec((tm, tn), lambda i, j: (i, j))],
        out_shape=[jax.ShapeDtypeStruct((2, s, f), BF16), jax.ShapeDtypeStruct((s, f), BF16)],
        compiler_params=_params("parallel", "parallel"),
    )(h, w_gu, w_gu)


def _down_bwd_act(dy, w_down, gu, alpha, *, name, tm=512):
    s, d = dy.shape
    f = w_down.shape[0]
    tm, tn = min(tm, s), f // 2
    assert s % tm == 0 and tn % LANES == 0

    def body(dy_ref, w_ref, gu_ref, dgu_ref):
        da = _dot(dy_ref[...].astype(BF16), w_ref[...], _NT) * alpha
        gate = gu_ref[0].astype(F32)
        up = gu_ref[1].astype(F32)
        sig = jax.nn.sigmoid(gate)
        silu = gate * sig
        dgu_ref[0] = (da * up * (sig + silu * (1.0 - sig))).astype(dgu_ref.dtype)
        dgu_ref[1] = (da * silu).astype(dgu_ref.dtype)

    slab = pl.BlockSpec((2, tm, tn), lambda i, j: (0, i, j))
    return pl.pallas_call(
        body, name=name, grid=(s // tm, f // tn),
        in_specs=[pl.BlockSpec((tm, d), lambda i, j: (i, 0)), pl.BlockSpec((tn, d), lambda i, j: (j, 0)), slab],
        out_specs=slab, out_shape=jax.ShapeDtypeStruct((2, s, f), BF16),
        compiler_params=_params("parallel", "parallel"),
    )(dy, w_down, gu)


_NT = (((1,), (1,)), ((), ()))
_NN = (((1,), (0,)), ((), ()))
_TN = (((0,), (0,)), ((), ()))
_QK_SCALE = HEAD_DIM ** -0.5


def _dot(a, b, dims):
    return lax.dot_general(a, b, dims, preferred_element_type=F32)


SKIP_BELOW = 50.0
_SMEM = pl.BlockSpec(memory_space=pltpu.SMEM)


def _attn_bounds(z, c, *, qcol_units, kcol_units, aw, t, name):
    s = z.shape[0]
    nq = s // t
    nh = aw // HEAD_DIM

    def body(q_ref, k_ref, o_ref):
        d = lax.broadcasted_iota(jnp.int32, (aw, LANES), 0)
        hh = lax.broadcasted_iota(jnp.int32, (aw, LANES), 1)
        onehot = ((d >= hh * HEAD_DIM) & (d < (hh + 1) * HEAD_DIM)).astype(BF16)
        for r, ref in enumerate((q_ref, k_ref)):
            v = ref[...].astype(F32)
            sq = _dot((v * v).astype(BF16), onehot, _NN)
            o_ref[r:r + 1, :] = jnp.max(sq, axis=0, keepdims=True)
        o_ref[2:, :] = jnp.zeros((HALO - 2, LANES), F32)

    sq = pl.pallas_call(
        body, name=name, grid=(nq,),
        in_specs=[pl.BlockSpec((t, aw), lambda i: (i, qcol_units)), pl.BlockSpec((t, aw), lambda i: (i, kcol_units))],
        out_specs=pl.BlockSpec((None, HALO, LANES), lambda i: (i, 0, 0)),
        out_shape=jax.ShapeDtypeStruct((nq, HALO, LANES), F32),
        compiler_params=_params("parallel"),
    )(z, z)
    norms = jnp.sqrt(sq[:, :2, :nh]) * 1.01
    qn = (norms[:, 0, :] * _QK_SCALE).T.reshape(-1)
    kn = norms[:, 1, :].T.reshape(-1)
    cs = c[0::t, :nh].T.reshape(-1)
    ce = c[t - 1::t, :nh].T.reshape(-1)
    return qn, kn, cs, ce


def _block_active(bounds, head, i, j, nq):
    qn_ref, kn_ref, cs_ref, ce_ref = bounds
    qi = qn_ref[head * nq + i]
    upper = qi * kn_ref[head * nq + j] + (cs_ref[head * nq + i] - ce_ref[head * nq + j])
    lower = -(qi * kn_ref[head * nq + i])
    return upper - lower > -SKIP_BELOW


def _for_active_heads(bounds, pair, i, j, nq, head_step):
    act = [_block_active(bounds, 2 * pair + h, i, j, nq) for h in range(2)]

    def run(heads):
        for h in heads:
            head_step(h)

    pl.when(act[0] & act[1])(functools.partial(run, (0, 1)))
    pl.when(act[0] & jnp.logical_not(act[1]))(functools.partial(run, (0,)))
    pl.when(jnp.logical_not(act[0]) & act[1])(functools.partial(run, (1,)))


def _attn_operands(z, ccol, *, qcol, kcol, vcol, t, name):
    s = z.shape[0]
    npairs = ccol.shape[0]

    def body(q_ref, k_ref, v_ref, c_ref, qa_ref, ka_ref, va_ref):
        lane = lax.broadcasted_iota(jnp.int32, (t, LANES), 1)
        q2 = q_ref[...] * jnp.asarray(_QK_SCALE, BF16)
        k2, v2 = k_ref[...], v_ref[...]
        one, zero = jnp.ones((t, LANES), BF16), jnp.zeros((t, LANES), BF16)
        for h in range(2):
            base = HEAD_DIM * (1 - h)
            mine = (lane < HEAD_DIM) if h == 0 else (lane >= HEAD_DIM)
            c = c_ref[:, h:h + 1]
            hi = c.astype(BF16)
            r1 = c - hi.astype(F32)
            mid = r1.astype(BF16)
            lo = (r1 - mid.astype(F32)).astype(BF16)
            qa, ka = jnp.where(mine, q2, zero), jnp.where(mine, k2, zero)
            for r, word in enumerate((hi, mid, lo)):
                qa = jnp.where(lane == base + r, word, qa)
                qa = jnp.where(lane == base + 3 + r, one, qa)
                ka = jnp.where(lane == base + r, one, ka)
                ka = jnp.where(lane == base + 3 + r, -word, ka)
            qa_ref[h] = qa
            ka_ref[h] = ka
            va_ref[h] = jnp.where((lane >= base) & (lane < base + 3), one, jnp.where(mine, v2, zero))

    out = jax.ShapeDtypeStruct((npairs, 2, s, LANES), BF16)
    blk = pl.BlockSpec((None, 2, t, LANES), lambda p, i: (p, 0, i, 0))
    return pl.pallas_call(
        body, name=name, grid=(npairs, s // t),
        in_specs=[pl.BlockSpec((t, LANES), lambda p, i: (i, qcol + p)), pl.BlockSpec((t, LANES), lambda p, i: (i, kcol + p)),
                  pl.BlockSpec((t, LANES), lambda p, i: (i, vcol + p)), pl.BlockSpec((None, t, 2), lambda p, i: (p, i, 0))],
        out_specs=[blk, blk, blk], out_shape=[out, out, out],
        compiler_params=_params("parallel", "parallel"),
    )(z, z, z, ccol)


def _fattn_fwd(qa, ka, va, bounds, *, t, name):
    npairs, _, s, _ = qa.shape
    nq = s // t
    reps = t // LANES

    def body(qn_ref, kn_ref, cs_ref, ce_ref, q_ref, k_ref, v_ref, o_ref, lse_ref, m_scr, acc_scr):
        pair, i = pl.program_id(0), pl.program_id(1)
        m_scr[...] = jnp.full(m_scr.shape, NEG_BIG, F32)
        acc_scr[...] = jnp.zeros(acc_scr.shape, F32)

        def head_step(h, j, diagonal):
            off = pl.multiple_of(j * t, t)
            sc = _dot(q_ref[h], k_ref[h, pl.ds(off, t), :], _NT)
            if diagonal:
                row = lax.broadcasted_iota(jnp.int32, (t, t), 0)
                col = lax.broadcasted_iota(jnp.int32, (t, t), 1)
                sc = jnp.where(row >= col, sc, NEG_BIG)
            m_old = m_scr[h]
            m_new = jnp.maximum(m_old, jnp.max(sc, axis=1, keepdims=True))
            p = jnp.exp(sc - jnp.tile(m_new, (1, reps)))
            acc_scr[h] = acc_scr[h] * jnp.exp(m_old - m_new) + _dot(p.astype(BF16), v_ref[h, pl.ds(off, t), :], _NN)
            m_scr[h] = m_new

        def loop_body(j, carry):
            _for_active_heads((qn_ref, kn_ref, cs_ref, ce_ref), pair, i, j, nq, lambda h: head_step(h, j, False))
            return carry

        lax.fori_loop(0, i, loop_body, 0)
        for h in range(2):
            head_step(h, i, True)
        is_a = lax.broadcasted_iota(jnp.int32, (t, LANES), 1) < HEAD_DIM
        l = (acc_scr[0][:, HEAD_DIM:HEAD_DIM + 1], acc_scr[1][:, 0:1])
        o_ref[...] = jnp.where(is_a, acc_scr[0] / l[0], acc_scr[1] / l[1])
        lse_ref[:, 0:1] = m_scr[0][:, 0:1] + jnp.log(l[0])
        lse_ref[:, 1:2] = m_scr[1][:, 0:1] + jnp.log(l[1])

    return pl.pallas_call(
        body, name=name, grid=(npairs, nq),
        in_specs=[_SMEM, _SMEM, _SMEM, _SMEM,
                  pl.BlockSpec((None, 2, t, LANES), lambda p, i: (p, 0, i, 0)),
                  pl.BlockSpec((None, 2, s, LANES), lambda p, i: (p, 0, 0, 0)),
                  pl.BlockSpec((None, 2, s, LANES), lambda p, i: (p, 0, 0, 0))],
        out_specs=[pl.BlockSpec((t, LANES), lambda p, i: (i, p)), pl.BlockSpec((None, t, 2), lambda p, i: (p, i, 0))],
        out_shape=[jax.ShapeDtypeStruct((s, npairs * LANES), F32), jax.ShapeDtypeStruct((npairs, s, 2), F32)],
        scratch_shapes=[pltpu.VMEM((2, t, LANES), F32), pltpu.VMEM((2, t, LANES), F32)],
        compiler_params=_params("parallel", "arbitrary"),
    )(*bounds, qa, ka, va)


def _split3(x):
    hi = x.astype(BF16)
    r1 = x - hi.astype(F32)
    mid = r1.astype(BF16)
    return hi, mid, (r1 - mid.astype(F32)).astype(BF16)


def _attn_bwd_operands(qa, ka, dy, y, lse, *, t, name):
    npairs, _, s, _ = qa.shape

    def body(qa_ref, ka_ref, dy_ref, y_ref, lse_ref, qb_ref, kb_ref, dyb_ref):
        lane = lax.broadcasted_iota(jnp.int32, (t, LANES), 1)
        dyf = dy_ref[...]
        prod = dyf * y_ref[...]
        dyh = dyf.astype(BF16)
        one, zero = jnp.ones((t, LANES), BF16), jnp.zeros((t, LANES), BF16)
        for h in range(2):
            base = HEAD_DIM * (1 - h)
            mine = (lane < HEAD_DIM) if h == 0 else (lane >= HEAD_DIM)
            delta = jnp.sum(jnp.where(mine, prod, 0.0), axis=1, keepdims=True)
            qb, kb, dyb = qa_ref[h], ka_ref[h], jnp.where(mine, dyh, zero)
            for r, (lw, dw) in enumerate(zip(_split3(lse_ref[:, h:h + 1]), _split3(delta))):
                qb = jnp.where(lane == base + 6 + r, -lw, qb)
                kb = jnp.where(lane == base + 6 + r, one, kb)
                dyb = jnp.where(lane == base + r, -dw, dyb)
            qb_ref[h] = qb
            kb_ref[h] = kb
            dyb_ref[h] = dyb

    out = jax.ShapeDtypeStruct((npairs, 2, s, LANES), BF16)
    blk = pl.BlockSpec((None, 2, t, LANES), lambda p, i: (p, 0, i, 0))
    tile = pl.BlockSpec((t, LANES), lambda p, i: (i, p))
    return pl.pallas_call(
        body, name=name, grid=(npairs, s // t),
        in_specs=[blk, blk, tile, tile, pl.BlockSpec((None, t, 2), lambda p, i: (p, i, 0))],
        out_specs=[blk, blk, blk], out_shape=[out, out, out],
        compiler_params=_params("parallel", "parallel"),
    )(qa, ka, dy, y, lse)


def _diag_mask(sc, t, queries_on_rows):
    row = lax.broadcasted_iota(jnp.int32, (t, t), 0)
    col = lax.broadcasted_iota(jnp.int32, (t, t), 1)
    return jnp.where((row >= col) if queries_on_rows else (col >= row), sc, NEG_BIG)


def _fattn_dq(qb, kb, va, dyb, bounds, *, t, name):
    npairs, _, s, _ = qb.shape
    nq = s // t

    def body(qn_ref, kn_ref, cs_ref, ce_ref, q_ref, k_ref, v_ref, dy_ref, dq_ref, dcq_ref, acc_scr):
        pair, i = pl.program_id(0), pl.program_id(1)
        acc_scr[...] = jnp.zeros(acc_scr.shape, F32)

        def head_step(h, j, diagonal):
            off = pl.multiple_of(j * t, t)
            kj = k_ref[h, pl.ds(off, t), :]
            sc = _dot(q_ref[h], kj, _NT)
            if diagonal:
                sc = _diag_mask(sc, t, True)
            ds = jnp.exp(sc) * _dot(dy_ref[h], v_ref[h, pl.ds(off, t), :], _NT)
            acc_scr[h] += _dot(ds.astype(BF16), kj, _NN)

        def loop_body(j, carry):
            _for_active_heads((qn_ref, kn_ref, cs_ref, ce_ref), pair, i, j, nq, lambda h: head_step(h, j, False))
            return carry

        lax.fori_loop(0, i, loop_body, 0)
        for h in range(2):
            head_step(h, i, True)
        is_a = lax.broadcasted_iota(jnp.int32, (t, LANES), 1) < HEAD_DIM
        dq_ref[...] = (jnp.where(is_a, acc_scr[0], acc_scr[1]) * _QK_SCALE).astype(dq_ref.dtype)
        dcq_ref[:, 0:1] = acc_scr[0][:, HEAD_DIM:HEAD_DIM + 1]
        dcq_ref[:, 1:2] = acc_scr[1][:, 0:1]

    tile2 = pl.BlockSpec((None, 2, t, LANES), lambda p, i: (p, 0, i, 0))
    whole = pl.BlockSpec((None, 2, s, LANES), lambda p, i: (p, 0, 0, 0))
    return pl.pallas_call(
        body, name=name, grid=(npairs, nq),
        in_specs=[_SMEM, _SMEM, _SMEM, _SMEM, tile2, whole, whole, tile2],
        out_specs=[pl.BlockSpec((t, LANES), lambda p, i: (i, p)), pl.BlockSpec((None, t, 2), lambda p, i: (p, i, 0))],
        out_shape=[jax.ShapeDtypeStruct((s, npairs * LANES), BF16), jax.ShapeDtypeStruct((npairs, s, 2), F32)],
        scratch_shapes=[pltpu.VMEM((2, t, LANES), F32)],
        compiler_params=_params("parallel", "arbitrary"),
    )(*bounds, qb, kb, va, dyb)


def _fattn_dkv(qb, kb, va, dyb, bounds, *, t, name):
    npairs, _, s, _ = qb.shape
    nq = s // t

    def body(qn_ref, kn_ref, cs_ref, ce_ref, k_ref, v_ref, q_ref, dy_ref, dk_ref, dv_ref, dc_ref, dk_scr, dv_scr):
        pair, j = pl.program_id(0), pl.program_id(1)
        dk_scr[...] = jnp.zeros(dk_scr.shape, F32)
        dv_scr[...] = jnp.zeros(dv_scr.shape, F32)

        def head_step(h, i, diagonal):
            off = pl.multiple_of(i * t, t)
            qi = q_ref[h, pl.ds(off, t), :]
            dyi = dy_ref[h, pl.ds(off, t), :]
            st = _dot(k_ref[h], qi, _NT)
            if diagonal:
                st = _diag_mask(st, t, False)
            pt = jnp.exp(st)
            dv_scr[h] += _dot(pt.astype(BF16), dyi, _NN)
            dst = pt * _dot(v_ref[h], dyi, _NT)
            dk_scr[h] += _dot(dst.astype(BF16), qi, _NN)

        def loop_body(i, carry):
            _for_active_heads((qn_ref, kn_ref, cs_ref, ce_ref), pair, i, j, nq, lambda h: head_step(h, i, False))
            return carry

        for h in range(2):
            head_step(h, j, True)
        lax.fori_loop(j + 1, nq, loop_body, 0)
        is_a = lax.broadcasted_iota(jnp.int32, (t, LANES), 1) < HEAD_DIM
        dk_ref[...] = jnp.where(is_a, dk_scr[0], dk_scr[1]).astype(dk_ref.dtype)
        dv_ref[...] = jnp.where(is_a, dv_scr[0], dv_scr[1]).astype(dv_ref.dtype)
        dc_ref[:, 0:1] = -dk_scr[0][:, HEAD_DIM + 3:HEAD_DIM + 4]
        dc_ref[:, 1:2] = -dk_scr[1][:, 3:4]

    tile2 = pl.BlockSpec((None, 2, t, LANES), lambda p, j: (p, 0, j, 0))
    whole = pl.BlockSpec((None, 2, s, LANES), lambda p, j: (p, 0, 0, 0))
    tile = pl.BlockSpec((t, LANES), lambda p, j: (j, p))
    return pl.pallas_call(
        body, name=name, grid=(npairs, nq),
        in_specs=[_SMEM, _SMEM, _SMEM, _SMEM, tile2, tile2, whole, whole],
        out_specs=[tile, tile, pl.BlockSpec((None, t, 2), lambda p, j: (p, j, 0))],
        out_shape=[jax.ShapeDtypeStruct((s, npairs * LANES), BF16), jax.ShapeDtypeStruct((s, npairs * LANES), BF16),
                   jax.ShapeDtypeStruct((npairs, s, 2), F32)],
        scratch_shapes=[pltpu.VMEM((2, t, LANES), F32), pltpu.VMEM((2, t, LANES), F32)],
        compiler_params=_params("parallel", "arbitrary"),
    )(*bounds, kb, va, qb, dyb)


def _log_sigmoid(x):
    return jnp.minimum(x, 0.0) - jnp.log(1.0 + jnp.exp(-jnp.abs(x)))


def _cumsum_fwd(zf, b, *, name, t=512):
    s, w = zf.shape
    t = _row_tile(s, t)

    def body(zf_ref, b_ref, c_ref, carry):
        @pl.when(pl.program_id(0) == 0)
        def _():
            carry[...] = jnp.zeros(carry.shape, F32)

        lf = _log_sigmoid(zf_ref[...] + b_ref[...])
        row = lax.broadcasted_iota(jnp.int32, (t, t), 0)
        col = lax.broadcasted_iota(jnp.int32, (t, t), 1)
        tri = (row >= col).astype(F32)
        c = lax.dot_general(tri, lf, _NN, precision=lax.Precision.HIGHEST, preferred_element_type=F32) + carry[...]
        c_ref[...] = c
        carry[...] = c[t - 1:t, :]

    return pl.pallas_call(
        body, name=name, grid=(s // t,),
        in_specs=[pl.BlockSpec((t, w), lambda i: (i, 0)), pl.BlockSpec((1, w), lambda i: (0, 0))],
        out_specs=pl.BlockSpec((t, w), lambda i: (i, 0)),
        out_shape=jax.ShapeDtypeStruct((s, w), F32),
        scratch_shapes=[pltpu.VMEM((1, w), F32)],
        compiler_params=_params("arbitrary"),
    )(zf, b)


def _cumsum_bwd(dcq, dck, zf, b, *, name, t=512):
    s, w = zf.shape
    t = _row_tile(s, t)
    nb = s // t

    def body(dcq_ref, dck_ref, zf_ref, b_ref, dzf_ref, db_ref, carry):
        @pl.when(pl.program_id(0) == 0)
        def _():
            carry[...] = jnp.zeros(carry.shape, F32)
            db_ref[...] = jnp.zeros(db_ref.shape, F32)

        row = lax.broadcasted_iota(jnp.int32, (t, t), 0)
        col = lax.broadcasted_iota(jnp.int32, (t, t), 1)
        tri = (row <= col).astype(F32)
        dc = dcq_ref[...] + dck_ref[...]
        dlf = lax.dot_general(tri, dc, _NN, precision=lax.Precision.HIGHEST, preferred_element_type=F32) + carry[...]
        carry[...] = dlf[0:1, :]
        dzf = dlf * jax.nn.sigmoid(-(zf_ref[...] + b_ref[...]))
        dzf_ref[...] = dzf
        db_ref[...] += jnp.sum(dzf, axis=0, keepdims=True)

    blk = pl.BlockSpec((t, w), lambda i: (nb - 1 - i, 0))
    vec = pl.BlockSpec((1, w), lambda i: (0, 0))
    return pl.pallas_call(
        body, name=name, grid=(nb,), in_specs=[blk, blk, blk, vec], out_specs=[blk, vec],
        out_shape=[jax.ShapeDtypeStruct((s, w), F32), jax.ShapeDtypeStruct((1, w), F32)],
        scratch_shapes=[pltpu.VMEM((1, w), F32)],
        compiler_params=_params("arbitrary"),
    )(dcq, dck, zf, b)


HALO = 8


def _rms_rows(v):
    return lax.rsqrt(jnp.mean(v * v, axis=-1, keepdims=True) + EPS)


def _mixpost_fwd(z, yatt, wconv, gc, ga, *, name, tr=512):
    s = z.shape[0]
    cw, aw = gc.shape[-1], ga.shape[-1]
    tr = _row_tile(s, tr)

    def body(zb_ref, zc_ref, zv_ref, ya_ref, w_ref, gc_ref, ga_ref, ycat_ref, cv_ref, u_scr):
        @pl.when(pl.program_id(0) == 0)
        def _():
            u_scr[0:HALO, :] = jnp.zeros((HALO, cw), F32)

        u = zc_ref[...].astype(F32) * zv_ref[...].astype(F32)
        u_scr[HALO:HALO + tr, :] = u
        cv = w_ref[0:1, :] * u_scr[HALO - 2:HALO - 2 + tr, :] + w_ref[1:2, :] * u_scr[HALO - 1:HALO - 1 + tr, :] + w_ref[2:3, :] * u
        u_scr[0:HALO, :] = u_scr[tr:tr + HALO, :]
        cv_ref[...] = cv
        yc = zb_ref[...].astype(F32) * cv
        ya = ya_ref[...]
        ycat_ref[:, :cw] = (yc * _rms_rows(yc) * gc_ref[...]).astype(ycat_ref.dtype)
        ycat_ref[:, cw:] = (ya * _rms_rows(ya) * ga_ref[...]).astype(ycat_ref.dtype)

    return pl.pallas_call(
        body, name=name, grid=(s // tr,),
        in_specs=[
            pl.BlockSpec((tr, cw), lambda i: (i, 0)), pl.BlockSpec((tr, cw), lambda i: (i, 1)),
            pl.BlockSpec((tr, cw), lambda i: (i, 2)), pl.BlockSpec((tr, aw), lambda i: (i, 0)),
            pl.BlockSpec((HALO, cw), lambda i: (0, 0)), pl.BlockSpec((1, cw), lambda i: (0, 0)),
            pl.BlockSpec((1, aw), lambda i: (0, 0)),
        ],
        out_specs=[pl.BlockSpec((tr, cw + aw), lambda i: (i, 0)), pl.BlockSpec((tr, cw), lambda i: (i, 0))],
        out_shape=[jax.ShapeDtypeStruct((s, cw + aw), BF16), jax.ShapeDtypeStruct((s, cw), F32)],
        scratch_shapes=[pltpu.VMEM((tr + HALO, cw), F32)],
        compiler_params=_params("arbitrary"),
    )(z, z, z, yatt, wconv, gc.reshape(1, cw), ga.reshape(1, aw))


def _mixpost_bwd(dycat, z, yatt, cv, wconv, gc, ga, *, name, tr=512):
    s = z.shape[0]
    cw, aw = gc.shape[-1], ga.shape[-1]
    tr = _row_tile(s, tr)
    nb = s // tr

    def body(dy_ref, zb_ref, zc_ref, zv_ref, ya_ref, cv_ref, w_ref, gc_ref, ga_ref,
             dz_ref, dya_ref, dw_ref, dgc_ref, dga_ref, d_scr):
        @pl.when(pl.program_id(0) == 0)
        def _():
            d_scr[tr:tr + HALO, :] = jnp.zeros((HALO, cw), F32)
            dw_ref[...] = jnp.zeros(dw_ref.shape, F32)
            dgc_ref[...] = jnp.zeros(dgc_ref.shape, F32)
            dga_ref[...] = jnp.zeros(dga_ref.shape, F32)

        zb, zc, zv = zb_ref[...].astype(F32), zc_ref[...].astype(F32), zv_ref[...].astype(F32)
        cvv = cv_ref[...]

        def norm_bwd(v, dn, g):
            r = _rms_rows(v)
            vh = v * r
            dvh = dn * g
            return r * (dvh - vh * jnp.mean(dvh * vh, axis=-1, keepdims=True)), jnp.sum(dn * vh, axis=0, keepdims=True)

        dyc, dgc = norm_bwd(zb * cvv, dy_ref[:, :cw], gc_ref[...])
        dya, dga = norm_bwd(ya_ref[...], dy_ref[:, cw:], ga_ref[...])
        dgc_ref[...] += dgc
        dga_ref[...] += dga
        dya_ref[...] = dya
        dcv = dyc * zb
        d_scr[0:tr, :] = dcv
        d1 = d_scr[1:tr + 1, :]
        d2 = d_scr[2:tr + 2, :]
        du = w_ref[2:3, :] * dcv + w_ref[1:2, :] * d1 + w_ref[0:1, :] * d2
        u = zc * zv
        dw_ref[0:1, :] += jnp.sum(u * d2, axis=0, keepdims=True)
        dw_ref[1:2, :] += jnp.sum(u * d1, axis=0, keepdims=True)
        dw_ref[2:3, :] += jnp.sum(u * dcv, axis=0, keepdims=True)
        d_scr[tr:tr + HALO, :] = d_scr[0:HALO, :]
        dz_ref[:, :cw] = (dyc * cvv).astype(dz_ref.dtype)
        dz_ref[:, cw:2 * cw] = (du * zv).astype(dz_ref.dtype)
        dz_ref[:, 2 * cw:] = (du * zc).astype(dz_ref.dtype)

    def rows(width, colblk=0):
        return pl.BlockSpec((tr, width), lambda i: (nb - 1 - i, colblk))

    def fixed(r, width):
        return pl.BlockSpec((r, width), lambda i: (0, 0))

    return pl.pallas_call(
        body, name=name, grid=(nb,),
        in_specs=[rows(cw + aw), rows(cw, 0), rows(cw, 1), rows(cw, 2), rows(aw), rows(cw),
                  fixed(HALO, cw), fixed(1, cw), fixed(1, aw)],
        out_specs=[rows(3 * cw), rows(aw), fixed(HALO, cw), fixed(1, cw), fixed(1, aw)],
        out_shape=[jax.ShapeDtypeStruct((s, 3 * cw), BF16), jax.ShapeDtypeStruct((s, aw), F32),
                   jax.ShapeDtypeStruct((HALO, cw), F32), jax.ShapeDtypeStruct((1, cw), F32),
                   jax.ShapeDtypeStruct((1, aw), F32)],
        scratch_shapes=[pltpu.VMEM((tr + HALO, cw), F32)],
        compiler_params=_params("arbitrary"),
    )(dycat, z, z, z, yatt, cv, wconv, gc.reshape(1, cw), ga.reshape(1, aw))


def _xattn_fwd(q, kv, *, name, tq=1024):
    s, d = q.shape
    m = kv.shape[0]
    dh = d // N_XHEADS
    scale = dh ** -0.5
    tq = _row_tile(s, tq)

    def body(q_ref, kv_ref, o_ref):
        for h in range(N_XHEADS):
            lo, hi = h * dh, (h + 1) * dh
            sc = _dot(q_ref[:, lo:hi], kv_ref[:, lo:hi], _NT) * scale
            p = jnp.exp(sc - jnp.max(sc, axis=1, keepdims=True))
            o = _dot(p.astype(BF16), kv_ref[:, d + lo:d + hi], _NN) / jnp.sum(p, axis=1, keepdims=True)
            o_ref[:, lo:hi] = o.astype(o_ref.dtype)

    return pl.pallas_call(
        body, name=name, grid=(s // tq,),
        in_specs=[pl.BlockSpec((tq, d), lambda i: (i, 0)), pl.BlockSpec((m, 2 * d), lambda i: (0, 0))],
        out_specs=pl.BlockSpec((tq, d), lambda i: (i, 0)),
        out_shape=jax.ShapeDtypeStruct((s, d), BF16),
        compiler_params=_params("parallel"),
    )(q, kv)


def _xattn_bwd(q, kv, do, *, name, tq=1024):
    s, d = q.shape
    m = kv.shape[0]
    dh = d // N_XHEADS
    scale = dh ** -0.5
    tq = _row_tile(s, tq)

    def body(q_ref, kv_ref, do_ref, dq_ref, dkv_ref):
        @pl.when(pl.program_id(0) == 0)
        def _():
            dkv_ref[...] = jnp.zeros(dkv_ref.shape, F32)

        for h in range(N_XHEADS):
            lo, hi = h * dh, (h + 1) * dh
            qh, kh, vh, doh = q_ref[:, lo:hi], kv_ref[:, lo:hi], kv_ref[:, d + lo:d + hi], do_ref[:, lo:hi]
            sc = _dot(qh, kh, _NT) * scale
            e = jnp.exp(sc - jnp.max(sc, axis=1, keepdims=True))
            p = e / jnp.sum(e, axis=1, keepdims=True)
            dp = _dot(doh, vh, _NT)
            ds = p * (dp - jnp.sum(dp * p, axis=1, keepdims=True))
            dsb = ds.astype(BF16)
            dq_ref[:, lo:hi] = (_dot(dsb, kh, _NN) * scale).astype(dq_ref.dtype)
            dkv_ref[:, lo:hi] += _dot(dsb, qh, _TN) * scale
            dkv_ref[:, d + lo:d + hi] += _dot(p.astype(BF16), doh, _TN)

    return pl.pallas_call(
        body, name=name, grid=(s // tq,),
        in_specs=[pl.BlockSpec((tq, d), lambda i: (i, 0)), pl.BlockSpec((m, 2 * d), lambda i: (0, 0)),
                  pl.BlockSpec((tq, d), lambda i: (i, 0))],
        out_specs=[pl.BlockSpec((tq, d), lambda i: (i, 0)), pl.BlockSpec((m, 2 * d), lambda i: (0, 0))],
        out_shape=[jax.ShapeDtypeStruct((s, d), BF16), jax.ShapeDtypeStruct((m, 2 * d), F32)],
        compiler_params=_params("arbitrary"),
    )(q, kv, do)


def _final_loss(x, g, target, *, name, tr=512):
    s, d = x.shape
    tr = _row_tile(s, tr)

    def body(x_ref, g_ref, t_ref, dx_ref, dg_ref, sq_ref):
        @pl.when(pl.program_id(0) == 0)
        def _():
            dg_ref[...] = jnp.zeros(dg_ref.shape, F32)
            sq_ref[...] = jnp.zeros(sq_ref.shape, F32)

        xf = x_ref[...]
        r = _rms_rows(xf)
        xhat = xf * r
        err = xhat * g_ref[...] - t_ref[...]
        sq_ref[...] += jnp.sum(err * err, axis=0, keepdims=True)
        dy = err * (1.0 / d)
        dg_ref[...] += jnp.sum(dy * xhat, axis=0, keepdims=True)
        dxhat = dy * g_ref[...]
        dx_ref[...] = r * (dxhat - xhat * jnp.mean(dxhat * xhat, axis=-1, keepdims=True))

    row = pl.BlockSpec((tr, d), lambda i: (i, 0))
    vec = pl.BlockSpec((1, d), lambda i: (0, 0))
    return pl.pallas_call(
        body, name=name, grid=(s // tr,), in_specs=[row, vec, row], out_specs=[row, vec, vec],
        out_shape=[jax.ShapeDtypeStruct((s, d), F32), jax.ShapeDtypeStruct((1, d), F32), jax.ShapeDtypeStruct((1, d), F32)],
        compiler_params=_params("arbitrary"),
    )(x, g.reshape(1, d), target)


def _adamw(w, g, m, v, *, name, tr=512):
    shape = w.shape
    cols = shape[-1]
    rows = w.size // cols
    tr = tr if rows % tr == 0 else rows

    def body(w_ref, g_ref, m_ref, v_ref, d_ref, nm_ref, nv_ref):
        gf = g_ref[...]
        nm = ADAM_B1 * m_ref[...] + (1.0 - ADAM_B1) * gf
        nv = ADAM_B2 * v_ref[...] + (1.0 - ADAM_B2) * (gf * gf)
        m_hat = nm / (1.0 - ADAM_B1 ** ADAM_STEP)
        v_hat = nv / (1.0 - ADAM_B2 ** ADAM_STEP)
        d_ref[...] = -ADAM_LR * (m_hat / (jnp.sqrt(v_hat) + ADAM_EPS) + ADAM_WD * w_ref[...])
        nm_ref[...] = nm
        nv_ref[...] = nv

    blk = pl.BlockSpec((tr, cols), lambda i: (i, 0))
    out = jax.ShapeDtypeStruct((rows, cols), F32)
    outs = pl.pallas_call(
        body, name=name, grid=(rows // tr,), in_specs=[blk] * 4, out_specs=[blk] * 3, out_shape=[out] * 3,
        compiler_params=_params("parallel"),
    )(*[t.reshape(rows, cols) for t in (w, g, m, v)])
    return tuple(o.reshape(shape) for o in outs)


_HBM = pl.BlockSpec(memory_space=pl.ANY)
_MESH_ID = pl.DeviceIdType.MESH


def _place():
    x, y, c = (lax.axis_index(a) for a in MESH_AXES)
    return x, y, c, [(1 - x, y), (x, 1 - y), (1 - x, 1 - y)]


def _remote(src, dst, send_sems, recv_sems, k, to):
    return pltpu.make_async_remote_copy(src_ref=src, dst_ref=dst, send_sem=send_sems.at[k], recv_sem=recv_sems.at[k],
                                        device_id=to, device_id_type=_MESH_ID)


def _comm_call(body, arrays, out_shapes, n_remote, name):
    return pl.pallas_call(
        body, name=name, in_specs=[_HBM] * len(arrays), out_specs=[_HBM] * len(out_shapes), out_shape=out_shapes,
        scratch_shapes=[pltpu.SemaphoreType.DMA((n_remote,)), pltpu.SemaphoreType.DMA((n_remote,)),
                        pltpu.SemaphoreType.DMA((len(arrays),))],
    )(*arrays)


def _allgather_weights(halves, *, name):
    n = len(halves)

    def body(*refs):
        w, out = refs[:n], refs[n:2 * n]
        send_sems, recv_sems, local_sems = refs[2 * n:]
        x, y, c, chips = _place()
        me = 2 * x + y
        sibling = (x, y, 1 - c)
        slots = [2 * px + py for px, py in chips]

        def copy(t, k, slot, half, to, src=None):
            dst = out[t].at[slot, half]
            return _remote(dst if src is None else src, dst, send_sems, recv_sems, 7 * t + k, to)

        local = [pltpu.make_async_copy(w[t].at[c], out[t].at[me, c], local_sems.at[t]) for t in range(n)]
        first = []
        for t in range(n):
            local[t].start()
            first.append(copy(t, 0, me, c, sibling, src=w[t].at[c]))
            first += [copy(t, 1 + j, me, c, (px, py, c), src=w[t].at[c]) for j, (px, py) in enumerate(chips)]
        for cp in first:
            cp.start()
        passed = []
        for j, (px, py) in enumerate(chips):
            for t in range(n):
                copy(t, 1 + j, slots[j], c, (px, py, c)).wait_recv()
                passed.append(copy(t, 4 + j, slots[j], c, sibling))
                passed[-1].start()
        for t in range(n):
            copy(t, 0, me, 1 - c, sibling).wait_recv()
            for j in range(3):
                copy(t, 4 + j, slots[j], 1 - c, sibling).wait_recv()
        for cp in first + passed:
            cp.wait_send()
        for cp in local:
            cp.wait()

    shapes = [jax.ShapeDtypeStruct((N_CHIPS,) + h.shape, h.dtype) for h in halves]
    return _comm_call(body, halves, shapes, 7 * n, name)


def _swap_halves(gs, *, name):
    n = len(gs)

    def body(*refs):
        g, out = refs[:n], refs[n:2 * n]
        send_sems, recv_sems, _ = refs[2 * n:]
        x, y, c, _ = _place()
        copies = [_remote(g[t].at[1 - c], out[t], send_sems, recv_sems, t, (x, y, 1 - c)) for t in range(n)]
        for cp in copies:
            cp.start()
        for cp in copies:
            cp.wait()

    return _comm_call(body, gs, [jax.ShapeDtypeStruct(g.shape[1:], g.dtype) for g in gs], n, name)


def _scatter_chips(ps, *, name):
    n = len(ps)

    def body(*refs):
        p, out = refs[:n], refs[n:2 * n]
        send_sems, recv_sems, local_sems = refs[2 * n:]
        x, y, c, chips = _place()
        me = 2 * x + y
        local = [pltpu.make_async_copy(p[t].at[me], out[t].at[me], local_sems.at[t]) for t in range(n)]
        sends = [_remote(p[t].at[2 * px + py], out[t].at[me], send_sems, recv_sems, 3 * t + j, (px, py, c))
                 for t in range(n) for j, (px, py) in enumerate(chips)]
        for cp in local + sends:
            cp.start()
        for t in range(n):
            for j, (px, py) in enumerate(chips):
                slot = out[t].at[2 * px + py]
                _remote(slot, slot, send_sems, recv_sems, 3 * t + j, (px, py, c)).wait_recv()
        for cp in sends:
            cp.wait_send()
        for cp in local:
            cp.wait()

    return _comm_call(body, ps, [jax.ShapeDtypeStruct(p.shape, p.dtype) for p in ps], 3 * n, name)


def _join_halves(rs, *, name):
    n = len(rs)

    def body(*refs):
        r, out = refs[:n], refs[n:2 * n]
        send_sems, recv_sems, _ = refs[2 * n:]
        x, y, c, _ = _place()
        sends = [_remote(r[t].at[c], out[t].at[c], send_sems, recv_sems, t, (x, y, 1 - c)) for t in range(n)]
        for cp in sends:
            cp.start()
        for t in range(n):
            slot = out[t].at[1 - c]
            _remote(slot, slot, send_sems, recv_sems, t, (x, y, 1 - c)).wait_recv()
        for cp in sends:
            cp.wait_send()

    return pl.pallas_call(
        body, name=name, in_specs=[_HBM] * n, out_specs=[_HBM] * n,
        out_shape=[jax.ShapeDtypeStruct(r.shape, r.dtype) for r in rs],
        input_output_aliases={t: t for t in range(n)},
        scratch_shapes=[pltpu.SemaphoreType.DMA((n,)), pltpu.SemaphoreType.DMA((n,)), pltpu.SemaphoreType.DMA((n,))],
    )(*rs)


def _pick_rows(rows, cap=512):
    for t in range(min(rows, cap), 0, -1):
        if rows % t == 0 and t % 16 == 0:
            return t
    return rows


def _add_halves(g, recv, c, out_dtype, *, name):
    cols = g.shape[-1]
    rows = recv.size // (N_CHIPS * cols)
    tr = _pick_rows(rows)

    def body(c_ref, g_ref, r_ref, o_ref):
        o_ref[...] = (g_ref[...] + r_ref[...]).astype(o_ref.dtype)

    blk = pl.BlockSpec((None, tr, cols), lambda b, i, c_ref: (b, i, 0))
    out = pl.pallas_call(
        body, name=name,
        grid_spec=pltpu.PrefetchScalarGridSpec(
            num_scalar_prefetch=1, grid=(N_CHIPS, rows // tr),
            in_specs=[pl.BlockSpec((None, None, tr, cols), lambda b, i, c_ref: (c_ref[0], b, i, 0)), blk],
            out_specs=blk),
        out_shape=jax.ShapeDtypeStruct((N_CHIPS, rows, cols), out_dtype),
        compiler_params=_params("parallel", "parallel"),
    )(c.reshape(1).astype(jnp.int32), g.reshape(2, N_CHIPS, rows, cols), recv.reshape(N_CHIPS, rows, cols))
    return out.reshape(recv.shape)


def _sum_slots(p, c, *, name):
    cols = p.shape[-1]
    rows = p.size // (N_CHIPS * cols)
    tr = _pick_rows(rows)

    def body(c_ref, p_ref, o_ref):
        acc = p_ref[0].astype(F32)
        for q in range(1, N_CHIPS):
            acc = acc + p_ref[q].astype(F32)
        o_ref[...] = acc

    out = pl.pallas_call(
        body, name=name,
        grid_spec=pltpu.PrefetchScalarGridSpec(
            num_scalar_prefetch=1, grid=(rows // tr,),
            in_specs=[pl.BlockSpec((N_CHIPS, tr, cols), lambda i, c_ref: (0, i, 0))],
            out_specs=pl.BlockSpec((None, tr, cols), lambda i, c_ref: (c_ref[0], i, 0))),
        out_shape=jax.ShapeDtypeStruct((2, rows, cols), F32),
        compiler_params=_params("parallel"),
    )(c.reshape(1).astype(jnp.int32), p.reshape(N_CHIPS, rows, cols))
    return out.reshape((2,) + p.shape[1:])


GROUPS = (("gu", ("w_ffn1_gu", "w_ffn2_gu"), "col"), ("down", ("w_ffn1_down", "w_ffn2_down"), "row"),
          ("square", ("w_mix_out", "w_xq", "w_xo"), "row"), ("mix_in", ("w_mix_in",), "col"), ("xkv", ("w_xkv",), "col"))
REPLICATED = ("g_ffn1", "g_mix", "b_f", "g_conv_out", "g_att_out", "g_xattn", "g_mem", "g_ffn2", "g_final")
WEIGHTS = ("g_ffn1", "w_ffn1_gu", "w_ffn1_down", "g_mix", "w_mix_in", "w_conv", "b_f", "g_conv_out", "g_att_out",
           "w_mix_out", "g_xattn", "g_mem", "w_xq", "w_xkv", "w_xo", "g_ffn2", "w_ffn2_gu", "w_ffn2_down", "g_final")
SMALL_COLS = 1024
SMALL_ROW_UNIT = 16


def _gather_weights(shards):
    packs = []
    for _, members, _ in GROUPS:
        hs = []
        for name in members:
            l, a, b = shards[name].shape
            hs.append(shards[name].astype(BF16).reshape(l, 2, a // 2, b).transpose(1, 0, 2, 3))
        packs.append(jnp.stack(hs, axis=1))
    wc = shards["w_conv"]
    packs.append(jnp.stack([wc, wc]))
    gathered = _allgather_weights(packs, name="allgather_weights")
    full = {}
    for (_, members, kind), got in zip(GROUPS, gathered):
        _, _, g, l, a2, b = got.shape
        if kind == "col":
            whole = got.transpose(2, 3, 1, 4, 0, 5).reshape(g, l, 2 * a2, N_CHIPS * b)
        else:
            whole = got.transpose(2, 3, 0, 1, 4, 5).reshape(g, l, N_CHIPS * 2 * a2, b)
        for gi, name in enumerate(members):
            full[name] = whole[gi]
    l, k, b = wc.shape
    full["w_conv"] = gathered[-1][:, 0].transpose(1, 2, 0, 3).reshape(l, k, N_CHIPS * b)
    return full


def _small_rows(v):
    flat = v.reshape(-1)
    return jnp.pad(flat, (0, -flat.shape[0] % SMALL_COLS)).reshape(-1, SMALL_COLS)


def _reduce_gradients(grads, c):
    packs = []
    for _, members, kind in GROUPS:
        cut = []
        for name in members:
            l, a, b = grads[name].shape
            if kind == "col":
                cut.append(grads[name].reshape(l, 2, a // 2, N_CHIPS, b // N_CHIPS).transpose(1, 3, 0, 2, 4))
            else:
                cut.append(grads[name].reshape(l, N_CHIPS, 2, a // (2 * N_CHIPS), b).transpose(2, 1, 0, 3, 4))
        packs.append(jnp.stack(cut, axis=2))
    rep = jnp.concatenate([_small_rows(grads[n]) for n in REPLICATED])
    l, k, b = grads["w_conv"].shape
    conv = grads["w_conv"].reshape(l, k, N_CHIPS, b // N_CHIPS).transpose(2, 0, 1, 3)
    conv_rows = [_small_rows(conv[q]) for q in range(N_CHIPS)]
    n_rows = rep.shape[0] + conv_rows[0].shape[0]
    fill = jnp.zeros((-n_rows % SMALL_ROW_UNIT, SMALL_COLS), F32)
    small = jnp.stack([jnp.concatenate([rep, conv_rows[q], fill]) for q in range(N_CHIPS)])
    half_rows = small.shape[1] // 2
    packs.append(small.reshape(N_CHIPS, 2, half_rows, SMALL_COLS).transpose(1, 0, 2, 3))

    from_sibling = _swap_halves(packs, name="grad_swap_halves")
    wire = [BF16] * len(GROUPS) + [F32]
    chip_sums = [_add_halves(g, r, c, dt, name=f"grad_add_halves_{i}")
                 for i, (g, r, dt) in enumerate(zip(packs, from_sibling, wire))]
    from_chips = _scatter_chips(chip_sums, name="grad_scatter_chips")
    halves = [_sum_slots(p, c, name=f"grad_sum_chips_{i}") for i, p in enumerate(from_chips)]
    reduced = _join_halves(halves, name="grad_join_halves")

    out = {}
    for (_, members, _), r in zip(GROUPS, reduced):
        _, g, l, a2, b = r.shape
        whole = r.transpose(1, 2, 0, 3, 4).reshape(g, l, 2 * a2, b)
        for gi, name in enumerate(members):
            out[name] = whole[gi]
    rows = reduced[-1].reshape(-1, SMALL_COLS)
    off = 0
    for name in REPLICATED:
        n = -(-grads[name].size // SMALL_COLS)
        out[name] = rows[off:off + n].reshape(-1)[:grads[name].size].reshape(grads[name].shape)
        off += n
    l, k, b = grads["w_conv"].shape
    out["w_conv"] = rows[off:off + conv_rows[0].shape[0]].reshape(-1)[:l * k * b // N_CHIPS].reshape(l, k, b // N_CHIPS)
    return out


def _residual_out(a, w_out, x, g_next, name, alpha=1.0):
    kw = dict(tm=1024, tn=1024, tk=w_out.shape[0], alpha=alpha, residual=x, name=name)
    if g_next is None:
        return _matmul(a, w_out, "nn", F32, **kw), None
    return _matmul(a, w_out, "nn", F32, norm_gain=g_next, **kw)


def _ffn_fwd(x, h, w_gu, w_down, g_next, tag):
    gu, a = _gate_up_act(h, w_gu, name=f"{tag}_gu_act")
    y, h_next = _residual_out(a, w_down, x, g_next, f"{tag}_down", alpha=0.5)
    return y, h_next, (x, h, gu, a)


def _ffn_bwd(dy, saved, g, w_gu, w_down, tag):
    x, h, gu, a = saved
    f = w_down.shape[0]
    dgu = _down_bwd_act(dy, w_down, gu, 0.5, name=f"{tag}_da_dact")
    dw_down = _matmul(a, dy, "tn", F32, tm=f // 2, tn=1024, tk=1024, alpha=0.5, name=f"{tag}_dwdown")
    dw_gu = jnp.concatenate([_matmul(h, dgu, "tn", F32, tm=1024, tn=f // 2, tk=2048, b_slab=half, name=f"{tag}_dwgu{half}")
                             for half in range(2)], axis=1)
    dx, dg = _matmul(dgu, w_gu, "nt", F32, tm=1024, tn=1024, tk=f // 2, residual=dy, dnorm=(x, g), a_slabs=True,
                     name=f"{tag}_dh_dnorm")
    return dx, dg[0], dw_gu, dw_down


def _mix_fwd(x, h, w, l, g_next, tag):
    s, d = x.shape
    gc, ga = w["g_conv_out"][l], w["g_att_out"][l]
    cw, aw = gc.shape[0], ga.shape[0]
    nh = aw // HEAD_DIM
    zw = 3 * cw + 3 * aw
    t = 256 if s >= 1024 else s // 4
    cols = dict(qcol=3 * cw // LANES, kcol=(3 * cw + aw) // LANES, vcol=(3 * cw + 2 * aw) // LANES)
    w_in = w["w_mix_in"][l]
    w_main = w_in[:, :zw]
    w_f = jnp.pad(w_in[:, zw:], ((0, 0), (0, LANES - nh)))
    z = _matmul(h, w_main, "nn", BF16, tm=2048, tn=512, tk=d, name=f"{tag}_in")
    zf = _matmul(h, w_f, "nn", F32, tm=2048, tn=LANES, tk=d, name=f"{tag}_in_f")
    b = jnp.pad(w["b_f"][l], (0, LANES - nh)).reshape(1, LANES)
    c = _cumsum_fwd(zf, b, name=f"{tag}_cumsum")
    ccol = c[:, :nh].reshape(s, nh // 2, 2).transpose(1, 0, 2)
    assert (3 * cw) % aw == 0
    bounds = _attn_bounds(z, c, qcol_units=3 * cw // aw, kcol_units=3 * cw // aw + 1, aw=aw, t=t, name=f"{tag}_bounds")
    qa, ka, va = _attn_operands(z, ccol, t=_row_tile(s, 512), name=f"{tag}_attn_operands", **cols)
    yatt, lse = _fattn_fwd(qa, ka, va, bounds, t=t, name=f"{tag}_attn")
    wc = jnp.pad(w["w_conv"][l], ((0, HALO - CONV_K), (0, 0)))
    ycat, cv = _mixpost_fwd(z, yatt, wc, gc, ga, name=f"{tag}_post")
    y, h_next = _residual_out(ycat, w["w_mix_out"][l], x, g_next, f"{tag}_out")
    return y, h_next, (x, h, w_main, w_f, z, zf, b, qa, ka, va, bounds, yatt, lse, wc, ycat, cv, t)


def _mix_bwd(dy, saved, w, l, tag):
    x, h, w_main, w_f, z, zf, b, qa, ka, va, bounds, yatt, lse, wc, ycat, cv, t = saved
    s, d = x.shape
    gc, ga = w["g_conv_out"][l], w["g_att_out"][l]
    nh = ga.shape[0] // HEAD_DIM
    zw = w_main.shape[1]
    dycat = _matmul(dy, w["w_mix_out"][l], "nt", F32, tm=1024, tn=1024, tk=d, name=f"{tag}_dycat")
    dw_out = _matmul(ycat, dy, "tn", F32, tm=1024, tn=1024, tk=1024, name=f"{tag}_dwout")
    dz_conv, dyatt, dwc, dgc, dga = _mixpost_bwd(dycat, z, yatt, cv, wc, gc, ga, name=f"{tag}_dpost")
    qb, kb, dyb = _attn_bwd_operands(qa, ka, dyatt, yatt, lse, t=_row_tile(s, 512), name=f"{tag}_attn_bwd_operands")
    dq, dcq = _fattn_dq(qb, kb, va, dyb, bounds, t=t, name=f"{tag}_attn_dq")
    dk, dv, dck = _fattn_dkv(qb, kb, va, dyb, bounds, t=t, name=f"{tag}_attn_dkv")
    def heads_on_lanes(v):
        return jnp.pad(v.transpose(1, 0, 2).reshape(s, nh), ((0, 0), (0, LANES - nh)))

    dzf, db = _cumsum_bwd(heads_on_lanes(dcq), heads_on_lanes(dck), zf, b, name=f"{tag}_dcumsum")
    dz = jnp.concatenate([dz_conv, dq, dk, dv], axis=1)
    dw_main = _matmul(h, dz, "tn", F32, tm=1024, tn=512, tk=2048, name=f"{tag}_dwin")
    dw_f = _matmul(h, dzf, "tn", F32, tm=1024, tn=LANES, tk=2048, name=f"{tag}_dwin_f")
    dh_f = _matmul(dzf, w_f, "nt", F32, tm=1024, tn=1024, tk=LANES, name=f"{tag}_dh_f")
    dx, dg = _matmul(dz, w_main, "nt", F32, tm=512, tn=1024, tk=zw // 2, addend=dh_f, residual=dy,
                     dnorm=(x, w["g_mix"][l]), name=f"{tag}_dh_dnorm")
    grads = dict(g_mix=dg[0], w_mix_in=jnp.concatenate([dw_main, dw_f[:, :nh]], axis=1), w_conv=dwc[:CONV_K], b_f=db[0, :nh],
                 g_conv_out=dgc[0], g_att_out=dga[0], w_mix_out=dw_out)
    return dx, grads


def _xattn_block_fwd(x, h, mem, w, l, g_next, tag):
    d = x.shape[1]
    mn = _rmsnorm_fwd(mem, w["g_mem"][l], name=f"{tag}_mem_norm")
    q = _matmul(h, w["w_xq"][l], "nn", BF16, tm=1024, tn=1024, tk=d, name=f"{tag}_q")
    kv = _matmul(mn, w["w_xkv"][l], "nn", BF16, tm=1024, tn=1024, tk=d, name=f"{tag}_kv")
    o = _xattn_fwd(q, kv, name=f"{tag}_core")
    y, h_next = _residual_out(o, w["w_xo"][l], x, g_next, f"{tag}_o")
    return y, h_next, (x, h, mn, q, kv, o)


def _xattn_block_bwd(dy, saved, mem, w, l, tag):
    x, h, mn, q, kv, o = saved
    d = x.shape[1]
    do = _matmul(dy, w["w_xo"][l], "nt", BF16, tm=1024, tn=1024, tk=d, name=f"{tag}_do")
    dw_xo = _matmul(o, dy, "tn", F32, tm=1024, tn=1024, tk=1024, name=f"{tag}_dwo")
    dq, dkv = _xattn_bwd(q, kv, do, name=f"{tag}_dcore")
    dw_xq = _matmul(h, dq, "tn", F32, tm=1024, tn=1024, tk=2048, name=f"{tag}_dwq")
    dx, dg = _matmul(dq, w["w_xq"][l], "nt", F32, tm=1024, tn=1024, tk=d, residual=dy, dnorm=(x, w["g_xattn"][l]),
                     name=f"{tag}_dh_dnorm")
    dw_xkv = _matmul(mn, dkv, "tn", F32, tm=1024, tn=1024, tk=1024, name=f"{tag}_dwkv")
    dmn = _matmul(dkv, w["w_xkv"][l], "nt", F32, tm=1024, tn=1024, tk=1024, name=f"{tag}_dmem")
    _, dg_mem = _rmsnorm_bwd(mem, w["g_mem"][l], dmn, None, name=f"{tag}_dmem_norm")
    return dx, dict(g_xattn=dg[0], g_mem=dg_mem, w_xq=dw_xq, w_xkv=dw_xkv, w_xo=dw_xo)


def kernel(x, mem, g_ffn1, w_ffn1_gu, w_ffn1_down, g_mix, w_mix_in, w_conv, b_f, g_conv_out, g_att_out, w_mix_out, g_xattn, g_mem, w_xq, w_xkv, w_xo, g_ffn2, w_ffn2_gu, w_ffn2_down, g_final, loss_target, m_g_ffn1, m_w_ffn1_gu, m_w_ffn1_down, m_g_mix, m_w_mix_in, m_w_conv, m_b_f, m_g_conv_out, m_g_att_out, m_w_mix_out, m_g_xattn, m_g_mem, m_w_xq, m_w_xkv, m_w_xo, m_g_ffn2, m_w_ffn2_gu, m_w_ffn2_down, m_g_final, v_g_ffn1, v_w_ffn1_gu, v_w_ffn1_down, v_g_mix, v_w_mix_in, v_w_conv, v_b_f, v_g_conv_out, v_g_att_out, v_w_mix_out, v_g_xattn, v_g_mem, v_w_xq, v_w_xkv, v_w_xo, v_g_ffn2, v_w_ffn2_gu, v_w_ffn2_down, v_g_final):
    local = dict(zip(WEIGHTS, (g_ffn1, w_ffn1_gu, w_ffn1_down, g_mix, w_mix_in, w_conv, b_f, g_conv_out, g_att_out, w_mix_out,
                               g_xattn, g_mem, w_xq, w_xkv, w_xo, g_ffn2, w_ffn2_gu, w_ffn2_down, g_final)))
    mom1 = dict(zip(WEIGHTS, (m_g_ffn1, m_w_ffn1_gu, m_w_ffn1_down, m_g_mix, m_w_mix_in, m_w_conv, m_b_f, m_g_conv_out,
                              m_g_att_out, m_w_mix_out, m_g_xattn, m_g_mem, m_w_xq, m_w_xkv, m_w_xo, m_g_ffn2, m_w_ffn2_gu,
                              m_w_ffn2_down, m_g_final)))
    mom2 = dict(zip(WEIGHTS, (v_g_ffn1, v_w_ffn1_gu, v_w_ffn1_down, v_g_mix, v_w_mix_in, v_w_conv, v_b_f, v_g_conv_out,
                              v_g_att_out, v_w_mix_out, v_g_xattn, v_g_mem, v_w_xq, v_w_xkv, v_w_xo, v_g_ffn2, v_w_ffn2_gu,
                              v_w_ffn2_down, v_g_final)))
    depth = g_ffn1.shape[0]
    s, d = x.shape[1], x.shape[2]
    w = dict(local)
    w.update(_gather_weights(local))

    xs = x.reshape(s, d)
    mems = mem.reshape(mem.shape[1], d)
    saved = []
    h = _rmsnorm_fwd(xs, w["g_ffn1"][0], name="l0_ffn1_norm")
    for l in range(depth):
        g_after = w["g_ffn1"][l + 1] if l + 1 < depth else None
        xs, h, s1 = _ffn_fwd(xs, h, w["w_ffn1_gu"][l], w["w_ffn1_down"][l], w["g_mix"][l], f"l{l}_ffn1")
        xs, h, s2 = _mix_fwd(xs, h, w, l, w["g_xattn"][l], f"l{l}_mix")
        xs, h, s3 = _xattn_block_fwd(xs, h, mems, w, l, w["g_ffn2"][l], f"l{l}_xattn")
        xs, h, s4 = _ffn_fwd(xs, h, w["w_ffn2_gu"][l], w["w_ffn2_down"][l], g_after, f"l{l}_ffn2")
        saved.append((s1, s2, s3, s4))

    dx, dg_final, sq = _final_loss(xs, g_final, loss_target.reshape(s, d), name="loss_head")
    loss = lax.psum(jnp.sum(sq) * (0.5 / d), MESH_AXES)

    per_layer = []
    for l in reversed(range(depth)):
        s1, s2, s3, s4 = saved[l]
        grads = {}
        dx, grads["g_ffn2"], grads["w_ffn2_gu"], grads["w_ffn2_down"] = _ffn_bwd(
            dx, s4, w["g_ffn2"][l], w["w_ffn2_gu"][l], w["w_ffn2_down"][l], f"l{l}_ffn2")
        dx, g3 = _xattn_block_bwd(dx, s3, mems, w, l, f"l{l}_xattn")
        dx, g2 = _mix_bwd(dx, s2, w, l, f"l{l}_mix")
        dx, grads["g_ffn1"], grads["w_ffn1_gu"], grads["w_ffn1_down"] = _ffn_bwd(
            dx, s1, w["g_ffn1"][l], w["w_ffn1_gu"][l], w["w_ffn1_down"][l], f"l{l}_ffn1")
        grads.update(g2)
        grads.update(g3)
        per_layer.append(grads)
    per_layer.reverse()
    grads = {name: jnp.stack([per_layer[l][name] for l in range(depth)]) for name in WEIGHTS if name != "g_final"}
    grads["g_final"] = dg_final.reshape(d)

    reduced = _reduce_gradients(grads, lax.axis_index("c"))
    deltas, new_m, new_v = {}, {}, {}
    for name in WEIGHTS:
        deltas[name], new_m[name], new_v[name] = _adamw(local[name], reduced[name], mom1[name], mom2[name], name=f"adamw_{name}")
    return (loss, dx.reshape(x.shape), *[reduced[n] for n in WEIGHTS], *[deltas[n] for n in WEIGHTS],
            *[new_m[n] for n in WEIGHTS], *[new_v[n] for n in WEIGHTS])
```

```python
import functools

import jax
import jax.numpy as jnp
from jax import lax
from jax.experimental import pallas as pl
from jax.experimental.pallas import tpu as pltpu

F32 = jnp.float32
BF16 = jnp.bfloat16

EPS = 1e-6
HEAD_DIM = 64
LANES = 128
N_XHEADS = 4
CONV_K = 3
ADAM_LR, ADAM_B1, ADAM_B2, ADAM_EPS, ADAM_WD, ADAM_STEP = 0.001, 0.9, 0.999, 1e-08, 0.01, 10
VMEM_LIMIT_BYTES = 56 * 1024 * 1024
NEG_BIG = -1e30
MESH_AXES = ("x", "y", "c")
N_CHIPS = 4


def _params(*sem):
    return pltpu.CompilerParams(dimension_semantics=sem, vmem_limit_bytes=VMEM_LIMIT_BYTES)


_DIMS = {"nn": (((1,), (0,)), ((), ())), "nt": (((1,), (1,)), ((), ())), "tn": (((0,), (0,)), ((), ()))}


def _matmul(a, b, mode, out_dtype, *, tm, tn, tk, name, alpha=1.0, addend=None, residual=None, norm_gain=None, dnorm=None,
            a_slabs=False, b_slab=None):
    if a_slabs:
        assert mode == "nt"
        n_slab, m, k_slab = a.shape
        k, (n, k2) = n_slab * k_slab, b.shape
        tk = min(tk, k_slab)
        assert k_slab % tk == 0
    elif b_slab is not None:
        assert mode == "tn"
        (k, m), (_, k2, n) = a.shape, b.shape
    elif mode == "nn":
        (m, k), (k2, n) = a.shape, b.shape
    elif mode == "nt":
        (m, k), (n, k2) = a.shape, b.shape
    else:
        (k, m), (k2, n) = a.shape, b.shape
    assert k == k2, (a.shape, b.shape, mode)
    tm, tn, tk = min(tm, m), min(tn, n), min(tk, k)
    assert m % tm == 0 and n % tn == 0 and k % tk == 0, (m, n, k, tm, tn, tk)
    assert (norm_gain is None and dnorm is None) or tn == n
    nk = k // tk
    dims = _DIMS[mode]

    def body(*refs):
        refs = list(refs)
        a_ref, b_ref = refs[:2]
        del refs[:2]
        add_ref = refs.pop(0) if addend is not None else None
        r_ref = refs.pop(0) if residual is not None else None
        g_ref = refs.pop(0) if (norm_gain is not None or dnorm is not None) else None
        x_ref = refs.pop(0) if dnorm is not None else None
        o_ref = refs.pop(0)
        h_ref = refs.pop(0) if norm_gain is not None else None
        dg_ref = refs.pop(0) if dnorm is not None else None
        scratch = refs
        prod = lax.dot_general(a_ref[...].astype(BF16), b_ref[...].astype(BF16), dims, preferred_element_type=F32)

        def finish(acc):
            if alpha != 1.0:
                acc = acc * alpha
            if add_ref is not None:
                acc = acc + add_ref[...]
            if dnorm is not None:
                xf = x_ref[...]
                r = _rms_rows(xf)
                xhat = xf * r
                dxhat = acc * g_ref[...]

                @pl.when((pl.program_id(0) == 0))
                def _():
                    dg_ref[...] = jnp.zeros(dg_ref.shape, F32)

                dg_ref[...] += jnp.sum(acc * xhat, axis=0, keepdims=True)
                acc = r * (dxhat - xhat * jnp.mean(dxhat * xhat, axis=-1, keepdims=True))
            if r_ref is not None:
                acc = acc + r_ref[...].astype(F32)
            o_ref[...] = acc.astype(o_ref.dtype)
            if norm_gain is not None:
                h_ref[...] = (acc * _rms_rows(acc) * g_ref[...]).astype(h_ref.dtype)

        if nk == 1:
            finish(prod)
        else:
            acc_ref = scratch[0]
            kk = pl.program_id(2)

            @pl.when(kk == 0)
            def _():
                acc_ref[...] = prod

            @pl.when(kk > 0)
            def _():
                acc_ref[...] += prod

            @pl.when(kk == nk - 1)
            def _():
                finish(acc_ref[...])

    if mode == "nn":
        a_spec = pl.BlockSpec((tm, tk), lambda i, j, kk: (i, kk))
        b_spec = pl.BlockSpec((tk, tn), lambda i, j, kk: (kk, j))
    elif mode == "nt":
        a_spec = pl.BlockSpec((tm, tk), lambda i, j, kk: (i, kk))
        b_spec = pl.BlockSpec((tn, tk), lambda i, j, kk: (j, kk))
    else:
        a_spec = pl.BlockSpec((tk, tm), lambda i, j, kk: (kk, i))
        b_spec = pl.BlockSpec((tk, tn), lambda i, j, kk: (kk, j))
    if a_slabs:
        per_slab = k_slab // tk
        assert n_slab == 2

        def a_index(i, j, kk):
            second = (kk >= per_slab).astype(jnp.int32)
            return second, i, kk - second * per_slab

        a_spec = pl.BlockSpec((None, tm, tk), a_index)
    if b_slab is not None:
        b_spec = pl.BlockSpec((None, tk, tn), lambda i, j, kk: (b_slab, kk, j))
    o_spec = pl.BlockSpec((tm, tn), lambda i, j, kk: (i, j))
    vec_spec = pl.BlockSpec((1, n), lambda i, j, kk: (0, 0))
    in_specs, args = [a_spec, b_spec], [a, b]
    out_specs, out_shape = [o_spec], [jax.ShapeDtypeStruct((m, n), out_dtype)]
    if addend is not None:
        in_specs.append(o_spec)
        args.append(addend)
    if residual is not None:
        in_specs.append(o_spec)
        args.append(residual)
    if norm_gain is not None:
        in_specs.append(vec_spec)
        args.append(norm_gain.reshape(1, n))
        out_specs.append(o_spec)
        out_shape.append(jax.ShapeDtypeStruct((m, n), BF16))
    if dnorm is not None:
        in_specs += [vec_spec, o_spec]
        args += [dnorm[1].reshape(1, n), dnorm[0]]
        out_specs.append(vec_spec)
        out_shape.append(jax.ShapeDtypeStruct((1, n), F32))
    outs = pl.pallas_call(
        body, name=name, grid=(m // tm, n // tn, nk), in_specs=in_specs, out_specs=out_specs, out_shape=out_shape,
        scratch_shapes=[pltpu.VMEM((tm, tn), F32)] if nk > 1 else [],
        compiler_params=_params(*(("arbitrary",) * 3 if dnorm is not None else ("parallel", "parallel", "arbitrary"))),
    )(*args)
    return outs[0] if len(outs) == 1 else tuple(outs)


def _row_tile(rows, want):
    t = min(rows, want)
    assert rows % t == 0, (rows, t)
    return t


def _rmsnorm_fwd(x, g, *, name, tr=1024):
    s, d = x.shape
    tr = _row_tile(s, tr)

    def body(x_ref, g_ref, o_ref):
        xf = x_ref[...]
        r = lax.rsqrt(jnp.mean(xf * xf, axis=-1, keepdims=True) + EPS)
        o_ref[...] = (xf * r * g_ref[...]).astype(o_ref.dtype)

    return pl.pallas_call(
        body, name=name, grid=(s // tr,),
        in_specs=[pl.BlockSpec((tr, d), lambda i: (i, 0)), pl.BlockSpec((1, d), lambda i: (0, 0))],
        out_specs=pl.BlockSpec((tr, d), lambda i: (i, 0)),
        out_shape=jax.ShapeDtypeStruct((s, d), BF16),
        compiler_params=_params("parallel"),
    )(x, g.reshape(1, d))


def _rmsnorm_bwd(x, g, dh, dres, *, name, tr=512):
    s, d = x.shape
    tr = _row_tile(s, tr)

    def body(x_ref, g_ref, dh_ref, *rest):
        if dres is None:
            dx_ref, dg_ref = rest
        else:
            dres_ref, dx_ref, dg_ref = rest
        xf = x_ref[...]
        r = lax.rsqrt(jnp.mean(xf * xf, axis=-1, keepdims=True) + EPS)
        xhat = xf * r
        dhf = dh_ref[...].astype(F32)
        dxhat = dhf * g_ref[...]
        dx = r * (dxhat - xhat * jnp.mean(dxhat * xhat, axis=-1, keepdims=True))
        if dres is not None:
            dx = dx + dres_ref[...]
        dx_ref[...] = dx

        @pl.when(pl.program_id(0) == 0)
        def _():
            dg_ref[...] = jnp.zeros_like(dg_ref)

        dg_ref[...] += jnp.sum(dhf * xhat, axis=0, keepdims=True)

    row = pl.BlockSpec((tr, d), lambda i: (i, 0))
    vec = pl.BlockSpec((1, d), lambda i: (0, 0))
    in_specs, args = [row, vec, row], [x, g.reshape(1, d), dh]
    if dres is not None:
        in_specs.append(row)
        args.append(dres)
    dx, dg = pl.pallas_call(
        body, name=name, grid=(s // tr,), in_specs=in_specs, out_specs=[row, vec],
        out_shape=[jax.ShapeDtypeStruct((s, d), F32), jax.ShapeDtypeStruct((1, d), F32)],
        compiler_params=_params("arbitrary"),
    )(*args)
    return dx, dg.reshape(d)


def _gate_up_act(h, w_gu, *, name, tm=2048, tn=256):
    s, d = h.shape
    f = w_gu.shape[1] // 2
    tm, tn = min(tm, s), min(tn, f)
    assert s % tm == 0 and f % tn == 0
    nf = f // tn

    def body(h_ref, wg_ref, wu_ref, gu_ref, a_ref):
        hb = h_ref[...]
        gate = _dot(hb, wg_ref[...], _NN)
        up = _dot(hb, wu_ref[...], _NN)
        gu_ref[0] = gate.astype(gu_ref.dtype)
        gu_ref[1] = up.astype(gu_ref.dtype)
        a_ref[...] = (gate * jax.nn.sigmoid(gate) * up).astype(a_ref.dtype)

    return pl.pallas_call(
        body, name=name, grid=(s // tm, nf),
        in_specs=[pl.BlockSpec((tm, d), lambda i, j: (i, 0)), pl.BlockSpec((d, tn), lambda i, j: (0, j)),
                  pl.BlockSpec((d, tn), lambda i, j: (0, j + nf))],
        out_specs=[pl.BlockSpec((2, tm, tn), lambda i, j: (0, i, j)), pl.BlockSpec((tm, tn), lambda i, j: (i, j))],
        out_shape=[jax.ShapeDtypeStruct((2, s, f), BF16), jax.ShapeDtypeStruct((s, f), BF16)],
        compiler_params=_params("parallel", "parallel"),
    )(h, w_gu, w_gu)


def _down_bwd_act(dy, w_down, gu, alpha, *, name, tm=512):
    s, d = dy.shape
    f = w_down.shape[0]
    tm, tn = min(tm, s), f // 2
    assert s % tm == 0 and tn % LANES == 0

    def body(dy_ref, w_ref, gu_ref, dgu_ref):
        da = _dot(dy_ref[...].astype(BF16), w_ref[...], _NT) * alpha
        gate = gu_ref[0].astype(F32)
        up = gu_ref[1].astype(F32)
        sig = jax.nn.sigmoid(gate)
        silu = gate * sig
        dgu_ref[0] = (da * up * (sig + silu * (1.0 - sig))).astype(dgu_ref.dtype)
        dgu_ref[1] = (da * silu).astype(dgu_ref.dtype)

    slab = pl.BlockSpec((2, tm, tn), lambda i, j: (0, i, j))
    return pl.pallas_call(
        body, name=name, grid=(s // tm, f // tn),
        in_specs=[pl.BlockSpec((tm, d), lambda i, j: (i, 0)), pl.BlockSpec((tn, d), lambda i, j: (j, 0)), slab],
        out_specs=slab, out_shape=jax.ShapeDtypeStruct((2, s, f), BF16),
        compiler_params=_params("parallel", "parallel"),
    )(dy, w_down, gu)


_NT = (((1,), (1,)), ((), ()))
_NN = (((1,), (0,)), ((), ()))
_TN = (((0,), (0,)), ((), ()))
_QK_SCALE = HEAD_DIM ** -0.5


def _dot(a, b, dims):
    return lax.dot_general(a, b, dims, preferred_element_type=F32)


SKIP_BELOW = 40.0
_SMEM = pl.BlockSpec(memory_space=pltpu.SMEM)


def _attn_bounds(z, c, *, qcol_units, kcol_units, aw, t, name):
    s = z.shape[0]
    nq = s // t
    nh = aw // HEAD_DIM

    def body(q_ref, k_ref, o_ref):
        d = lax.broadcasted_iota(jnp.int32, (aw, LANES), 0)
        hh = lax.broadcasted_iota(jnp.int32, (aw, LANES), 1)
        onehot = ((d >= hh * HEAD_DIM) & (d < (hh + 1) * HEAD_DIM)).astype(BF16)
        for r, ref in enumerate((q_ref, k_ref)):
            v = ref[...].astype(F32)
            sq = _dot((v * v).astype(BF16), onehot, _NN)
            o_ref[r:r + 1, :] = jnp.max(sq, axis=0, keepdims=True)
        o_ref[2:, :] = jnp.zeros((HALO - 2, LANES), F32)

    sq = pl.pallas_call(
        body, name=name, grid=(nq,),
        in_specs=[pl.BlockSpec((t, aw), lambda i: (i, qcol_units)), pl.BlockSpec((t, aw), lambda i: (i, kcol_units))],
        out_specs=pl.BlockSpec((None, HALO, LANES), lambda i: (i, 0, 0)),
        out_shape=jax.ShapeDtypeStruct((nq, HALO, LANES), F32),
        compiler_params=_params("parallel"),
    )(z, z)
    norms = jnp.sqrt(sq[:, :2, :nh]) * 1.01
    qn = (norms[:, 0, :] * _QK_SCALE).T.reshape(-1)
    kn = norms[:, 1, :].T.reshape(-1)
    cs = c[0::t, :nh].T.reshape(-1)
    ce = c[t - 1::t, :nh].T.reshape(-1)
    return qn, kn, cs, ce


def _block_active(bounds, head, i, j, nq):
    qn_ref, kn_ref, cs_ref, ce_ref = bounds
    qi = qn_ref[head * nq + i]
    upper = qi * kn_ref[head * nq + j] + (cs_ref[head * nq + i] - ce_ref[head * nq + j])
    lower = -(qi * kn_ref[head * nq + i])
    return upper - lower > -SKIP_BELOW


def _for_active_heads(bounds, pair, i, j, nq, head_step):
    act = [_block_active(bounds, 2 * pair + h, i, j, nq) for h in range(2)]

    def run(heads):
        for h in heads:
            head_step(h)

    pl.when(act[0] & act[1])(functools.partial(run, (0, 1)))
    pl.when(act[0] & jnp.logical_not(act[1]))(functools.partial(run, (0,)))
    pl.when(jnp.logical_not(act[0]) & act[1])(functools.partial(run, (1,)))


def _attn_operands(z, ccol, *, qcol, kcol, vcol, t, name):
    s = z.shape[0]
    npairs = ccol.shape[0]

    def body(q_ref, k_ref, v_ref, c_ref, qa_ref, ka_ref, va_ref):
        lane = lax.broadcasted_iota(jnp.int32, (t, LANES), 1)
        q2 = q_ref[...] * jnp.asarray(_QK_SCALE, BF16)
        k2, v2 = k_ref[...], v_ref[...]
        one, zero = jnp.ones((t, LANES), BF16), jnp.zeros((t, LANES), BF16)
        for h in range(2):
            base = HEAD_DIM * (1 - h)
            mine = (lane < HEAD_DIM) if h == 0 else (lane >= HEAD_DIM)
            c = c_ref[:, h:h + 1]
            hi = c.astype(BF16)
            r1 = c - hi.astype(F32)
            mid = r1.astype(BF16)
            lo = (r1 - mid.astype(F32)).astype(BF16)
            qa, ka = jnp.where(mine, q2, zero), jnp.where(mine, k2, zero)
            for r, word in enumerate((hi, mid, lo)):
                qa = jnp.where(lane == base + r, word, qa)
                qa = jnp.where(lane == base + 3 + r, one, qa)
                ka = jnp.where(lane == base + r, one, ka)
                ka = jnp.where(lane == base + 3 + r, -word, ka)
            qa_ref[h] = qa
            ka_ref[h] = ka
            va_ref[h] = jnp.where((lane >= base) & (lane < base + 3), one, jnp.where(mine, v2, zero))

    out = jax.ShapeDtypeStruct((npairs, 2, s, LANES), BF16)
    blk = pl.BlockSpec((None, 2, t, LANES), lambda p, i: (p, 0, i, 0))
    return pl.pallas_call(
        body, name=name, grid=(npairs, s // t),
        in_specs=[pl.BlockSpec((t, LANES), lambda p, i: (i, qcol + p)), pl.BlockSpec((t, LANES), lambda p, i: (i, kcol + p)),
                  pl.BlockSpec((t, LANES), lambda p, i: (i, vcol + p)), pl.BlockSpec((None, t, 2), lambda p, i: (p, i, 0))],
        out_specs=[blk, blk, blk], out_shape=[out, out, out],
        compiler_params=_params("parallel", "parallel"),
    )(z, z, z, ccol)


def _fattn_fwd(qa, ka, va, bounds, *, t, name):
    npairs, _, s, _ = qa.shape
    nq = s // t
    reps = t // LANES

    def body(qn_ref, kn_ref, cs_ref, ce_ref, q_ref, k_ref, v_ref, o_ref, lse_ref, m_scr, acc_scr):
        pair, i = pl.program_id(0), pl.program_id(1)
        m_scr[...] = jnp.full(m_scr.shape, NEG_BIG, F32)
        acc_scr[...] = jnp.zeros(acc_scr.shape, F32)

        def head_step(h, j, diagonal):
            off = pl.multiple_of(j * t, t)
            sc = _dot(q_ref[h], k_ref[h, pl.ds(off, t), :], _NT)
            if diagonal:
                row = lax.broadcasted_iota(jnp.int32, (t, t), 0)
                col = lax.broadcasted_iota(jnp.int32, (t, t), 1)
                sc = jnp.where(row >= col, sc, NEG_BIG)
            m_old = m_scr[h]
            m_new = jnp.maximum(m_old, jnp.max(sc, axis=1, keepdims=True))
            p = jnp.exp(sc - jnp.tile(m_new, (1, reps)))
            acc_scr[h] = acc_scr[h] * jnp.exp(m_old - m_new) + _dot(p.astype(BF16), v_ref[h, pl.ds(off, t), :], _NN)
            m_scr[h] = m_new

        def loop_body(j, carry):
            _for_active_heads((qn_ref, kn_ref, cs_ref, ce_ref), pair, i, j, nq, lambda h: head_step(h, j, False))
            return carry

        lax.fori_loop(0, i, loop_body, 0)
        for h in range(2):
            head_step(h, i, True)
        is_a = lax.broadcasted_iota(jnp.int32, (t, LANES), 1) < HEAD_DIM
        l = (acc_scr[0][:, HEAD_DIM:HEAD_DIM + 1], acc_scr[1][:, 0:1])
        o_ref[...] = jnp.where(is_a, acc_scr[0] / l[0], acc_scr[1] / l[1])
        lse_ref[:, 0:1] = m_scr[0][:, 0:1] + jnp.log(l[0])
        lse_ref[:, 1:2] = m_scr[1][:, 0:1] + jnp.log(l[1])

    return pl.pallas_call(
        body, name=name, grid=(npairs, nq),
        in_specs=[_SMEM, _SMEM, _SMEM, _SMEM,
                  pl.BlockSpec((None, 2, t, LANES), lambda p, i: (p, 0, i, 0)),
                  pl.BlockSpec((None, 2, s, LANES), lambda p, i: (p, 0, 0, 0)),
                  pl.BlockSpec((None, 2, s, LANES), lambda p, i: (p, 0, 0, 0))],
        out_specs=[pl.BlockSpec((t, LANES), lambda p, i: (i, p)), pl.BlockSpec((None, t, 2), lambda p, i: (p, i, 0))],
        out_shape=[jax.ShapeDtypeStruct((s, npairs * LANES), F32), jax.ShapeDtypeStruct((npairs, s, 2), F32)],
        scratch_shapes=[pltpu.VMEM((2, t, LANES), F32), pltpu.VMEM((2, t, LANES), F32)],
        compiler_params=_params("parallel", "arbitrary"),
    )(*bounds, qa, ka, va)


def _split3(x):
    hi = x.astype(BF16)
    r1 = x - hi.astype(F32)
    mid = r1.astype(BF16)
    return hi, mid, (r1 - mid.astype(F32)).astype(BF16)


def _attn_bwd_operands(qa, ka, dy, y, lse, *, t, name):
    npairs, _, s, _ = qa.shape

    def body(qa_ref, ka_ref, dy_ref, y_ref, lse_ref, qb_ref, kb_ref, dyb_ref):
        lane = lax.broadcasted_iota(jnp.int32, (t, LANES), 1)
        dyf = dy_ref[...]
        prod = dyf * y_ref[...]
        dyh = dyf.astype(BF16)
        one, zero = jnp.ones((t, LANES), BF16), jnp.zeros((t, LANES), BF16)
        for h in range(2):
            base = HEAD_DIM * (1 - h)
            mine = (lane < HEAD_DIM) if h == 0 else (lane >= HEAD_DIM)
            delta = jnp.sum(jnp.where(mine, prod, 0.0), axis=1, keepdims=True)
            qb, kb, dyb = qa_ref[h], ka_ref[h], jnp.where(mine, dyh, zero)
            for r, (lw, dw) in enumerate(zip(_split3(lse_ref[:, h:h + 1]), _split3(delta))):
                qb = jnp.where(lane == base + 6 + r, -lw, qb)
                kb = jnp.where(lane == base + 6 + r, one, kb)
                dyb = jnp.where(lane == base + r, -dw, dyb)
            qb_ref[h] = qb
            kb_ref[h] = kb
            dyb_ref[h] = dyb

    out = jax.ShapeDtypeStruct((npairs, 2, s, LANES), BF16)
    blk = pl.BlockSpec((None, 2, t, LANES), lambda p, i: (p, 0, i, 0))
    tile = pl.BlockSpec((t, LANES), lambda p, i: (i, p))
    return pl.pallas_call(
        body, name=name, grid=(npairs, s // t),
        in_specs=[blk, blk, tile, tile, pl.BlockSpec((None, t, 2), lambda p, i: (p, i, 0))],
        out_specs=[blk, blk, blk], out_shape=[out, out, out],
        compiler_params=_params("parallel", "parallel"),
    )(qa, ka, dy, y, lse)


def _diag_mask(sc, t, queries_on_rows):
    row = lax.broadcasted_iota(jnp.int32, (t, t), 0)
    col = lax.broadcasted_iota(jnp.int32, (t, t), 1)
    return jnp.where((row >= col) if queries_on_rows else (col >= row), sc, NEG_BIG)


def _fattn_dq(qb, kb, va, dyb, bounds, *, t, name):
    npairs, _, s, _ = qb.shape
    nq = s // t

    def body(qn_ref, kn_ref, cs_ref, ce_ref, q_ref, k_ref, v_ref, dy_ref, dq_ref, dcq_ref, acc_scr):
        pair, i = pl.program_id(0), pl.program_id(1)
        acc_scr[...] = jnp.zeros(acc_scr.shape, F32)

        def head_step(h, j, diagonal):
            off = pl.multiple_of(j * t, t)
            kj = k_ref[h, pl.ds(off, t), :]
            sc = _dot(q_ref[h], kj, _NT)
            if diagonal:
                sc = _diag_mask(sc, t, True)
            ds = jnp.exp(sc) * _dot(dy_ref[h], v_ref[h, pl.ds(off, t), :], _NT)
            acc_scr[h] += _dot(ds.astype(BF16), kj, _NN)

        def loop_body(j, carry):
            _for_active_heads((qn_ref, kn_ref, cs_ref, ce_ref), pair, i, j, nq, lambda h: head_step(h, j, False))
            return carry

        lax.fori_loop(0, i, loop_body, 0)
        for h in range(2):
            head_step(h, i, True)
        is_a = lax.broadcasted_iota(jnp.int32, (t, LANES), 1) < HEAD_DIM
        dq_ref[...] = (jnp.where(is_a, acc_scr[0], acc_scr[1]) * _QK_SCALE).astype(dq_ref.dtype)
        dcq_ref[:, 0:1] = acc_scr[0][:, HEAD_DIM:HEAD_DIM + 1]
        dcq_ref[:, 1:2] = acc_scr[1][:, 0:1]

    tile2 = pl.BlockSpec((None, 2, t, LANES), lambda p, i: (p, 0, i, 0))
    whole = pl.BlockSpec((None, 2, s, LANES), lambda p, i: (p, 0, 0, 0))
    return pl.pallas_call(
        body, name=name, grid=(npairs, nq),
        in_specs=[_SMEM, _SMEM, _SMEM, _SMEM, tile2, whole, whole, tile2],
        out_specs=[pl.BlockSpec((t, LANES), lambda p, i: (i, p)), pl.BlockSpec((None, t, 2), lambda p, i: (p, i, 0))],
        out_shape=[jax.ShapeDtypeStruct((s, npairs * LANES), BF16), jax.ShapeDtypeStruct((npairs, s, 2), F32)],
        scratch_shapes=[pltpu.VMEM((2, t, LANES), F32)],
        compiler_params=_params("parallel", "arbitrary"),
    )(*bounds, qb, kb, va, dyb)


def _fattn_dkv(qb, kb, va, dyb, bounds, *, t, name):
    npairs, _, s, _ = qb.shape
    nq = s // t

    def body(qn_ref, kn_ref, cs_ref, ce_ref, k_ref, v_ref, q_ref, dy_ref, dk_ref, dv_ref, dc_ref, dk_scr, dv_scr):
        pair, j = pl.program_id(0), pl.program_id(1)
        dk_scr[...] = jnp.zeros(dk_scr.shape, F32)
        dv_scr[...] = jnp.zeros(dv_scr.shape, F32)

        def head_step(h, i, diagonal):
            off = pl.multiple_of(i * t, t)
            qi = q_ref[h, pl.ds(off, t), :]
            dyi = dy_ref[h, pl.ds(off, t), :]
            st = _dot(k_ref[h], qi, _NT)
            if diagonal:
                st = _diag_mask(st, t, False)
            pt = jnp.exp(st)
            dv_scr[h] += _dot(pt.astype(BF16), dyi, _NN)
            dst = pt * _dot(v_ref[h], dyi, _NT)
            dk_scr[h] += _dot(dst.astype(BF16), qi, _NN)

        def loop_body(i, carry):
            _for_active_heads((qn_ref, kn_ref, cs_ref, ce_ref), pair, i, j, nq, lambda h: head_step(h, i, False))
            return carry

        for h in range(2):
            head_step(h, j, True)
        lax.fori_loop(j + 1, nq, loop_body, 0)
        is_a = lax.broadcasted_iota(jnp.int32, (t, LANES), 1) < HEAD_DIM
        dk_ref[...] = jnp.where(is_a, dk_scr[0], dk_scr[1]).astype(dk_ref.dtype)
        dv_ref[...] = jnp.where(is_a, dv_scr[0], dv_scr[1]).astype(dv_ref.dtype)
        dc_ref[:, 0:1] = -dk_scr[0][:, HEAD_DIM + 3:HEAD_DIM + 4]
        dc_ref[:, 1:2] = -dk_scr[1][:, 3:4]

    tile2 = pl.BlockSpec((None, 2, t, LANES), lambda p, j: (p, 0, j, 0))
    whole = pl.BlockSpec((None, 2, s, LANES), lambda p, j: (p, 0, 0, 0))
    tile = pl.BlockSpec((t, LANES), lambda p, j: (j, p))
    return pl.pallas_call(
        body, name=name, grid=(npairs, nq),
        in_specs=[_SMEM, _SMEM, _SMEM, _SMEM, tile2, tile2, whole, whole],
        out_specs=[tile, tile, pl.BlockSpec((None, t, 2), lambda p, j: (p, j, 0))],
        out_shape=[jax.ShapeDtypeStruct((s, npairs * LANES), BF16), jax.ShapeDtypeStruct((s, npairs * LANES), BF16),
                   jax.ShapeDtypeStruct((npairs, s, 2), F32)],
        scratch_shapes=[pltpu.VMEM((2, t, LANES), F32), pltpu.VMEM((2, t, LANES), F32)],
        compiler_params=_params("parallel", "arbitrary"),
    )(*bounds, kb, va, qb, dyb)


def _log_sigmoid(x):
    return jnp.minimum(x, 0.0) - jnp.log(1.0 + jnp.exp(-jnp.abs(x)))


def _cumsum_fwd(zf, b, *, name, t=512):
    s, w = zf.shape
    t = _row_tile(s, t)

    def body(zf_ref, b_ref, c_ref, carry):
        @pl.when(pl.program_id(0) == 0)
        def _():
            carry[...] = jnp.zeros(carry.shape, F32)

        lf = _log_sigmoid(zf_ref[...] + b_ref[...])
        row = lax.broadcasted_iota(jnp.int32, (t, t), 0)
        col = lax.broadcasted_iota(jnp.int32, (t, t), 1)
        tri = (row >= col).astype(F32)
        c = lax.dot_general(tri, lf, _NN, precision=lax.Precision.HIGHEST, preferred_element_type=F32) + carry[...]
        c_ref[...] = c
        carry[...] = c[t - 1:t, :]

    return pl.pallas_call(
        body, name=name, grid=(s // t,),
        in_specs=[pl.BlockSpec((t, w), lambda i: (i, 0)), pl.BlockSpec((1, w), lambda i: (0, 0))],
        out_specs=pl.BlockSpec((t, w), lambda i: (i, 0)),
        out_shape=jax.ShapeDtypeStruct((s, w), F32),
        scratch_shapes=[pltpu.VMEM((1, w), F32)],
        compiler_params=_params("arbitrary"),
    )(zf, b)


def _cumsum_bwd(dcq, dck, zf, b, *, name, t=512):
    s, w = zf.shape
    t = _row_tile(s, t)
    nb = s // t

    def body(dcq_ref, dck_ref, zf_ref, b_ref, dzf_ref, db_ref, carry):
        @pl.when(pl.program_id(0) == 0)
        def _():
            carry[...] = jnp.zeros(carry.shape, F32)
            db_ref[...] = jnp.zeros(db_ref.shape, F32)

        row = lax.broadcasted_iota(jnp.int32, (t, t), 0)
        col = lax.broadcasted_iota(jnp.int32, (t, t), 1)
        tri = (row <= col).astype(F32)
        dc = dcq_ref[...] + dck_ref[...]
        dlf = lax.dot_general(tri, dc, _NN, precision=lax.Precision.HIGHEST, preferred_element_type=F32) + carry[...]
        carry[...] = dlf[0:1, :]
        dzf = dlf * jax.nn.sigmoid(-(zf_ref[...] + b_ref[...]))
        dzf_ref[...] = dzf
        db_ref[...] += jnp.sum(dzf, axis=0, keepdims=True)

    blk = pl.BlockSpec((t, w), lambda i: (nb - 1 - i, 0))
    vec = pl.BlockSpec((1, w), lambda i: (0, 0))
    return pl.pallas_call(
        body, name=name, grid=(nb,), in_specs=[blk, blk, blk, vec], out_specs=[blk, vec],
        out_shape=[jax.ShapeDtypeStruct((s, w), F32), jax.ShapeDtypeStruct((1, w), F32)],
        scratch_shapes=[pltpu.VMEM((1, w), F32)],
        compiler_params=_params("arbitrary"),
    )(dcq, dck, zf, b)


HALO = 8


def _rms_rows(v):
    return lax.rsqrt(jnp.mean(v * v, axis=-1, keepdims=True) + EPS)


def _mixpost_fwd(z, yatt, wconv, gc, ga, *, name, tr=512):
    s = z.shape[0]
    cw, aw = gc.shape[-1], ga.shape[-1]
    tr = _row_tile(s, tr)

    def body(zb_ref, zc_ref, zv_ref, ya_ref, w_ref, gc_ref, ga_ref, ycat_ref, cv_ref, u_scr):
        @pl.when(pl.program_id(0) == 0)
        def _():
            u_scr[0:HALO, :] = jnp.zeros((HALO, cw), F32)

        u = zc_ref[...].astype(F32) * zv_ref[...].astype(F32)
        u_scr[HALO:HALO + tr, :] = u
        cv = w_ref[0:1, :] * u_scr[HALO - 2:HALO - 2 + tr, :] + w_ref[1:2, :] * u_scr[HALO - 1:HALO - 1 + tr, :] + w_ref[2:3, :] * u
        u_scr[0:HALO, :] = u_scr[tr:tr + HALO, :]
        cv_ref[...] = cv
        yc = zb_ref[...].astype(F32) * cv
        ya = ya_ref[...]
        ycat_ref[:, :cw] = (yc * _rms_rows(yc) * gc_ref[...]).astype(ycat_ref.dtype)
        ycat_ref[:, cw:] = (ya * _rms_rows(ya) * ga_ref[...]).astype(ycat_ref.dtype)

    return pl.pallas_call(
        body, name=name, grid=(s // tr,),
        in_specs=[
            pl.BlockSpec((tr, cw), lambda i: (i, 0)), pl.BlockSpec((tr, cw), lambda i: (i, 1)),
            pl.BlockSpec((tr, cw), lambda i: (i, 2)), pl.BlockSpec((tr, aw), lambda i: (i, 0)),
            pl.BlockSpec((HALO, cw), lambda i: (0, 0)), pl.BlockSpec((1, cw), lambda i: (0, 0)),
            pl.BlockSpec((1, aw), lambda i: (0, 0)),
        ],
        out_specs=[pl.BlockSpec((tr, cw + aw), lambda i: (i, 0)), pl.BlockSpec((tr, cw), lambda i: (i, 0))],
        out_shape=[jax.ShapeDtypeStruct((s, cw + aw), BF16), jax.ShapeDtypeStruct((s, cw), F32)],
        scratch_shapes=[pltpu.VMEM((tr + HALO, cw), F32)],
        compiler_params=_params("arbitrary"),
    )(z, z, z, yatt, wconv, gc.reshape(1, cw), ga.reshape(1, aw))


def _mixpost_bwd(dycat, z, yatt, cv, wconv, gc, ga, *, name, tr=512):
    s = z.shape[0]
    cw, aw = gc.shape[-1], ga.shape[-1]
    tr = _row_tile(s, tr)
    nb = s // tr

    def body(dy_ref, zb_ref, zc_ref, zv_ref, ya_ref, cv_ref, w_ref, gc_ref, ga_ref,
             dz_ref, dya_ref, dw_ref, dgc_ref, dga_ref, d_scr):
        @pl.when(pl.program_id(0) == 0)
        def _():
            d_scr[tr:tr + HALO, :] = jnp.zeros((HALO, cw), F32)
            dw_ref[...] = jnp.zeros(dw_ref.shape, F32)
            dgc_ref[...] = jnp.zeros(dgc_ref.shape, F32)
            dga_ref[...] = jnp.zeros(dga_ref.shape, F32)

        zb, zc, zv = zb_ref[...].astype(F32), zc_ref[...].astype(F32), zv_ref[...].astype(F32)
        cvv = cv_ref[...]

        def norm_bwd(v, dn, g):
            r = _rms_rows(v)
            vh = v * r
            dvh = dn * g
            return r * (dvh - vh * jnp.mean(dvh * vh, axis=-1, keepdims=True)), jnp.sum(dn * vh, axis=0, keepdims=True)

        dyc, dgc = norm_bwd(zb * cvv, dy_ref[:, :cw], gc_ref[...])
        dya, dga = norm_bwd(ya_ref[...], dy_ref[:, cw:], ga_ref[...])
        dgc_ref[...] += dgc
        dga_ref[...] += dga
        dya_ref[...] = dya
        dcv = dyc * zb
        d_scr[0:tr, :] = dcv
        d1 = d_scr[1:tr + 1, :]
        d2 = d_scr[2:tr + 2, :]
        du = w_ref[2:3, :] * dcv + w_ref[1:2, :] * d1 + w_ref[0:1, :] * d2
        u = zc * zv
        dw_ref[0:1, :] += jnp.sum(u * d2, axis=0, keepdims=True)
        dw_ref[1:2, :] += jnp.sum(u * d1, axis=0, keepdims=True)
        dw_ref[2:3, :] += jnp.sum(u * dcv, axis=0, keepdims=True)
        d_scr[tr:tr + HALO, :] = d_scr[0:HALO, :]
        dz_ref[:, :cw] = (dyc * cvv).astype(dz_ref.dtype)
        dz_ref[:, cw:2 * cw] = (du * zv).astype(dz_ref.dtype)
        dz_ref[:, 2 * cw:] = (du * zc).astype(dz_ref.dtype)

    def rows(width, colblk=0):
        return pl.BlockSpec((tr, width), lambda i: (nb - 1 - i, colblk))

    def fixed(r, width):
        return pl.BlockSpec((r, width), lambda i: (0, 0))

    return pl.pallas_call(
        body, name=name, grid=(nb,),
        in_specs=[rows(cw + aw), rows(cw, 0), rows(cw, 1), rows(cw, 2), rows(aw), rows(cw),
                  fixed(HALO, cw), fixed(1, cw), fixed(1, aw)],
        out_specs=[rows(3 * cw), rows(aw), fixed(HALO, cw), fixed(1, cw), fixed(1, aw)],
        out_shape=[jax.ShapeDtypeStruct((s, 3 * cw), BF16), jax.ShapeDtypeStruct((s, aw), F32),
                   jax.ShapeDtypeStruct((HALO, cw), F32), jax.ShapeDtypeStruct((1, cw), F32),
                   jax.ShapeDtypeStruct((1, aw), F32)],
        scratch_shapes=[pltpu.VMEM((tr + HALO, cw), F32)],
        compiler_params=_params("arbitrary"),
    )(dycat, z, z, z, yatt, cv, wconv, gc.reshape(1, cw), ga.reshape(1, aw))


def _xattn_fwd(q, kv, *, name, tq=1024):
    s, d = q.shape
    m = kv.shape[0]
    dh = d // N_XHEADS
    scale = dh ** -0.5
    tq = _row_tile(s, tq)

    def body(q_ref, kv_ref, o_ref):
        for h in range(N_XHEADS):
            lo, hi = h * dh, (h + 1) * dh
            sc = _dot(q_ref[:, lo:hi], kv_ref[:, lo:hi], _NT) * scale
            p = jnp.exp(sc - jnp.max(sc, axis=1, keepdims=True))
            o = _dot(p.astype(BF16), kv_ref[:, d + lo:d + hi], _NN) / jnp.sum(p, axis=1, keepdims=True)
            o_ref[:, lo:hi] = o.astype(o_ref.dtype)

    return pl.pallas_call(
        body, name=name, grid=(s // tq,),
        in_specs=[pl.BlockSpec((tq, d), lambda i: (i, 0)), pl.BlockSpec((m, 2 * d), lambda i: (0, 0))],
        out_specs=pl.BlockSpec((tq, d), lambda i: (i, 0)),
        out_shape=jax.ShapeDtypeStruct((s, d), BF16),
        compiler_params=_params("parallel"),
    )(q, kv)


def _xattn_bwd(q, kv, do, *, name, tq=1024):
    s, d = q.shape
    m = kv.shape[0]
    dh = d // N_XHEADS
    scale = dh ** -0.5
    tq = _row_tile(s, tq)

    def body(q_ref, kv_ref, do_ref, dq_ref, dkv_ref):
        @pl.when(pl.program_id(0) == 0)
        def _():
            dkv_ref[...] = jnp.zeros(dkv_ref.shape, F32)

        for h in range(N_XHEADS):
            lo, hi = h * dh, (h + 1) * dh
            qh, kh, vh, doh = q_ref[:, lo:hi], kv_ref[:, lo:hi], kv_ref[:, d + lo:d + hi], do_ref[:, lo:hi]
            sc = _dot(qh, kh, _NT) * scale
            e = jnp.exp(sc - jnp.max(sc, axis=1, keepdims=True))
            p = e / jnp.sum(e, axis=1, keepdims=True)
            dp = _dot(doh, vh, _NT)
            ds = p * (dp - jnp.sum(dp * p, axis=1, keepdims=True))
            dsb = ds.astype(BF16)
            dq_ref[:, lo:hi] = (_dot(dsb, kh, _NN) * scale).astype(dq_ref.dtype)
            dkv_ref[:, lo:hi] += _dot(dsb, qh, _TN) * scale
            dkv_ref[:, d + lo:d + hi] += _dot(p.astype(BF16), doh, _TN)

    return pl.pallas_call(
        body, name=name, grid=(s // tq,),
        in_specs=[pl.BlockSpec((tq, d), lambda i: (i, 0)), pl.BlockSpec((m, 2 * d), lambda i: (0, 0)),
                  pl.BlockSpec((tq, d), lambda i: (i, 0))],
        out_specs=[pl.BlockSpec((tq, d), lambda i: (i, 0)), pl.BlockSpec((m, 2 * d), lambda i: (0, 0))],
        out_shape=[jax.ShapeDtypeStruct((s, d), BF16), jax.ShapeDtypeStruct((m, 2 * d), F32)],
        compiler_params=_params("arbitrary"),
    )(q, kv, do)


def _final_loss(x, g, target, *, name, tr=512):
    s, d = x.shape
    tr = _row_tile(s, tr)

    def body(x_ref, g_ref, t_ref, dx_ref, dg_ref, sq_ref):
        @pl.when(pl.program_id(0) == 0)
        def _():
            dg_ref[...] = jnp.zeros(dg_ref.shape, F32)
            sq_ref[...] = jnp.zeros(sq_ref.shape, F32)

        xf = x_ref[...]
        r = _rms_rows(xf)
        xhat = xf * r
        err = xhat * g_ref[...] - t_ref[...]
        sq_ref[...] += jnp.sum(err * err, axis=0, keepdims=True)
        dy = err * (1.0 / d)
        dg_ref[...] += jnp.sum(dy * xhat, axis=0, keepdims=True)
        dxhat = dy * g_ref[...]
        dx_ref[...] = r * (dxhat - xhat * jnp.mean(dxhat * xhat, axis=-1, keepdims=True))

    row = pl.BlockSpec((tr, d), lambda i: (i, 0))
    vec = pl.BlockSpec((1, d), lambda i: (0, 0))
    return pl.pallas_call(
        body, name=name, grid=(s // tr,), in_specs=[row, vec, row], out_specs=[row, vec, vec],
        out_shape=[jax.ShapeDtypeStruct((s, d), F32), jax.ShapeDtypeStruct((1, d), F32), jax.ShapeDtypeStruct((1, d), F32)],
        compiler_params=_params("arbitrary"),
    )(x, g.reshape(1, d), target)


def _adamw(w, g, m, v, *, name, tr=512):
    shape = w.shape
    cols = shape[-1]
    rows = w.size // cols
    tr = tr if rows % tr == 0 else rows

    def body(w_ref, g_ref, m_ref, v_ref, d_ref, nm_ref, nv_ref):
        gf = g_ref[...]
        nm = ADAM_B1 * m_ref[...] + (1.0 - ADAM_B1) * gf
        nv = ADAM_B2 * v_ref[...] + (1.0 - ADAM_B2) * (gf * gf)
        m_hat = nm / (1.0 - ADAM_B1 ** ADAM_STEP)
        v_hat = nv / (1.0 - ADAM_B2 ** ADAM_STEP)
        d_ref[...] = -ADAM_LR * (m_hat / (jnp.sqrt(v_hat) + ADAM_EPS) + ADAM_WD * w_ref[...])
        nm_ref[...] = nm
        nv_ref[...] = nv

    blk = pl.BlockSpec((tr, cols), lambda i: (i, 0))
    out = jax.ShapeDtypeStruct((rows, cols), F32)
    outs = pl.pallas_call(
        body, name=name, grid=(rows // tr,), in_specs=[blk] * 4, out_specs=[blk] * 3, out_shape=[out] * 3,
        compiler_params=_params("parallel"),
    )(*[t.reshape(rows, cols) for t in (w, g, m, v)])
    return tuple(o.reshape(shape) for o in outs)


_HBM = pl.BlockSpec(memory_space=pl.ANY)
_MESH_ID = pl.DeviceIdType.MESH


def _place():
    x, y, c = (lax.axis_index(a) for a in MESH_AXES)
    return x, y, c, [(1 - x, y), (x, 1 - y), (1 - x, 1 - y)]


def _remote(src, dst, send_sems, recv_sems, k, to):
    return pltpu.make_async_remote_copy(src_ref=src, dst_ref=dst, send_sem=send_sems.at[k], recv_sem=recv_sems.at[k],
                                        device_id=to, device_id_type=_MESH_ID)


def _comm_call(body, arrays, out_shapes, n_remote, name):
    return pl.pallas_call(
        body, name=name, in_specs=[_HBM] * len(arrays), out_specs=[_HBM] * len(out_shapes), out_shape=out_shapes,
        scratch_shapes=[pltpu.SemaphoreType.DMA((n_remote,)), pltpu.SemaphoreType.DMA((n_remote,)),
                        pltpu.SemaphoreType.DMA((len(arrays),))],
    )(*arrays)


def _allgather_weights(halves, *, name):
    n = len(halves)

    def body(*refs):
        w, out = refs[:n], refs[n:2 * n]
        send_sems, recv_sems, local_sems = refs[2 * n:]
        x, y, c, chips = _place()
        me = 2 * x + y
        sibling = (x, y, 1 - c)
        slots = [2 * px + py for px, py in chips]

        def copy(t, k, slot, half, to, src=None):
            dst = out[t].at[slot, half]
            return _remote(dst if src is None else src, dst, send_sems, recv_sems, 7 * t + k, to)

        local = [pltpu.make_async_copy(w[t].at[c], out[t].at[me, c], local_sems.at[t]) for t in range(n)]
        first = []
        for t in range(n):
            local[t].start()
            first.append(copy(t, 0, me, c, sibling, src=w[t].at[c]))
            first += [copy(t, 1 + j, me, c, (px, py, c), src=w[t].at[c]) for j, (px, py) in enumerate(chips)]
        for cp in first:
            cp.start()
        passed = []
        for j, (px, py) in enumerate(chips):
            for t in range(n):
                copy(t, 1 + j, slots[j], c, (px, py, c)).wait_recv()
                passed.append(copy(t, 4 + j, slots[j], c, sibling))
                passed[-1].start()
        for t in range(n):
            copy(t, 0, me, 1 - c, sibling).wait_recv()
            for j in range(3):
                copy(t, 4 + j, slots[j], 1 - c, sibling).wait_recv()
        for cp in first + passed:
            cp.wait_send()
        for cp in local:
            cp.wait()

    shapes = [jax.ShapeDtypeStruct((N_CHIPS,) + h.shape, h.dtype) for h in halves]
    return _comm_call(body, halves, shapes, 7 * n, name)


def _swap_halves(gs, *, name):
    n = len(gs)

    def body(*refs):
        g, out = refs[:n], refs[n:2 * n]
        send_sems, recv_sems, _ = refs[2 * n:]
        x, y, c, _ = _place()
        copies = [_remote(g[t].at[1 - c], out[t], send_sems, recv_sems, t, (x, y, 1 - c)) for t in range(n)]
        for cp in copies:
            cp.start()
        for cp in copies:
            cp.wait()

    return _comm_call(body, gs, [jax.ShapeDtypeStruct(g.shape[1:], g.dtype) for g in gs], n, name)


def _scatter_chips(ps, *, name):
    n = len(ps)

    def body(*refs):
        p, out = refs[:n], refs[n:2 * n]
        send_sems, recv_sems, local_sems = refs[2 * n:]
        x, y, c, chips = _place()
        me = 2 * x + y
        local = [pltpu.make_async_copy(p[t].at[me], out[t].at[me], local_sems.at[t]) for t in range(n)]
        sends = [_remote(p[t].at[2 * px + py], out[t].at[me], send_sems, recv_sems, 3 * t + j, (px, py, c))
                 for t in range(n) for j, (px, py) in enumerate(chips)]
        for cp in local + sends:
            cp.start()
        for t in range(n):
            for j, (px, py) in enumerate(chips):
                slot = out[t].at[2 * px + py]
                _remote(slot, slot, send_sems, recv_sems, 3 * t + j, (px, py, c)).wait_recv()
        for cp in sends:
            cp.wait_send()
        for cp in local:
            cp.wait()

    return _comm_call(body, ps, [jax.ShapeDtypeStruct(p.shape, p.dtype) for p in ps], 3 * n, name)


def _join_halves(rs, *, name):
    n = len(rs)

    def body(*refs):
        r, out = refs[:n], refs[n:2 * n]
        send_sems, recv_sems, _ = refs[2 * n:]
        x, y, c, _ = _place()
        sends = [_remote(r[t].at[c], out[t].at[c], send_sems, recv_sems, t, (x, y, 1 - c)) for t in range(n)]
        for cp in sends:
            cp.start()
        for t in range(n):
            slot = out[t].at[1 - c]
            _remote(slot, slot, send_sems, recv_sems, t, (x, y, 1 - c)).wait_recv()
        for cp in sends:
            cp.wait_send()

    return pl.pallas_call(
        body, name=name, in_specs=[_HBM] * n, out_specs=[_HBM] * n,
        out_shape=[jax.ShapeDtypeStruct(r.shape, r.dtype) for r in rs],
        input_output_aliases={t: t for t in range(n)},
        scratch_shapes=[pltpu.SemaphoreType.DMA((n,)), pltpu.SemaphoreType.DMA((n,)), pltpu.SemaphoreType.DMA((n,))],
    )(*rs)


def _pick_rows(rows, cap=512):
    for t in range(min(rows, cap), 0, -1):
        if rows % t == 0 and t % 16 == 0:
            return t
    return rows


def _add_halves(g, recv, c, out_dtype, *, name):
    cols = g.shape[-1]
    rows = recv.size // (N_CHIPS * cols)
    tr = _pick_rows(rows)

    def body(c_ref, g_ref, r_ref, o_ref):
        o_ref[...] = (g_ref[...] + r_ref[...]).astype(o_ref.dtype)

    blk = pl.BlockSpec((None, tr, cols), lambda b, i, c_ref: (b, i, 0))
    out = pl.pallas_call(
        body, name=name,
        grid_spec=pltpu.PrefetchScalarGridSpec(
            num_scalar_prefetch=1, grid=(N_CHIPS, rows // tr),
            in_specs=[pl.BlockSpec((None, None, tr, cols), lambda b, i, c_ref: (c_ref[0], b, i, 0)), blk],
            out_specs=blk),
        out_shape=jax.ShapeDtypeStruct((N_CHIPS, rows, cols), out_dtype),
        compiler_params=_params("parallel", "parallel"),
    )(c.reshape(1).astype(jnp.int32), g.reshape(2, N_CHIPS, rows, cols), recv.reshape(N_CHIPS, rows, cols))
    return out.reshape(recv.shape)


def _sum_slots(p, c, *, name):
    cols = p.shape[-1]
    rows = p.size // (N_CHIPS * cols)
    tr = _pick_rows(rows)

    def body(c_ref, p_ref, o_ref):
        acc = p_ref[0].astype(F32)
        for q in range(1, N_CHIPS):
            acc = acc + p_ref[q].astype(F32)
        o_ref[...] = acc

    out = pl.pallas_call(
        body, name=name,
        grid_spec=pltpu.PrefetchScalarGridSpec(
            num_scalar_prefetch=1, grid=(rows // tr,),
            in_specs=[pl.BlockSpec((N_CHIPS, tr, cols), lambda i, c_ref: (0, i, 0))],
            out_specs=pl.BlockSpec((None, tr, cols), lambda i, c_ref: (c_ref[0], i, 0))),
        out_shape=jax.ShapeDtypeStruct((2, rows, cols), F32),
        compiler_params=_params("parallel"),
    )(c.reshape(1).astype(jnp.int32), p.reshape(N_CHIPS, rows, cols))
    return out.reshape((2,) + p.shape[1:])


GROUPS = (("gu", ("w_ffn1_gu", "w_ffn2_gu"), "col"), ("down", ("w_ffn1_down", "w_ffn2_down"), "row"),
          ("square", ("w_mix_out", "w_xq", "w_xo"), "row"), ("mix_in", ("w_mix_in",), "col"), ("xkv", ("w_xkv",), "col"))
REPLICATED = ("g_ffn1", "g_mix", "b_f", "g_conv_out", "g_att_out", "g_xattn", "g_mem", "g_ffn2", "g_final")
WEIGHTS = ("g_ffn1", "w_ffn1_gu", "w_ffn1_down", "g_mix", "w_mix_in", "w_conv", "b_f", "g_conv_out", "g_att_out",
           "w_mix_out", "g_xattn", "g_mem", "w_xq", "w_xkv", "w_xo", "g_ffn2", "w_ffn2_gu", "w_ffn2_down", "g_final")
SMALL_COLS = 1024
SMALL_ROW_UNIT = 16


def _gather_weights(shards):
    packs = []
    for _, members, _ in GROUPS:
        hs = []
        for name in members:
            l, a, b = shards[name].shape
            hs.append(shards[name].astype(BF16).reshape(l, 2, a // 2, b).transpose(1, 0, 2, 3))
        packs.append(jnp.stack(hs, axis=1))
    wc = shards["w_conv"]
    packs.append(jnp.stack([wc, wc]))
    gathered = _allgather_weights(packs, name="allgather_weights")
    full = {}
    for (_, members, kind), got in zip(GROUPS, gathered):
        _, _, g, l, a2, b = got.shape
        if kind == "col":
            whole = got.transpose(2, 3, 1, 4, 0, 5).reshape(g, l, 2 * a2, N_CHIPS * b)
        else:
            whole = got.transpose(2, 3, 0, 1, 4, 5).reshape(g, l, N_CHIPS * 2 * a2, b)
        for gi, name in enumerate(members):
            full[name] = whole[gi]
    l, k, b = wc.shape
    full["w_conv"] = gathered[-1][:, 0].transpose(1, 2, 0, 3).reshape(l, k, N_CHIPS * b)
    return full


def _small_rows(v):
    flat = v.reshape(-1)
    return jnp.pad(flat, (0, -flat.shape[0] % SMALL_COLS)).reshape(-1, SMALL_COLS)


def _reduce_gradients(grads, c):
    packs = []
    for _, members, kind in GROUPS:
        cut = []
        for name in members:
            l, a, b = grads[name].shape
            if kind == "col":
                cut.append(grads[name].reshape(l, 2, a // 2, N_CHIPS, b // N_CHIPS).transpose(1, 3, 0, 2, 4))
            else:
                cut.append(grads[name].reshape(l, N_CHIPS, 2, a // (2 * N_CHIPS), b).transpose(2, 1, 0, 3, 4))
        packs.append(jnp.stack(cut, axis=2))
    rep = jnp.concatenate([_small_rows(grads[n]) for n in REPLICATED])
    l, k, b = grads["w_conv"].shape
    conv = grads["w_conv"].reshape(l, k, N_CHIPS, b // N_CHIPS).transpose(2, 0, 1, 3)
    conv_rows = [_small_rows(conv[q]) for q in range(N_CHIPS)]
    n_rows = rep.shape[0] + conv_rows[0].shape[0]
    fill = jnp.zeros((-n_rows % SMALL_ROW_UNIT, SMALL_COLS), F32)
    small = jnp.stack([jnp.concatenate([rep, conv_rows[q], fill]) for q in range(N_CHIPS)])
    half_rows = small.shape[1] // 2
    packs.append(small.reshape(N_CHIPS, 2, half_rows, SMALL_COLS).transpose(1, 0, 2, 3))

    from_sibling = _swap_halves(packs, name="grad_swap_halves")
    wire = [BF16] * len(GROUPS) + [F32]
    chip_sums = [_add_halves(g, r, c, dt, name=f"grad_add_halves_{i}")
                 for i, (g, r, dt) in enumerate(zip(packs, from_sibling, wire))]
    from_chips = _scatter_chips(chip_sums, name="grad_scatter_chips")
    halves = [_sum_slots(p, c, name=f"grad_sum_chips_{i}") for i, p in enumerate(from_chips)]
    reduced = _join_halves(halves, name="grad_join_halves")

    out = {}
    for (_, members, _), r in zip(GROUPS, reduced):
        _, g, l, a2, b = r.shape
        whole = r.transpose(1, 2, 0, 3, 4).reshape(g, l, 2 * a2, b)
        for gi, name in enumerate(members):
            out[name] = whole[gi]
    rows = reduced[-1].reshape(-1, SMALL_COLS)
    off = 0
    for name in REPLICATED:
        n = -(-grads[name].size // SMALL_COLS)
        out[name] = rows[off:off + n].reshape(-1)[:grads[name].size].reshape(grads[name].shape)
        off += n
    l, k, b = grads["w_conv"].shape
    out["w_conv"] = rows[off:off + conv_rows[0].shape[0]].reshape(-1)[:l * k * b // N_CHIPS].reshape(l, k, b // N_CHIPS)
    return out


def _residual_out(a, w_out, x, g_next, name, alpha=1.0):
    kw = dict(tm=1024, tn=1024, tk=w_out.shape[0], alpha=alpha, residual=x, name=name)
    if g_next is None:
        return _matmul(a, w_out, "nn", F32, **kw), None
    return _matmul(a, w_out, "nn", F32, norm_gain=g_next, **kw)


def _ffn_fwd(x, h, w_gu, w_down, g_next, tag):
    gu, a = _gate_up_act(h, w_gu, name=f"{tag}_gu_act")
    y, h_next = _residual_out(a, w_down, x, g_next, f"{tag}_down", alpha=0.5)
    return y, h_next, (x, h, gu, a)


def _ffn_bwd(dy, saved, g, w_gu, w_down, tag):
    x, h, gu, a = saved
    f = w_down.shape[0]
    dgu = _down_bwd_act(dy, w_down, gu, 0.5, name=f"{tag}_da_dact")
    dw_down = _matmul(a, dy, "tn", F32, tm=f // 2, tn=1024, tk=1024, alpha=0.5, name=f"{tag}_dwdown")
    dw_gu = jnp.concatenate([_matmul(h, dgu, "tn", F32, tm=1024, tn=f // 2, tk=2048, b_slab=half, name=f"{tag}_dwgu{half}")
                             for half in range(2)], axis=1)
    dx, dg = _matmul(dgu, w_gu, "nt", F32, tm=1024, tn=1024, tk=f // 2, residual=dy, dnorm=(x, g), a_slabs=True,
                     name=f"{tag}_dh_dnorm")
    return dx, dg[0], dw_gu, dw_down


def _mix_fwd(x, h, w, l, g_next, tag):
    s, d = x.shape
    gc, ga = w["g_conv_out"][l], w["g_att_out"][l]
    cw, aw = gc.shape[0], ga.shape[0]
    nh = aw // HEAD_DIM
    zw = 3 * cw + 3 * aw
    t = 512 if s >= 2048 else s // 4
    cols = dict(qcol=3 * cw // LANES, kcol=(3 * cw + aw) // LANES, vcol=(3 * cw + 2 * aw) // LANES)
    w_in = w["w_mix_in"][l]
    w_main = w_in[:, :zw]
    w_f = jnp.pad(w_in[:, zw:], ((0, 0), (0, LANES - nh)))
    z = _matmul(h, w_main, "nn", BF16, tm=2048, tn=512, tk=d, name=f"{tag}_in")
    zf = _matmul(h, w_f, "nn", F32, tm=2048, tn=LANES, tk=d, name=f"{tag}_in_f")
    b = jnp.pad(w["b_f"][l], (0, LANES - nh)).reshape(1, LANES)
    c = _cumsum_fwd(zf, b, name=f"{tag}_cumsum")
    ccol = c[:, :nh].reshape(s, nh // 2, 2).transpose(1, 0, 2)
    assert (3 * cw) % aw == 0
    bounds = _attn_bounds(z, c, qcol_units=3 * cw // aw, kcol_units=3 * cw // aw + 1, aw=aw, t=t, name=f"{tag}_bounds")
    qa, ka, va = _attn_operands(z, ccol, t=t, name=f"{tag}_attn_operands", **cols)
    yatt, lse = _fattn_fwd(qa, ka, va, bounds, t=t, name=f"{tag}_attn")
    wc = jnp.pad(w["w_conv"][l], ((0, HALO - CONV_K), (0, 0)))
    ycat, cv = _mixpost_fwd(z, yatt, wc, gc, ga, name=f"{tag}_post")
    y, h_next = _residual_out(ycat, w["w_mix_out"][l], x, g_next, f"{tag}_out")
    return y, h_next, (x, h, w_main, w_f, z, zf, b, qa, ka, va, bounds, yatt, lse, wc, ycat, cv, t)


def _mix_bwd(dy, saved, w, l, tag):
    x, h, w_main, w_f, z, zf, b, qa, ka, va, bounds, yatt, lse, wc, ycat, cv, t = saved
    s, d = x.shape
    gc, ga = w["g_conv_out"][l], w["g_att_out"][l]
    nh = ga.shape[0] // HEAD_DIM
    zw = w_main.shape[1]
    dycat = _matmul(dy, w["w_mix_out"][l], "nt", F32, tm=1024, tn=1024, tk=d, name=f"{tag}_dycat")
    dw_out = _matmul(ycat, dy, "tn", F32, tm=1024, tn=1024, tk=1024, name=f"{tag}_dwout")
    dz_conv, dyatt, dwc, dgc, dga = _mixpost_bwd(dycat, z, yatt, cv, wc, gc, ga, name=f"{tag}_dpost")
    qb, kb, dyb = _attn_bwd_operands(qa, ka, dyatt, yatt, lse, t=t, name=f"{tag}_attn_bwd_operands")
    dq, dcq = _fattn_dq(qb, kb, va, dyb, bounds, t=t, name=f"{tag}_attn_dq")
    dk, dv, dck = _fattn_dkv(qb, kb, va, dyb, bounds, t=t, name=f"{tag}_attn_dkv")
    def heads_on_lanes(v):
        return jnp.pad(v.transpose(1, 0, 2).reshape(s, nh), ((0, 0), (0, LANES - nh)))

    dzf, db = _cumsum_bwd(heads_on_lanes(dcq), heads_on_lanes(dck), zf, b, name=f"{tag}_dcumsum")
    dz = jnp.concatenate([dz_conv, dq, dk, dv], axis=1)
    dw_main = _matmul(h, dz, "tn", F32, tm=1024, tn=512, tk=2048, name=f"{tag}_dwin")
    dw_f = _matmul(h, dzf, "tn", F32, tm=1024, tn=LANES, tk=2048, name=f"{tag}_dwin_f")
    dh_f = _matmul(dzf, w_f, "nt", F32, tm=1024, tn=1024, tk=LANES, name=f"{tag}_dh_f")
    dx, dg = _matmul(dz, w_main, "nt", F32, tm=512, tn=1024, tk=zw // 2, addend=dh_f, residual=dy,
                     dnorm=(x, w["g_mix"][l]), name=f"{tag}_dh_dnorm")
    grads = dict(g_mix=dg[0], w_mix_in=jnp.concatenate([dw_main, dw_f[:, :nh]], axis=1), w_conv=dwc[:CONV_K], b_f=db[0, :nh],
                 g_conv_out=dgc[0], g_att_out=dga[0], w_mix_out=dw_out)
    return dx, grads


def _xattn_block_fwd(x, h, mem, w, l, g_next, tag):
    d = x.shape[1]
    mn = _rmsnorm_fwd(mem, w["g_mem"][l], name=f"{tag}_mem_norm")
    q = _matmul(h, w["w_xq"][l], "nn", BF16, tm=1024, tn=1024, tk=d, name=f"{tag}_q")
    kv = _matmul(mn, w["w_xkv"][l], "nn", BF16, tm=1024, tn=1024, tk=d, name=f"{tag}_kv")
    o = _xattn_fwd(q, kv, name=f"{tag}_core")
    y, h_next = _residual_out(o, w["w_xo"][l], x, g_next, f"{tag}_o")
    return y, h_next, (x, h, mn, q, kv, o)


def _xattn_block_bwd(dy, saved, mem, w, l, tag):
    x, h, mn, q, kv, o = saved
    d = x.shape[1]
    do = _matmul(dy, w["w_xo"][l], "nt", BF16, tm=1024, tn=1024, tk=d, name=f"{tag}_do")
    dw_xo = _matmul(o, dy, "tn", F32, tm=1024, tn=1024, tk=1024, name=f"{tag}_dwo")
    dq, dkv = _xattn_bwd(q, kv, do, name=f"{tag}_dcore")
    dw_xq = _matmul(h, dq, "tn", F32, tm=1024, tn=1024, tk=2048, name=f"{tag}_dwq")
    dx, dg = _matmul(dq, w["w_xq"][l], "nt", F32, tm=1024, tn=1024, tk=d, residual=dy, dnorm=(x, w["g_xattn"][l]),
                     name=f"{tag}_dh_dnorm")
    dw_xkv = _matmul(mn, dkv, "tn", F32, tm=1024, tn=1024, tk=1024, name=f"{tag}_dwkv")
    dmn = _matmul(dkv, w["w_xkv"][l], "nt", F32, tm=1024, tn=1024, tk=1024, name=f"{tag}_dmem")
    _, dg_mem = _rmsnorm_bwd(mem, w["g_mem"][l], dmn, None, name=f"{tag}_dmem_norm")
    return dx, dict(g_xattn=dg[0], g_mem=dg_mem, w_xq=dw_xq, w_xkv=dw_xkv, w_xo=dw_xo)


def kernel(x, mem, g_ffn1, w_ffn1_gu, w_ffn1_down, g_mix, w_mix_in, w_conv, b_f, g_conv_out, g_att_out, w_mix_out, g_xattn, g_mem, w_xq, w_xkv, w_xo, g_ffn2, w_ffn2_gu, w_ffn2_down, g_final, loss_target, m_g_ffn1, m_w_ffn1_gu, m_w_ffn1_down, m_g_mix, m_w_mix_in, m_w_conv, m_b_f, m_g_conv_out, m_g_att_out, m_w_mix_out, m_g_xattn, m_g_mem, m_w_xq, m_w_xkv, m_w_xo, m_g_ffn2, m_w_ffn2_gu, m_w_ffn2_down, m_g_final, v_g_ffn1, v_w_ffn1_gu, v_w_ffn1_down, v_g_mix, v_w_mix_in, v_w_conv, v_b_f, v_g_conv_out, v_g_att_out, v_w_mix_out, v_g_xattn, v_g_mem, v_w_xq, v_w_xkv, v_w_xo, v_g_ffn2, v_w_ffn2_gu, v_w_ffn2_down, v_g_final):
    local = dict(zip(WEIGHTS, (g_ffn1, w_ffn1_gu, w_ffn1_down, g_mix, w_mix_in, w_conv, b_f, g_conv_out, g_att_out, w_mix_out,
                               g_xattn, g_mem, w_xq, w_xkv, w_xo, g_ffn2, w_ffn2_gu, w_ffn2_down, g_final)))
    mom1 = dict(zip(WEIGHTS, (m_g_ffn1, m_w_ffn1_gu, m_w_ffn1_down, m_g_mix, m_w_mix_in, m_w_conv, m_b_f, m_g_conv_out,
                              m_g_att_out, m_w_mix_out, m_g_xattn, m_g_mem, m_w_xq, m_w_xkv, m_w_xo, m_g_ffn2, m_w_ffn2_gu,
                              m_w_ffn2_down, m_g_final)))
    mom2 = dict(zip(WEIGHTS, (v_g_ffn1, v_w_ffn1_gu, v_w_ffn1_down, v_g_mix, v_w_mix_in, v_w_conv, v_b_f, v_g_conv_out,
                              v_g_att_out, v_w_mix_out, v_g_xattn, v_g_mem, v_w_xq, v_w_xkv, v_w_xo, v_g_ffn2, v_w_ffn2_gu,
                              v_w_ffn2_down, v_g_final)))
    depth = g_ffn1.shape[0]
    s, d = x.shape[1], x.shape[2]
    w = dict(local)
    w.update(_gather_weights(local))

    xs = x.reshape(s, d)
    mems = mem.reshape(mem.shape[1], d)
    saved = []
    h = _rmsnorm_fwd(xs, w["g_ffn1"][0], name="l0_ffn1_norm")
    for l in range(depth):
        g_after = w["g_ffn1"][l + 1] if l + 1 < depth else None
        xs, h, s1 = _ffn_fwd(xs, h, w["w_ffn1_gu"][l], w["w_ffn1_down"][l], w["g_mix"][l], f"l{l}_ffn1")
        xs, h, s2 = _mix_fwd(xs, h, w, l, w["g_xattn"][l], f"l{l}_mix")
        xs, h, s3 = _xattn_block_fwd(xs, h, mems, w, l, w["g_ffn2"][l], f"l{l}_xattn")
        xs, h, s4 = _ffn_fwd(xs, h, w["w_ffn2_gu"][l], w["w_ffn2_down"][l], g_after, f"l{l}_ffn2")
        saved.append((s1, s2, s3, s4))

    dx, dg_final, sq = _final_loss(xs, g_final, loss_target.reshape(s, d), name="loss_head")
    loss = lax.psum(jnp.sum(sq) * (0.5 / d), MESH_AXES)

    per_layer = []
    for l in reversed(range(depth)):
        s1, s2, s3, s4 = saved[l]
        grads = {}
        dx, grads["g_ffn2"], grads["w_ffn2_gu"], grads["w_ffn2_down"] = _ffn_bwd(
            dx, s4, w["g_ffn2"][l], w["w_ffn2_gu"][l], w["w_ffn2_down"][l], f"l{l}_ffn2")
        dx, g3 = _xattn_block_bwd(dx, s3, mems, w, l, f"l{l}_xattn")
        dx, g2 = _mix_bwd(dx, s2, w, l, f"l{l}_mix")
        dx, grads["g_ffn1"], grads["w_ffn1_gu"], grads["w_ffn1_down"] = _ffn_bwd(
            dx, s1, w["g_ffn1"][l], w["w_ffn1_gu"][l], w["w_ffn1_down"][l], f"l{l}_ffn1")
        grads.update(g2)
        grads.update(g3)
        per_layer.append(grads)
    per_layer.reverse()
    grads = {name: jnp.stack([per_layer[l][name] for l in range(depth)]) for name in WEIGHTS if name != "g_final"}
    grads["g_final"] = dg_final.reshape(d)

    reduced = _reduce_gradients(grads, lax.axis_index("c"))
    deltas, new_m, new_v = {}, {}, {}
    for name in WEIGHTS:
        deltas[name], new_m[name], new_v[name] = _adamw(local[name], reduced[name], mom1[name], mom2[name], name=f"adamw_{name}")
    return (loss, dx.reshape(x.shape), *[reduced[n] for n in WEIGHTS], *[deltas[n] for n in WEIGHTS],
            *[new_m[n] for n in WEIGHTS], *[new_v[n] for n in WEIGHTS])
```
